```python
import math
import jax, jax.numpy as jnp
from jax import lax
import numpy as np

D_MODEL = 1024
BATCH = 2
SEQ = 8192
DEPTH = 1
DEC_BATCH = 128
DEC_SEQ = 1
PAST_LEN = 16384
PAGE_SIZE = 128

WINDOW = 128
ATT_BLOCK = 128
HEAD_DIM = 64
N_Q_HEADS = 8
N_KV_HEADS = 2
GQA_GROUP = N_Q_HEADS // N_KV_HEADS
ATT_WIDTH = N_Q_HEADS * HEAD_DIM
KV_WIDTH = N_KV_HEADS * HEAD_DIM
N_BUCKETS = 32
MAX_DISTANCE = WINDOW
M_HEADS = 4
M_DV = (D_MODEL // 2) // M_HEADS
M_DK = M_DV // 2
M_WIDTH = M_HEADS * M_DV
M_QK_WIDTH = M_HEADS * M_DK
MLSTM_CHUNK = 64
D_FF = -(-8 * D_MODEL // (3 * 256)) * 256
N_IN = ATT_WIDTH + 2 * KV_WIDTH + 2 * M_QK_WIDTH + 2 * M_WIDTH + 2 * M_HEADS + 2 * D_MODEL
EPS = 1e-6
NEG = -1e30

kernel_name = 'hybrid_swa_sink_mlstm_decoder_step'


def in_split_points():
    sizes = [ATT_WIDTH, KV_WIDTH, KV_WIDTH, M_QK_WIDTH, M_QK_WIDTH, M_WIDTH, M_WIDTH,
             M_HEADS, M_HEADS, D_MODEL, D_MODEL]
    return [int(s) for s in np.cumsum(sizes)[:-1]]


def rmsnorm(x, g):
    xf = x.astype(jnp.float32)
    y = xf * lax.rsqrt(jnp.mean(xf * xf, axis=-1, keepdims=True) + EPS)
    return (y * g.astype(jnp.float32)).astype(x.dtype)


def t5_bucket(dist):
    n = jnp.maximum(dist, 0)
    max_exact = N_BUCKETS // 2
    nf = jnp.maximum(n, 1).astype(jnp.float32)
    large = max_exact + (jnp.log(nf / max_exact) / math.log(MAX_DISTANCE / max_exact)
                         * (N_BUCKETS - max_exact)).astype(jnp.int32)
    large = jnp.minimum(large, N_BUCKETS - 1)
    return jnp.where(n < max_exact, n, large)


def sink_window_attention(q, k, v, dist, valid, rel_bias, sinks):
    qf = q.astype(jnp.float32) * (HEAD_DIM ** -0.5)
    s = jnp.einsum('...qhgd,...khd->...hgqk', qf, k.astype(jnp.float32))
    nq, nk = dist.shape
    bias = rel_bias.astype(jnp.float32)[t5_bucket(dist)]
    bias = jnp.transpose(bias, (2, 0, 1)).reshape(N_KV_HEADS, GQA_GROUP, nq, nk)
    s = jnp.where(valid, s + bias, NEG)
    sink = sinks.astype(jnp.float32).reshape(N_KV_HEADS, GQA_GROUP, 1, 1)
    m = jnp.maximum(jnp.max(s, axis=-1, keepdims=True), sink)
    p = jnp.exp(s - m)
    denom = jnp.sum(p, axis=-1, keepdims=True) + jnp.exp(sink - m)
    return jnp.einsum('...hgqk,...khd->...qhgd', p / denom, v.astype(jnp.float32))


def attn_prompt(q, k, v, rel_bias, sinks):
    bsz, s_len = q.shape[:2]
    nb = s_len // ATT_BLOCK
    qb = q.reshape(bsz, nb, ATT_BLOCK, N_KV_HEADS, GQA_GROUP, HEAD_DIM)

    def band(t):
        tb = t.reshape(bsz, nb, ATT_BLOCK, N_KV_HEADS, HEAD_DIM)
        prev = jnp.pad(tb, ((0, 0), (1, 0), (0, 0), (0, 0), (0, 0)))[:, :-1]
        return jnp.concatenate([prev, tb], axis=2)

    kb, vb = band(k), band(v)
    qi = jnp.arange(ATT_BLOCK)[:, None]
    kj = jnp.arange(2 * ATT_BLOCK)[None, :]
    dist = ATT_BLOCK + qi - kj
    blk = jnp.arange(nb)[:, None, None]
    valid = (dist >= 0) & (dist < WINDOW) & (blk * ATT_BLOCK - ATT_BLOCK + kj >= 0)
    o = sink_window_attention(qb, kb, vb, dist, valid[:, None, None], rel_bias, sinks)
    return o.reshape(bsz, s_len, ATT_WIDTH)


def attn_sample(q, k, v, buf_k, buf_v, rel_bias, sinks):
    bsz, s_len = q.shape[:2]
    wb = buf_k.shape[1]
    keys = jnp.concatenate([buf_k.astype(k.dtype), k], axis=1)
    vals = jnp.concatenate([buf_v.astype(v.dtype), v], axis=1)
    qi = jnp.arange(s_len)[:, None]
    kj = jnp.arange(wb + s_len)[None, :]
    dist = wb + qi - kj
    valid = (dist >= 0) & (dist < WINDOW)
    o = sink_window_attention(q, keys, vals, dist, valid, rel_bias, sinks)
    return o.reshape(bsz, s_len, ATT_WIDTH), keys[:, -wb:], vals[:, -wb:]


def mlstm_chunkwise(q, k, v, ig, lf, c0, n0, m0):
    bsz, s_len, nh, _ = q.shape
    L = math.gcd(s_len, MLSTM_CHUNK)
    nc = s_len // L

    def to_chunks(t):
        t = t.reshape((bsz, nc, L, nh) + t.shape[3:])
        return jnp.moveaxis(jnp.moveaxis(t, 1, 0), 3, 2)

    causal = jnp.tril(jnp.ones((L, L), dtype=bool))

    def step(carry, inp):
        c, n, m = carry
        qc, kc, vc, igc, lfc = inp
        b = jnp.cumsum(lfc, axis=-1)
        a = igc - b
        m_t = b + jnp.maximum(m[..., None], lax.cummax(a, axis=a.ndim - 1))
        log_d = a[..., None, :] + b[..., :, None] - m_t[..., :, None]
        d = jnp.exp(jnp.where(causal, log_d, NEG))
        inter = jnp.exp(m[..., None] + b - m_t)
        w = d * jnp.einsum('bhtd,bhsd->bhts', qc, kc)
        num = inter[..., None] * jnp.einsum('bhvd,bhtd->bhtv', c, qc) + jnp.einsum('bhts,bhsv->bhtv', w, vc)
        den = inter * jnp.einsum('bhd,bhtd->bht', n, qc) + jnp.sum(w, axis=-1)
        h = num / jnp.maximum(jnp.abs(den), jnp.exp(-m_t))[..., None]
        m_end = m_t[..., -1]
        w_s = jnp.exp(a + b[..., -1:] - m_end[..., None])
        decay = jnp.exp(m + b[..., -1] - m_end)
        c_new = decay[..., None, None] * c + jnp.einsum('bhs,bhsv,bhsd->bhvd', w_s, vc, kc)
        n_new = decay[..., None] * n + jnp.einsum('bhs,bhsd->bhd', w_s, kc)
        return (c_new, n_new, m_end), h

    xs = (to_chunks(q), to_chunks(k), to_chunks(v), to_chunks(ig), to_chunks(lf))
    (c1, n1, m1), hs = lax.scan(step, (c0, n0, m0), xs)
    hs = jnp.moveaxis(jnp.moveaxis(hs, 3, 2), 0, 1).reshape(bsz, s_len, nh, v.shape[-1])
    return hs, c1, n1, m1


def decoder_layer(x, buf_k, buf_v, c0, n0, m0, rel_bias, w_in, b_if, sinks, g_attn_norm, g_head,
                  w_att_out, w_mlstm_out, w_out, g_ffn_norm, w_gate, w_up, w_down):
    f32 = jnp.float32
    bsz, s_len, _ = x.shape
    h = rmsnorm(x, g_attn_norm)
    z = h @ w_in
    q_a, k_a, v_a, q_m, k_m, v_m, o_m, i_pre, f_pre, gate_a, gate_m = jnp.split(z, in_split_points(), axis=-1)
    q_a = q_a.reshape(bsz, s_len, N_KV_HEADS, GQA_GROUP, HEAD_DIM)
    k_a = k_a.reshape(bsz, s_len, N_KV_HEADS, HEAD_DIM)
    v_a = v_a.reshape(bsz, s_len, N_KV_HEADS, HEAD_DIM)
    if buf_k is None:
        y_att = attn_prompt(q_a, k_a, v_a, rel_bias, sinks)
        new_k, new_v = k_a[:, -WINDOW:], v_a[:, -WINDOW:]
    else:
        y_att, new_k, new_v = attn_sample(q_a, k_a, v_a, buf_k, buf_v, rel_bias, sinks)
    qm = q_m.astype(f32).reshape(bsz, s_len, M_HEADS, M_DK)
    km = k_m.astype(f32).reshape(bsz, s_len, M_HEADS, M_DK) * (M_DK ** -0.5)
    vm = v_m.astype(f32).reshape(bsz, s_len, M_HEADS, M_DV)
    ig = i_pre.astype(f32) + b_if[0].astype(f32)
    lf = jax.nn.log_sigmoid(f_pre.astype(f32) + b_if[1].astype(f32))
    if c0 is None:
        c0 = jnp.zeros((bsz, M_HEADS, M_DV, M_DK), f32)
        n0 = jnp.zeros((bsz, M_HEADS, M_DK), f32)
        m0 = jnp.zeros((bsz, M_HEADS), f32)
    hm, c1, n1, m1 = mlstm_chunkwise(qm, km, vm, ig, lf, c0.astype(f32), n0.astype(f32), m0.astype(f32))
    hm = hm * lax.rsqrt(jnp.mean(hm * hm, axis=-1, keepdims=True) + EPS)
    hm = hm.reshape(bsz, s_len, M_WIDTH) * g_head.astype(f32)
    y_mlstm = hm.astype(x.dtype) * jax.nn.sigmoid(o_m)
    y_a = y_att.astype(x.dtype) @ w_att_out
    y_m = y_mlstm @ w_mlstm_out
    mixed = jax.nn.sigmoid(gate_a) * y_a + jax.nn.sigmoid(gate_m) * y_m
    x = x + mixed @ w_out
    h2 = rmsnorm(x, g_ffn_norm)
    x = x + (jax.nn.silu(h2 @ w_gate) * (h2 @ w_up)) @ w_down
    return x, (new_k, new_v, c1, n1, m1)


def setup_inputs(seed: int = 0) -> dict:
    key = jax.random.key(seed)
    ks = jax.random.split(key, 24)
    f32 = jnp.float32

    def nrm(k, shape, scale):
        return scale * jax.random.normal(k, shape, f32)

    win_buf = min(WINDOW, PAST_LEN)
    b_if = jnp.stack([nrm(ks[9], (DEPTH, M_HEADS), 0.1),
                      jnp.linspace(3.0, 6.0, M_HEADS)[None, :] + nrm(ks[10], (DEPTH, M_HEADS), 0.1)], axis=1)
    return {
        'x_prompt': nrm(ks[0], (BATCH, SEQ, D_MODEL), 1.0),
        'x_sample': nrm(ks[1], (DEC_BATCH, DEC_SEQ, D_MODEL), 1.0),
        'cache_k_win': nrm(ks[2], (DEPTH, DEC_BATCH, win_buf, N_KV_HEADS, HEAD_DIM), 1.0),
        'cache_v_win': nrm(ks[3], (DEPTH, DEC_BATCH, win_buf, N_KV_HEADS, HEAD_DIM), 1.0),
        'state_mlstm_C': nrm(ks[4], (DEPTH, DEC_BATCH, M_HEADS, M_DV, M_DK), 0.1),
        'state_mlstm_n': nrm(ks[5], (DEPTH, DEC_BATCH, M_HEADS, M_DK), 0.1),
        'state_mlstm_m': nrm(ks[6], (DEPTH, DEC_BATCH, M_HEADS), 1.0),
        'rel_bias': nrm(ks[7], (N_BUCKETS, N_Q_HEADS), 0.1),
        'w_in': nrm(ks[8], (DEPTH, D_MODEL, N_IN), D_MODEL ** -0.5),
        'b_if': b_if,
        'sinks': nrm(ks[11], (DEPTH, N_Q_HEADS), 0.5),
        'g_attn_norm': 1.0 + nrm(ks[12], (DEPTH, D_MODEL), 0.05),
        'g_head': 1.0 + nrm(ks[13], (DEPTH, M_WIDTH), 0.05),
        'w_att_out': nrm(ks[14], (DEPTH, ATT_WIDTH, D_MODEL), ATT_WIDTH ** -0.5),
        'w_mlstm_out': nrm(ks[15], (DEPTH, M_WIDTH, D_MODEL), M_WIDTH ** -0.5),
        'w_out': nrm(ks[16], (DEPTH, D_MODEL, D_MODEL), D_MODEL ** -0.5),
        'g_ffn_norm': 1.0 + nrm(ks[17], (DEPTH, D_MODEL), 0.05),
        'w_gate': nrm(ks[18], (DEPTH, D_MODEL, D_FF), D_MODEL ** -0.5),
        'w_up': nrm(ks[19], (DEPTH, D_MODEL, D_FF), D_MODEL ** -0.5),
        'w_down': nrm(ks[20], (DEPTH, D_FF, D_MODEL), D_FF ** -0.5),
        'g_final': 1.0 + nrm(ks[21], (D_MODEL,), 0.05),
    }


def reference(x_prompt, x_sample, cache_k_win, cache_v_win, state_mlstm_C, state_mlstm_n, state_mlstm_m,
              rel_bias, w_in, b_if, sinks, g_attn_norm, g_head, w_att_out, w_mlstm_out, w_out,
              g_ffn_norm, w_gate, w_up, w_down, g_final):
    xp, xs = x_prompt, x_sample
    p_new, s_new = [], []
    for l in range(DEPTH):
        lw = (rel_bias, w_in[l], b_if[l], sinks[l], g_attn_norm[l], g_head[l], w_att_out[l],
              w_mlstm_out[l], w_out[l], g_ffn_norm[l], w_gate[l], w_up[l], w_down[l])
        xp, st_p = decoder_layer(xp, None, None, None, None, None, *lw)
        xs, st_s = decoder_layer(xs, cache_k_win[l], cache_v_win[l], state_mlstm_C[l],
                                 state_mlstm_n[l], state_mlstm_m[l], *lw)
        p_new.append(st_p)
        s_new.append(st_s)
    y_prompt = rmsnorm(xp, g_final)
    y_sample = rmsnorm(xs, g_final)
    p_k, p_v, p_c, p_n, p_m = [jnp.stack(a) for a in zip(*p_new)]
    s_k, s_v, s_c, s_n, s_m = [jnp.stack(a) for a in zip(*s_new)]
    return (y_prompt, y_sample, p_k, p_v, p_c, p_n, p_m, s_k, s_v, s_c, s_n, s_m)
```

```python
import functools
import math

import jax
import jax.numpy as jnp
import numpy as np
from jax import lax
from jax.experimental import pallas as pl
from jax.experimental.pallas import tpu as pltpu

F32 = jnp.float32
BF16 = jnp.bfloat16

HEAD_DIM = 64
N_Q_HEADS = 8
N_KV_HEADS = 2
WINDOW = 128
N_BUCKETS = 32
MAX_DISTANCE = WINDOW
M_HEADS = 4
EPS = 1e-6
NEG = -1e30

LANES = 128
VMEM_LIMIT_BYTES = 58 * 1024 * 1024

QA, KA, VA, QM, KM, VM, OM, GA, GM, IF = 0, 512, 640, 768, 1024, 1280, 1792, 2304, 3328, 4352
N_CAT = 4480
T_BLK = 256


def _dot(a, b):
    return jnp.dot(a, b, preferred_element_type=F32)


def _dot_nt(a, b):
    return lax.dot_general(a, b, (((1,), (1,)), ((), ())), preferred_element_type=F32)


def _dot_tn(a, b):
    return lax.dot_general(a, b, (((0,), (0,)), ((), ())), preferred_element_type=F32)


def _sigmoid(x):
    return 1.0 / (1.0 + jnp.exp(-x))


def _log_sigmoid(x):
    return jnp.minimum(x, 0.0) - jnp.log1p(jnp.exp(-jnp.abs(x)))


def _rms(x, g):
    ms = jnp.mean(x * x, axis=-1, keepdims=True)
    return x * lax.rsqrt(ms + EPS) * g


def _t5_bucket(dist):
    n = jnp.maximum(dist, 0)
    max_exact = N_BUCKETS // 2
    nf = jnp.maximum(n, 1).astype(F32)
    large = max_exact + (jnp.log(nf / max_exact) / math.log(MAX_DISTANCE / max_exact)
                         * (N_BUCKETS - max_exact)).astype(jnp.int32)
    large = jnp.minimum(large, N_BUCKETS - 1)
    return jnp.where(n < max_exact, n, large)


def _bias_from_buckets(buckets, relb_ref, head):
    acc = jnp.zeros(buckets.shape, F32)
    for k in range(N_BUCKETS):
        acc = jnp.where(buckets == k, relb_ref[k, head], acc)
    return acc


def _prompt_kernel(x_ref, wcat_ref, wao_ref, wmo_ref, wout_ref, wg_ref, wu_ref, wd_ref,
                   gattn_ref, gffn_ref, gfin_ref, ghead_ref, bif_ref, fbuck_ref, relb_ref, sinks_ref,
                   y_ref, kwin_ref, vwin_ref, c_ref, n_ref, m_ref,
                   z_scr, yatt_scr, ym_scr, a_scr, bias_scr, kprev_scr, vprev_scr, cbd_scr, st_scr):
    T = T_BLK
    b = pl.program_id(0)
    j = pl.program_id(1)
    nj = pl.num_programs(1)
    d_ff = wg_ref.shape[1]

    @pl.when(jnp.logical_and(b == 0, j == 0))
    def _build_bias():
        fb = fbuck_ref[...]
        for h in range(N_Q_HEADS):
            bias_scr[h] = _bias_from_buckets(fb, relb_ref, h)

    @pl.when(j == 0)
    def _reset_state():
        kprev_scr[...] = jnp.zeros_like(kprev_scr)
        vprev_scr[...] = jnp.zeros_like(vprev_scr)
        cbd_scr[...] = jnp.zeros_like(cbd_scr)
        st_scr[...] = jnp.zeros_like(st_scr)

    x = x_ref[...]
    h = _rms(x, gattn_ref[...]).astype(BF16)
    z_scr[...] = _dot(h, wcat_ref[...])

    lane = lax.broadcasted_iota(jnp.int32, (T, LANES), 1)
    lane_lo = lane < HEAD_DIM
    row = lax.broadcasted_iota(jnp.int32, (T, LANES), 0)
    qi = jnp.where(row < 128, row, row - 128)
    tri2 = lane <= qi
    valid0 = lane <= qi + jnp.where(j > 0, 2 * LANES, 0)
    rowc = lax.broadcasted_iota(jnp.int32, (T, 1), 0)
    k_all = z_scr[:, KA:KA + 128]
    v_all = z_scr[:, VA:VA + 128]
    for sb in range(2):
        r0 = 128 * sb
        if sb == 0:
            kp, vp = kprev_scr[...], vprev_scr[...]
        else:
            kp, vp = k_all[0:128], v_all[0:128]
        kcat = jnp.concatenate([kp, k_all[r0:r0 + 128]], axis=0)
        vcat = jnp.concatenate([vp, v_all[r0:r0 + 128]], axis=0)
        kroll = pltpu.roll(kcat, HEAD_DIM, 1)
        vroll = pltpu.roll(vcat, HEAD_DIM, 1)
        zero = jnp.zeros_like(kcat)
        k_mats = [jnp.where(lane_lo, kcat, zero), jnp.where(lane_lo, zero, kroll),
                  jnp.where(lane_lo, kroll, zero), jnp.where(lane_lo, zero, kcat)]
        v_mats = [jnp.where(lane_lo, vcat, zero), jnp.where(lane_lo, zero, vroll),
                  jnp.where(lane_lo, vroll, zero), jnp.where(lane_lo, zero, vcat)]
        q = (z_scr[r0:r0 + 128, QA:QA + 512] * (HEAD_DIM ** -0.5)).astype(BF16)
        lhs_a = jnp.concatenate([q[:, 0:128], q[:, 128:256]], axis=0)
        lhs_b = jnp.concatenate([q[:, 256:384], q[:, 384:512]], axis=0)
        groups = [(lhs_a, 0, 0, 2), (lhs_a, 1, 1, 3), (lhs_b, 2, 4, 6), (lhs_b, 3, 5, 7)]
        pcs = []
        for lhs, mi, ha, hb in groups:
            s = _dot_nt(lhs, k_mats[mi].astype(BF16))
            sf = jnp.where(tri2, s[:, 128:256], s[:, 0:128])
            sf = sf + jnp.concatenate([bias_scr[ha], bias_scr[hb]], axis=0)
            if sb == 0:
                sf = jnp.where(valid0, sf, NEG)
            sink = jnp.where(rowc < 128, sinks_ref[ha], sinks_ref[hb])
            mx = jnp.maximum(jnp.max(sf, axis=-1, keepdims=True), sink)
            p = jnp.exp(sf - mx)
            den = jnp.sum(p, axis=-1, keepdims=True) + jnp.exp(sink - mx)
            pn = p * (1.0 / den)
            zp = jnp.zeros_like(pn)
            pcs.append(jnp.concatenate([jnp.where(tri2, zp, pn), jnp.where(tri2, pn, zp)],
                                       axis=1).astype(BF16))
        o_a = _dot(jnp.concatenate([pcs[0], pcs[1]], axis=1),
                   jnp.concatenate([v_mats[0], v_mats[1]], axis=0).astype(BF16))
        o_b = _dot(jnp.concatenate([pcs[2], pcs[3]], axis=1),
                   jnp.concatenate([v_mats[2], v_mats[3]], axis=0).astype(BF16))
        yatt_scr[r0:r0 + 128, 0:128] = o_a[0:128].astype(BF16)
        yatt_scr[r0:r0 + 128, 128:256] = o_a[128:256].astype(BF16)
        yatt_scr[r0:r0 + 128, 256:384] = o_b[0:128].astype(BF16)
        yatt_scr[r0:r0 + 128, 384:512] = o_b[128:256].astype(BF16)
    kprev_scr[...] = k_all[128:256]
    vprev_scr[...] = v_all[128:256]

    @pl.when(j == nj - 1)
    def _write_window():
        kwin_ref[...] = k_all[128:256]
        vwin_ref[...] = v_all[128:256]

    zif = z_scr[:, IF:IF + 128] + bif_ref[...]
    gl = jnp.where(lane < M_HEADS, zif, _log_sigmoid(zif))
    gl_t = gl.T
    tr = lax.broadcasted_iota(jnp.int32, (T, T), 0)
    ts = lax.broadcasted_iota(jnp.int32, (T, T), 1)
    tril = ts <= tr
    triu = tr <= ts
    st = st_scr[...]
    row2 = lax.broadcasted_iota(jnp.int32, (2 * LANES, LANES), 0)
    lane2 = lax.broadcasted_iota(jnp.int32, (2 * LANES, LANES), 1)
    bd_mask = (row2 < LANES) == (lane2 < HEAD_DIM)
    for p in range(2):
        q_pair = z_scr[:, QM + 128 * p:QM + 128 * p + 128]
        k_pair = z_scr[:, KM + 128 * p:KM + 128 * p + 128] * (HEAD_DIM ** -0.5)
        v_pair = z_scr[:, VM + 256 * p:VM + 256 * p + 256]
        cbd = cbd_scr[p]
        n_pair = st[p:p + 1, :]
        q_bf = q_pair.astype(BF16)
        qc = _dot_nt(q_bf, cbd.astype(BF16))
        qn_prod = q_pair * n_pair
        ws, decays, m_ends = [], [], []
        for hh in range(2):
            hd = 2 * p + hh
            hmask = lane_lo if hh == 0 else jnp.logical_not(lane_lo)
            ig_c = gl[:, hd:hd + 1]
            lf_c = gl[:, M_HEADS + hd:M_HEADS + hd + 1]
            ig_r = gl_t[hd:hd + 1, :]
            lf_r = gl_t[M_HEADS + hd:M_HEADS + hd + 1, :]
            m_prev = st[2 + hd:3 + hd, 0:1]
            b_c = jnp.sum(jnp.where(tril, lf_r, 0.0), axis=1, keepdims=True)
            b_r = jnp.sum(jnp.where(triu, lf_c, 0.0), axis=0, keepdims=True)
            a_r = ig_r - b_r
            cm_c = jnp.max(jnp.where(tril, a_r, NEG), axis=1, keepdims=True)
            mt_c = b_c + jnp.maximum(m_prev, cm_c)
            g_c = b_c - mt_c
            dm = jnp.exp(jnp.where(tril, a_r + g_c, NEG))
            inter_c = jnp.exp(m_prev + g_c)
            k_h = jnp.where(hmask, k_pair, 0.0).astype(BF16)
            w = dm * _dot_nt(q_bf, k_h)
            v_h = v_pair[:, 128 * hh:128 * hh + 128]
            num = inter_c * qc[:, 128 * hh:128 * hh + 128] + _dot(w.astype(BF16), v_h.astype(BF16))
            qn = jnp.sum(jnp.where(hmask, qn_prod, 0.0), axis=1, keepdims=True)
            den = inter_c * qn + jnp.sum(w, axis=1, keepdims=True)
            hv = num / jnp.maximum(jnp.abs(den), jnp.exp(-mt_c))
            hn = hv * lax.rsqrt(jnp.mean(hv * hv, axis=-1, keepdims=True) + EPS)
            hn = hn * ghead_ref[:, 128 * hd:128 * hd + 128]
            om = z_scr[:, OM + 128 * hd:OM + 128 * hd + 128]
            ym_scr[:, 128 * hd:128 * hd + 128] = (hn * _sigmoid(om)).astype(BF16)
            b_end = b_c[T - 1:T, :]
            m_end = mt_c[T - 1:T, :]
            ws.append(jnp.exp((ig_c - b_c) + b_end - m_end))
            decays.append(jnp.exp(m_prev + b_end - m_end))
            m_ends.append(m_end)
        kw = k_pair * jnp.where(lane_lo, ws[0], ws[1])
        upd = _dot(v_pair.T.astype(BF16), kw.astype(BF16))
        dec_rows = jnp.where(row2[:, 0:1] < LANES, decays[0], decays[1])
        cbd_scr[p] = dec_rows * cbd + jnp.where(bd_mask, upd, 0.0)
        dec_lanes = jnp.where(lane_lo[0:1, :], decays[0], decays[1])
        st_scr[p:p + 1, :] = dec_lanes * n_pair + jnp.sum(kw, axis=0, keepdims=True)
        for hh in range(2):
            hd = 2 * p + hh
            st_scr[2 + hd:3 + hd, :] = jnp.broadcast_to(m_ends[hh], (1, LANES))

    @pl.when(j == nj - 1)
    def _write_state():
        stn = st_scr[...]
        for hd in range(M_HEADS):
            p, hh = hd // 2, hd % 2
            c_ref[hd] = cbd_scr[p, 128 * hh:128 * hh + 128, 64 * hh:64 * hh + 64]
            n_ref[hd:hd + 1, :] = stn[p:p + 1, 64 * hh:64 * hh + 64]
            m_ref[0:1, hd:hd + 1] = stn[2 + hd:3 + hd, 0:1]

    ya = _dot(yatt_scr[...], wao_ref[...])
    ymm = _dot(ym_scr[...], wmo_ref[...])
    mixed = _sigmoid(z_scr[:, GA:GA + 1024]) * ya + _sigmoid(z_scr[:, GM:GM + 1024]) * ymm
    x1 = x + _dot(mixed.astype(BF16), wout_ref[...])
    h2 = _rms(x1, gffn_ref[...]).astype(BF16)
    for c0 in range(0, d_ff, 256):
        g = _dot(h2, wg_ref[:, c0:c0 + 256])
        u = _dot(h2, wu_ref[:, c0:c0 + 256])
        a_scr[:, c0:c0 + 256] = (g * _sigmoid(g) * u).astype(BF16)
    x2 = x1 + _dot(a_scr[...], wd_ref[...])
    y_ref[...] = _rms(x2, gfin_ref[...])


def _resident(shape):
    zeros = (0,) * len(shape)
    return pl.BlockSpec(shape, lambda b, j: zeros, pipeline_mode=pl.Buffered(1))


def _smem():
    return pl.BlockSpec(memory_space=pltpu.SMEM)


def _prompt_call(x, wcat, wao, wmo, wout, wg, wu, wd, gattn, gffn, gfin, ghead, bif, fbuck, relb, sinks):
    bsz, s_len, d = x.shape
    assert s_len % T_BLK == 0 and d == 1024
    d_ff = wg.shape[1]
    nj = s_len // T_BLK
    out_shapes = (
        jax.ShapeDtypeStruct((bsz, s_len, d), F32),
        jax.ShapeDtypeStruct((bsz, WINDOW, 128), F32),
        jax.ShapeDtypeStruct((bsz, WINDOW, 128), F32),
        jax.ShapeDtypeStruct((bsz, M_HEADS, 128, 64), F32),
        jax.ShapeDtypeStruct((bsz, M_HEADS, 64), F32),
        jax.ShapeDtypeStruct((bsz, 1, M_HEADS), F32),
    )
    in_specs = [
        pl.BlockSpec((None, T_BLK, d), lambda b, j: (b, j, 0)),
        _resident(wcat.shape), _resident(wao.shape), _resident(wmo.shape), _resident(wout.shape),
        _resident(wg.shape), _resident(wu.shape), _resident(wd.shape),
        _resident(gattn.shape), _resident(gffn.shape), _resident(gfin.shape), _resident(ghead.shape),
        _resident(bif.shape), _resident(fbuck.shape), _smem(), _smem(),
    ]
    out_specs = (
        pl.BlockSpec((None, T_BLK, d), lambda b, j: (b, j, 0)),
        pl.BlockSpec((None, WINDOW, 128), lambda b, j: (b, 0, 0)),
        pl.BlockSpec((None, WINDOW, 128), lambda b, j: (b, 0, 0)),
        pl.BlockSpec((None, M_HEADS, 128, 64), lambda b, j: (b, 0, 0, 0)),
        pl.BlockSpec((None, M_HEADS, 64), lambda b, j: (b, 0, 0)),
        pl.BlockSpec((None, 1, M_HEADS), lambda b, j: (b, 0, 0)),
    )
    scratch = [
        pltpu.VMEM((T_BLK, N_CAT), F32),
        pltpu.VMEM((T_BLK, 512), BF16),
        pltpu.VMEM((T_BLK, 512), BF16),
        pltpu.VMEM((T_BLK, d_ff), BF16),
        pltpu.VMEM((N_Q_HEADS, 128, 128), F32),
        pltpu.VMEM((128, 128), F32),
        pltpu.VMEM((128, 128), F32),
        pltpu.VMEM((2, 256, 128), F32),
        pltpu.VMEM((8, 128), F32),
    ]
    return pl.pallas_call(
        _prompt_kernel,
        grid=(bsz, nj),
        in_specs=in_specs,
        out_specs=out_specs,
        out_shape=out_shapes,
        scratch_shapes=scratch,
        compiler_params=pltpu.CompilerParams(
            dimension_semantics=("arbitrary", "arbitrary"),
            vmem_limit_bytes=VMEM_LIMIT_BYTES),
        name="prompt_layer",
    )(x, wcat, wao, wmo, wout, wg, wu, wd, gattn, gffn, gfin, ghead, bif, fbuck, relb, sinks)


SK1_CHUNK = 640


def _sample_inproj_kernel(x_ref, g_ref, w_ref, z_ref):
    h = _rms(x_ref[...], g_ref[...]).astype(BF16)
    z_ref[...] = _dot(h, w_ref[...])


def _sample_inproj_call(x, gattn, wcat):
    n, d = x.shape
    return pl.pallas_call(
        _sample_inproj_kernel,
        grid=(N_CAT // SK1_CHUNK,),
        in_specs=[pl.BlockSpec((n, d), lambda i: (0, 0)),
                  pl.BlockSpec((1, d), lambda i: (0, 0)),
                  pl.BlockSpec((d, SK1_CHUNK), lambda i: (0, i))],
        out_specs=pl.BlockSpec((n, SK1_CHUNK), lambda i: (0, i)),
        out_shape=jax.ShapeDtypeStruct((n, N_CAT), F32),
        compiler_params=pltpu.CompilerParams(dimension_semantics=("arbitrary",)),
        name="sample_inproj",
    )(x, gattn, wcat)


SK2_BB = 8


def _rows8(x4):
    r = lax.broadcasted_iota(jnp.int32, (8, x4.shape[1]), 0)
    out = jnp.zeros((8, x4.shape[1]), x4.dtype)
    for hd in range(x4.shape[0]):
        out = jnp.where(r == hd, x4[hd:hd + 1, :], out)
    return out


def _sample_mixer_kernel(z_ref, ck_ref, cv_ref, c_ref, n_ref, m_ref, ghead_ref, bif_ref, sbuck_ref,
                         relb_ref, sinks_ref,
                         yatt_ref, ym_ref, ko_ref, vo_ref, co_ref, no_ref, mo_ref,
                         sbias_scr):
    i = pl.program_id(0)

    @pl.when(i == 0)
    def _build_bias():
        sb = sbuck_ref[...]
        r8 = lax.broadcasted_iota(jnp.int32, (8, LANES), 0)
        acc = jnp.zeros((8, LANES), F32)
        for rrow, hd in enumerate((0, 2, 4, 6, 1, 3, 5, 7)):
            acc = jnp.where(r8 == rrow, _bias_from_buckets(sb, relb_ref, hd), acc)
        sbias_scr[...] = acc

    r8 = lax.broadcasted_iota(jnp.int32, (8, LANES), 0)
    l8 = lax.broadcasted_iota(jnp.int32, (8, LANES), 1)
    lo8 = l8 < HEAD_DIM
    r8c = lax.broadcasted_iota(jnp.int32, (8, 1), 0)
    sink = jnp.zeros((8, 1), F32)
    for rrow, hd in enumerate((0, 2, 4, 6, 1, 3, 5, 7)):
        sink = jnp.where(r8c == rrow, sinks_ref[hd], sink)
    row128 = lax.broadcasted_iota(jnp.int32, (WINDOW, LANES), 0)
    last_row = row128 == WINDOW - 1
    sbias = sbias_scr[...]
    z8 = jnp.zeros((8, LANES), F32)
    l512 = lax.broadcasted_iota(jnp.int32, (8, 4 * LANES), 1)
    r512 = lax.broadcasted_iota(jnp.int32, (8, 4 * LANES), 0)
    head_blk = (l512 // LANES) == r512
    r4c = r8c
    ghead4 = ghead_ref[...]
    bif = bif_ref[...]

    for bi in range(SK2_BB):
        zb = z_ref[bi]
        knew = zb[KA // LANES:KA // LANES + 1, :]
        vnew = zb[VA // LANES:VA // LANES + 1, :]
        kt = jnp.where(last_row, knew, pltpu.roll(ck_ref[bi], WINDOW - 1, 0))
        vt = jnp.where(last_row, vnew, pltpu.roll(cv_ref[bi], WINDOW - 1, 0))
        ko_ref[bi] = kt
        vo_ref[bi] = vt
        xq = zb[0:8, :] * (HEAD_DIM ** -0.5)
        xs = pltpu.roll(xq, 4, 0)
        xr = pltpu.roll(xq, HEAD_DIM, 1)
        xsr = pltpu.roll(xs, HEAD_DIM, 1)
        qm = jnp.where(r8 < 2, jnp.where(lo8, xq, z8),
                       jnp.where(r8 < 4, jnp.where(lo8, z8, xr),
                                 jnp.where(r8 < 6, jnp.where(lo8, xsr, z8), jnp.where(lo8, z8, xs))))
        s = _dot_nt(qm.astype(BF16), kt.astype(BF16)) + sbias
        mx = jnp.maximum(jnp.max(s, axis=-1, keepdims=True), sink)
        pe = jnp.exp(s - mx)
        den = jnp.sum(pe, axis=-1, keepdims=True) + jnp.exp(sink - mx)
        pn = pe * (1.0 / den)
        oa = _dot(pn.astype(BF16), vt.astype(BF16))
        oas = pltpu.roll(oa, 4, 0)
        oar = pltpu.roll(oa, HEAD_DIM, 1)
        oasr = pltpu.roll(oas, HEAD_DIM, 1)
        ypair = jnp.where(r8 < 2, jnp.where(lo8, oa, oasr), jnp.where(lo8, oar, oas))
        yatt_ref[bi] = ypair[0:4, :]

        gm = jnp.broadcast_to(zb[IF // LANES:IF // LANES + 1, :] + bif, (8, LANES))
        ig = jnp.sum(jnp.where(l8 == r8, gm, 0.0), axis=1, keepdims=True)
        fp = jnp.sum(jnp.where(l8 == r8 + M_HEADS, gm, 0.0), axis=1, keepdims=True)
        lf = _log_sigmoid(fp)
        m0 = _rows8(m_ref[bi])
        a = ig - lf
        m_t = lf + jnp.maximum(m0, a)
        dgate = jnp.exp(a + lf - m_t)
        inter = jnp.exp(m0 + lf - m_t)
        qp = zb[QM // LANES:QM // LANES + 8, :]
        qpr = pltpu.roll(qp, HEAD_DIM, 1)
        q4 = jnp.where(lo8, jnp.where(r8 == 0, qp[0:1], jnp.where(r8 == 1, qpr[0:1],
                       jnp.where(r8 == 2, qp[1:2], jnp.where(r8 == 3, qpr[1:2], 0.0)))), 0.0)
        k4 = jnp.where(lo8, jnp.where(r8 == 0, qp[2:3], jnp.where(r8 == 1, qpr[2:3],
                       jnp.where(r8 == 2, qp[3:4], jnp.where(r8 == 3, qpr[3:4], 0.0)))), 0.0)
        k4 = k4 * (HEAD_DIM ** -0.5)
        n4 = _rows8(n_ref[bi])
        v8 = zb[VM // LANES:VM // LANES + 8, :]
        v4 = jnp.where(r8 < M_HEADS, v8, 0.0)
        o8 = zb[OM // LANES:OM // LANES + 8, :]
        call = jnp.concatenate([c_ref[bi, hd] for hd in range(M_HEADS)], axis=0)
        rq = _dot_nt(q4[:, 0:HEAD_DIM].astype(BF16), call.astype(BF16))
        cq = jnp.where(r8 == 0, rq[:, 0:128], jnp.where(r8 == 1, rq[:, 128:256],
                       jnp.where(r8 == 2, rq[:, 256:384], jnp.where(r8 == 3, rq[:, 384:512], 0.0))))
        qk = jnp.sum(q4 * k4, axis=1, keepdims=True)
        nq = jnp.sum(n4 * q4[:, 0:HEAD_DIM], axis=1, keepdims=True)
        w = dgate * qk
        num = inter * cq + w * v4
        den_m = inter * nq + w
        hv = num / jnp.maximum(jnp.abs(den_m), jnp.exp(-m_t))
        hn = hv * lax.rsqrt(jnp.mean(hv * hv, axis=-1, keepdims=True) + EPS)
        ym8 = hn[0:4, :] * ghead4 * _sigmoid(o8[0:4, :])
        ym_ref[bi] = ym8
        a512 = jnp.where(head_blk, jnp.concatenate([v4, v4, v4, v4], axis=1), 0.0)
        b128 = dgate * k4
        upd = _dot_tn(a512, b128)
        for hd in range(M_HEADS):
            co_ref[bi, hd] = (inter[hd:hd + 1, :] * c_ref[bi, hd]
                              + upd[128 * hd:128 * hd + 128, 0:HEAD_DIM])
        no_ref[bi] = (inter * n4 + dgate * k4[:, 0:HEAD_DIM])[0:4, :]
        mo_ref[bi] = m_t[0:4, :]


def _sample_mixer_call(z3, ck, cv, c0, n0, m0, ghead4, bif, sbuck, relb, sinks):
    nb = z3.shape[0]
    assert nb % SK2_BB == 0
    bb = SK2_BB
    out_shapes = (
        jax.ShapeDtypeStruct((nb, 4, 128), F32),
        jax.ShapeDtypeStruct((nb, 4, 128), F32),
        jax.ShapeDtypeStruct(ck.shape, F32),
        jax.ShapeDtypeStruct(cv.shape, F32),
        jax.ShapeDtypeStruct(c0.shape, F32),
        jax.ShapeDtypeStruct(n0.shape, F32),
        jax.ShapeDtypeStruct(m0.shape, F32),
    )
    blk3 = lambda s: pl.BlockSpec((bb,) + tuple(s[1:]), lambda i: (i,) + (0,) * (len(s) - 1))
    full = lambda s: pl.BlockSpec(tuple(s), lambda i: (0,) * len(s))
    in_specs = [blk3(z3.shape), blk3(ck.shape), blk3(cv.shape), blk3(c0.shape), blk3(n0.shape),
                blk3(m0.shape), full(ghead4.shape), full(bif.shape), full(sbuck.shape), _smem(), _smem()]
    out_specs = (blk3((nb, 4, 128)), blk3((nb, 4, 128)), blk3(ck.shape), blk3(cv.shape),
                 blk3(c0.shape), blk3(n0.shape), blk3(m0.shape))
    return pl.pallas_call(
        _sample_mixer_kernel,
        grid=(nb // bb,),
        in_specs=in_specs,
        out_specs=out_specs,
        out_shape=out_shapes,
        scratch_shapes=[pltpu.VMEM((8, LANES), F32)],
        compiler_params=pltpu.CompilerParams(dimension_semantics=("arbitrary",)),
        name="sample_mixers",
    )(z3, ck, cv, c0, n0, m0, ghead4, bif, sbuck, relb, sinks)


FFN_CHUNK = 256


def _sample_tail_kernel(x_ref, z_ref, yatt_ref, ym_ref, wao_ref, wmo_ref, wout_ref,
                        wg_ref, wu_ref, wd_ref, gffn_ref, gfin_ref, y_ref,
                        x1_scr, h2_scr, acc_scr):
    c = pl.program_id(0)

    @pl.when(c == 0)
    def _mix():
        ya = _dot(yatt_ref[...].astype(BF16), wao_ref[...])
        ymm = _dot(ym_ref[...].astype(BF16), wmo_ref[...])
        mixed = _sigmoid(z_ref[:, GA:GA + 1024]) * ya + _sigmoid(z_ref[:, GM:GM + 1024]) * ymm
        x1 = x_ref[...] + _dot(mixed.astype(BF16), wout_ref[...])
        x1_scr[...] = x1
        h2_scr[...] = _rms(x1, gffn_ref[...]).astype(BF16)
        acc_scr[...] = jnp.zeros_like(acc_scr)

    h2 = h2_scr[...]
    g = _dot(h2, wg_ref[...])
    u = _dot(h2, wu_ref[...])
    acc_scr[...] += _dot((g * _sigmoid(g) * u).astype(BF16), wd_ref[...])

    @pl.when(c == pl.num_programs(0) - 1)
    def _finish():
        y_ref[...] = _rms(x1_scr[...] + acc_scr[...], gfin_ref[...])


def _sample_tail_call(x, z, yatt, ym, wao, wmo, wout, wg, wu, wd, gffn, gfin):
    n, d = x.shape
    d_ff = wg.shape[1]
    assert d_ff % FFN_CHUNK == 0
    full = lambda a: pl.BlockSpec(a.shape, lambda c: (0,) * a.ndim)
    return pl.pallas_call(
        _sample_tail_kernel,
        grid=(d_ff // FFN_CHUNK,),
        in_specs=[full(x), full(z), full(yatt), full(ym), full(wao), full(wmo), full(wout),
                  pl.BlockSpec((d, FFN_CHUNK), lambda c: (0, c)),
                  pl.BlockSpec((d, FFN_CHUNK), lambda c: (0, c)),
                  pl.BlockSpec((FFN_CHUNK, d), lambda c: (c, 0)),
                  full(gffn), full(gfin)],
        out_specs=pl.BlockSpec((n, d), lambda c: (0, 0)),
        out_shape=jax.ShapeDtypeStruct((n, d), F32),
        scratch_shapes=[pltpu.VMEM((n, d), F32), pltpu.VMEM((n, d), BF16), pltpu.VMEM((n, d), F32)],
        compiler_params=pltpu.CompilerParams(dimension_semantics=("arbitrary",),
                                             vmem_limit_bytes=VMEM_LIMIT_BYTES),
        name="sample_tail",
    )(x, z, yatt, ym, wao, wmo, wout, wg, wu, wd, gffn, gfin)


def kernel(x_prompt, x_sample, cache_k_win, cache_v_win, state_mlstm_C, state_mlstm_n, state_mlstm_m,
           rel_bias, w_in, b_if, sinks, g_attn_norm, g_head, w_att_out, w_mlstm_out, w_out,
           g_ffn_norm, w_gate, w_up, w_down, g_final):
    depth = w_in.shape[0]
    assert depth == 1
    bsz, s_len, d = x_prompt.shape
    nb = x_sample.shape[0]
    assert x_sample.shape[1] == 1 and cache_k_win.shape[2] == WINDOW

    w = w_in[0]
    wcat = jnp.concatenate([w[:, :2304], w[:, 2312:4360], w[:, 2304:2312],
                            jnp.zeros((d, N_CAT - 4360), w.dtype)], axis=1).astype(BF16)
    wao = w_att_out[0].astype(BF16)
    wmo = w_mlstm_out[0].astype(BF16)
    wout = w_out[0].astype(BF16)
    wg = w_gate[0].astype(BF16)
    wu = w_up[0].astype(BF16)
    wd = w_down[0].astype(BF16)
    gattn = g_attn_norm[0].reshape(1, d)
    gffn = g_ffn_norm[0].reshape(1, d)
    gfin = g_final.reshape(1, d)
    ghead = g_head[0].reshape(1, 512)
    bif = jnp.concatenate([b_if[0].reshape(1, 2 * M_HEADS), jnp.zeros((1, LANES - 2 * M_HEADS), F32)], axis=1)
    sink_v = sinks[0]

    qi = jnp.arange(WINDOW)[:, None]
    kj = jnp.arange(WINDOW)[None, :]
    fbuck = _t5_bucket(jnp.where(kj <= qi, qi - kj, WINDOW + qi - kj)).astype(jnp.int32)
    sbuck = jnp.broadcast_to(_t5_bucket(WINDOW - 1 - kj), (8, WINDOW)).astype(jnp.int32)

    y_p, k_p, v_p, c_p, n_p, m_p = _prompt_call(
        x_prompt, wcat, wao, wmo, wout, wg, wu, wd, gattn, gffn, gfin, ghead, bif, fbuck, rel_bias, sink_v)

    xs = x_sample.reshape(nb, d)
    z_s = _sample_inproj_call(xs, gattn, wcat)
    yatt_s, ym_s, k_s, v_s, c_s, n_s, m_s = _sample_mixer_call(
        z_s.reshape(nb, N_CAT // LANES, LANES),
        cache_k_win[0].reshape(nb, WINDOW, 128), cache_v_win[0].reshape(nb, WINDOW, 128),
        state_mlstm_C[0], state_mlstm_n[0], state_mlstm_m[0].reshape(nb, M_HEADS, 1),
        ghead.reshape(M_HEADS, 128), bif, sbuck, rel_bias, sink_v)
    y_s = _sample_tail_call(xs, z_s, yatt_s.reshape(nb, 512), ym_s.reshape(nb, 512),
                            wao, wmo, wout, wg, wu, wd, gffn, gfin)

    kv_shape_p = (1, bsz, WINDOW, N_KV_HEADS, HEAD_DIM)
    kv_shape_s = (1, nb, WINDOW, N_KV_HEADS, HEAD_DIM)
    return (y_p, y_s.reshape(nb, 1, d),
            k_p.reshape(kv_shape_p), v_p.reshape(kv_shape_p),
            c_p[None], n_p[None], m_p.reshape(1, bsz, M_HEADS),
            k_s.reshape(kv_shape_s), v_s.reshape(kv_shape_s),
            c_s[None], n_s[None], m_s.reshape(1, nb, M_HEADS))
```

```python
import functools
import math

import jax
import jax.numpy as jnp
import numpy as np
from jax import lax
from jax.experimental import pallas as pl
from jax.experimental.pallas import tpu as pltpu

F32 = jnp.float32
BF16 = jnp.bfloat16

HEAD_DIM = 64
N_Q_HEADS = 8
N_KV_HEADS = 2
WINDOW = 128
N_BUCKETS = 32
MAX_DISTANCE = WINDOW
M_HEADS = 4
EPS = 1e-6
NEG = -1e30

LANES = 128
VMEM_LIMIT_BYTES = 58 * 1024 * 1024

QA, KA, VA, QM, KM, VM, OM, IF, GA, GM = 0, 512, 640, 768, 1024, 1280, 1792, 2304, 2432, 3456
N_CAT = 4480
Z_GROUPS = ((0, 768), (768, 2432), (2432, 4480))
T_BLK = 256


def _dot(a, b):
    return jnp.dot(a, b, preferred_element_type=F32)


def _dot_nt(a, b):
    return lax.dot_general(a, b, (((1,), (1,)), ((), ())), preferred_element_type=F32)


def _dot_tn(a, b):
    return lax.dot_general(a, b, (((0,), (0,)), ((), ())), preferred_element_type=F32)


def _sigmoid(x):
    return 1.0 / (1.0 + jnp.exp(-x))


def _log_sigmoid(x):
    return jnp.minimum(x, 0.0) - jnp.log1p(jnp.exp(-jnp.abs(x)))


def _rms(x, g):
    ms = jnp.mean(x * x, axis=-1, keepdims=True)
    return x * lax.rsqrt(ms + EPS) * g


def _t5_bucket(dist):
    n = jnp.maximum(dist, 0)
    max_exact = N_BUCKETS // 2
    nf = jnp.maximum(n, 1).astype(F32)
    large = max_exact + (jnp.log(nf / max_exact) / math.log(MAX_DISTANCE / max_exact)
                         * (N_BUCKETS - max_exact)).astype(jnp.int32)
    large = jnp.minimum(large, N_BUCKETS - 1)
    return jnp.where(n < max_exact, n, large)


def _bias_from_buckets(buckets, relb_ref, head):
    acc = jnp.zeros(buckets.shape, F32)
    for k in range(N_BUCKETS):
        acc = jnp.where(buckets == k, relb_ref[k, head], acc)
    return acc


def _prompt_kernel(x_ref, wcat_ref, wao_ref, wmo_ref, wout_ref, wg_ref, wu_ref, wd_ref,
                   gattn_ref, gffn_ref, gfin_ref, ghead_ref, bif_ref, fbuck_ref, relb_ref, sinks_ref,
                   y_ref, kwin_ref, vwin_ref, c_ref, n_ref, m_ref,
                   z_scr, yatt_scr, ym_scr, a_scr, bias_scr, kprev_scr, vprev_scr, cbd_scr, st_scr):
    T = T_BLK
    b = pl.program_id(0)
    j = pl.program_id(1)
    nj = pl.num_programs(1)
    d_ff = wg_ref.shape[1]

    @pl.when(jnp.logical_and(b == 0, j == 0))
    def _build_bias():
        fb = fbuck_ref[...]
        for h in range(N_Q_HEADS):
            bias_scr[h] = _bias_from_buckets(fb, relb_ref, h)

    @pl.when(j == 0)
    def _reset_state():
        kprev_scr[...] = jnp.zeros_like(kprev_scr)
        vprev_scr[...] = jnp.zeros_like(vprev_scr)
        cbd_scr[...] = jnp.zeros_like(cbd_scr)
        st_scr[...] = jnp.zeros_like(st_scr)

    x = x_ref[...]
    h = _rms(x, gattn_ref[...]).astype(BF16)
    for c0, c1 in Z_GROUPS:
        z_scr[:, c0:c1] = _dot(h, wcat_ref[:, c0:c1])

    lane = lax.broadcasted_iota(jnp.int32, (T, LANES), 1)
    lane_lo = lane < HEAD_DIM
    row = lax.broadcasted_iota(jnp.int32, (T, LANES), 0)
    qi = jnp.where(row < 128, row, row - 128)
    tri2 = lane <= qi
    valid0 = lane <= qi + jnp.where(j > 0, 2 * LANES, 0)
    rowc = lax.broadcasted_iota(jnp.int32, (T, 1), 0)
    k_all = z_scr[:, KA:KA + 128]
    v_all = z_scr[:, VA:VA + 128]
    for sb in range(2):
        r0 = 128 * sb
        if sb == 0:
            kp, vp = kprev_scr[...], vprev_scr[...]
        else:
            kp, vp = k_all[0:128], v_all[0:128]
        kcat = jnp.concatenate([kp, k_all[r0:r0 + 128]], axis=0)
        vcat = jnp.concatenate([vp, v_all[r0:r0 + 128]], axis=0)
        kroll = pltpu.roll(kcat, HEAD_DIM, 1)
        vroll = pltpu.roll(vcat, HEAD_DIM, 1)
        zero = jnp.zeros_like(kcat)
        k_mats = [jnp.where(lane_lo, kcat, zero), jnp.where(lane_lo, zero, kroll),
                  jnp.where(lane_lo, kroll, zero), jnp.where(lane_lo, zero, kcat)]
        v_mats = [jnp.where(lane_lo, vcat, zero), jnp.where(lane_lo, zero, vroll),
                  jnp.where(lane_lo, vroll, zero), jnp.where(lane_lo, zero, vcat)]
        q = (z_scr[r0:r0 + 128, QA:QA + 512] * (HEAD_DIM ** -0.5)).astype(BF16)
        lhs_a = jnp.concatenate([q[:, 0:128], q[:, 128:256]], axis=0)
        lhs_b = jnp.concatenate([q[:, 256:384], q[:, 384:512]], axis=0)
        groups = [(lhs_a, 0, 0, 2), (lhs_a, 1, 1, 3), (lhs_b, 2, 4, 6), (lhs_b, 3, 5, 7)]
        pcs = []
        for lhs, mi, ha, hb in groups:
            s = _dot_nt(lhs, k_mats[mi].astype(BF16))
            sf = jnp.where(tri2, s[:, 128:256], s[:, 0:128])
            sf = sf + jnp.concatenate([bias_scr[ha], bias_scr[hb]], axis=0)
            if sb == 0:
                sf = jnp.where(valid0, sf, NEG)
            sink = jnp.where(rowc < 128, sinks_ref[ha], sinks_ref[hb])
            mx = jnp.maximum(jnp.max(sf, axis=-1, keepdims=True), sink)
            p = jnp.exp(sf - mx)
            den = jnp.sum(p, axis=-1, keepdims=True) + jnp.exp(sink - mx)
            pn = p * (1.0 / den)
            zp = jnp.zeros_like(pn)
            pcs.append(jnp.concatenate([jnp.where(tri2, zp, pn), jnp.where(tri2, pn, zp)],
                                       axis=1).astype(BF16))
        o_a = _dot(jnp.concatenate([pcs[0], pcs[1]], axis=1),
                   jnp.concatenate([v_mats[0], v_mats[1]], axis=0).astype(BF16))
        o_b = _dot(jnp.concatenate([pcs[2], pcs[3]], axis=1),
                   jnp.concatenate([v_mats[2], v_mats[3]], axis=0).astype(BF16))
        yatt_scr[r0:r0 + 128, 0:128] = o_a[0:128].astype(BF16)
        yatt_scr[r0:r0 + 128, 128:256] = o_a[128:256].astype(BF16)
        yatt_scr[r0:r0 + 128, 256:384] = o_b[0:128].astype(BF16)
        yatt_scr[r0:r0 + 128, 384:512] = o_b[128:256].astype(BF16)
    kprev_scr[...] = k_all[128:256]
    vprev_scr[...] = v_all[128:256]

    zif = z_scr[:, IF:IF + 128] + bif_ref[...]
    gl = jnp.where(lane < M_HEADS, zif, _log_sigmoid(zif))
    gl_t = gl.T
    tr = lax.broadcasted_iota(jnp.int32, (T, T), 0)
    ts = lax.broadcasted_iota(jnp.int32, (T, T), 1)
    tril = ts <= tr
    triu = tr <= ts
    st = st_scr[...]
    row2 = lax.broadcasted_iota(jnp.int32, (2 * LANES, LANES), 0)
    lane2 = lax.broadcasted_iota(jnp.int32, (2 * LANES, LANES), 1)
    bd_mask = (row2 < LANES) == (lane2 < HEAD_DIM)
    for p in range(2):
        q_pair = z_scr[:, QM + 128 * p:QM + 128 * p + 128]
        k_pair = z_scr[:, KM + 128 * p:KM + 128 * p + 128] * (HEAD_DIM ** -0.5)
        v_pair = z_scr[:, VM + 256 * p:VM + 256 * p + 256]
        cbd = cbd_scr[p]
        n_pair = st[p:p + 1, :]
        q_bf = q_pair.astype(BF16)
        qc = _dot_nt(q_bf, cbd.astype(BF16))
        qn_prod = q_pair * n_pair
        ws, decays, m_ends = [], [], []
        for hh in range(2):
            hd = 2 * p + hh
            hmask = lane_lo if hh == 0 else jnp.logical_not(lane_lo)
            ig_c = gl[:, hd:hd + 1]
            lf_c = gl[:, M_HEADS + hd:M_HEADS + hd + 1]
            ig_r = gl_t[hd:hd + 1, :]
            lf_r = gl_t[M_HEADS + hd:M_HEADS + hd + 1, :]
            m_prev = st[2 + hd:3 + hd, 0:1]
            b_c = jnp.sum(jnp.where(tril, lf_r, 0.0), axis=1, keepdims=True)
            b_r = jnp.sum(jnp.where(triu, lf_c, 0.0), axis=0, keepdims=True)
            a_r = ig_r - b_r
            cm_c = jnp.max(jnp.where(tril, a_r, NEG), axis=1, keepdims=True)
            mt_c = b_c + jnp.maximum(m_prev, cm_c)
            g_c = b_c - mt_c
            dm = jnp.exp(jnp.where(tril, a_r + g_c, NEG))
            inter_c = jnp.exp(m_prev + g_c)
            k_h = jnp.where(hmask, k_pair, 0.0).astype(BF16)
            w = dm * _dot_nt(q_bf, k_h)
            v_h = v_pair[:, 128 * hh:128 * hh + 128]
            num = inter_c * qc[:, 128 * hh:128 * hh + 128] + _dot(w.astype(BF16), v_h.astype(BF16))
            qn = jnp.sum(jnp.where(hmask, qn_prod, 0.0), axis=1, keepdims=True)
            den = inter_c * qn + jnp.sum(w, axis=1, keepdims=True)
            hv = num / jnp.maximum(jnp.abs(den), jnp.exp(-mt_c))
            hn = hv * lax.rsqrt(jnp.mean(hv * hv, axis=-1, keepdims=True) + EPS)
            hn = hn * ghead_ref[:, 128 * hd:128 * hd + 128]
            om = z_scr[:, OM + 128 * hd:OM + 128 * hd + 128]
            ym_scr[:, 128 * hd:128 * hd + 128] = (hn * _sigmoid(om)).astype(BF16)
            b_end = b_c[T - 1:T, :]
            m_end = mt_c[T - 1:T, :]
            ws.append(jnp.exp((ig_c - b_c) + b_end - m_end))
            decays.append(jnp.exp(m_prev + b_end - m_end))
            m_ends.append(m_end)
        kw = k_pair * jnp.where(lane_lo, ws[0], ws[1])
        upd = _dot(v_pair.T.astype(BF16), kw.astype(BF16))
        dec_rows = jnp.where(row2[:, 0:1] < LANES, decays[0], decays[1])
        cbd_scr[p] = dec_rows * cbd + jnp.where(bd_mask, upd, 0.0)
        dec_lanes = jnp.where(lane_lo[0:1, :], decays[0], decays[1])
        st_scr[p:p + 1, :] = dec_lanes * n_pair + jnp.sum(kw, axis=0, keepdims=True)
        for hh in range(2):
            hd = 2 * p + hh
            st_scr[2 + hd:3 + hd, :] = jnp.broadcast_to(m_ends[hh], (1, LANES))

    ya = _dot(yatt_scr[...], wao_ref[...])
    ymm = _dot(ym_scr[...], wmo_ref[...])
    mixed = _sigmoid(z_scr[:, GA:GA + 1024]) * ya + _sigmoid(z_scr[:, GM:GM + 1024]) * ymm
    x1 = x + _dot(mixed.astype(BF16), wout_ref[...])
    h2 = _rms(x1, gffn_ref[...]).astype(BF16)
    for c0 in range(0, d_ff, 256):
        g = _dot(h2, wg_ref[:, c0:c0 + 256])
        u = _dot(h2, wu_ref[:, c0:c0 + 256])
        a_scr[:, c0:c0 + 256] = (g * _sigmoid(g) * u).astype(BF16)
    x2 = x1 + _dot(a_scr[...], wd_ref[...])
    y_ref[...] = _rms(x2, gfin_ref[...])

    @pl.when(j == nj - 1)
    def _write_state():
        kwin_ref[...] = kprev_scr[...]
        vwin_ref[...] = vprev_scr[...]
        stn = st_scr[...]
        for hd in range(M_HEADS):
            p, hh = hd // 2, hd % 2
            c_ref[hd] = cbd_scr[p, 128 * hh:128 * hh + 128, 64 * hh:64 * hh + 64]
            n_ref[hd:hd + 1, :] = stn[p:p + 1, 64 * hh:64 * hh + 64]
            m_ref[0:1, hd:hd + 1] = stn[2 + hd:3 + hd, 0:1]


def _resident(shape):
    zeros = (0,) * len(shape)
    return pl.BlockSpec(shape, lambda b, j: zeros, pipeline_mode=pl.Buffered(1))


def _smem():
    return pl.BlockSpec(memory_space=pltpu.SMEM)


def _prompt_call(x, wcat, wao, wmo, wout, wg, wu, wd, gattn, gffn, gfin, ghead, bif, fbuck, relb, sinks):
    bsz, s_len, d = x.shape
    assert s_len % T_BLK == 0 and d == 1024
    d_ff = wg.shape[1]
    nj = s_len // T_BLK
    out_shapes = (
        jax.ShapeDtypeStruct((bsz, s_len, d), F32),
        jax.ShapeDtypeStruct((bsz, WINDOW, 128), F32),
        jax.ShapeDtypeStruct((bsz, WINDOW, 128), F32),
        jax.ShapeDtypeStruct((bsz, M_HEADS, 128, 64), F32),
        jax.ShapeDtypeStruct((bsz, M_HEADS, 64), F32),
        jax.ShapeDtypeStruct((bsz, 1, M_HEADS), F32),
    )
    in_specs = [
        pl.BlockSpec((None, T_BLK, d), lambda b, j: (b, j, 0)),
        _resident(wcat.shape), _resident(wao.shape), _resident(wmo.shape), _resident(wout.shape),
        _resident(wg.shape), _resident(wu.shape), _resident(wd.shape),
        _resident(gattn.shape), _resident(gffn.shape), _resident(gfin.shape), _resident(ghead.shape),
        _resident(bif.shape), _resident(fbuck.shape), _smem(), _smem(),
    ]
    out_specs = (
        pl.BlockSpec((None, T_BLK, d), lambda b, j: (b, j, 0)),
        pl.BlockSpec((None, WINDOW, 128), lambda b, j: (b, 0, 0)),
        pl.BlockSpec((None, WINDOW, 128), lambda b, j: (b, 0, 0)),
        pl.BlockSpec((None, M_HEADS, 128, 64), lambda b, j: (b, 0, 0, 0)),
        pl.BlockSpec((None, M_HEADS, 64), lambda b, j: (b, 0, 0)),
        pl.BlockSpec((None, 1, M_HEADS), lambda b, j: (b, 0, 0)),
    )
    scratch = [
        pltpu.VMEM((T_BLK, N_CAT), F32),
        pltpu.VMEM((T_BLK, 512), BF16),
        pltpu.VMEM((T_BLK, 512), BF16),
        pltpu.VMEM((T_BLK, d_ff), BF16),
        pltpu.VMEM((N_Q_HEADS, 128, 128), F32),
        pltpu.VMEM((128, 128), F32),
        pltpu.VMEM((128, 128), F32),
        pltpu.VMEM((2, 256, 128), F32),
        pltpu.VMEM((8, 128), F32),
    ]
    return pl.pallas_call(
        _prompt_kernel,
        grid=(bsz, nj),
        in_specs=in_specs,
        out_specs=out_specs,
        out_shape=out_shapes,
        scratch_shapes=scratch,
        compiler_params=pltpu.CompilerParams(
            dimension_semantics=("arbitrary", "arbitrary"),
            vmem_limit_bytes=VMEM_LIMIT_BYTES),
        name="prompt_layer",
    )(x, wcat, wao, wmo, wout, wg, wu, wd, gattn, gffn, gfin, ghead, bif, fbuck, relb, sinks)


SK1_CHUNK = 640


def _sample_inproj_kernel(x_ref, g_ref, w_ref, z_ref):
    h = _rms(x_ref[...], g_ref[...]).astype(BF16)
    z_ref[...] = _dot(h, w_ref[...])


def _sample_inproj_call(x, gattn, wcat):
    n, d = x.shape
    return pl.pallas_call(
        _sample_inproj_kernel,
        grid=(N_CAT // SK1_CHUNK,),
        in_specs=[pl.BlockSpec((n, d), lambda i: (0, 0)),
                  pl.BlockSpec((1, d), lambda i: (0, 0)),
                  pl.BlockSpec((d, SK1_CHUNK), lambda i: (0, i))],
        out_specs=pl.BlockSpec((n, SK1_CHUNK), lambda i: (0, i)),
        out_shape=jax.ShapeDtypeStruct((n, N_CAT), F32),
        compiler_params=pltpu.CompilerParams(dimension_semantics=("arbitrary",)),
        name="sample_inproj",
    )(x, gattn, wcat)


SK2_BB = 8


def _rows8(x4):
    r = lax.broadcasted_iota(jnp.int32, (8, x4.shape[1]), 0)
    out = jnp.zeros((8, x4.shape[1]), x4.dtype)
    for hd in range(x4.shape[0]):
        out = jnp.where(r == hd, x4[hd:hd + 1, :], out)
    return out


def _sample_mixer_kernel(z_ref, ck_ref, cv_ref, c_ref, n_ref, m_ref, ghead_ref, bif_ref, sbuck_ref,
                         relb_ref, sinks_ref,
                         yatt_ref, ym_ref, ko_ref, vo_ref, co_ref, no_ref, mo_ref,
                         sbias_scr):
    i = pl.program_id(0)

    @pl.when(i == 0)
    def _build_bias():
        sb = sbuck_ref[...]
        r8 = lax.broadcasted_iota(jnp.int32, (8, LANES), 0)
        acc = jnp.zeros((8, LANES), F32)
        for rrow, hd in enumerate((0, 2, 4, 6, 1, 3, 5, 7)):
            acc = jnp.where(r8 == rrow, _bias_from_buckets(sb, relb_ref, hd), acc)
        sbias_scr[...] = acc

    r8 = lax.broadcasted_iota(jnp.int32, (8, LANES), 0)
    l8 = lax.broadcasted_iota(jnp.int32, (8, LANES), 1)
    lo8 = l8 < HEAD_DIM
    r8c = lax.broadcasted_iota(jnp.int32, (8, 1), 0)
    sink = jnp.zeros((8, 1), F32)
    for rrow, hd in enumerate((0, 2, 4, 6, 1, 3, 5, 7)):
        sink = jnp.where(r8c == rrow, sinks_ref[hd], sink)
    row128 = lax.broadcasted_iota(jnp.int32, (WINDOW, LANES), 0)
    last_row = row128 == WINDOW - 1
    sbias = sbias_scr[...]
    z8 = jnp.zeros((8, LANES), F32)
    l512 = lax.broadcasted_iota(jnp.int32, (8, 4 * LANES), 1)
    r512 = lax.broadcasted_iota(jnp.int32, (8, 4 * LANES), 0)
    head_blk = (l512 // LANES) == r512
    r4c = r8c
    ghead4 = ghead_ref[...]
    bif = bif_ref[...]

    for bi in range(SK2_BB):
        zb = z_ref[bi]
        knew = zb[KA // LANES:KA // LANES + 1, :]
        vnew = zb[VA // LANES:VA // LANES + 1, :]
        kt = jnp.where(last_row, knew, pltpu.roll(ck_ref[bi], WINDOW - 1, 0))
        vt = jnp.where(last_row, vnew, pltpu.roll(cv_ref[bi], WINDOW - 1, 0))
        ko_ref[bi] = kt
        vo_ref[bi] = vt
        xq = zb[0:8, :] * (HEAD_DIM ** -0.5)
        xs = pltpu.roll(xq, 4, 0)
        xr = pltpu.roll(xq, HEAD_DIM, 1)
        xsr = pltpu.roll(xs, HEAD_DIM, 1)
        qm = jnp.where(r8 < 2, jnp.where(lo8, xq, z8),
                       jnp.where(r8 < 4, jnp.where(lo8, z8, xr),
                                 jnp.where(r8 < 6, jnp.where(lo8, xsr, z8), jnp.where(lo8, z8, xs))))
        s = _dot_nt(qm.astype(BF16), kt.astype(BF16)) + sbias
        mx = jnp.maximum(jnp.max(s, axis=-1, keepdims=True), sink)
        pe = jnp.exp(s - mx)
        den = jnp.sum(pe, axis=-1, keepdims=True) + jnp.exp(sink - mx)
        pn = pe * (1.0 / den)
        oa = _dot(pn.astype(BF16), vt.astype(BF16))
        oas = pltpu.roll(oa, 4, 0)
        oar = pltpu.roll(oa, HEAD_DIM, 1)
        oasr = pltpu.roll(oas, HEAD_DIM, 1)
        ypair = jnp.where(r8 < 2, jnp.where(lo8, oa, oasr), jnp.where(lo8, oar, oas))
        yatt_ref[bi] = ypair[0:4, :]

        gm = jnp.broadcast_to(zb[IF // LANES:IF // LANES + 1, :] + bif, (8, LANES))
        ig = jnp.sum(jnp.where(l8 == r8, gm, 0.0), axis=1, keepdims=True)
        fp = jnp.sum(jnp.where(l8 == r8 + M_HEADS, gm, 0.0), axis=1, keepdims=True)
        lf = _log_sigmoid(fp)
        m0 = _rows8(m_ref[bi])
        a = ig - lf
        m_t = lf + jnp.maximum(m0, a)
        dgate = jnp.exp(a + lf - m_t)
        inter = jnp.exp(m0 + lf - m_t)
        qp = zb[QM // LANES:QM // LANES + 8, :]
        qpr = pltpu.roll(qp, HEAD_DIM, 1)
        q4 = jnp.where(lo8, jnp.where(r8 == 0, qp[0:1], jnp.where(r8 == 1, qpr[0:1],
                       jnp.where(r8 == 2, qp[1:2], jnp.where(r8 == 3, qpr[1:2], 0.0)))), 0.0)
        k4 = jnp.where(lo8, jnp.where(r8 == 0, qp[2:3], jnp.where(r8 == 1, qpr[2:3],
                       jnp.where(r8 == 2, qp[3:4], jnp.where(r8 == 3, qpr[3:4], 0.0)))), 0.0)
        k4 = k4 * (HEAD_DIM ** -0.5)
        n4 = _rows8(n_ref[bi])
        v8 = zb[VM // LANES:VM // LANES + 8, :]
        v4 = jnp.where(r8 < M_HEADS, v8, 0.0)
        o8 = zb[OM // LANES:OM // LANES + 8, :]
        call = jnp.concatenate([c_ref[bi, hd] for hd in range(M_HEADS)], axis=0)
        rq = _dot_nt(q4[:, 0:HEAD_DIM].astype(BF16), call.astype(BF16))
        cq = jnp.where(r8 == 0, rq[:, 0:128], jnp.where(r8 == 1, rq[:, 128:256],
                       jnp.where(r8 == 2, rq[:, 256:384], jnp.where(r8 == 3, rq[:, 384:512], 0.0))))
        qk = jnp.sum(q4 * k4, axis=1, keepdims=True)
        nq = jnp.sum(n4 * q4[:, 0:HEAD_DIM], axis=1, keepdims=True)
        w = dgate * qk
        num = inter * cq + w * v4
        den_m = inter * nq + w
        hv = num / jnp.maximum(jnp.abs(den_m), jnp.exp(-m_t))
        hn = hv * lax.rsqrt(jnp.mean(hv * hv, axis=-1, keepdims=True) + EPS)
        ym8 = hn[0:4, :] * ghead4 * _sigmoid(o8[0:4, :])
        ym_ref[bi] = ym8
        a512 = jnp.where(head_blk, jnp.concatenate([v4, v4, v4, v4], axis=1), 0.0)
        b128 = dgate * k4
        upd = _dot_tn(a512, b128)
        for hd in range(M_HEADS):
            co_ref[bi, hd] = (inter[hd:hd + 1, :] * c_ref[bi, hd]
                              + upd[128 * hd:128 * hd + 128, 0:HEAD_DIM])
        no_ref[bi] = (inter * n4 + dgate * k4[:, 0:HEAD_DIM])[0:4, :]
        mo_ref[bi] = m_t[0:4, :]


def _sample_mixer_call(z3, ck, cv, c0, n0, m0, ghead4, bif, sbuck, relb, sinks):
    nb = z3.shape[0]
    assert nb % SK2_BB == 0
    bb = SK2_BB
    out_shapes = (
        jax.ShapeDtypeStruct((nb, 4, 128), F32),
        jax.ShapeDtypeStruct((nb, 4, 128), F32),
        jax.ShapeDtypeStruct(ck.shape, F32),
        jax.ShapeDtypeStruct(cv.shape, F32),
        jax.ShapeDtypeStruct(c0.shape, F32),
        jax.ShapeDtypeStruct(n0.shape, F32),
        jax.ShapeDtypeStruct(m0.shape, F32),
    )
    blk3 = lambda s: pl.BlockSpec((bb,) + tuple(s[1:]), lambda i: (i,) + (0,) * (len(s) - 1))
    full = lambda s: pl.BlockSpec(tuple(s), lambda i: (0,) * len(s))
    in_specs = [blk3(z3.shape), blk3(ck.shape), blk3(cv.shape), blk3(c0.shape), blk3(n0.shape),
                blk3(m0.shape), full(ghead4.shape), full(bif.shape), full(sbuck.shape), _smem(), _smem()]
    out_specs = (blk3((nb, 4, 128)), blk3((nb, 4, 128)), blk3(ck.shape), blk3(cv.shape),
                 blk3(c0.shape), blk3(n0.shape), blk3(m0.shape))
    return pl.pallas_call(
        _sample_mixer_kernel,
        grid=(nb // bb,),
        in_specs=in_specs,
        out_specs=out_specs,
        out_shape=out_shapes,
        scratch_shapes=[pltpu.VMEM((8, LANES), F32)],
        compiler_params=pltpu.CompilerParams(dimension_semantics=("arbitrary",)),
        name="sample_mixers",
    )(z3, ck, cv, c0, n0, m0, ghead4, bif, sbuck, relb, sinks)


FFN_CHUNK = 256


def _sample_tail_kernel(x_ref, z_ref, yatt_ref, ym_ref, wao_ref, wmo_ref, wout_ref,
                        wg_ref, wu_ref, wd_ref, gffn_ref, gfin_ref, y_ref,
                        x1_scr, h2_scr, acc_scr):
    c = pl.program_id(0)

    @pl.when(c == 0)
    def _mix():
        ya = _dot(yatt_ref[...].astype(BF16), wao_ref[...])
        ymm = _dot(ym_ref[...].astype(BF16), wmo_ref[...])
        mixed = _sigmoid(z_ref[:, GA:GA + 1024]) * ya + _sigmoid(z_ref[:, GM:GM + 1024]) * ymm
        x1 = x_ref[...] + _dot(mixed.astype(BF16), wout_ref[...])
        x1_scr[...] = x1
        h2_scr[...] = _rms(x1, gffn_ref[...]).astype(BF16)
        acc_scr[...] = jnp.zeros_like(acc_scr)

    h2 = h2_scr[...]
    g = _dot(h2, wg_ref[...])
    u = _dot(h2, wu_ref[...])
    acc_scr[...] += _dot((g * _sigmoid(g) * u).astype(BF16), wd_ref[...])

    @pl.when(c == pl.num_programs(0) - 1)
    def _finish():
        y_ref[...] = _rms(x1_scr[...] + acc_scr[...], gfin_ref[...])


def _sample_tail_call(x, z, yatt, ym, wao, wmo, wout, wg, wu, wd, gffn, gfin):
    n, d = x.shape
    d_ff = wg.shape[1]
    assert d_ff % FFN_CHUNK == 0
    full = lambda a: pl.BlockSpec(a.shape, lambda c: (0,) * a.ndim)
    return pl.pallas_call(
        _sample_tail_kernel,
        grid=(d_ff // FFN_CHUNK,),
        in_specs=[full(x), full(z), full(yatt), full(ym), full(wao), full(wmo), full(wout),
                  pl.BlockSpec((d, FFN_CHUNK), lambda c: (0, c)),
                  pl.BlockSpec((d, FFN_CHUNK), lambda c: (0, c)),
                  pl.BlockSpec((FFN_CHUNK, d), lambda c: (c, 0)),
                  full(gffn), full(gfin)],
        out_specs=pl.BlockSpec((n, d), lambda c: (0, 0)),
        out_shape=jax.ShapeDtypeStruct((n, d), F32),
        scratch_shapes=[pltpu.VMEM((n, d), F32), pltpu.VMEM((n, d), BF16), pltpu.VMEM((n, d), F32)],
        compiler_params=pltpu.CompilerParams(dimension_semantics=("arbitrary",),
                                             vmem_limit_bytes=VMEM_LIMIT_BYTES),
        name="sample_tail",
    )(x, z, yatt, ym, wao, wmo, wout, wg, wu, wd, gffn, gfin)


def kernel(x_prompt, x_sample, cache_k_win, cache_v_win, state_mlstm_C, state_mlstm_n, state_mlstm_m,
           rel_bias, w_in, b_if, sinks, g_attn_norm, g_head, w_att_out, w_mlstm_out, w_out,
           g_ffn_norm, w_gate, w_up, w_down, g_final):
    depth = w_in.shape[0]
    assert depth == 1
    bsz, s_len, d = x_prompt.shape
    nb = x_sample.shape[0]
    assert x_sample.shape[1] == 1 and cache_k_win.shape[2] == WINDOW

    w = w_in[0]
    wcat = jnp.concatenate([w[:, :2312], jnp.zeros((d, N_CAT - 4360), w.dtype), w[:, 2312:4360]],
                           axis=1).astype(BF16)
    wao = w_att_out[0].astype(BF16)
    wmo = w_mlstm_out[0].astype(BF16)
    wout = w_out[0].astype(BF16)
    wg = w_gate[0].astype(BF16)
    wu = w_up[0].astype(BF16)
    wd = w_down[0].astype(BF16)
    gattn = g_attn_norm[0].reshape(1, d)
    gffn = g_ffn_norm[0].reshape(1, d)
    gfin = g_final.reshape(1, d)
    ghead = g_head[0].reshape(1, 512)
    bif = jnp.concatenate([b_if[0].reshape(1, 2 * M_HEADS), jnp.zeros((1, LANES - 2 * M_HEADS), F32)], axis=1)
    sink_v = sinks[0]

    qi = jnp.arange(WINDOW)[:, None]
    kj = jnp.arange(WINDOW)[None, :]
    fbuck = _t5_bucket(jnp.where(kj <= qi, qi - kj, WINDOW + qi - kj)).astype(jnp.int32)
    sbuck = jnp.broadcast_to(_t5_bucket(WINDOW - 1 - kj), (8, WINDOW)).astype(jnp.int32)

    y_p, k_p, v_p, c_p, n_p, m_p = _prompt_call(
        x_prompt, wcat, wao, wmo, wout, wg, wu, wd, gattn, gffn, gfin, ghead, bif, fbuck, rel_bias, sink_v)

    xs = x_sample.reshape(nb, d)
    z_s = _sample_inproj_call(xs, gattn, wcat)
    yatt_s, ym_s, k_s, v_s, c_s, n_s, m_s = _sample_mixer_call(
        z_s.reshape(nb, N_CAT // LANES, LANES),
        cache_k_win[0].reshape(nb, WINDOW, 128), cache_v_win[0].reshape(nb, WINDOW, 128),
        state_mlstm_C[0], state_mlstm_n[0], state_mlstm_m[0].reshape(nb, M_HEADS, 1),
        ghead.reshape(M_HEADS, 128), bif, sbuck, rel_bias, sink_v)
    y_s = _sample_tail_call(xs, z_s, yatt_s.reshape(nb, 512), ym_s.reshape(nb, 512),
                            wao, wmo, wout, wg, wu, wd, gffn, gfin)

    kv_shape_p = (1, bsz, WINDOW, N_KV_HEADS, HEAD_DIM)
    kv_shape_s = (1, nb, WINDOW, N_KV_HEADS, HEAD_DIM)
    return (y_p, y_s.reshape(nb, 1, d),
            k_p.reshape(kv_shape_p), v_p.reshape(kv_shape_p),
            c_p[None], n_p[None], m_p.reshape(1, bsz, M_HEADS),
            k_s.reshape(kv_shape_s), v_s.reshape(kv_shape_s),
            c_s[None], n_s[None], m_s.reshape(1, nb, M_HEADS))
```

```python
import functools
import math

import jax
import jax.numpy as jnp
import numpy as np
from jax import lax
from jax.experimental import pallas as pl
from jax.experimental.pallas import tpu as pltpu

F32 = jnp.float32
BF16 = jnp.bfloat16

HEAD_DIM = 64
N_Q_HEADS = 8
N_KV_HEADS = 2
WINDOW = 128
N_BUCKETS = 32
MAX_DISTANCE = WINDOW
M_HEADS = 4
EPS = 1e-6
NEG = -1e30

LANES = 128
VMEM_LIMIT_BYTES = 58 * 1024 * 1024

QA, KA, VA, QM, KM, VM, OM, IF, GA, GM = 0, 512, 640, 768, 1024, 1280, 1792, 2304, 2432, 3456
N_CAT = 4480
Z_GROUPS = ((0, 768), (768, 2432), (2432, 4480))
T_BLK = 256
N_SUB = 2


def _dot(a, b):
    return jnp.dot(a, b, preferred_element_type=F32)


def _dot_nt(a, b):
    return lax.dot_general(a, b, (((1,), (1,)), ((), ())), preferred_element_type=F32)


def _dot_tn(a, b):
    return lax.dot_general(a, b, (((0,), (0,)), ((), ())), preferred_element_type=F32)


def _sigmoid(x):
    return 1.0 / (1.0 + jnp.exp(-x))


def _log_sigmoid(x):
    return jnp.minimum(x, 0.0) - jnp.log1p(jnp.exp(-jnp.abs(x)))


def _rms(x, g):
    ms = jnp.mean(x * x, axis=-1, keepdims=True)
    return x * lax.rsqrt(ms + EPS) * g


def _t5_bucket(dist):
    n = jnp.maximum(dist, 0)
    max_exact = N_BUCKETS // 2
    nf = jnp.maximum(n, 1).astype(F32)
    large = max_exact + (jnp.log(nf / max_exact) / math.log(MAX_DISTANCE / max_exact)
                         * (N_BUCKETS - max_exact)).astype(jnp.int32)
    large = jnp.minimum(large, N_BUCKETS - 1)
    return jnp.where(n < max_exact, n, large)


def _bias_from_buckets(buckets, relb_ref, head):
    acc = jnp.zeros(buckets.shape, F32)
    for k in range(N_BUCKETS):
        acc = jnp.where(buckets == k, relb_ref[k, head], acc)
    return acc


def _prompt_kernel(x_ref, wcat_ref, wao_ref, wmo_ref, wout_ref, wg_ref, wu_ref, wd_ref,
                   gattn_ref, gffn_ref, gfin_ref, ghead_ref, bif_ref, fbuck_ref, relb_ref, sinks_ref,
                   y_ref, kwin_ref, vwin_ref, c_ref, n_ref, m_ref,
                   z_scr, yatt_scr, ym_scr, a_scr, bias_scr, kprev_scr, vprev_scr, cbd_scr, st_scr):
    T = T_BLK
    b = pl.program_id(0)
    j = pl.program_id(1)
    nj = pl.num_programs(1)

    @pl.when(jnp.logical_and(b == 0, j == 0))
    def _build_bias():
        fb = fbuck_ref[...]
        for h in range(N_Q_HEADS):
            bias_scr[h] = _bias_from_buckets(fb, relb_ref, h)

    @pl.when(j == 0)
    def _reset_state():
        kprev_scr[...] = jnp.zeros_like(kprev_scr)
        vprev_scr[...] = jnp.zeros_like(vprev_scr)
        cbd_scr[...] = jnp.zeros_like(cbd_scr)
        st_scr[...] = jnp.zeros_like(st_scr)

    for s in range(N_SUB):
        _in_projection(x_ref.at[s * T:(s + 1) * T, :], gattn_ref, wcat_ref, z_scr.at[s])

    carry = _load_carry(kprev_scr, vprev_scr, cbd_scr, st_scr)
    for s in range(N_SUB):
        carry = _mixers(z_scr.at[s], yatt_scr.at[s], ym_scr.at[s], carry, ghead_ref, bif_ref, bias_scr,
                        sinks_ref, first_of_sequence=(j == 0) if s == 0 else None)
    _store_carry(carry, kprev_scr, vprev_scr, cbd_scr, st_scr)

    for s in range(N_SUB):
        _tail(x_ref.at[s * T:(s + 1) * T, :], z_scr.at[s], yatt_scr.at[s], ym_scr.at[s], a_scr.at[s],
              wao_ref, wmo_ref, wout_ref, wg_ref, wu_ref, wd_ref, gffn_ref, gfin_ref,
              y_ref.at[s * T:(s + 1) * T, :])

    @pl.when(j == nj - 1)
    def _write_state():
        kwin_ref[...] = kprev_scr[...]
        vwin_ref[...] = vprev_scr[...]
        stn = st_scr[...]
        for hd in range(M_HEADS):
            p, hh = hd // 2, hd % 2
            c_ref[hd] = cbd_scr[p, 128 * hh:128 * hh + 128, 64 * hh:64 * hh + 64]
            n_ref[hd:hd + 1, :] = stn[p:p + 1, 64 * hh:64 * hh + 64]
            m_ref[0:1, hd:hd + 1] = stn[2 + hd:3 + hd, 0:1]


def _in_projection(x_ref, gattn_ref, wcat_ref, z_ref):
    h = _rms(x_ref[...], gattn_ref[...]).astype(BF16)
    for c0, c1 in Z_GROUPS:
        z_ref[:, c0:c1] = _dot(h, wcat_ref[:, c0:c1])


def _load_carry(kprev_scr, vprev_scr, cbd_scr, st_scr):
    st = st_scr[...]
    return dict(kp=kprev_scr[...], vp=vprev_scr[...], cbd=[cbd_scr[0], cbd_scr[1]],
                n=[st[0:1, :], st[1:2, :]], m=[st[2 + hd:3 + hd, 0:1] for hd in range(M_HEADS)])


def _store_carry(carry, kprev_scr, vprev_scr, cbd_scr, st_scr):
    kprev_scr[...] = carry["kp"]
    vprev_scr[...] = carry["vp"]
    for p in range(2):
        cbd_scr[p] = carry["cbd"][p]
        st_scr[p:p + 1, :] = carry["n"][p]
    for hd in range(M_HEADS):
        st_scr[2 + hd:3 + hd, :] = jnp.broadcast_to(carry["m"][hd], (1, LANES))


def _tail(x_ref, z_ref, yatt_ref, ym_ref, a_ref, wao_ref, wmo_ref, wout_ref, wg_ref, wu_ref, wd_ref,
          gffn_ref, gfin_ref, y_ref):
    d_ff = wg_ref.shape[1]
    ya = _dot(yatt_ref[...], wao_ref[...])
    ymm = _dot(ym_ref[...], wmo_ref[...])
    mixed = _sigmoid(z_ref[:, GA:GA + 1024]) * ya + _sigmoid(z_ref[:, GM:GM + 1024]) * ymm
    x1 = x_ref[...] + _dot(mixed.astype(BF16), wout_ref[...])
    h2 = _rms(x1, gffn_ref[...]).astype(BF16)
    for c0 in range(0, d_ff, 256):
        g = _dot(h2, wg_ref[:, c0:c0 + 256])
        u = _dot(h2, wu_ref[:, c0:c0 + 256])
        a_ref[:, c0:c0 + 256] = (g * _sigmoid(g) * u).astype(BF16)
    x2 = x1 + _dot(a_ref[...], wd_ref[...])
    y_ref[...] = _rms(x2, gfin_ref[...])


def _mixers(z_ref, yatt_ref, ym_ref, carry, ghead_ref, bif_ref, bias_scr, sinks_ref, first_of_sequence):
    T = T_BLK
    lane = lax.broadcasted_iota(jnp.int32, (T, LANES), 1)
    lane_lo = lane < HEAD_DIM
    row = lax.broadcasted_iota(jnp.int32, (T, LANES), 0)
    qi = jnp.where(row < 128, row, row - 128)
    tri2 = lane <= qi
    if first_of_sequence is not None:
        valid0 = lane <= qi + jnp.where(first_of_sequence, 0, 2 * LANES)
    rowc = lax.broadcasted_iota(jnp.int32, (T, 1), 0)
    k_all = z_ref[:, KA:KA + 128]
    v_all = z_ref[:, VA:VA + 128]
    for sb in range(2):
        r0 = 128 * sb
        if sb == 0:
            kp, vp = carry["kp"], carry["vp"]
        else:
            kp, vp = k_all[0:128], v_all[0:128]
        kcat = jnp.concatenate([kp, k_all[r0:r0 + 128]], axis=0)
        vcat = jnp.concatenate([vp, v_all[r0:r0 + 128]], axis=0)
        kroll = pltpu.roll(kcat, HEAD_DIM, 1)
        vroll = pltpu.roll(vcat, HEAD_DIM, 1)
        zero = jnp.zeros_like(kcat)
        k_mats = [jnp.where(lane_lo, kcat, zero), jnp.where(lane_lo, zero, kroll),
                  jnp.where(lane_lo, kroll, zero), jnp.where(lane_lo, zero, kcat)]
        v_mats = [jnp.where(lane_lo, vcat, zero), jnp.where(lane_lo, zero, vroll),
                  jnp.where(lane_lo, vroll, zero), jnp.where(lane_lo, zero, vcat)]
        q = (z_ref[r0:r0 + 128, QA:QA + 512] * (HEAD_DIM ** -0.5)).astype(BF16)
        lhs_a = jnp.concatenate([q[:, 0:128], q[:, 128:256]], axis=0)
        lhs_b = jnp.concatenate([q[:, 256:384], q[:, 384:512]], axis=0)
        groups = [(lhs_a, 0, 0, 2), (lhs_a, 1, 1, 3), (lhs_b, 2, 4, 6), (lhs_b, 3, 5, 7)]
        pcs = []
        for lhs, mi, ha, hb in groups:
            s = _dot_nt(lhs, k_mats[mi].astype(BF16))
            sf = jnp.where(tri2, s[:, 128:256], s[:, 0:128])
            sf = sf + jnp.concatenate([bias_scr[ha], bias_scr[hb]], axis=0)
            if sb == 0 and first_of_sequence is not None:
                sf = jnp.where(valid0, sf, NEG)
            sink = jnp.where(rowc < 128, sinks_ref[ha], sinks_ref[hb])
            mx = jnp.maximum(jnp.max(sf, axis=-1, keepdims=True), sink)
            p = jnp.exp(sf - mx)
            den = jnp.sum(p, axis=-1, keepdims=True) + jnp.exp(sink - mx)
            pn = p * (1.0 / den)
            zp = jnp.zeros_like(pn)
            pcs.append(jnp.concatenate([jnp.where(tri2, zp, pn), jnp.where(tri2, pn, zp)],
                                       axis=1).astype(BF16))
        o_a = _dot(jnp.concatenate([pcs[0], pcs[1]], axis=1),
                   jnp.concatenate([v_mats[0], v_mats[1]], axis=0).astype(BF16))
        o_b = _dot(jnp.concatenate([pcs[2], pcs[3]], axis=1),
                   jnp.concatenate([v_mats[2], v_mats[3]], axis=0).astype(BF16))
        yatt_ref[r0:r0 + 128, 0:128] = o_a[0:128].astype(BF16)
        yatt_ref[r0:r0 + 128, 128:256] = o_a[128:256].astype(BF16)
        yatt_ref[r0:r0 + 128, 256:384] = o_b[0:128].astype(BF16)
        yatt_ref[r0:r0 + 128, 384:512] = o_b[128:256].astype(BF16)
    new_carry = dict(kp=k_all[128:256], vp=v_all[128:256], cbd=[None, None], n=[None, None],
                     m=[None] * M_HEADS)

    zif = z_ref[:, IF:IF + 128] + bif_ref[...]
    gl = jnp.where(lane < M_HEADS, zif, _log_sigmoid(zif))
    gl_t = gl.T
    tr = lax.broadcasted_iota(jnp.int32, (T, T), 0)
    ts = lax.broadcasted_iota(jnp.int32, (T, T), 1)
    tril = ts <= tr
    triu = tr <= ts
    row2 = lax.broadcasted_iota(jnp.int32, (2 * LANES, LANES), 0)
    lane2 = lax.broadcasted_iota(jnp.int32, (2 * LANES, LANES), 1)
    bd_mask = (row2 < LANES) == (lane2 < HEAD_DIM)
    for p in range(2):
        q_pair = z_ref[:, QM + 128 * p:QM + 128 * p + 128]
        k_pair = z_ref[:, KM + 128 * p:KM + 128 * p + 128] * (HEAD_DIM ** -0.5)
        v_pair = z_ref[:, VM + 256 * p:VM + 256 * p + 256]
        cbd = carry["cbd"][p]
        n_pair = carry["n"][p]
        q_bf = q_pair.astype(BF16)
        qc = _dot_nt(q_bf, cbd.astype(BF16))
        qn_prod = q_pair * n_pair
        ws, decays, m_ends = [], [], []
        for hh in range(2):
            hd = 2 * p + hh
            hmask = lane_lo if hh == 0 else jnp.logical_not(lane_lo)
            ig_c = gl[:, hd:hd + 1]
            lf_c = gl[:, M_HEADS + hd:M_HEADS + hd + 1]
            ig_r = gl_t[hd:hd + 1, :]
            lf_r = gl_t[M_HEADS + hd:M_HEADS + hd + 1, :]
            m_prev = carry["m"][hd]
            b_c = jnp.sum(jnp.where(tril, lf_r, 0.0), axis=1, keepdims=True)
            b_r = jnp.sum(jnp.where(triu, lf_c, 0.0), axis=0, keepdims=True)
            a_r = ig_r - b_r
            cm_c = jnp.max(jnp.where(tril, a_r, NEG), axis=1, keepdims=True)
            mt_c = b_c + jnp.maximum(m_prev, cm_c)
            g_c = b_c - mt_c
            dm = jnp.exp(jnp.where(tril, a_r + g_c, NEG))
            inter_c = jnp.exp(m_prev + g_c)
            k_h = jnp.where(hmask, k_pair, 0.0).astype(BF16)
            w = dm * _dot_nt(q_bf, k_h)
            v_h = v_pair[:, 128 * hh:128 * hh + 128]
            num = inter_c * qc[:, 128 * hh:128 * hh + 128] + _dot(w.astype(BF16), v_h.astype(BF16))
            qn = jnp.sum(jnp.where(hmask, qn_prod, 0.0), axis=1, keepdims=True)
            den = inter_c * qn + jnp.sum(w, axis=1, keepdims=True)
            hv = num / jnp.maximum(jnp.abs(den), jnp.exp(-mt_c))
            hn = hv * lax.rsqrt(jnp.mean(hv * hv, axis=-1, keepdims=True) + EPS)
            hn = hn * ghead_ref[:, 128 * hd:128 * hd + 128]
            om = z_ref[:, OM + 128 * hd:OM + 128 * hd + 128]
            ym_ref[:, 128 * hd:128 * hd + 128] = (hn * _sigmoid(om)).astype(BF16)
            b_end = b_c[T - 1:T, :]
            m_end = mt_c[T - 1:T, :]
            ws.append(jnp.exp((ig_c - b_c) + b_end - m_end))
            decays.append(jnp.exp(m_prev + b_end - m_end))
            m_ends.append(m_end)
        kw = k_pair * jnp.where(lane_lo, ws[0], ws[1])
        upd = _dot(v_pair.T.astype(BF16), kw.astype(BF16))
        dec_rows = jnp.where(row2[:, 0:1] < LANES, decays[0], decays[1])
        new_carry["cbd"][p] = dec_rows * cbd + jnp.where(bd_mask, upd, 0.0)
        dec_lanes = jnp.where(lane_lo[0:1, :], decays[0], decays[1])
        new_carry["n"][p] = dec_lanes * n_pair + jnp.sum(kw, axis=0, keepdims=True)
        for hh in range(2):
            new_carry["m"][2 * p + hh] = m_ends[hh]
    return new_carry


def _resident(shape):
    zeros = (0,) * len(shape)
    return pl.BlockSpec(shape, lambda b, j: zeros, pipeline_mode=pl.Buffered(1))


def _smem():
    return pl.BlockSpec(memory_space=pltpu.SMEM)


def _prompt_call(x, wcat, wao, wmo, wout, wg, wu, wd, gattn, gffn, gfin, ghead, bif, fbuck, relb, sinks):
    bsz, s_len, d = x.shape
    t_step = T_BLK * N_SUB
    assert s_len % t_step == 0 and d == 1024
    d_ff = wg.shape[1]
    nj = s_len // t_step
    out_shapes = (
        jax.ShapeDtypeStruct((bsz, s_len, d), F32),
        jax.ShapeDtypeStruct((bsz, WINDOW, 128), F32),
        jax.ShapeDtypeStruct((bsz, WINDOW, 128), F32),
        jax.ShapeDtypeStruct((bsz, M_HEADS, 128, 64), F32),
        jax.ShapeDtypeStruct((bsz, M_HEADS, 64), F32),
        jax.ShapeDtypeStruct((bsz, 1, M_HEADS), F32),
    )
    in_specs = [
        pl.BlockSpec((None, t_step, d), lambda b, j: (b, j, 0)),
        _resident(wcat.shape), _resident(wao.shape), _resident(wmo.shape), _resident(wout.shape),
        _resident(wg.shape), _resident(wu.shape), _resident(wd.shape),
        _resident(gattn.shape), _resident(gffn.shape), _resident(gfin.shape), _resident(ghead.shape),
        _resident(bif.shape), _resident(fbuck.shape), _smem(), _smem(),
    ]
    out_specs = (
        pl.BlockSpec((None, t_step, d), lambda b, j: (b, j, 0)),
        pl.BlockSpec((None, WINDOW, 128), lambda b, j: (b, 0, 0)),
        pl.BlockSpec((None, WINDOW, 128), lambda b, j: (b, 0, 0)),
        pl.BlockSpec((None, M_HEADS, 128, 64), lambda b, j: (b, 0, 0, 0)),
        pl.BlockSpec((None, M_HEADS, 64), lambda b, j: (b, 0, 0)),
        pl.BlockSpec((None, 1, M_HEADS), lambda b, j: (b, 0, 0)),
    )
    scratch = [
        pltpu.VMEM((N_SUB, T_BLK, N_CAT), F32),
        pltpu.VMEM((N_SUB, T_BLK, 512), BF16),
        pltpu.VMEM((N_SUB, T_BLK, 512), BF16),
        pltpu.VMEM((N_SUB, T_BLK, d_ff), BF16),
        pltpu.VMEM((N_Q_HEADS, 128, 128), F32),
        pltpu.VMEM((128, 128), F32),
        pltpu.VMEM((128, 128), F32),
        pltpu.VMEM((2, 256, 128), F32),
        pltpu.VMEM((8, 128), F32),
    ]
    return pl.pallas_call(
        _prompt_kernel,
        grid=(bsz, nj),
        in_specs=in_specs,
        out_specs=out_specs,
        out_shape=out_shapes,
        scratch_shapes=scratch,
        compiler_params=pltpu.CompilerParams(
            dimension_semantics=("arbitrary", "arbitrary"),
            vmem_limit_bytes=VMEM_LIMIT_BYTES),
        name="prompt_layer",
    )(x, wcat, wao, wmo, wout, wg, wu, wd, gattn, gffn, gfin, ghead, bif, fbuck, relb, sinks)


SK1_CHUNK = 640


def _sample_inproj_kernel(x_ref, g_ref, w_ref, z_ref):
    h = _rms(x_ref[...], g_ref[...]).astype(BF16)
    z_ref[...] = _dot(h, w_ref[...])


def _sample_inproj_call(x, gattn, wcat):
    n, d = x.shape
    return pl.pallas_call(
        _sample_inproj_kernel,
        grid=(N_CAT // SK1_CHUNK,),
        in_specs=[pl.BlockSpec((n, d), lambda i: (0, 0)),
                  pl.BlockSpec((1, d), lambda i: (0, 0)),
                  pl.BlockSpec((d, SK1_CHUNK), lambda i: (0, i))],
        out_specs=pl.BlockSpec((n, SK1_CHUNK), lambda i: (0, i)),
        out_shape=jax.ShapeDtypeStruct((n, N_CAT), F32),
        compiler_params=pltpu.CompilerParams(dimension_semantics=("arbitrary",)),
        name="sample_inproj",
    )(x, gattn, wcat)


SK2_BB = 8


def _rows8(x4):
    r = lax.broadcasted_iota(jnp.int32, (8, x4.shape[1]), 0)
    out = jnp.zeros((8, x4.shape[1]), x4.dtype)
    for hd in range(x4.shape[0]):
        out = jnp.where(r == hd, x4[hd:hd + 1, :], out)
    return out


def _sample_mixer_kernel(z_ref, ck_ref, cv_ref, c_ref, n_ref, m_ref, ghead_ref, bif_ref, sbuck_ref,
                         relb_ref, sinks_ref,
                         yatt_ref, ym_ref, ko_ref, vo_ref, co_ref, no_ref, mo_ref,
                         sbias_scr):
    i = pl.program_id(0)

    @pl.when(i == 0)
    def _build_bias():
        sb = sbuck_ref[...]
        r8 = lax.broadcasted_iota(jnp.int32, (8, LANES), 0)
        acc = jnp.zeros((8, LANES), F32)
        for rrow, hd in enumerate((0, 2, 4, 6, 1, 3, 5, 7)):
            acc = jnp.where(r8 == rrow, _bias_from_buckets(sb, relb_ref, hd), acc)
        sbias_scr[...] = acc

    r8 = lax.broadcasted_iota(jnp.int32, (8, LANES), 0)
    l8 = lax.broadcasted_iota(jnp.int32, (8, LANES), 1)
    lo8 = l8 < HEAD_DIM
    r8c = lax.broadcasted_iota(jnp.int32, (8, 1), 0)
    sink = jnp.zeros((8, 1), F32)
    for rrow, hd in enumerate((0, 2, 4, 6, 1, 3, 5, 7)):
        sink = jnp.where(r8c == rrow, sinks_ref[hd], sink)
    row128 = lax.broadcasted_iota(jnp.int32, (WINDOW, LANES), 0)
    last_row = row128 == WINDOW - 1
    sbias = sbias_scr[...]
    z8 = jnp.zeros((8, LANES), F32)
    l512 = lax.broadcasted_iota(jnp.int32, (8, 4 * LANES), 1)
    r512 = lax.broadcasted_iota(jnp.int32, (8, 4 * LANES), 0)
    head_blk = (l512 // LANES) == r512
    r4c = r8c
    ghead4 = ghead_ref[...]
    bif = bif_ref[...]

    for bi in range(SK2_BB):
        zb = z_ref[bi]
        knew = zb[KA // LANES:KA // LANES + 1, :]
        vnew = zb[VA // LANES:VA // LANES + 1, :]
        kt = jnp.where(last_row, knew, pltpu.roll(ck_ref[bi], WINDOW - 1, 0))
        vt = jnp.where(last_row, vnew, pltpu.roll(cv_ref[bi], WINDOW - 1, 0))
        ko_ref[bi] = kt
        vo_ref[bi] = vt
        xq = zb[0:8, :] * (HEAD_DIM ** -0.5)
        xs = pltpu.roll(xq, 4, 0)
        xr = pltpu.roll(xq, HEAD_DIM, 1)
        xsr = pltpu.roll(xs, HEAD_DIM, 1)
        qm = jnp.where(r8 < 2, jnp.where(lo8, xq, z8),
                       jnp.where(r8 < 4, jnp.where(lo8, z8, xr),
                                 jnp.where(r8 < 6, jnp.where(lo8, xsr, z8), jnp.where(lo8, z8, xs))))
        s = _dot_nt(qm.astype(BF16), kt.astype(BF16)) + sbias
        mx = jnp.maximum(jnp.max(s, axis=-1, keepdims=True), sink)
        pe = jnp.exp(s - mx)
        den = jnp.sum(pe, axis=-1, keepdims=True) + jnp.exp(sink - mx)
        pn = pe * (1.0 / den)
        oa = _dot(pn.astype(BF16), vt.astype(BF16))
        oas = pltpu.roll(oa, 4, 0)
        oar = pltpu.roll(oa, HEAD_DIM, 1)
        oasr = pltpu.roll(oas, HEAD_DIM, 1)
        ypair = jnp.where(r8 < 2, jnp.where(lo8, oa, oasr), jnp.where(lo8, oar, oas))
        yatt_ref[bi] = ypair[0:4, :]

        gm = jnp.broadcast_to(zb[IF // LANES:IF // LANES + 1, :] + bif, (8, LANES))
        ig = jnp.sum(jnp.where(l8 == r8, gm, 0.0), axis=1, keepdims=True)
        fp = jnp.sum(jnp.where(l8 == r8 + M_HEADS, gm, 0.0), axis=1, keepdims=True)
        lf = _log_sigmoid(fp)
        m0 = _rows8(m_ref[bi])
        a = ig - lf
        m_t = lf + jnp.maximum(m0, a)
        dgate = jnp.exp(a + lf - m_t)
        inter = jnp.exp(m0 + lf - m_t)
        qp = zb[QM // LANES:QM // LANES + 8, :]
        qpr = pltpu.roll(qp, HEAD_DIM, 1)
        q4 = jnp.where(lo8, jnp.where(r8 == 0, qp[0:1], jnp.where(r8 == 1, qpr[0:1],
                       jnp.where(r8 == 2, qp[1:2], jnp.where(r8 == 3, qpr[1:2], 0.0)))), 0.0)
        k4 = jnp.where(lo8, jnp.where(r8 == 0, qp[2:3], jnp.where(r8 == 1, qpr[2:3],
                       jnp.where(r8 == 2, qp[3:4], jnp.where(r8 == 3, qpr[3:4], 0.0)))), 0.0)
        k4 = k4 * (HEAD_DIM ** -0.5)
        n4 = _rows8(n_ref[bi])
        v8 = zb[VM // LANES:VM // LANES + 8, :]
        v4 = jnp.where(r8 < M_HEADS, v8, 0.0)
        o8 = zb[OM // LANES:OM // LANES + 8, :]
        call = jnp.concatenate([c_ref[bi, hd] for hd in range(M_HEADS)], axis=0)
        rq = _dot_nt(q4[:, 0:HEAD_DIM].astype(BF16), call.astype(BF16))
        cq = jnp.where(r8 == 0, rq[:, 0:128], jnp.where(r8 == 1, rq[:, 128:256],
                       jnp.where(r8 == 2, rq[:, 256:384], jnp.where(r8 == 3, rq[:, 384:512], 0.0))))
        qk = jnp.sum(q4 * k4, axis=1, keepdims=True)
        nq = jnp.sum(n4 * q4[:, 0:HEAD_DIM], axis=1, keepdims=True)
        w = dgate * qk
        num = inter * cq + w * v4
        den_m = inter * nq + w
        hv = num / jnp.maximum(jnp.abs(den_m), jnp.exp(-m_t))
        hn = hv * lax.rsqrt(jnp.mean(hv * hv, axis=-1, keepdims=True) + EPS)
        ym8 = hn[0:4, :] * ghead4 * _sigmoid(o8[0:4, :])
        ym_ref[bi] = ym8
        a512 = jnp.where(head_blk, jnp.concatenate([v4, v4, v4, v4], axis=1), 0.0)
        b128 = dgate * k4
        upd = _dot_tn(a512, b128)
        for hd in range(M_HEADS):
            co_ref[bi, hd] = (inter[hd:hd + 1, :] * c_ref[bi, hd]
                              + upd[128 * hd:128 * hd + 128, 0:HEAD_DIM])
        no_ref[bi] = (inter * n4 + dgate * k4[:, 0:HEAD_DIM])[0:4, :]
        mo_ref[bi] = m_t[0:4, :]


def _sample_mixer_call(z3, ck, cv, c0, n0, m0, ghead4, bif, sbuck, relb, sinks):
    nb = z3.shape[0]
    assert nb % SK2_BB == 0
    bb = SK2_BB
    out_shapes = (
        jax.ShapeDtypeStruct((nb, 4, 128), F32),
        jax.ShapeDtypeStruct((nb, 4, 128), F32),
        jax.ShapeDtypeStruct(ck.shape, F32),
        jax.ShapeDtypeStruct(cv.shape, F32),
        jax.ShapeDtypeStruct(c0.shape, F32),
        jax.ShapeDtypeStruct(n0.shape, F32),
        jax.ShapeDtypeStruct(m0.shape, F32),
    )
    blk3 = lambda s: pl.BlockSpec((bb,) + tuple(s[1:]), lambda i: (i,) + (0,) * (len(s) - 1))
    full = lambda s: pl.BlockSpec(tuple(s), lambda i: (0,) * len(s))
    in_specs = [blk3(z3.shape), blk3(ck.shape), blk3(cv.shape), blk3(c0.shape), blk3(n0.shape),
                blk3(m0.shape), full(ghead4.shape), full(bif.shape), full(sbuck.shape), _smem(), _smem()]
    out_specs = (blk3((nb, 4, 128)), blk3((nb, 4, 128)), blk3(ck.shape), blk3(cv.shape),
                 blk3(c0.shape), blk3(n0.shape), blk3(m0.shape))
    return pl.pallas_call(
        _sample_mixer_kernel,
        grid=(nb // bb,),
        in_specs=in_specs,
        out_specs=out_specs,
        out_shape=out_shapes,
        scratch_shapes=[pltpu.VMEM((8, LANES), F32)],
        compiler_params=pltpu.CompilerParams(dimension_semantics=("arbitrary",)),
        name="sample_mixers",
    )(z3, ck, cv, c0, n0, m0, ghead4, bif, sbuck, relb, sinks)


FFN_CHUNK = 256


def _sample_tail_kernel(x_ref, z_ref, yatt_ref, ym_ref, wao_ref, wmo_ref, wout_ref,
                        wg_ref, wu_ref, wd_ref, gffn_ref, gfin_ref, y_ref,
                        x1_scr, h2_scr, acc_scr):
    c = pl.program_id(0)

    @pl.when(c == 0)
    def _mix():
        ya = _dot(yatt_ref[...].astype(BF16), wao_ref[...])
        ymm = _dot(ym_ref[...].astype(BF16), wmo_ref[...])
        mixed = _sigmoid(z_ref[:, GA:GA + 1024]) * ya + _sigmoid(z_ref[:, GM:GM + 1024]) * ymm
        x1 = x_ref[...] + _dot(mixed.astype(BF16), wout_ref[...])
        x1_scr[...] = x1
        h2_scr[...] = _rms(x1, gffn_ref[...]).astype(BF16)
        acc_scr[...] = jnp.zeros_like(acc_scr)

    h2 = h2_scr[...]
    g = _dot(h2, wg_ref[...])
    u = _dot(h2, wu_ref[...])
    acc_scr[...] += _dot((g * _sigmoid(g) * u).astype(BF16), wd_ref[...])

    @pl.when(c == pl.num_programs(0) - 1)
    def _finish():
        y_ref[...] = _rms(x1_scr[...] + acc_scr[...], gfin_ref[...])


def _sample_tail_call(x, z, yatt, ym, wao, wmo, wout, wg, wu, wd, gffn, gfin):
    n, d = x.shape
    d_ff = wg.shape[1]
    assert d_ff % FFN_CHUNK == 0
    full = lambda a: pl.BlockSpec(a.shape, lambda c: (0,) * a.ndim)
    return pl.pallas_call(
        _sample_tail_kernel,
        grid=(d_ff // FFN_CHUNK,),
        in_specs=[full(x), full(z), full(yatt), full(ym), full(wao), full(wmo), full(wout),
                  pl.BlockSpec((d, FFN_CHUNK), lambda c: (0, c)),
                  pl.BlockSpec((d, FFN_CHUNK), lambda c: (0, c)),
                  pl.BlockSpec((FFN_CHUNK, d), lambda c: (c, 0)),
                  full(gffn), full(gfin)],
        out_specs=pl.BlockSpec((n, d), lambda c: (0, 0)),
        out_shape=jax.ShapeDtypeStruct((n, d), F32),
        scratch_shapes=[pltpu.VMEM((n, d), F32), pltpu.VMEM((n, d), BF16), pltpu.VMEM((n, d), F32)],
        compiler_params=pltpu.CompilerParams(dimension_semantics=("arbitrary",),
                                             vmem_limit_bytes=VMEM_LIMIT_BYTES),
        name="sample_tail",
    )(x, z, yatt, ym, wao, wmo, wout, wg, wu, wd, gffn, gfin)


def kernel(x_prompt, x_sample, cache_k_win, cache_v_win, state_mlstm_C, state_mlstm_n, state_mlstm_m,
           rel_bias, w_in, b_if, sinks, g_attn_norm, g_head, w_att_out, w_mlstm_out, w_out,
           g_ffn_norm, w_gate, w_up, w_down, g_final):
    depth = w_in.shape[0]
    assert depth == 1
    bsz, s_len, d = x_prompt.shape
    nb = x_sample.shape[0]
    assert x_sample.shape[1] == 1 and cache_k_win.shape[2] == WINDOW

    w = w_in[0]
    wcat = jnp.concatenate([w[:, :2312], jnp.zeros((d, N_CAT - 4360), w.dtype), w[:, 2312:4360]],
                           axis=1).astype(BF16)
    wao = w_att_out[0].astype(BF16)
    wmo = w_mlstm_out[0].astype(BF16)
    wout = w_out[0].astype(BF16)
    wg = w_gate[0].astype(BF16)
    wu = w_up[0].astype(BF16)
    wd = w_down[0].astype(BF16)
    gattn = g_attn_norm[0].reshape(1, d)
    gffn = g_ffn_norm[0].reshape(1, d)
    gfin = g_final.reshape(1, d)
    ghead = g_head[0].reshape(1, 512)
    bif = jnp.concatenate([b_if[0].reshape(1, 2 * M_HEADS), jnp.zeros((1, LANES - 2 * M_HEADS), F32)], axis=1)
    sink_v = sinks[0]

    qi = jnp.arange(WINDOW)[:, None]
    kj = jnp.arange(WINDOW)[None, :]
    fbuck = _t5_bucket(jnp.where(kj <= qi, qi - kj, WINDOW + qi - kj)).astype(jnp.int32)
    sbuck = jnp.broadcast_to(_t5_bucket(WINDOW - 1 - kj), (8, WINDOW)).astype(jnp.int32)

    y_p, k_p, v_p, c_p, n_p, m_p = _prompt_call(
        x_prompt, wcat, wao, wmo, wout, wg, wu, wd, gattn, gffn, gfin, ghead, bif, fbuck, rel_bias, sink_v)

    xs = x_sample.reshape(nb, d)
    z_s = _sample_inproj_call(xs, gattn, wcat)
    yatt_s, ym_s, k_s, v_s, c_s, n_s, m_s = _sample_mixer_call(
        z_s.reshape(nb, N_CAT // LANES, LANES),
        cache_k_win[0].reshape(nb, WINDOW, 128), cache_v_win[0].reshape(nb, WINDOW, 128),
        state_mlstm_C[0], state_mlstm_n[0], state_mlstm_m[0].reshape(nb, M_HEADS, 1),
        ghead.reshape(M_HEADS, 128), bif, sbuck, rel_bias, sink_v)
    y_s = _sample_tail_call(xs, z_s, yatt_s.reshape(nb, 512), ym_s.reshape(nb, 512),
                            wao, wmo, wout, wg, wu, wd, gffn, gfin)

    kv_shape_p = (1, bsz, WINDOW, N_KV_HEADS, HEAD_DIM)
    kv_shape_s = (1, nb, WINDOW, N_KV_HEADS, HEAD_DIM)
    return (y_p, y_s.reshape(nb, 1, d),
            k_p.reshape(kv_shape_p), v_p.reshape(kv_shape_p),
            c_p[None], n_p[None], m_p.reshape(1, bsz, M_HEADS),
            k_s.reshape(kv_shape_s), v_s.reshape(kv_shape_s),
            c_s[None], n_s[None], m_s.reshape(1, nb, M_HEADS))
```

```python
import math

import jax
import jax.numpy as jnp
from jax import lax
from jax.experimental import pallas as pl
from jax.experimental.pallas import tpu as pltpu

F32 = jnp.float32
BF16 = jnp.bfloat16

HEAD_DIM = 64
N_Q_HEADS = 8
N_KV_HEADS = 2
WINDOW = 128
N_BUCKETS = 32
MAX_DISTANCE = WINDOW
M_HEADS = 4
EPS = 1e-6
NEG = -1e30

LANES = 128
VMEM_LIMIT_BYTES = 58 * 1024 * 1024

QA, KA, VA, QM, KM, VM, OM, IF, GA, GM = 0, 512, 640, 768, 1024, 1280, 1792, 2304, 2432, 3456
N_CAT = 4480
Z_GROUPS = ((0, 768), (768, 2432), (2432, 4480))
T_BLK = 256
N_SUB = 2


def _dot(a, b):
    return jnp.dot(a, b, preferred_element_type=F32)


def _dot_nt(a, b):
    return lax.dot_general(a, b, (((1,), (1,)), ((), ())), preferred_element_type=F32)


def _dot_tn(a, b):
    return lax.dot_general(a, b, (((0,), (0,)), ((), ())), preferred_element_type=F32)


def _sigmoid(x):
    return 1.0 / (1.0 + jnp.exp(-x))


def _log_sigmoid(x):
    return jnp.minimum(x, 0.0) - jnp.log1p(jnp.exp(-jnp.abs(x)))


def _rms(x, g):
    ms = jnp.mean(x * x, axis=-1, keepdims=True)
    return x * lax.rsqrt(ms + EPS) * g


def _t5_bucket(dist):
    n = jnp.maximum(dist, 0)
    max_exact = N_BUCKETS // 2
    nf = jnp.maximum(n, 1).astype(F32)
    large = max_exact + jnp.floor(jnp.log(nf / max_exact) / math.log(MAX_DISTANCE / max_exact)
                                  * (N_BUCKETS - max_exact)).astype(jnp.int32)
    large = jnp.minimum(large, N_BUCKETS - 1)
    return jnp.where(n < max_exact, n, large)


def _bias_from_buckets(buckets, relb_ref, head):
    acc = jnp.zeros(buckets.shape, F32)
    for k in range(N_BUCKETS):
        acc = jnp.where(buckets == k, relb_ref[head, k], acc)
    return acc


def _prompt_kernel(x_ref, wcat_ref, wao_ref, wmo_ref, wout_ref, wg_ref, wu_ref, wd_ref,
                   gattn_ref, gffn_ref, gfin_ref, ghead_ref, bif_ref, fbuck_ref, relb_ref, sinks_ref,
                   y_ref, kwin_ref, vwin_ref, c_ref, n_ref, m_ref,
                   z_scr, yatt_scr, ym_scr, a_scr, bias_scr, kprev_scr, vprev_scr, cbd_scr, st_scr):
    T = T_BLK
    b = pl.program_id(0)
    j = pl.program_id(1)
    nj = pl.num_programs(1)

    @pl.when(jnp.logical_and(b == 0, j == 0))
    def _build_bias():
        fb = fbuck_ref[...]
        for h in range(N_Q_HEADS):
            bias_scr[h] = _bias_from_buckets(fb, relb_ref, h)

    @pl.when(j == 0)
    def _reset_state():
        kprev_scr[...] = jnp.zeros_like(kprev_scr)
        vprev_scr[...] = jnp.zeros_like(vprev_scr)
        cbd_scr[...] = jnp.zeros_like(cbd_scr)
        st_scr[...] = jnp.zeros_like(st_scr)

    for s in range(N_SUB):
        _in_projection(x_ref.at[s * T:(s + 1) * T, :], gattn_ref, wcat_ref, z_scr.at[s])

    carry = _load_carry(kprev_scr, vprev_scr, cbd_scr, st_scr)
    for s in range(N_SUB):
        carry = _mixers(z_scr.at[s], yatt_scr.at[s], ym_scr.at[s], carry, ghead_ref, bif_ref, bias_scr,
                        sinks_ref, first_of_sequence=(j == 0) if s == 0 else None)
    _store_carry(carry, kprev_scr, vprev_scr, cbd_scr, st_scr)

    for s in range(N_SUB):
        _tail(x_ref.at[s * T:(s + 1) * T, :], z_scr.at[s], yatt_scr.at[s], ym_scr.at[s], a_scr.at[s],
              wao_ref, wmo_ref, wout_ref, wg_ref, wu_ref, wd_ref, gffn_ref, gfin_ref,
              y_ref.at[s * T:(s + 1) * T, :])

    @pl.when(j == nj - 1)
    def _write_state():
        kwin_ref[...] = kprev_scr[...].T
        vwin_ref[...] = vprev_scr[...].T
        stn = st_scr[...]
        cts = [cbd_scr[0].T, cbd_scr[1].T]
        for hd in range(M_HEADS):
            p, hh = hd // 2, hd % 2
            c_ref[hd] = cts[p][64 * hh:64 * hh + 64, 128 * hh:128 * hh + 128]
            n_ref[hd:hd + 1, :] = stn[p:p + 1, 64 * hh:64 * hh + 64]
            m_ref[0:1, hd:hd + 1] = stn[2 + hd:3 + hd, 0:1]


def _in_projection(x_ref, gattn_ref, wcat_ref, z_ref):
    h = _rms(x_ref[...], gattn_ref[...]).astype(BF16)
    for c0, c1 in Z_GROUPS:
        z_ref[:, c0:c1] = _dot_nt(h, wcat_ref[c0:c1, :])


def _load_carry(kprev_scr, vprev_scr, cbd_scr, st_scr):
    st = st_scr[...]
    return dict(kp=kprev_scr[...], vp=vprev_scr[...], cbd=[cbd_scr[0], cbd_scr[1]],
                n=[st[0:1, :], st[1:2, :]], m=[st[2 + hd:3 + hd, 0:1] for hd in range(M_HEADS)])


def _store_carry(carry, kprev_scr, vprev_scr, cbd_scr, st_scr):
    kprev_scr[...] = carry["kp"]
    vprev_scr[...] = carry["vp"]
    for p in range(2):
        cbd_scr[p] = carry["cbd"][p]
        st_scr[p:p + 1, :] = carry["n"][p]
    for hd in range(M_HEADS):
        st_scr[2 + hd:3 + hd, :] = jnp.broadcast_to(carry["m"][hd], (1, LANES))


def _tail(x_ref, z_ref, yatt_ref, ym_ref, a_ref, wao_ref, wmo_ref, wout_ref, wg_ref, wu_ref, wd_ref,
          gffn_ref, gfin_ref, y_ref):
    d_ff = wg_ref.shape[1]
    ya = _dot(yatt_ref[...], wao_ref[...])
    ymm = _dot(ym_ref[...], wmo_ref[...])
    mixed = _sigmoid(z_ref[:, GA:GA + 1024]) * ya + _sigmoid(z_ref[:, GM:GM + 1024]) * ymm
    x1 = x_ref[...] + _dot(mixed.astype(BF16), wout_ref[...])
    h2 = _rms(x1, gffn_ref[...]).astype(BF16)
    for c0 in range(0, d_ff, 256):
        g = _dot(h2, wg_ref[:, c0:c0 + 256])
        u = _dot(h2, wu_ref[:, c0:c0 + 256])
        a_ref[:, c0:c0 + 256] = (g * _sigmoid(g) * u).astype(BF16)
    x2 = x1 + _dot(a_ref[...], wd_ref[...])
    y_ref[...] = _rms(x2, gfin_ref[...])


def _mixers(z_ref, yatt_ref, ym_ref, carry, ghead_ref, bif_ref, bias_scr, sinks_ref, first_of_sequence):
    T = T_BLK
    lane = lax.broadcasted_iota(jnp.int32, (T, LANES), 1)
    lane_lo = lane < HEAD_DIM
    row = lax.broadcasted_iota(jnp.int32, (T, LANES), 0)
    qi = jnp.where(row < 128, row, row - 128)
    tri2 = lane <= qi
    if first_of_sequence is not None:
        valid0 = lane <= qi + jnp.where(first_of_sequence, 0, 2 * LANES)
    rowc = lax.broadcasted_iota(jnp.int32, (T, 1), 0)
    k_all = z_ref[:, KA:KA + 128]
    v_all = z_ref[:, VA:VA + 128]
    for sb in range(2):
        r0 = 128 * sb
        if sb == 0:
            kp, vp = carry["kp"], carry["vp"]
        else:
            kp, vp = k_all[0:128], v_all[0:128]
        kcat = jnp.concatenate([kp, k_all[r0:r0 + 128]], axis=0)
        vcat = jnp.concatenate([vp, v_all[r0:r0 + 128]], axis=0)
        kroll = pltpu.roll(kcat, HEAD_DIM, 1)
        vroll = pltpu.roll(vcat, HEAD_DIM, 1)
        zero = jnp.zeros_like(kcat)
        k_mats = [jnp.where(lane_lo, kcat, zero), jnp.where(lane_lo, zero, kroll),
                  jnp.where(lane_lo, kroll, zero), jnp.where(lane_lo, zero, kcat)]
        v_mats = [jnp.where(lane_lo, vcat, zero), jnp.where(lane_lo, zero, vroll),
                  jnp.where(lane_lo, vroll, zero), jnp.where(lane_lo, zero, vcat)]
        q = (z_ref[r0:r0 + 128, QA:QA + 512] * (HEAD_DIM ** -0.5)).astype(BF16)
        lhs_a = jnp.concatenate([q[:, 0:128], q[:, 128:256]], axis=0)
        lhs_b = jnp.concatenate([q[:, 256:384], q[:, 384:512]], axis=0)
        groups = [(lhs_a, 0, 0, 2), (lhs_a, 1, 1, 3), (lhs_b, 2, 4, 6), (lhs_b, 3, 5, 7)]
        pcs = []
        for lhs, mi, ha, hb in groups:
            s = _dot_nt(lhs, k_mats[mi].astype(BF16))
            sf = jnp.where(tri2, s[:, 128:256], s[:, 0:128])
            sf = sf + jnp.concatenate([bias_scr[ha], bias_scr[hb]], axis=0)
            if sb == 0 and first_of_sequence is not None:
                sf = jnp.where(valid0, sf, NEG)
            sink = jnp.where(rowc < 128, sinks_ref[ha], sinks_ref[hb])
            mx = jnp.maximum(jnp.max(sf, axis=-1, keepdims=True), sink)
            p = jnp.exp(sf - mx)
            den = jnp.sum(p, axis=-1, keepdims=True) + jnp.exp(sink - mx)
            pn = p * (1.0 / den)
            zp = jnp.zeros_like(pn)
            pcs.append(jnp.concatenate([jnp.where(tri2, zp, pn), jnp.where(tri2, pn, zp)],
                                       axis=1).astype(BF16))
        o_a = _dot(jnp.concatenate([pcs[0], pcs[1]], axis=1),
                   jnp.concatenate([v_mats[0], v_mats[1]], axis=0).astype(BF16))
        o_b = _dot(jnp.concatenate([pcs[2], pcs[3]], axis=1),
                   jnp.concatenate([v_mats[2], v_mats[3]], axis=0).astype(BF16))
        yatt_ref[r0:r0 + 128, 0:128] = o_a[0:128].astype(BF16)
        yatt_ref[r0:r0 + 128, 128:256] = o_a[128:256].astype(BF16)
        yatt_ref[r0:r0 + 128, 256:384] = o_b[0:128].astype(BF16)
        yatt_ref[r0:r0 + 128, 384:512] = o_b[128:256].astype(BF16)
    new_carry = dict(kp=k_all[128:256], vp=v_all[128:256], cbd=[None, None], n=[None, None],
                     m=[None] * M_HEADS)

    zif = z_ref[:, IF:IF + 128] + bif_ref[...]
    gl = jnp.where(lane < M_HEADS, zif, _log_sigmoid(zif))
    gl_t = gl.T
    tr = lax.broadcasted_iota(jnp.int32, (T, T), 0)
    ts = lax.broadcasted_iota(jnp.int32, (T, T), 1)
    tril = ts <= tr
    triu = tr <= ts
    row2 = lax.broadcasted_iota(jnp.int32, (2 * LANES, LANES), 0)
    lane2 = lax.broadcasted_iota(jnp.int32, (2 * LANES, LANES), 1)
    bd_mask = (row2 < LANES) == (lane2 < HEAD_DIM)
    for p in range(2):
        q_pair = z_ref[:, QM + 128 * p:QM + 128 * p + 128]
        k_pair = z_ref[:, KM + 128 * p:KM + 128 * p + 128] * (HEAD_DIM ** -0.5)
        v_pair = z_ref[:, VM + 256 * p:VM + 256 * p + 256]
        cbd = carry["cbd"][p]
        n_pair = carry["n"][p]
        q_bf = q_pair.astype(BF16)
        qc = _dot_nt(q_bf, cbd.astype(BF16))
        qn_prod = q_pair * n_pair
        ws, decays, m_ends = [], [], []
        for hh in range(2):
            hd = 2 * p + hh
            hmask = lane_lo if hh == 0 else jnp.logical_not(lane_lo)
            ig_c = gl[:, hd:hd + 1]
            lf_c = gl[:, M_HEADS + hd:M_HEADS + hd + 1]
            ig_r = gl_t[hd:hd + 1, :]
            lf_r = gl_t[M_HEADS + hd:M_HEADS + hd + 1, :]
            m_prev = carry["m"][hd]
            b_c = jnp.sum(jnp.where(tril, lf_r, 0.0), axis=1, keepdims=True)
            b_r = jnp.sum(jnp.where(triu, lf_c, 0.0), axis=0, keepdims=True)
            a_r = ig_r - b_r
            cm_c = jnp.max(jnp.where(tril, a_r, NEG), axis=1, keepdims=True)
            mt_c = b_c + jnp.maximum(m_prev, cm_c)
            g_c = b_c - mt_c
            dm = jnp.exp(jnp.where(tril, a_r + g_c, NEG))
            inter_c = jnp.exp(m_prev + g_c)
            k_h = jnp.where(hmask, k_pair, 0.0).astype(BF16)
            w = dm * _dot_nt(q_bf, k_h)
            v_h = v_pair[:, 128 * hh:128 * hh + 128]
            num = inter_c * qc[:, 128 * hh:128 * hh + 128] + _dot(w.astype(BF16), v_h.astype(BF16))
            qn = jnp.sum(jnp.where(hmask, qn_prod, 0.0), axis=1, keepdims=True)
            den = inter_c * qn + jnp.sum(w, axis=1, keepdims=True)
            hv = num / jnp.maximum(jnp.abs(den), jnp.exp(-mt_c))
            hn = hv * lax.rsqrt(jnp.mean(hv * hv, axis=-1, keepdims=True) + EPS)
            hn = hn * ghead_ref[:, 128 * hd:128 * hd + 128]
            om = z_ref[:, OM + 128 * hd:OM + 128 * hd + 128]
            ym_ref[:, 128 * hd:128 * hd + 128] = (hn * _sigmoid(om)).astype(BF16)
            b_end = b_c[T - 1:T, :]
            m_end = mt_c[T - 1:T, :]
            ws.append(jnp.exp((ig_c - b_c) + b_end - m_end))
            decays.append(jnp.exp(m_prev + b_end - m_end))
            m_ends.append(m_end)
        kw = k_pair * jnp.where(lane_lo, ws[0], ws[1])
        upd = _dot(v_pair.T.astype(BF16), kw.astype(BF16))
        dec_rows = jnp.where(row2[:, 0:1] < LANES, decays[0], decays[1])
        new_carry["cbd"][p] = dec_rows * cbd + jnp.where(bd_mask, upd, 0.0)
        dec_lanes = jnp.where(lane_lo[0:1, :], decays[0], decays[1])
        new_carry["n"][p] = dec_lanes * n_pair + jnp.sum(kw, axis=0, keepdims=True)
        for hh in range(2):
            new_carry["m"][2 * p + hh] = m_ends[hh]
    return new_carry


def _resident(shape):
    zeros = (0,) * len(shape)
    return pl.BlockSpec(shape, lambda b, j: zeros, pipeline_mode=pl.Buffered(1))


def _smem():
    return pl.BlockSpec(memory_space=pltpu.SMEM)


def _prompt_call(x, wcat, wao, wmo, wout, wg, wu, wd, gattn, gffn, gfin, ghead, bif, fbuck, relb, sinks):
    bsz, s_len, d = x.shape
    t_step = T_BLK * N_SUB
    assert s_len % t_step == 0 and d == 1024
    d_ff = wg.shape[1]
    nj = s_len // t_step
    out_shapes = (
        jax.ShapeDtypeStruct((bsz, s_len, d), F32),
        jax.ShapeDtypeStruct((bsz, WINDOW, 128), F32),
        jax.ShapeDtypeStruct((bsz, WINDOW, 128), F32),
        jax.ShapeDtypeStruct((bsz, M_HEADS, 64, 128), F32),
        jax.ShapeDtypeStruct((bsz, M_HEADS, 64), F32),
        jax.ShapeDtypeStruct((bsz, 1, M_HEADS), F32),
    )
    in_specs = [
        pl.BlockSpec((None, t_step, d), lambda b, j: (b, j, 0)),
        _resident(wcat.shape), _resident(wao.shape), _resident(wmo.shape), _resident(wout.shape),
        _resident(wg.shape), _resident(wu.shape), _resident(wd.shape),
        _resident(gattn.shape), _resident(gffn.shape), _resident(gfin.shape), _resident(ghead.shape),
        _resident(bif.shape), _resident(fbuck.shape), _smem(), _smem(),
    ]
    out_specs = (
        pl.BlockSpec((None, t_step, d), lambda b, j: (b, j, 0)),
        pl.BlockSpec((None, WINDOW, 128), lambda b, j: (b, 0, 0)),
        pl.BlockSpec((None, WINDOW, 128), lambda b, j: (b, 0, 0)),
        pl.BlockSpec((None, M_HEADS, 64, 128), lambda b, j: (b, 0, 0, 0)),
        pl.BlockSpec((None, M_HEADS, 64), lambda b, j: (b, 0, 0)),
        pl.BlockSpec((None, 1, M_HEADS), lambda b, j: (b, 0, 0)),
    )
    scratch = [
        pltpu.VMEM((N_SUB, T_BLK, N_CAT), F32),
        pltpu.VMEM((N_SUB, T_BLK, 512), BF16),
        pltpu.VMEM((N_SUB, T_BLK, 512), BF16),
        pltpu.VMEM((N_SUB, T_BLK, d_ff), BF16),
        pltpu.VMEM((N_Q_HEADS, 128, 128), F32),
        pltpu.VMEM((128, 128), F32),
        pltpu.VMEM((128, 128), F32),
        pltpu.VMEM((2, 256, 128), F32),
        pltpu.VMEM((8, 128), F32),
    ]
    return pl.pallas_call(
        _prompt_kernel,
        grid=(bsz, nj),
        in_specs=in_specs,
        out_specs=out_specs,
        out_shape=out_shapes,
        scratch_shapes=scratch,
        compiler_params=pltpu.CompilerParams(
            dimension_semantics=("arbitrary", "arbitrary"),
            vmem_limit_bytes=VMEM_LIMIT_BYTES),
        name="prompt_layer",
    )(x, wcat, wao, wmo, wout, wg, wu, wd, gattn, gffn, gfin, ghead, bif, fbuck, relb, sinks)


W_QA, W_KA, W_VA, W_QM, W_KM, W_VM, W_OM, W_IF, W_GA, W_GM, W_END = (
    0, 512, 640, 768, 1024, 1280, 1792, 2304, 2312, 3336, 4360)
ZT_KA, ZT_VA, ZT_QM, ZT_KM, ZT_IF, ZT_ROWS = 0, 128, 256, 512, 768, 776


def _sample_inproj_kernel(x_ref, g_ref, wt_ref, z_ref, zt_ref):
    h32 = _rms(x_ref[...], g_ref[...])
    h = h32.astype(BF16)
    ht = h32.T.astype(BF16)
    for a, b, dst in ((W_QA, 768, QA), (768, 1536, 768), (1536, W_IF, 1536),
                      (W_GA, W_GM, GA), (W_GM, W_END, GM)):
        z_ref[:, dst:dst + (b - a)] = _dot_nt(h, wt_ref[a:b, :].astype(BF16))
    w_if = jnp.concatenate([wt_ref[W_IF:W_GA, :], jnp.zeros((LANES - 8, wt_ref.shape[1]), F32)],
                           axis=0).astype(BF16)
    z_ref[:, IF:IF + LANES] = _dot_nt(h, w_if)
    zt_ref[ZT_KA:ZT_QM, :] = _dot(wt_ref[W_KA:W_QM, :].astype(BF16), ht)
    zt_ref[ZT_QM:ZT_IF, :] = _dot(wt_ref[W_QM:W_VM, :].astype(BF16), ht)
    zt_ref[ZT_IF:ZT_ROWS, :] = _dot(w_if, ht)[0:8, :]


def _sample_inproj_call(x3, gattn, wt):
    n, _, d = x3.shape
    return pl.pallas_call(
        _sample_inproj_kernel,
        grid=(1,),
        in_specs=[pl.BlockSpec((n, None, d), lambda i: (0, 0, 0)),
                  pl.BlockSpec((1, d), lambda i: (0, 0)),
                  pl.BlockSpec(wt.shape, lambda i: (0, 0), pipeline_mode=pl.Buffered(1))],
        out_specs=(pl.BlockSpec((n, N_CAT), lambda i: (0, 0)),
                   pl.BlockSpec((ZT_ROWS, n), lambda i: (0, 0))),
        out_shape=(jax.ShapeDtypeStruct((n, N_CAT), F32), jax.ShapeDtypeStruct((ZT_ROWS, n), F32)),
        compiler_params=pltpu.CompilerParams(dimension_semantics=("arbitrary",),
                                             vmem_limit_bytes=VMEM_LIMIT_BYTES),
        name="sample_inproj",
    )(x3, gattn, wt)


SK2_BB = 8
SC_INTER, SC_W, SC_SCALE, SC_D = 0, 4, 8, 12
HEAD_ROW_ORDER = (0, 2, 4, 6, 1, 3, 5, 7)


def _sample_mixer_kernel(z_ref, zt_ref, kt_ref, vt_ref, ct_ref, nt_ref, mt_ref, sbuck_ref,
                         bif_ref, relb_ref, sinks_ref,
                         yatt_ref, cq_ref, ko_ref, vo_ref, co_ref, no_ref, mo_ref, scal_ref,
                         sbias_scr):
    i = pl.program_id(0)
    r8 = lax.broadcasted_iota(jnp.int32, (8, LANES), 0)
    l8 = lax.broadcasted_iota(jnp.int32, (8, LANES), 1)
    lo8 = l8 < HEAD_DIM
    r8c = lax.broadcasted_iota(jnp.int32, (8, 1), 0)

    @pl.when(i == 0)
    def _prologue():
        sb = sbuck_ref[...]
        acc = jnp.zeros((8, LANES), F32)
        for rrow, hd in enumerate(HEAD_ROW_ORDER):
            acc = jnp.where(r8 == rrow, _bias_from_buckets(sb, relb_ref, hd), acc)
        sbias_scr[...] = acc
        bias_col = jnp.zeros((8, 1), F32)
        for r in range(8):
            bias_col = jnp.where(r8c == r, bif_ref[r // M_HEADS, r % M_HEADS], bias_col)
        ift = zt_ref[ZT_IF:ZT_ROWS, :] + bias_col
        m0 = mt_ref[...]
        r16 = lax.broadcasted_iota(jnp.int32, (16, LANES), 0)
        table = jnp.zeros((16, LANES), F32)
        for hd in range(M_HEADS):
            ig = ift[hd:hd + 1, :]
            lf = _log_sigmoid(ift[M_HEADS + hd:M_HEADS + hd + 1, :])
            m_prev = m0[hd:hd + 1, :]
            a = ig - lf
            m_t = lf + jnp.maximum(m_prev, a)
            dgate = jnp.exp(a + lf - m_t)
            inter = jnp.exp(m_prev + lf - m_t)
            qt = zt_ref[ZT_QM + 64 * hd:ZT_QM + 64 * hd + 64, :]
            kt = zt_ref[ZT_KM + 64 * hd:ZT_KM + 64 * hd + 64, :] * (HEAD_DIM ** -0.5)
            nt = nt_ref[hd]
            no_ref[hd] = inter * nt + dgate * kt
            mo_ref[hd:hd + 1, :] = m_t
            qk = jnp.sum(qt * kt, axis=0, keepdims=True)
            nq = jnp.sum(nt * qt, axis=0, keepdims=True)
            w = dgate * qk
            den = inter * nq + w
            scale = 1.0 / jnp.maximum(jnp.abs(den), jnp.exp(-m_t))
            for base, val in ((SC_INTER, inter), (SC_W, w), (SC_SCALE, scale), (SC_D, dgate)):
                table = jnp.where(r16 == base + hd, val, table)
        full = jnp.concatenate([table, jnp.zeros((LANES - 16, LANES), F32)], axis=0)
        scal_ref[...] = full.T

    sink = jnp.zeros((8, 1), F32)
    for rrow, hd in enumerate(HEAD_ROW_ORDER):
        sink = jnp.where(r8c == rrow, sinks_ref[hd], sink)
    sbias = sbias_scr[...]
    lane_w = lax.broadcasted_iota(jnp.int32, (WINDOW, LANES), 1)
    last_lane = lane_w == WINDOW - 1
    r256 = lax.broadcasted_iota(jnp.int32, (8, 2 * LANES), 0)
    l256 = lax.broadcasted_iota(jnp.int32, (8, 2 * LANES), 1)
    own64 = (l256 // HEAD_DIM) == r256
    r512 = lax.broadcasted_iota(jnp.int32, (8, 4 * LANES), 0)
    l512 = lax.broadcasted_iota(jnp.int32, (8, 4 * LANES), 1)
    own128 = (l512 // LANES) == r512
    l256r = lax.broadcasted_iota(jnp.int32, (1, 2 * LANES), 1) // HEAD_DIM
    lo1 = lax.broadcasted_iota(jnp.int32, (1, LANES), 1) < HEAD_DIM

    for bi in range(SK2_BB):
        b = i * SK2_BB + bi
        srow = scal_ref[pl.ds(b, 1), :]
        qp = [z_ref[bi:bi + 1, QA + 128 * p:QA + 128 * p + 128] * (HEAD_DIM ** -0.5) for p in range(4)]
        qpr = [pltpu.roll(x, HEAD_DIM, 1) for x in qp]
        z1 = jnp.zeros((1, LANES), F32)
        rows = [jnp.where(lo1, qp[0], z1), jnp.where(lo1, qp[1], z1),
                jnp.where(lo1, z1, qpr[2]), jnp.where(lo1, z1, qpr[3]),
                jnp.where(lo1, qpr[0], z1), jnp.where(lo1, qpr[1], z1),
                jnp.where(lo1, z1, qp[2]), jnp.where(lo1, z1, qp[3])]
        qm = jnp.zeros((8, LANES), F32)
        for r in range(8):
            qm = jnp.where(r8 == r, rows[r], qm)
        knew = z_ref[bi:bi + 1, KA:KA + 128]
        vnew = z_ref[bi:bi + 1, VA:VA + 128]
        kt_b = kt_ref[bi]
        vt_b = vt_ref[bi]
        s_old = _dot(qm.astype(BF16), kt_b.astype(BF16))
        s_new = jnp.sum(qm * knew, axis=1, keepdims=True)
        s = jnp.where(l8 == WINDOW - 1, s_new, pltpu.roll(s_old, WINDOW - 1, 1)) + sbias
        mx = jnp.maximum(jnp.max(s, axis=-1, keepdims=True), sink)
        pe = jnp.exp(s - mx)
        den = jnp.sum(pe, axis=-1, keepdims=True) + jnp.exp(sink - mx)
        pn = pe * (1.0 / den)
        p_old = jnp.where(l8 == 0, 0.0, pltpu.roll(pn, 1, 1))
        oa = _dot_nt(p_old.astype(BF16), vt_b.astype(BF16)) + pn[:, WINDOW - 1:WINDOW] * vnew
        oar = pltpu.roll(oa, HEAD_DIM, 1)
        pairs = [jnp.where(lo1, oa[0:1], oar[4:5]), jnp.where(lo1, oa[1:2], oar[5:6]),
                 jnp.where(lo1, oar[2:3], oa[6:7]), jnp.where(lo1, oar[3:4], oa[7:8])]
        for p in range(4):
            yatt_ref[bi:bi + 1, 128 * p:128 * p + 128] = pairs[p]
        shift = WINDOW - 1 - b
        kcol = pltpu.roll(zt_ref[ZT_KA:ZT_KA + 128, :], shift, 1)
        vcol = pltpu.roll(zt_ref[ZT_VA:ZT_VA + 128, :], shift, 1)
        ko_ref[bi] = jnp.where(last_lane, kcol, pltpu.roll(kt_b, WINDOW - 1, 1))
        vo_ref[bi] = jnp.where(last_lane, vcol, pltpu.roll(vt_b, WINDOW - 1, 1))

        qm_row = z_ref[bi:bi + 1, QM:QM + 256]
        km_row = z_ref[bi:bi + 1, KM:KM + 256] * (HEAD_DIM ** -0.5)
        vm_row = z_ref[bi:bi + 1, VM:VM + 512]
        qbd = jnp.where(own64, qm_row, 0.0)
        ct_all = jnp.concatenate([ct_ref[bi, hd] for hd in range(M_HEADS)], axis=0)
        cq = _dot(qbd.astype(BF16), ct_all.astype(BF16))
        for hd in range(M_HEADS):
            cq_ref[bi:bi + 1, 128 * hd:128 * hd + 128] = cq[hd:hd + 1, :]
        dsel = jnp.zeros((1, 2 * LANES), F32)
        for hd in range(M_HEADS):
            dsel = jnp.where(l256r == hd, srow[:, SC_D + hd:SC_D + hd + 1], dsel)
        a_mat = jnp.where(own64, km_row * dsel, 0.0)
        b_mat = jnp.where(own128, vm_row, 0.0)
        for pr in range(2):
            upd = _dot_tn(a_mat[:, 128 * pr:128 * pr + 128], b_mat[:, 256 * pr:256 * pr + 256])
            for hh in range(2):
                hd = 2 * pr + hh
                co_ref[bi, hd] = (srow[:, SC_INTER + hd:SC_INTER + hd + 1] * ct_ref[bi, hd]
                                  + upd[64 * hh:64 * hh + 64, 128 * hh:128 * hh + 128])


def _sample_mixer_call(z, zt, kt, vt, ct, nt, mt, sbuck, bif2, relb_t, sinks):
    nb = z.shape[0]
    assert nb % SK2_BB == 0 and nb == LANES
    bb = SK2_BB
    out_shapes = (
        jax.ShapeDtypeStruct((nb, 512), F32),
        jax.ShapeDtypeStruct((nb, 512), F32),
        jax.ShapeDtypeStruct(kt.shape, F32),
        jax.ShapeDtypeStruct(vt.shape, F32),
        jax.ShapeDtypeStruct(ct.shape, F32),
        jax.ShapeDtypeStruct(nt.shape, F32),
        jax.ShapeDtypeStruct(mt.shape, F32),
        jax.ShapeDtypeStruct((nb, LANES), F32),
    )
    blk = lambda s: pl.BlockSpec((bb,) + tuple(s[1:]), lambda i: (i,) + (0,) * (len(s) - 1))
    full = lambda s: pl.BlockSpec(tuple(s), lambda i: (0,) * len(s))
    in_specs = [blk(z.shape), full(zt.shape), blk(kt.shape), blk(vt.shape), blk(ct.shape),
                full(nt.shape), full(mt.shape), full(sbuck.shape), _smem(), _smem(), _smem()]
    out_specs = (blk((nb, 512)), blk((nb, 512)), blk(kt.shape), blk(vt.shape), blk(ct.shape),
                 full(nt.shape), full(mt.shape), full((nb, LANES)))
    return pl.pallas_call(
        _sample_mixer_kernel,
        grid=(nb // bb,),
        in_specs=in_specs,
        out_specs=out_specs,
        out_shape=out_shapes,
        scratch_shapes=[pltpu.VMEM((8, LANES), F32)],
        compiler_params=pltpu.CompilerParams(dimension_semantics=("arbitrary",)),
        name="sample_mixers",
    )(z, zt, kt, vt, ct, nt, mt, sbuck, bif2, relb_t, sinks)


FFN_CHUNK = 256


def _sample_tail_kernel(x_ref, z_ref, yatt_ref, cq_ref, scal_ref, ghead_ref, wao_ref, wmo_ref, wout_ref,
                        wg_ref, wu_ref, wd_ref, gffn_ref, gfin_ref, y_ref,
                        x1_scr, h2_scr, acc_scr):
    c = pl.program_id(0)

    @pl.when(c == 0)
    def _mix():
        scal = scal_ref[...]
        yms = []
        for hd in range(M_HEADS):
            inter = scal[:, SC_INTER + hd:SC_INTER + hd + 1]
            w = scal[:, SC_W + hd:SC_W + hd + 1]
            scale = scal[:, SC_SCALE + hd:SC_SCALE + hd + 1]
            hv = (inter * cq_ref[:, 128 * hd:128 * hd + 128]
                  + w * z_ref[:, VM + 128 * hd:VM + 128 * hd + 128]) * scale
            hn = hv * lax.rsqrt(jnp.mean(hv * hv, axis=-1, keepdims=True) + EPS)
            hn = hn * ghead_ref[:, 128 * hd:128 * hd + 128]
            yms.append((hn * _sigmoid(z_ref[:, OM + 128 * hd:OM + 128 * hd + 128])).astype(BF16))
        ya = _dot(yatt_ref[...].astype(BF16), wao_ref[...])
        ymm = _dot(jnp.concatenate(yms, axis=1), wmo_ref[...])
        mixed = _sigmoid(z_ref[:, GA:GA + 1024]) * ya + _sigmoid(z_ref[:, GM:GM + 1024]) * ymm
        x1 = x_ref[...] + _dot(mixed.astype(BF16), wout_ref[...])
        x1_scr[...] = x1
        h2_scr[...] = _rms(x1, gffn_ref[...]).astype(BF16)
        acc_scr[...] = jnp.zeros_like(acc_scr)

    h2 = h2_scr[...]
    g = _dot(h2, wg_ref[...])
    u = _dot(h2, wu_ref[...])
    acc_scr[...] += _dot((g * _sigmoid(g) * u).astype(BF16), wd_ref[...])

    @pl.when(c == pl.num_programs(0) - 1)
    def _finish():
        y_ref[...] = _rms(x1_scr[...] + acc_scr[...], gfin_ref[...])


def _sample_tail_call(x3, z, yatt, cq, scal, ghead, wao, wmo, wout, wg, wu, wd, gffn, gfin):
    n, _, d = x3.shape
    d_ff = wg.shape[1]
    assert d_ff % FFN_CHUNK == 0
    full = lambda a: pl.BlockSpec(a.shape, lambda c: (0,) * a.ndim)
    x_spec = pl.BlockSpec((n, None, d), lambda c: (0, 0, 0))
    return pl.pallas_call(
        _sample_tail_kernel,
        grid=(d_ff // FFN_CHUNK,),
        in_specs=[x_spec, full(z), full(yatt), full(cq), full(scal), full(ghead),
                  full(wao), full(wmo), full(wout),
                  pl.BlockSpec((d, FFN_CHUNK), lambda c: (0, c)),
                  pl.BlockSpec((d, FFN_CHUNK), lambda c: (0, c)),
                  pl.BlockSpec((FFN_CHUNK, d), lambda c: (c, 0)),
                  full(gffn), full(gfin)],
        out_specs=x_spec,
        out_shape=jax.ShapeDtypeStruct((n, 1, d), F32),
        scratch_shapes=[pltpu.VMEM((n, d), F32), pltpu.VMEM((n, d), BF16), pltpu.VMEM((n, d), F32)],
        compiler_params=pltpu.CompilerParams(dimension_semantics=("arbitrary",),
                                             vmem_limit_bytes=VMEM_LIMIT_BYTES),
        name="sample_tail",
    )(x3, z, yatt, cq, scal, ghead, wao, wmo, wout, wg, wu, wd, gffn, gfin)


def kernel(x_prompt, x_sample, cache_k_win, cache_v_win, state_mlstm_C, state_mlstm_n, state_mlstm_m,
           rel_bias, w_in, b_if, sinks, g_attn_norm, g_head, w_att_out, w_mlstm_out, w_out,
           g_ffn_norm, w_gate, w_up, w_down, g_final):
    depth = w_in.shape[0]
    assert depth == 1
    bsz, s_len, d = x_prompt.shape
    nb = x_sample.shape[0]
    assert x_sample.shape[1] == 1 and cache_k_win.shape[2] == WINDOW

    w = w_in[0]
    wt = w.T
    wcat = jnp.concatenate([wt[:2312], jnp.zeros((N_CAT - 4360, d), w.dtype), wt[2312:]],
                           axis=0).astype(BF16)
    wao = w_att_out[0].astype(BF16)
    wmo = w_mlstm_out[0].astype(BF16)
    wout = w_out[0].astype(BF16)
    wg = w_gate[0].astype(BF16)
    wu = w_up[0].astype(BF16)
    wd = w_down[0].astype(BF16)
    gattn = g_attn_norm[0].reshape(1, d)
    gffn = g_ffn_norm[0].reshape(1, d)
    gfin = g_final.reshape(1, d)
    ghead = g_head[0].reshape(1, 512)
    bif = jnp.concatenate([b_if[0].reshape(1, 2 * M_HEADS), jnp.zeros((1, LANES - 2 * M_HEADS), F32)], axis=1)
    sink_v = sinks[0]
    relb_t = rel_bias.T

    qi = jnp.arange(WINDOW)[:, None]
    kj = jnp.arange(WINDOW)[None, :]
    fbuck = _t5_bucket(jnp.where(kj <= qi, qi - kj, WINDOW + qi - kj)).astype(jnp.int32)
    sbuck = jnp.broadcast_to(_t5_bucket(WINDOW - 1 - kj), (8, WINDOW)).astype(jnp.int32)

    y_p, kt_p, vt_p, ct_p, n_p, m_p = _prompt_call(
        x_prompt, wcat, wao, wmo, wout, wg, wu, wd, gattn, gffn, gfin, ghead, bif, fbuck, relb_t, sink_v)

    feat = N_KV_HEADS * HEAD_DIM
    kt_in = jnp.transpose(cache_k_win[0], (0, 2, 3, 1)).reshape(nb, feat, WINDOW)
    vt_in = jnp.transpose(cache_v_win[0], (0, 2, 3, 1)).reshape(nb, feat, WINDOW)
    ct_in = jnp.transpose(state_mlstm_C[0], (0, 1, 3, 2))
    nt_in = jnp.transpose(state_mlstm_n[0], (1, 2, 0))
    mt_in = state_mlstm_m[0].T
    z_s, zt_s = _sample_inproj_call(x_sample, gattn, wt)
    yatt_s, cq_s, kt_s, vt_s, ct_s, nt_s, mt_s, scal_s = _sample_mixer_call(
        z_s, zt_s, kt_in, vt_in, ct_in, nt_in, mt_in, sbuck, b_if[0], relb_t, sink_v)
    y_s = _sample_tail_call(x_sample, z_s, yatt_s, cq_s, scal_s, ghead,
                            wao, wmo, wout, wg, wu, wd, gffn, gfin)

    def window_out(t, n):
        return jnp.transpose(t.reshape(n, N_KV_HEADS, HEAD_DIM, WINDOW), (0, 3, 1, 2))[None]

    return (y_p, y_s,
            window_out(kt_p, bsz), window_out(vt_p, bsz),
            jnp.transpose(ct_p, (0, 1, 3, 2))[None], n_p[None], m_p.reshape(1, bsz, M_HEADS),
            window_out(kt_s, nb), window_out(vt_s, nb),
            jnp.transpose(ct_s, (0, 1, 3, 2))[None], jnp.transpose(nt_s, (2, 0, 1))[None], mt_s.T[None])
```

```python
import math

import jax
import jax.numpy as jnp
from jax import lax
from jax.experimental import pallas as pl
from jax.experimental.pallas import tpu as pltpu

F32 = jnp.float32
BF16 = jnp.bfloat16

HEAD_DIM = 64
N_Q_HEADS = 8
N_KV_HEADS = 2
WINDOW = 128
N_BUCKETS = 32
MAX_DISTANCE = WINDOW
M_HEADS = 4
EPS = 1e-6
NEG = -1e30

LANES = 128
VMEM_LIMIT_BYTES = 58 * 1024 * 1024

QA, KA, VA, QM, KM, VM, OM, IF, GA, GM = 0, 512, 640, 768, 1024, 1280, 1792, 2304, 2432, 3456
N_CAT = 4480
Z_GROUPS = ((0, 768), (768, 2432), (2432, 4480))
T_BLK = 256
N_SUB = 2


def _dot(a, b):
    return jnp.dot(a, b, preferred_element_type=F32)


def _dot_nt(a, b):
    return lax.dot_general(a, b, (((1,), (1,)), ((), ())), preferred_element_type=F32)


def _sigmoid(x):
    return 1.0 / (1.0 + jnp.exp(-x))


def _log_sigmoid(x):
    return jnp.minimum(x, 0.0) - jnp.log1p(jnp.exp(-jnp.abs(x)))


def _rms(x, g):
    ms = jnp.mean(x * x, axis=-1, keepdims=True)
    return x * lax.rsqrt(ms + EPS) * g


def _t5_bucket(dist):
    n = jnp.maximum(dist, 0)
    max_exact = N_BUCKETS // 2
    nf = jnp.maximum(n, 1).astype(F32)
    large = max_exact + jnp.floor(jnp.log(nf / max_exact) / math.log(MAX_DISTANCE / max_exact)
                                  * (N_BUCKETS - max_exact)).astype(jnp.int32)
    large = jnp.minimum(large, N_BUCKETS - 1)
    return jnp.where(n < max_exact, n, large)


def _bias_from_buckets(buckets, relb_ref, head):
    acc = jnp.zeros(buckets.shape, F32)
    for k in range(N_BUCKETS):
        acc = jnp.where(buckets == k, relb_ref[head, k], acc)
    return acc


def _prompt_kernel(x_ref, wcat_ref, wao_ref, wmo_ref, wout_ref, wg_ref, wu_ref, wd_ref,
                   gattn_ref, gffn_ref, gfin_ref, ghead_ref, bif_ref, fbuck_ref, relb_ref, sinks_ref,
                   y_ref, kwin_ref, vwin_ref, c_ref, n_ref, m_ref,
                   z_scr, yatt_scr, ym_scr, a_scr, bias_scr, kprev_scr, vprev_scr, cbd_scr, st_scr):
    T = T_BLK
    b = pl.program_id(0)
    j = pl.program_id(1)
    nj = pl.num_programs(1)

    @pl.when(jnp.logical_and(b == 0, j == 0))
    def _build_bias():
        fb = fbuck_ref[...]
        for h in range(N_Q_HEADS):
            bias_scr[h] = _bias_from_buckets(fb, relb_ref, h)

    @pl.when(j == 0)
    def _reset_state():
        kprev_scr[...] = jnp.zeros_like(kprev_scr)
        vprev_scr[...] = jnp.zeros_like(vprev_scr)
        cbd_scr[...] = jnp.zeros_like(cbd_scr)
        st_scr[...] = jnp.zeros_like(st_scr)

    for s in range(N_SUB):
        _in_projection(x_ref.at[s * T:(s + 1) * T, :], gattn_ref, wcat_ref, z_scr.at[s])

    carry = _load_carry(kprev_scr, vprev_scr, cbd_scr, st_scr)
    for s in range(N_SUB):
        carry = _mixers(z_scr.at[s], yatt_scr.at[s], ym_scr.at[s], carry, ghead_ref, bif_ref, bias_scr,
                        sinks_ref, first_of_sequence=(j == 0) if s == 0 else None)
    _store_carry(carry, kprev_scr, vprev_scr, cbd_scr, st_scr)

    for s in range(N_SUB):
        _tail(x_ref.at[s * T:(s + 1) * T, :], z_scr.at[s], yatt_scr.at[s], ym_scr.at[s], a_scr.at[s],
              wao_ref, wmo_ref, wout_ref, wg_ref, wu_ref, wd_ref, gffn_ref, gfin_ref,
              y_ref.at[s * T:(s + 1) * T, :])

    @pl.when(j == nj - 1)
    def _write_state():
        kwin_ref[...] = kprev_scr[...].T
        vwin_ref[...] = vprev_scr[...].T
        stn = st_scr[...]
        cts = [cbd_scr[0].T, cbd_scr[1].T]
        for hd in range(M_HEADS):
            p, hh = hd // 2, hd % 2
            c_ref[hd] = cts[p][64 * hh:64 * hh + 64, 128 * hh:128 * hh + 128]
            n_ref[hd:hd + 1, :] = stn[p:p + 1, 64 * hh:64 * hh + 64]
            m_ref[0:1, hd:hd + 1] = stn[2 + hd:3 + hd, 0:1]


def _in_projection(x_ref, gattn_ref, wcat_ref, z_ref):
    h = _rms(x_ref[...], gattn_ref[...]).astype(BF16)
    for c0, c1 in Z_GROUPS:
        z_ref[:, c0:c1] = _dot_nt(h, wcat_ref[c0:c1, :])


def _load_carry(kprev_scr, vprev_scr, cbd_scr, st_scr):
    st = st_scr[...]
    return dict(kp=kprev_scr[...], vp=vprev_scr[...], cbd=[cbd_scr[0], cbd_scr[1]],
                n=[st[0:1, :], st[1:2, :]], m=[st[2 + hd:3 + hd, 0:1] for hd in range(M_HEADS)])


def _store_carry(carry, kprev_scr, vprev_scr, cbd_scr, st_scr):
    kprev_scr[...] = carry["kp"]
    vprev_scr[...] = carry["vp"]
    for p in range(2):
        cbd_scr[p] = carry["cbd"][p]
        st_scr[p:p + 1, :] = carry["n"][p]
    for hd in range(M_HEADS):
        st_scr[2 + hd:3 + hd, :] = jnp.broadcast_to(carry["m"][hd], (1, LANES))


def _tail(x_ref, z_ref, yatt_ref, ym_ref, a_ref, wao_ref, wmo_ref, wout_ref, wg_ref, wu_ref, wd_ref,
          gffn_ref, gfin_ref, y_ref):
    d_ff = wg_ref.shape[1]
    ya = _dot(yatt_ref[...], wao_ref[...])
    ymm = _dot(ym_ref[...], wmo_ref[...])
    mixed = _sigmoid(z_ref[:, GA:GA + 1024]) * ya + _sigmoid(z_ref[:, GM:GM + 1024]) * ymm
    x1 = x_ref[...] + _dot(mixed.astype(BF16), wout_ref[...])
    h2 = _rms(x1, gffn_ref[...]).astype(BF16)
    for c0 in range(0, d_ff, 256):
        g = _dot(h2, wg_ref[:, c0:c0 + 256])
        u = _dot(h2, wu_ref[:, c0:c0 + 256])
        a_ref[:, c0:c0 + 256] = (g * _sigmoid(g) * u).astype(BF16)
    x2 = x1 + _dot(a_ref[...], wd_ref[...])
    y_ref[...] = _rms(x2, gfin_ref[...])


def _mixers(z_ref, yatt_ref, ym_ref, carry, ghead_ref, bif_ref, bias_scr, sinks_ref, first_of_sequence):
    T = T_BLK
    lane = lax.broadcasted_iota(jnp.int32, (T, LANES), 1)
    lane_lo = lane < HEAD_DIM
    row = lax.broadcasted_iota(jnp.int32, (T, LANES), 0)
    qi = jnp.where(row < 128, row, row - 128)
    tri2 = lane <= qi
    if first_of_sequence is not None:
        valid0 = lane <= qi + jnp.where(first_of_sequence, 0, 2 * LANES)
    rowc = lax.broadcasted_iota(jnp.int32, (T, 1), 0)
    k_all = z_ref[:, KA:KA + 128]
    v_all = z_ref[:, VA:VA + 128]
    for sb in range(2):
        r0 = 128 * sb
        if sb == 0:
            kp, vp = carry["kp"], carry["vp"]
        else:
            kp, vp = k_all[0:128], v_all[0:128]
        kcat = jnp.concatenate([kp, k_all[r0:r0 + 128]], axis=0)
        vcat = jnp.concatenate([vp, v_all[r0:r0 + 128]], axis=0)
        kroll = pltpu.roll(kcat, HEAD_DIM, 1)
        vroll = pltpu.roll(vcat, HEAD_DIM, 1)
        zero = jnp.zeros_like(kcat)
        k_mats = [jnp.where(lane_lo, kcat, zero), jnp.where(lane_lo, zero, kroll),
                  jnp.where(lane_lo, kroll, zero), jnp.where(lane_lo, zero, kcat)]
        v_mats = [jnp.where(lane_lo, vcat, zero), jnp.where(lane_lo, zero, vroll),
                  jnp.where(lane_lo, vroll, zero), jnp.where(lane_lo, zero, vcat)]
        q = (z_ref[r0:r0 + 128, QA:QA + 512] * (HEAD_DIM ** -0.5)).astype(BF16)
        lhs_a = jnp.concatenate([q[:, 0:128], q[:, 128:256]], axis=0)
        lhs_b = jnp.concatenate([q[:, 256:384], q[:, 384:512]], axis=0)
        groups = [(lhs_a, 0, 0, 2), (lhs_a, 1, 1, 3), (lhs_b, 2, 4, 6), (lhs_b, 3, 5, 7)]
        pcs = []
        for lhs, mi, ha, hb in groups:
            s = _dot_nt(lhs, k_mats[mi].astype(BF16))
            sf = jnp.where(tri2, s[:, 128:256], s[:, 0:128])
            sf = sf + jnp.concatenate([bias_scr[ha], bias_scr[hb]], axis=0)
            if sb == 0 and first_of_sequence is not None:
                sf = jnp.where(valid0, sf, NEG)
            sink = jnp.where(rowc < 128, sinks_ref[ha], sinks_ref[hb])
            mx = jnp.maximum(jnp.max(sf, axis=-1, keepdims=True), sink)
            p = jnp.exp(sf - mx)
            den = jnp.sum(p, axis=-1, keepdims=True) + jnp.exp(sink - mx)
            pn = p * (1.0 / den)
            zp = jnp.zeros_like(pn)
            pcs.append(jnp.concatenate([jnp.where(tri2, zp, pn), jnp.where(tri2, pn, zp)],
                                       axis=1).astype(BF16))
        o_a = _dot(jnp.concatenate([pcs[0], pcs[1]], axis=1),
                   jnp.concatenate([v_mats[0], v_mats[1]], axis=0).astype(BF16))
        o_b = _dot(jnp.concatenate([pcs[2], pcs[3]], axis=1),
                   jnp.concatenate([v_mats[2], v_mats[3]], axis=0).astype(BF16))
        yatt_ref[r0:r0 + 128, 0:128] = o_a[0:128].astype(BF16)
        yatt_ref[r0:r0 + 128, 128:256] = o_a[128:256].astype(BF16)
        yatt_ref[r0:r0 + 128, 256:384] = o_b[0:128].astype(BF16)
        yatt_ref[r0:r0 + 128, 384:512] = o_b[128:256].astype(BF16)
    new_carry = dict(kp=k_all[128:256], vp=v_all[128:256], cbd=[None, None], n=[None, None],
                     m=[None] * M_HEADS)

    zif = z_ref[:, IF:IF + 128] + bif_ref[...]
    gl = jnp.where(lane < M_HEADS, zif, _log_sigmoid(zif))
    gl_t = gl.T
    tr = lax.broadcasted_iota(jnp.int32, (T, T), 0)
    ts = lax.broadcasted_iota(jnp.int32, (T, T), 1)
    tril = ts <= tr
    triu = tr <= ts
    row2 = lax.broadcasted_iota(jnp.int32, (2 * LANES, LANES), 0)
    lane2 = lax.broadcasted_iota(jnp.int32, (2 * LANES, LANES), 1)
    bd_mask = (row2 < LANES) == (lane2 < HEAD_DIM)
    for p in range(2):
        q_pair = z_ref[:, QM + 128 * p:QM + 128 * p + 128]
        k_pair = z_ref[:, KM + 128 * p:KM + 128 * p + 128] * (HEAD_DIM ** -0.5)
        v_pair = z_ref[:, VM + 256 * p:VM + 256 * p + 256]
        cbd = carry["cbd"][p]
        n_pair = carry["n"][p]
        q_bf = q_pair.astype(BF16)
        qc = _dot_nt(q_bf, cbd.astype(BF16))
        qn_prod = q_pair * n_pair
        ws, decays, m_ends = [], [], []
        for hh in range(2):
            hd = 2 * p + hh
            hmask = lane_lo if hh == 0 else jnp.logical_not(lane_lo)
            ig_c = gl[:, hd:hd + 1]
            lf_c = gl[:, M_HEADS + hd:M_HEADS + hd + 1]
            ig_r = gl_t[hd:hd + 1, :]
            lf_r = gl_t[M_HEADS + hd:M_HEADS + hd + 1, :]
            m_prev = carry["m"][hd]
            b_c = jnp.sum(jnp.where(tril, lf_r, 0.0), axis=1, keepdims=True)
            b_r = jnp.sum(jnp.where(triu, lf_c, 0.0), axis=0, keepdims=True)
            a_r = ig_r - b_r
            cm_c = jnp.max(jnp.where(tril, a_r, NEG), axis=1, keepdims=True)
            mt_c = b_c + jnp.maximum(m_prev, cm_c)
            g_c = b_c - mt_c
            dm = jnp.exp(jnp.where(tril, a_r + g_c, NEG))
            inter_c = jnp.exp(m_prev + g_c)
            k_h = jnp.where(hmask, k_pair, 0.0).astype(BF16)
            w = dm * _dot_nt(q_bf, k_h)
            v_h = v_pair[:, 128 * hh:128 * hh + 128]
            num = inter_c * qc[:, 128 * hh:128 * hh + 128] + _dot(w.astype(BF16), v_h.astype(BF16))
            qn = jnp.sum(jnp.where(hmask, qn_prod, 0.0), axis=1, keepdims=True)
            den = inter_c * qn + jnp.sum(w, axis=1, keepdims=True)
            hv = num / jnp.maximum(jnp.abs(den), jnp.exp(-mt_c))
            hn = hv * lax.rsqrt(jnp.mean(hv * hv, axis=-1, keepdims=True) + EPS)
            hn = hn * ghead_ref[:, 128 * hd:128 * hd + 128]
            om = z_ref[:, OM + 128 * hd:OM + 128 * hd + 128]
            ym_ref[:, 128 * hd:128 * hd + 128] = (hn * _sigmoid(om)).astype(BF16)
            b_end = b_c[T - 1:T, :]
            m_end = mt_c[T - 1:T, :]
            ws.append(jnp.exp((ig_c - b_c) + b_end - m_end))
            decays.append(jnp.exp(m_prev + b_end - m_end))
            m_ends.append(m_end)
        kw = k_pair * jnp.where(lane_lo, ws[0], ws[1])
        upd = _dot(v_pair.T.astype(BF16), kw.astype(BF16))
        dec_rows = jnp.where(row2[:, 0:1] < LANES, decays[0], decays[1])
        new_carry["cbd"][p] = dec_rows * cbd + jnp.where(bd_mask, upd, 0.0)
        dec_lanes = jnp.where(lane_lo[0:1, :], decays[0], decays[1])
        new_carry["n"][p] = dec_lanes * n_pair + jnp.sum(kw, axis=0, keepdims=True)
        for hh in range(2):
            new_carry["m"][2 * p + hh] = m_ends[hh]
    return new_carry


def _resident(shape):
    zeros = (0,) * len(shape)
    return pl.BlockSpec(shape, lambda b, j: zeros, pipeline_mode=pl.Buffered(1))


def _smem():
    return pl.BlockSpec(memory_space=pltpu.SMEM)


def _prompt_call(x, wcat, wao, wmo, wout, wg, wu, wd, gattn, gffn, gfin, ghead, bif, fbuck, relb, sinks):
    bsz, s_len, d = x.shape
    t_step = T_BLK * N_SUB
    assert s_len % t_step == 0 and d == 1024
    d_ff = wg.shape[1]
    nj = s_len // t_step
    out_shapes = (
        jax.ShapeDtypeStruct((bsz, s_len, d), F32),
        jax.ShapeDtypeStruct((bsz, WINDOW, 128), F32),
        jax.ShapeDtypeStruct((bsz, WINDOW, 128), F32),
        jax.ShapeDtypeStruct((bsz, M_HEADS, 64, 128), F32),
        jax.ShapeDtypeStruct((bsz, M_HEADS, 64), F32),
        jax.ShapeDtypeStruct((bsz, 1, M_HEADS), F32),
    )
    in_specs = [
        pl.BlockSpec((None, t_step, d), lambda b, j: (b, j, 0)),
        _resident(wcat.shape), _resident(wao.shape), _resident(wmo.shape), _resident(wout.shape),
        _resident(wg.shape), _resident(wu.shape), _resident(wd.shape),
        _resident(gattn.shape), _resident(gffn.shape), _resident(gfin.shape), _resident(ghead.shape),
        _resident(bif.shape), _resident(fbuck.shape), _smem(), _smem(),
    ]
    out_specs = (
        pl.BlockSpec((None, t_step, d), lambda b, j: (b, j, 0)),
        pl.BlockSpec((None, WINDOW, 128), lambda b, j: (b, 0, 0)),
        pl.BlockSpec((None, WINDOW, 128), lambda b, j: (b, 0, 0)),
        pl.BlockSpec((None, M_HEADS, 64, 128), lambda b, j: (b, 0, 0, 0)),
        pl.BlockSpec((None, M_HEADS, 64), lambda b, j: (b, 0, 0)),
        pl.BlockSpec((None, 1, M_HEADS), lambda b, j: (b, 0, 0)),
    )
    scratch = [
        pltpu.VMEM((N_SUB, T_BLK, N_CAT), F32),
        pltpu.VMEM((N_SUB, T_BLK, 512), BF16),
        pltpu.VMEM((N_SUB, T_BLK, 512), BF16),
        pltpu.VMEM((N_SUB, T_BLK, d_ff), BF16),
        pltpu.VMEM((N_Q_HEADS, 128, 128), F32),
        pltpu.VMEM((128, 128), F32),
        pltpu.VMEM((128, 128), F32),
        pltpu.VMEM((2, 256, 128), F32),
        pltpu.VMEM((8, 128), F32),
    ]
    return pl.pallas_call(
        _prompt_kernel,
        grid=(bsz, nj),
        in_specs=in_specs,
        out_specs=out_specs,
        out_shape=out_shapes,
        scratch_shapes=scratch,
        compiler_params=pltpu.CompilerParams(
            dimension_semantics=("arbitrary", "arbitrary"),
            vmem_limit_bytes=VMEM_LIMIT_BYTES),
        name="prompt_layer",
    )(x, wcat, wao, wmo, wout, wg, wu, wd, gattn, gffn, gfin, ghead, bif, fbuck, relb, sinks)


W_QA, W_KA, W_VA, W_QM, W_KM, W_VM, W_OM, W_IF, W_GA, W_GM, W_END = (
    0, 512, 640, 768, 1024, 1280, 1792, 2304, 2312, 3336, 4360)
ZT_KA, ZT_VA, ZT_QM, ZT_KM, ZT_IF, ZT_ROWS = 0, 128, 256, 512, 768, 776


def _sample_inproj_kernel(x_ref, g_ref, wt_ref, nt_ref, mt_ref, bif_ref,
                          z_ref, zt_ref, no_ref, mo_ref, scal_ref):
    _sample_projection(x_ref, g_ref, wt_ref, z_ref, zt_ref)
    r8c = lax.broadcasted_iota(jnp.int32, (8, 1), 0)
    bias_col = jnp.zeros((8, 1), F32)
    for r in range(8):
        bias_col = jnp.where(r8c == r, bif_ref[r // M_HEADS, r % M_HEADS], bias_col)
    ift = zt_ref[ZT_IF:ZT_ROWS, :] + bias_col
    m0 = mt_ref[...]
    r16 = lax.broadcasted_iota(jnp.int32, (16, LANES), 0)
    table = jnp.zeros((16, LANES), F32)
    for hd in range(M_HEADS):
        ig = ift[hd:hd + 1, :]
        lf = _log_sigmoid(ift[M_HEADS + hd:M_HEADS + hd + 1, :])
        m_prev = m0[hd:hd + 1, :]
        a = ig - lf
        m_t = lf + jnp.maximum(m_prev, a)
        dgate = jnp.exp(a + lf - m_t)
        inter = jnp.exp(m_prev + lf - m_t)
        qt = zt_ref[ZT_QM + 64 * hd:ZT_QM + 64 * hd + 64, :]
        kt = zt_ref[ZT_KM + 64 * hd:ZT_KM + 64 * hd + 64, :] * (HEAD_DIM ** -0.5)
        nt = nt_ref[hd]
        no_ref[hd] = inter * nt + dgate * kt
        mo_ref[hd:hd + 1, :] = m_t
        qk = jnp.sum(qt * kt, axis=0, keepdims=True)
        nq = jnp.sum(nt * qt, axis=0, keepdims=True)
        w = dgate * qk
        den = inter * nq + w
        scale = 1.0 / jnp.maximum(jnp.abs(den), jnp.exp(-m_t))
        for base, val in ((SC_INTER, inter), (SC_W, w), (SC_SCALE, scale), (SC_D, dgate)):
            table = jnp.where(r16 == base + hd, val, table)
    full = jnp.concatenate([table, jnp.zeros((LANES - 16, LANES), F32)], axis=0)
    scal_ref[...] = full.T


def _sample_projection(x_ref, g_ref, wt_ref, z_ref, zt_ref):
    h32 = _rms(x_ref[...], g_ref[...])
    h = h32.astype(BF16)
    ht = h32.T.astype(BF16)
    for a, b, dst in ((W_QA, 768, QA), (768, 1536, 768), (1536, W_IF, 1536),
                      (W_GA, W_GM, GA), (W_GM, W_END, GM)):
        z_ref[:, dst:dst + (b - a)] = _dot_nt(h, wt_ref[a:b, :].astype(BF16))
    w_if = jnp.concatenate([wt_ref[W_IF:W_GA, :], jnp.zeros((LANES - 8, wt_ref.shape[1]), F32)],
                           axis=0).astype(BF16)
    z_ref[:, IF:IF + LANES] = _dot_nt(h, w_if)
    zt_ref[ZT_KA:ZT_QM, :] = _dot(wt_ref[W_KA:W_QM, :].astype(BF16), ht)
    zt_ref[ZT_QM:ZT_IF, :] = _dot(wt_ref[W_QM:W_VM, :].astype(BF16), ht)
    zt_ref[ZT_IF:ZT_ROWS, :] = _dot(w_if, ht)[0:8, :]


def _sample_inproj_call(x3, gattn, wt, nt, mt, bif2):
    n, _, d = x3.shape
    assert n == LANES
    full = lambda s: pl.BlockSpec(tuple(s), lambda i: (0,) * len(s))
    return pl.pallas_call(
        _sample_inproj_kernel,
        grid=(1,),
        in_specs=[pl.BlockSpec((n, None, d), lambda i: (0, 0, 0)),
                  pl.BlockSpec((1, d), lambda i: (0, 0)),
                  pl.BlockSpec(wt.shape, lambda i: (0, 0), pipeline_mode=pl.Buffered(1)),
                  full(nt.shape), full(mt.shape), _smem()],
        out_specs=(full((n, N_CAT)), full((ZT_ROWS, n)), full(nt.shape), full(mt.shape),
                   full((n, LANES))),
        out_shape=(jax.ShapeDtypeStruct((n, N_CAT), F32), jax.ShapeDtypeStruct((ZT_ROWS, n), F32),
                   jax.ShapeDtypeStruct(nt.shape, F32), jax.ShapeDtypeStruct(mt.shape, F32),
                   jax.ShapeDtypeStruct((n, LANES), F32)),
        compiler_params=pltpu.CompilerParams(dimension_semantics=("arbitrary",),
                                             vmem_limit_bytes=VMEM_LIMIT_BYTES),
        name="sample_inproj",
    )(x3, gattn, wt, nt, mt, bif2)


SK2_BB = 8
SC_INTER, SC_W, SC_SCALE, SC_D, SC_COLS = 0, 4, 8, 12, 16
HEAD_ROW_ORDER = (0, 2, 4, 6, 1, 3, 5, 7)


def _sample_mixer_kernel(z_ref, zt_ref, kt_ref, vt_ref, ct_ref, sbuck_ref,
                         scal_ref, relb_ref, sinks_ref,
                         yatt_ref, cq_ref, ko_ref, vo_ref, co_ref,
                         sbias_scr):
    i = pl.program_id(0)
    r8 = lax.broadcasted_iota(jnp.int32, (8, LANES), 0)
    l8 = lax.broadcasted_iota(jnp.int32, (8, LANES), 1)
    r8c = lax.broadcasted_iota(jnp.int32, (8, 1), 0)

    @pl.when(i == 0)
    def _prologue():
        sb = sbuck_ref[...]
        acc = jnp.zeros((8, LANES), F32)
        for rrow, hd in enumerate(HEAD_ROW_ORDER):
            acc = jnp.where(r8 == rrow, _bias_from_buckets(sb, relb_ref, hd), acc)
        sbias_scr[...] = acc

    sink = jnp.zeros((8, 1), F32)
    for rrow, hd in enumerate(HEAD_ROW_ORDER):
        sink = jnp.where(r8c == rrow, sinks_ref[hd], sink)
    sbias = sbias_scr[...]
    lane_w = lax.broadcasted_iota(jnp.int32, (WINDOW, LANES), 1)
    last_lane = lane_w == WINDOW - 1
    r256 = lax.broadcasted_iota(jnp.int32, (8, 2 * LANES), 0)
    l256 = lax.broadcasted_iota(jnp.int32, (8, 2 * LANES), 1)
    own64 = (l256 // HEAD_DIM) == r256
    l512r = lax.broadcasted_iota(jnp.int32, (1, 4 * LANES), 1) // LANES
    lo1 = lax.broadcasted_iota(jnp.int32, (1, LANES), 1) < HEAD_DIM
    row_b = lax.broadcasted_iota(jnp.int32, (LANES, 2 * LANES), 0)
    ktm = [(zt_ref[ZT_KM + 128 * pr:ZT_KM + 128 * pr + 128, :] * (HEAD_DIM ** -0.5)).astype(BF16)
           for pr in range(2)]

    nrow = 8 * SK2_BB
    lst = lax.broadcasted_iota(jnp.int32, (nrow, LANES), 1)
    z1 = jnp.zeros((1, LANES), F32)
    qm_l, so_l, kn_l, vn_l = [], [], [], []
    for bi in range(SK2_BB):
        qp = [z_ref[bi:bi + 1, QA + 128 * p:QA + 128 * p + 128] * (HEAD_DIM ** -0.5) for p in range(4)]
        qpr = [pltpu.roll(x, HEAD_DIM, 1) for x in qp]
        rows = [jnp.where(lo1, qp[0], z1), jnp.where(lo1, qp[1], z1),
                jnp.where(lo1, z1, qpr[2]), jnp.where(lo1, z1, qpr[3]),
                jnp.where(lo1, qpr[0], z1), jnp.where(lo1, qpr[1], z1),
                jnp.where(lo1, z1, qp[2]), jnp.where(lo1, z1, qp[3])]
        qm = jnp.zeros((8, LANES), F32)
        for r in range(8):
            qm = jnp.where(r8 == r, rows[r], qm)
        qm_l.append(qm)
        so_l.append(_dot(qm.astype(BF16), kt_ref[bi].astype(BF16)))
        kn_l.append(jnp.broadcast_to(z_ref[bi:bi + 1, KA:KA + 128], (8, LANES)))
        vn_l.append(jnp.broadcast_to(z_ref[bi:bi + 1, VA:VA + 128], (8, LANES)))
    qm_all = jnp.concatenate(qm_l, axis=0)
    s_old = jnp.concatenate(so_l, axis=0)
    vn_all = jnp.concatenate(vn_l, axis=0)
    s_new = jnp.sum(qm_all * jnp.concatenate(kn_l, axis=0), axis=1, keepdims=True)
    sbias_all = jnp.concatenate([sbias] * SK2_BB, axis=0)
    sink_all = jnp.concatenate([sink] * SK2_BB, axis=0)
    s = jnp.where(lst == WINDOW - 1, s_new, pltpu.roll(s_old, WINDOW - 1, 1)) + sbias_all
    mx = jnp.maximum(jnp.max(s, axis=-1, keepdims=True), sink_all)
    pe = jnp.exp(s - mx)
    den = jnp.sum(pe, axis=-1, keepdims=True) + jnp.exp(sink_all - mx)
    pn = pe * (1.0 / den)
    p_old = jnp.where(lst == 0, 0.0, pltpu.roll(pn, 1, 1))
    oa_l = [_dot_nt(p_old[8 * bi:8 * bi + 8].astype(BF16), vt_ref[bi].astype(BF16))
            for bi in range(SK2_BB)]
    oa = jnp.concatenate(oa_l, axis=0) + pn[:, WINDOW - 1:WINDOW] * vn_all
    oar = pltpu.roll(oa, HEAD_DIM, 1)
    for bi in range(SK2_BB):
        r0 = 8 * bi
        pairs = [jnp.where(lo1, oa[r0:r0 + 1], oar[r0 + 4:r0 + 5]),
                 jnp.where(lo1, oa[r0 + 1:r0 + 2], oar[r0 + 5:r0 + 6]),
                 jnp.where(lo1, oar[r0 + 2:r0 + 3], oa[r0 + 6:r0 + 7]),
                 jnp.where(lo1, oar[r0 + 3:r0 + 4], oa[r0 + 7:r0 + 8])]
        for p in range(4):
            yatt_ref[bi:bi + 1, 128 * p:128 * p + 128] = pairs[p]

    for bi in range(SK2_BB):
        b = i * SK2_BB + bi
        shift = WINDOW - 1 - b
        kcol = pltpu.roll(zt_ref[ZT_KA:ZT_KA + 128, :], shift, 1)
        vcol = pltpu.roll(zt_ref[ZT_VA:ZT_VA + 128, :], shift, 1)
        ko_ref[bi] = jnp.where(last_lane, kcol, pltpu.roll(kt_ref[bi], WINDOW - 1, 1))
        vo_ref[bi] = jnp.where(last_lane, vcol, pltpu.roll(vt_ref[bi], WINDOW - 1, 1))

        qm_row = z_ref[bi:bi + 1, QM:QM + 256]
        vm_row = z_ref[bi:bi + 1, VM:VM + 512]
        qbd = jnp.where(own64, qm_row, 0.0)
        ct_all = jnp.concatenate([ct_ref[bi, hd] for hd in range(M_HEADS)], axis=0)
        cq = _dot(qbd.astype(BF16), ct_all.astype(BF16))
        for hd in range(M_HEADS):
            cq_ref[bi:bi + 1, 128 * hd:128 * hd + 128] = cq[hd:hd + 1, :]
        dsel = jnp.zeros((1, 4 * LANES), F32)
        for hd in range(M_HEADS):
            dsel = jnp.where(l512r == hd, scal_ref[b, SC_D + hd], dsel)
        vs = vm_row * dsel
        for pr in range(2):
            vsel = jnp.where(row_b == b, vs[:, 256 * pr:256 * pr + 256], 0.0).astype(BF16)
            upd = _dot(ktm[pr], vsel)
            for hh in range(2):
                hd = 2 * pr + hh
                co_ref[bi, hd] = (scal_ref[b, SC_INTER + hd] * ct_ref[bi, hd]
                                  + upd[64 * hh:64 * hh + 64, 128 * hh:128 * hh + 128])


def _sample_mixer_call(z, zt, kt, vt, ct, sbuck, scal_small, relb_t, sinks):
    nb = z.shape[0]
    assert nb % SK2_BB == 0 and nb == LANES
    bb = SK2_BB
    out_shapes = (
        jax.ShapeDtypeStruct((nb, 512), F32),
        jax.ShapeDtypeStruct((nb, 512), F32),
        jax.ShapeDtypeStruct(kt.shape, F32),
        jax.ShapeDtypeStruct(vt.shape, F32),
        jax.ShapeDtypeStruct(ct.shape, F32),
    )
    blk = lambda s: pl.BlockSpec((bb,) + tuple(s[1:]), lambda i: (i,) + (0,) * (len(s) - 1))
    full = lambda s: pl.BlockSpec(tuple(s), lambda i: (0,) * len(s))
    in_specs = [blk(z.shape), full(zt.shape), blk(kt.shape), blk(vt.shape), blk(ct.shape),
                full(sbuck.shape), _smem(), _smem(), _smem()]
    out_specs = (blk((nb, 512)), blk((nb, 512)), blk(kt.shape), blk(vt.shape), blk(ct.shape))
    return pl.pallas_call(
        _sample_mixer_kernel,
        grid=(nb // bb,),
        in_specs=in_specs,
        out_specs=out_specs,
        out_shape=out_shapes,
        scratch_shapes=[pltpu.VMEM((8, LANES), F32)],
        compiler_params=pltpu.CompilerParams(dimension_semantics=("arbitrary",)),
        name="sample_mixers",
    )(z, zt, kt, vt, ct, sbuck, scal_small, relb_t, sinks)


FFN_CHUNK = 256


def _sample_tail_kernel(x_ref, z_ref, yatt_ref, cq_ref, scal_ref, ghead_ref, wao_ref, wmo_ref, wout_ref,
                        wg_ref, wu_ref, wd_ref, gffn_ref, gfin_ref, y_ref,
                        x1_scr, h2_scr, acc_scr):
    c = pl.program_id(0)

    @pl.when(c == 0)
    def _mix():
        scal = scal_ref[...]
        yms = []
        for hd in range(M_HEADS):
            inter = scal[:, SC_INTER + hd:SC_INTER + hd + 1]
            w = scal[:, SC_W + hd:SC_W + hd + 1]
            scale = scal[:, SC_SCALE + hd:SC_SCALE + hd + 1]
            hv = (inter * cq_ref[:, 128 * hd:128 * hd + 128]
                  + w * z_ref[:, VM + 128 * hd:VM + 128 * hd + 128]) * scale
            hn = hv * lax.rsqrt(jnp.mean(hv * hv, axis=-1, keepdims=True) + EPS)
            hn = hn * ghead_ref[:, 128 * hd:128 * hd + 128]
            yms.append((hn * _sigmoid(z_ref[:, OM + 128 * hd:OM + 128 * hd + 128])).astype(BF16))
        ya = _dot(yatt_ref[...].astype(BF16), wao_ref[...])
        ymm = _dot(jnp.concatenate(yms, axis=1), wmo_ref[...])
        mixed = _sigmoid(z_ref[:, GA:GA + 1024]) * ya + _sigmoid(z_ref[:, GM:GM + 1024]) * ymm
        x1 = x_ref[...] + _dot(mixed.astype(BF16), wout_ref[...])
        x1_scr[...] = x1
        h2_scr[...] = _rms(x1, gffn_ref[...]).astype(BF16)
        acc_scr[...] = jnp.zeros_like(acc_scr)

    h2 = h2_scr[...]
    g = _dot(h2, wg_ref[...])
    u = _dot(h2, wu_ref[...])
    acc_scr[...] += _dot((g * _sigmoid(g) * u).astype(BF16), wd_ref[...])

    @pl.when(c == pl.num_programs(0) - 1)
    def _finish():
        y_ref[...] = _rms(x1_scr[...] + acc_scr[...], gfin_ref[...])


def _sample_tail_call(x3, z, yatt, cq, scal, ghead, wao, wmo, wout, wg, wu, wd, gffn, gfin):
    n, _, d = x3.shape
    d_ff = wg.shape[1]
    assert d_ff % FFN_CHUNK == 0
    full = lambda a: pl.BlockSpec(a.shape, lambda c: (0,) * a.ndim)
    x_spec = pl.BlockSpec((n, None, d), lambda c: (0, 0, 0))
    return pl.pallas_call(
        _sample_tail_kernel,
        grid=(d_ff // FFN_CHUNK,),
        in_specs=[x_spec, full(z), full(yatt), full(cq), full(scal), full(ghead),
                  full(wao), full(wmo), full(wout),
                  pl.BlockSpec((d, FFN_CHUNK), lambda c: (0, c)),
                  pl.BlockSpec((d, FFN_CHUNK), lambda c: (0, c)),
                  pl.BlockSpec((FFN_CHUNK, d), lambda c: (c, 0)),
                  full(gffn), full(gfin)],
        out_specs=x_spec,
        out_shape=jax.ShapeDtypeStruct((n, 1, d), F32),
        scratch_shapes=[pltpu.VMEM((n, d), F32), pltpu.VMEM((n, d), BF16), pltpu.VMEM((n, d), F32)],
        compiler_params=pltpu.CompilerParams(dimension_semantics=("arbitrary",),
                                             vmem_limit_bytes=VMEM_LIMIT_BYTES),
        name="sample_tail",
    )(x3, z, yatt, cq, scal, ghead, wao, wmo, wout, wg, wu, wd, gffn, gfin)


def kernel(x_prompt, x_sample, cache_k_win, cache_v_win, state_mlstm_C, state_mlstm_n, state_mlstm_m,
           rel_bias, w_in, b_if, sinks, g_attn_norm, g_head, w_att_out, w_mlstm_out, w_out,
           g_ffn_norm, w_gate, w_up, w_down, g_final):
    depth = w_in.shape[0]
    assert depth == 1
    bsz, s_len, d = x_prompt.shape
    nb = x_sample.shape[0]
    assert x_sample.shape[1] == 1 and cache_k_win.shape[2] == WINDOW

    w = w_in[0]
    wt = w.T
    wcat = jnp.concatenate([wt[:2312], jnp.zeros((N_CAT - 4360, d), w.dtype), wt[2312:]],
                           axis=0).astype(BF16)
    wao = w_att_out[0].astype(BF16)
    wmo = w_mlstm_out[0].astype(BF16)
    wout = w_out[0].astype(BF16)
    wg = w_gate[0].astype(BF16)
    wu = w_up[0].astype(BF16)
    wd = w_down[0].astype(BF16)
    gattn = g_attn_norm[0].reshape(1, d)
    gffn = g_ffn_norm[0].reshape(1, d)
    gfin = g_final.reshape(1, d)
    ghead = g_head[0].reshape(1, 512)
    bif = jnp.concatenate([b_if[0].reshape(1, 2 * M_HEADS), jnp.zeros((1, LANES - 2 * M_HEADS), F32)], axis=1)
    sink_v = sinks[0]
    relb_t = rel_bias.T

    qi = jnp.arange(WINDOW)[:, None]
    kj = jnp.arange(WINDOW)[None, :]
    fbuck = _t5_bucket(jnp.where(kj <= qi, qi - kj, WINDOW + qi - kj)).astype(jnp.int32)
    sbuck = jnp.broadcast_to(_t5_bucket(WINDOW - 1 - kj), (8, WINDOW)).astype(jnp.int32)

    y_p, kt_p, vt_p, ct_p, n_p, m_p = _prompt_call(
        x_prompt, wcat, wao, wmo, wout, wg, wu, wd, gattn, gffn, gfin, ghead, bif, fbuck, relb_t, sink_v)

    feat = N_KV_HEADS * HEAD_DIM
    kt_in = jnp.transpose(cache_k_win[0], (0, 2, 3, 1)).reshape(nb, feat, WINDOW)
    vt_in = jnp.transpose(cache_v_win[0], (0, 2, 3, 1)).reshape(nb, feat, WINDOW)
    ct_in = jnp.transpose(state_mlstm_C[0], (0, 1, 3, 2))
    nt_in = jnp.transpose(state_mlstm_n[0], (1, 2, 0))
    mt_in = state_mlstm_m[0].T
    z_s, zt_s, nt_s, mt_s, scal_s = _sample_inproj_call(x_sample, gattn, wt, nt_in, mt_in, b_if[0])
    yatt_s, cq_s, kt_s, vt_s, ct_s = _sample_mixer_call(
        z_s, zt_s, kt_in, vt_in, ct_in, sbuck, scal_s[:, :SC_COLS], relb_t, sink_v)
    y_s = _sample_tail_call(x_sample, z_s, yatt_s, cq_s, scal_s, ghead,
                            wao, wmo, wout, wg, wu, wd, gffn, gfin)

    def window_out(t, n):
        return jnp.transpose(t.reshape(n, N_KV_HEADS, HEAD_DIM, WINDOW), (0, 3, 1, 2))[None]

    return (y_p, y_s,
            window_out(kt_p, bsz), window_out(vt_p, bsz),
            jnp.transpose(ct_p, (0, 1, 3, 2))[None], n_p[None], m_p.reshape(1, bsz, M_HEADS),
            window_out(kt_s, nb), window_out(vt_s, nb),
            jnp.transpose(ct_s, (0, 1, 3, 2))[None], jnp.transpose(nt_s, (2, 0, 1))[None], mt_s.T[None])
```

```python
import functools
import math

import jax
import jax.numpy as jnp
from jax import lax
from jax.experimental import pallas as pl
from jax.experimental.pallas import tpu as pltpu

F32 = jnp.float32
BF16 = jnp.bfloat16

HEAD_DIM = 64
N_Q_HEADS = 8
N_KV_HEADS = 2
WINDOW = 128
N_BUCKETS = 32
MAX_DISTANCE = WINDOW
M_HEADS = 4
EPS = 1e-6
NEG = -1e30

LANES = 128
VMEM_LIMIT_BYTES = 58 * 1024 * 1024

QA, KA, VA, QM, KM, VM, OM, IF, GA, GM = 0, 512, 640, 768, 1024, 1280, 1792, 2304, 2432, 3456
N_CAT = 4480
Z_GROUPS = ((768, 2432), (0, 768), (2432, 4480))
T_BLK = 256
N_SUB = 2


def _dot(a, b):
    return jnp.dot(a, b, preferred_element_type=F32)


def _dot_nt(a, b):
    return lax.dot_general(a, b, (((1,), (1,)), ((), ())), preferred_element_type=F32)


def _sigmoid(x):
    return 1.0 / (1.0 + jnp.exp(-x))


def _log_sigmoid(x):
    return jnp.minimum(x, 0.0) - jnp.log1p(jnp.exp(-jnp.abs(x)))


def _rms(x, g):
    ms = jnp.mean(x * x, axis=-1, keepdims=True)
    return x * lax.rsqrt(ms + EPS) * g


def _t5_bucket(dist):
    n = jnp.maximum(dist, 0)
    max_exact = N_BUCKETS // 2
    nf = jnp.maximum(n, 1).astype(F32)
    large = max_exact + jnp.floor(jnp.log(nf / max_exact) / math.log(MAX_DISTANCE / max_exact)
                                  * (N_BUCKETS - max_exact)).astype(jnp.int32)
    large = jnp.minimum(large, N_BUCKETS - 1)
    return jnp.where(n < max_exact, n, large)


def _bias_from_buckets(buckets, relb_ref, head):
    acc = jnp.zeros(buckets.shape, F32)
    for k in range(N_BUCKETS):
        acc = jnp.where(buckets == k, relb_ref[head, k], acc)
    return acc


def _layer_kernel(nj, x_ref, wcat_ref, wao_ref, wmo_ref, wout_ref, wg_ref, wu_ref, wd_ref,
                  gattn_ref, gffn_ref, gfin_ref, ghead_ref, bif_ref, fbuck_ref, relb_ref, sinks_ref,
                  xs_ref, zs_ref, yatts_ref, cqs_ref, scals_ref,
                  y_ref, kwin_ref, vwin_ref, c_ref, n_ref, m_ref, ys_ref,
                  z_scr, yatt_scr, ym_scr, a_scr, bias_scr, kprev_scr, vprev_scr, cbd_scr, st_scr):
    T = T_BLK
    t = pl.program_id(0)
    n_prompt = pl.num_programs(0) - 1
    is_prompt = t < n_prompt
    j = lax.rem(t, nj)

    @pl.when(t == 0)
    def _build_bias():
        fb = fbuck_ref[...]
        for h in range(N_Q_HEADS):
            bias_scr[h] = _bias_from_buckets(fb, relb_ref, h)

    @pl.when(jnp.logical_and(is_prompt, j == 0))
    def _reset_state():
        kprev_scr[...] = jnp.zeros_like(kprev_scr)
        vprev_scr[...] = jnp.zeros_like(vprev_scr)
        cbd_scr[...] = jnp.zeros_like(cbd_scr)
        st_scr[...] = jnp.zeros_like(st_scr)

    @pl.when(is_prompt)
    def _prompt_step():
        for s in range(N_SUB):
            _in_projection(x_ref.at[s * T:(s + 1) * T, :], gattn_ref, wcat_ref, z_scr.at[s])
        carry = _load_carry(kprev_scr, vprev_scr, cbd_scr, st_scr)
        for s in range(N_SUB):
            carry = _mixers(z_scr.at[s], yatt_scr.at[s], ym_scr.at[s], carry, ghead_ref, bif_ref,
                            bias_scr, sinks_ref, first_of_sequence=(j == 0) if s == 0 else None)
        _store_carry(carry, kprev_scr, vprev_scr, cbd_scr, st_scr)
        for s in range(N_SUB):
            _tail(x_ref.at[s * T:(s + 1) * T, :], z_scr.at[s], yatt_scr.at[s], ym_scr.at[s], a_scr,
                  wao_ref, wmo_ref, wout_ref, wg_ref, wu_ref, wd_ref, gffn_ref, gfin_ref,
                  y_ref.at[s * T:(s + 1) * T, :])

    @pl.when(jnp.logical_not(is_prompt))
    def _sample_step():
        nb = xs_ref.shape[0]
        scal = scals_ref[...]
        yatt_scr[0, 0:nb, :] = yatts_ref[...].astype(BF16)
        for hd in range(M_HEADS):
            inter = scal[:, SC_INTER + hd:SC_INTER + hd + 1]
            w = scal[:, SC_W + hd:SC_W + hd + 1]
            scale = scal[:, SC_SCALE + hd:SC_SCALE + hd + 1]
            hv = (inter * cqs_ref[:, 128 * hd:128 * hd + 128]
                  + w * zs_ref[:, VM + 128 * hd:VM + 128 * hd + 128]) * scale
            hn = hv * lax.rsqrt(jnp.mean(hv * hv, axis=-1, keepdims=True) + EPS)
            hn = hn * ghead_ref[:, 128 * hd:128 * hd + 128]
            om = zs_ref[:, OM + 128 * hd:OM + 128 * hd + 128]
            ym_scr[0, 0:nb, 128 * hd:128 * hd + 128] = (hn * _sigmoid(om)).astype(BF16)
        _tail(xs_ref, zs_ref, yatt_scr.at[0, 0:nb, :], ym_scr.at[0, 0:nb, :], a_scr.at[0:nb, :],
              wao_ref, wmo_ref, wout_ref, wg_ref, wu_ref, wd_ref, gffn_ref, gfin_ref, ys_ref)

    @pl.when(jnp.logical_and(is_prompt, j == nj - 1))
    def _write_state():
        kwin_ref[...] = kprev_scr[...].T
        vwin_ref[...] = vprev_scr[...].T
        stn = st_scr[...]
        cts = [cbd_scr[0].T, cbd_scr[1].T]
        for hd in range(M_HEADS):
            p, hh = hd // 2, hd % 2
            c_ref[hd] = cts[p][64 * hh:64 * hh + 64, 128 * hh:128 * hh + 128]
            n_ref[hd:hd + 1, :] = stn[p:p + 1, 64 * hh:64 * hh + 64]
            m_ref[0:1, hd:hd + 1] = stn[2 + hd:3 + hd, 0:1]


def _in_projection(x_ref, gattn_ref, wcat_ref, z_ref):
    h = _rms(x_ref[...], gattn_ref[...]).astype(BF16)
    for c0, c1 in Z_GROUPS:
        z_ref[:, c0:c1] = _dot_nt(h, wcat_ref[c0:c1, :])


def _load_carry(kprev_scr, vprev_scr, cbd_scr, st_scr):
    st = st_scr[...]
    return dict(kp=kprev_scr[...], vp=vprev_scr[...], cbd=[cbd_scr[0], cbd_scr[1]],
                n=[st[0:1, :], st[1:2, :]], m=[st[2 + hd:3 + hd, 0:1] for hd in range(M_HEADS)])


def _store_carry(carry, kprev_scr, vprev_scr, cbd_scr, st_scr):
    kprev_scr[...] = carry["kp"]
    vprev_scr[...] = carry["vp"]
    for p in range(2):
        cbd_scr[p] = carry["cbd"][p]
        st_scr[p:p + 1, :] = carry["n"][p]
    for hd in range(M_HEADS):
        st_scr[2 + hd:3 + hd, :] = jnp.broadcast_to(carry["m"][hd], (1, LANES))


def _tail(x_ref, z_ref, yatt_ref, ym_ref, a_ref, wao_ref, wmo_ref, wout_ref, wg_ref, wu_ref, wd_ref,
          gffn_ref, gfin_ref, y_ref):
    d_ff = wg_ref.shape[1]
    ya = _dot(yatt_ref[...], wao_ref[...])
    ymm = _dot(ym_ref[...], wmo_ref[...])
    mixed = _sigmoid(z_ref[:, GA:GA + 1024]) * ya + _sigmoid(z_ref[:, GM:GM + 1024]) * ymm
    x1 = x_ref[...] + _dot(mixed.astype(BF16), wout_ref[...])
    h2 = _rms(x1, gffn_ref[...]).astype(BF16)
    for c0 in range(0, d_ff, 256):
        g = _dot(h2, wg_ref[:, c0:c0 + 256])
        u = _dot(h2, wu_ref[:, c0:c0 + 256])
        a_ref[:, c0:c0 + 256] = (g * _sigmoid(g) * u).astype(BF16)
    x2 = x1 + _dot(a_ref[...], wd_ref[...])
    y_ref[...] = _rms(x2, gfin_ref[...])


def _mixers(z_ref, yatt_ref, ym_ref, carry, ghead_ref, bif_ref, bias_scr, sinks_ref, first_of_sequence):
    T = T_BLK
    lane = lax.broadcasted_iota(jnp.int32, (T, LANES), 1)
    lane_lo = lane < HEAD_DIM
    row = lax.broadcasted_iota(jnp.int32, (T, LANES), 0)
    qi = jnp.where(row < 128, row, row - 128)
    tri2 = lane <= qi
    if first_of_sequence is not None:
        valid0 = lane <= qi + jnp.where(first_of_sequence, 0, 2 * LANES)
    rowc = lax.broadcasted_iota(jnp.int32, (T, 1), 0)
    k_all = z_ref[:, KA:KA + 128]
    v_all = z_ref[:, VA:VA + 128]
    for sb in range(2):
        r0 = 128 * sb
        if sb == 0:
            kp, vp = carry["kp"], carry["vp"]
        else:
            kp, vp = k_all[0:128], v_all[0:128]
        kcat = jnp.concatenate([kp, k_all[r0:r0 + 128]], axis=0)
        vcat = jnp.concatenate([vp, v_all[r0:r0 + 128]], axis=0)
        kroll = pltpu.roll(kcat, HEAD_DIM, 1)
        vroll = pltpu.roll(vcat, HEAD_DIM, 1)
        zero = jnp.zeros_like(kcat)
        k_mats = [jnp.where(lane_lo, kcat, zero), jnp.where(lane_lo, zero, kroll),
                  jnp.where(lane_lo, kroll, zero), jnp.where(lane_lo, zero, kcat)]
        v_mats = [jnp.where(lane_lo, vcat, zero), jnp.where(lane_lo, zero, vroll),
                  jnp.where(lane_lo, vroll, zero), jnp.where(lane_lo, zero, vcat)]
        q = (z_ref[r0:r0 + 128, QA:QA + 512] * (HEAD_DIM ** -0.5)).astype(BF16)
        lhs_a = jnp.concatenate([q[:, 0:128], q[:, 128:256]], axis=0)
        lhs_b = jnp.concatenate([q[:, 256:384], q[:, 384:512]], axis=0)
        groups = [(lhs_a, 0, 0, 2), (lhs_a, 1, 1, 3), (lhs_b, 2, 4, 6), (lhs_b, 3, 5, 7)]
        pcs = []
        for lhs, mi, ha, hb in groups:
            s = _dot_nt(lhs, k_mats[mi].astype(BF16))
            sf = jnp.where(tri2, s[:, 128:256], s[:, 0:128])
            sf = sf + jnp.concatenate([bias_scr[ha], bias_scr[hb]], axis=0)
            if sb == 0 and first_of_sequence is not None:
                sf = jnp.where(valid0, sf, NEG)
            sink = jnp.where(rowc < 128, sinks_ref[ha], sinks_ref[hb])
            mx = jnp.maximum(jnp.max(sf, axis=-1, keepdims=True), sink)
            p = jnp.exp(sf - mx)
            den = jnp.sum(p, axis=-1, keepdims=True) + jnp.exp(sink - mx)
            pn = p * (1.0 / den)
            zp = jnp.zeros_like(pn)
            pcs.append(jnp.concatenate([jnp.where(tri2, zp, pn), jnp.where(tri2, pn, zp)],
                                       axis=1).astype(BF16))
        o_a = _dot(jnp.concatenate([pcs[0], pcs[1]], axis=1),
                   jnp.concatenate([v_mats[0], v_mats[1]], axis=0).astype(BF16))
        o_b = _dot(jnp.concatenate([pcs[2], pcs[3]], axis=1),
                   jnp.concatenate([v_mats[2], v_mats[3]], axis=0).astype(BF16))
        yatt_ref[r0:r0 + 128, 0:128] = o_a[0:128].astype(BF16)
        yatt_ref[r0:r0 + 128, 128:256] = o_a[128:256].astype(BF16)
        yatt_ref[r0:r0 + 128, 256:384] = o_b[0:128].astype(BF16)
        yatt_ref[r0:r0 + 128, 384:512] = o_b[128:256].astype(BF16)
    new_carry = dict(kp=k_all[128:256], vp=v_all[128:256], cbd=[None, None], n=[None, None],
                     m=[None] * M_HEADS)

    zif = z_ref[:, IF:IF + 128] + bif_ref[...]
    gl = jnp.where(lane < M_HEADS, zif, _log_sigmoid(zif))
    gl_t = gl.T
    tr = lax.broadcasted_iota(jnp.int32, (T, T), 0)
    ts = lax.broadcasted_iota(jnp.int32, (T, T), 1)
    tril = ts <= tr
    triu = tr <= ts
    row2 = lax.broadcasted_iota(jnp.int32, (2 * LANES, LANES), 0)
    lane2 = lax.broadcasted_iota(jnp.int32, (2 * LANES, LANES), 1)
    bd_mask = (row2 < LANES) == (lane2 < HEAD_DIM)
    for p in range(2):
        q_pair = z_ref[:, QM + 128 * p:QM + 128 * p + 128]
        k_pair = z_ref[:, KM + 128 * p:KM + 128 * p + 128] * (HEAD_DIM ** -0.5)
        v_pair = z_ref[:, VM + 256 * p:VM + 256 * p + 256]
        cbd = carry["cbd"][p]
        n_pair = carry["n"][p]
        q_bf = q_pair.astype(BF16)
        qc = _dot_nt(q_bf, cbd.astype(BF16))
        qn_prod = q_pair * n_pair
        ws, decays, m_ends = [], [], []
        for hh in range(2):
            hd = 2 * p + hh
            hmask = lane_lo if hh == 0 else jnp.logical_not(lane_lo)
            ig_c = gl[:, hd:hd + 1]
            lf_c = gl[:, M_HEADS + hd:M_HEADS + hd + 1]
            ig_r = gl_t[hd:hd + 1, :]
            lf_r = gl_t[M_HEADS + hd:M_HEADS + hd + 1, :]
            m_prev = carry["m"][hd]
            b_c = jnp.sum(jnp.where(tril, lf_r, 0.0), axis=1, keepdims=True)
            b_r = jnp.sum(jnp.where(triu, lf_c, 0.0), axis=0, keepdims=True)
            a_r = ig_r - b_r
            cm_c = jnp.max(jnp.where(tril, a_r, NEG), axis=1, keepdims=True)
            mt_c = b_c + jnp.maximum(m_prev, cm_c)
            g_c = b_c - mt_c
            dm = jnp.exp(jnp.where(tril, a_r + g_c, NEG))
            inter_c = jnp.exp(m_prev + g_c)
            k_h = jnp.where(hmask, k_pair, 0.0).astype(BF16)
            w = dm * _dot_nt(q_bf, k_h)
            v_h = v_pair[:, 128 * hh:128 * hh + 128]
            num = inter_c * qc[:, 128 * hh:128 * hh + 128] + _dot(w.astype(BF16), v_h.astype(BF16))
            qn = jnp.sum(jnp.where(hmask, qn_prod, 0.0), axis=1, keepdims=True)
            den = inter_c * qn + jnp.sum(w, axis=1, keepdims=True)
            hv = num / jnp.maximum(jnp.abs(den), jnp.exp(-mt_c))
            hn = hv * lax.rsqrt(jnp.mean(hv * hv, axis=-1, keepdims=True) + EPS)
            hn = hn * ghead_ref[:, 128 * hd:128 * hd + 128]
            om = z_ref[:, OM + 128 * hd:OM + 128 * hd + 128]
            ym_ref[:, 128 * hd:128 * hd + 128] = (hn * _sigmoid(om)).astype(BF16)
            b_end = b_c[T - 1:T, :]
            m_end = mt_c[T - 1:T, :]
            ws.append(jnp.exp((ig_c - b_c) + b_end - m_end))
            decays.append(jnp.exp(m_prev + b_end - m_end))
            m_ends.append(m_end)
        kw = k_pair * jnp.where(lane_lo, ws[0], ws[1])
        upd = _dot(v_pair.T.astype(BF16), kw.astype(BF16))
        dec_rows = jnp.where(row2[:, 0:1] < LANES, decays[0], decays[1])
        new_carry["cbd"][p] = dec_rows * cbd + jnp.where(bd_mask, upd, 0.0)
        dec_lanes = jnp.where(lane_lo[0:1, :], decays[0], decays[1])
        new_carry["n"][p] = dec_lanes * n_pair + jnp.sum(kw, axis=0, keepdims=True)
        for hh in range(2):
            new_carry["m"][2 * p + hh] = m_ends[hh]
    return new_carry


def _resident(shape):
    zeros = (0,) * len(shape)
    return pl.BlockSpec(shape, lambda t: zeros, pipeline_mode=pl.Buffered(1))


def _smem():
    return pl.BlockSpec(memory_space=pltpu.SMEM)


def _layer_call(x, wcat, wao, wmo, wout, wg, wu, wd, gattn, gffn, gfin, ghead, bif, fbuck, relb, sinks,
                xs3, zs, yatts, cqs, scals):
    bsz, s_len, d = x.shape
    nb = xs3.shape[0]
    t_step = T_BLK * N_SUB
    assert s_len % t_step == 0 and d == 1024 and nb <= T_BLK
    d_ff = wg.shape[1]
    nj = s_len // t_step
    n_prompt = bsz * nj
    out_shapes = (
        jax.ShapeDtypeStruct((bsz, s_len, d), F32),
        jax.ShapeDtypeStruct((bsz, WINDOW, 128), F32),
        jax.ShapeDtypeStruct((bsz, WINDOW, 128), F32),
        jax.ShapeDtypeStruct((bsz, M_HEADS, 64, 128), F32),
        jax.ShapeDtypeStruct((bsz, M_HEADS, 64), F32),
        jax.ShapeDtypeStruct((bsz, 1, M_HEADS), F32),
        jax.ShapeDtypeStruct((nb, 1, d), F32),
    )

    def seq(t):
        return jnp.minimum(t, n_prompt - 1) // nj

    def blk(t):
        return jnp.minimum(t, n_prompt - 1) % nj

    in_specs = [
        pl.BlockSpec((None, t_step, d), lambda t: (seq(t), blk(t), 0)),
        _resident(wcat.shape), _resident(wao.shape), _resident(wmo.shape), _resident(wout.shape),
        _resident(wg.shape), _resident(wu.shape), _resident(wd.shape),
        _resident(gattn.shape), _resident(gffn.shape), _resident(gfin.shape), _resident(ghead.shape),
        _resident(bif.shape), _resident(fbuck.shape), _smem(), _smem(),
        pl.BlockSpec((nb, None, d), lambda t: (0, 0, 0), pipeline_mode=pl.Buffered(1)),
        _resident(zs.shape), _resident(yatts.shape), _resident(cqs.shape), _resident(scals.shape),
    ]
    out_specs = (
        pl.BlockSpec((None, t_step, d), lambda t: (seq(t), blk(t), 0)),
        pl.BlockSpec((None, WINDOW, 128), lambda t: (seq(t), 0, 0)),
        pl.BlockSpec((None, WINDOW, 128), lambda t: (seq(t), 0, 0)),
        pl.BlockSpec((None, M_HEADS, 64, 128), lambda t: (seq(t), 0, 0, 0)),
        pl.BlockSpec((None, M_HEADS, 64), lambda t: (seq(t), 0, 0)),
        pl.BlockSpec((None, 1, M_HEADS), lambda t: (seq(t), 0, 0)),
        pl.BlockSpec((nb, None, d), lambda t: (0, 0, 0)),
    )
    scratch = [
        pltpu.VMEM((N_SUB, T_BLK, N_CAT), F32),
        pltpu.VMEM((N_SUB, T_BLK, 512), BF16),
        pltpu.VMEM((N_SUB, T_BLK, 512), BF16),
        pltpu.VMEM((T_BLK, d_ff), BF16),
        pltpu.VMEM((N_Q_HEADS, 128, 128), F32),
        pltpu.VMEM((128, 128), F32),
        pltpu.VMEM((128, 128), F32),
        pltpu.VMEM((2, 256, 128), F32),
        pltpu.VMEM((8, 128), F32),
    ]
    return pl.pallas_call(
        functools.partial(_layer_kernel, nj),
        grid=(n_prompt + 1,),
        in_specs=in_specs,
        out_specs=out_specs,
        out_shape=out_shapes,
        scratch_shapes=scratch,
        compiler_params=pltpu.CompilerParams(
            dimension_semantics=("arbitrary",),
            vmem_limit_bytes=VMEM_LIMIT_BYTES),
        name="layer",
    )(x, wcat, wao, wmo, wout, wg, wu, wd, gattn, gffn, gfin, ghead, bif, fbuck, relb, sinks,
      xs3, zs, yatts, cqs, scals)


W_QA, W_KA, W_VA, W_QM, W_KM, W_VM, W_OM, W_IF, W_GA, W_GM, W_END = (
    0, 512, 640, 768, 1024, 1280, 1792, 2304, 2312, 3336, 4360)
ZT_KA, ZT_VA, ZT_QM, ZT_KM, ZT_IF, ZT_ROWS = 0, 128, 256, 512, 768, 776


def _sample_inproj_kernel(x_ref, g_ref, wt_ref, nt_ref, mt_ref, bif_ref,
                          z_ref, zt_ref, no_ref, mo_ref, scal_ref):
    _sample_projection(x_ref, g_ref, wt_ref, z_ref, zt_ref)
    r8c = lax.broadcasted_iota(jnp.int32, (8, 1), 0)
    bias_col = jnp.zeros((8, 1), F32)
    for r in range(8):
        bias_col = jnp.where(r8c == r, bif_ref[r // M_HEADS, r % M_HEADS], bias_col)
    ift = zt_ref[ZT_IF:ZT_ROWS, :] + bias_col
    m0 = mt_ref[...]
    r16 = lax.broadcasted_iota(jnp.int32, (16, LANES), 0)
    table = jnp.zeros((16, LANES), F32)
    for hd in range(M_HEADS):
        ig = ift[hd:hd + 1, :]
        lf = _log_sigmoid(ift[M_HEADS + hd:M_HEADS + hd + 1, :])
        m_prev = m0[hd:hd + 1, :]
        a = ig - lf
        m_t = lf + jnp.maximum(m_prev, a)
        dgate = jnp.exp(a + lf - m_t)
        inter = jnp.exp(m_prev + lf - m_t)
        qt = zt_ref[ZT_QM + 64 * hd:ZT_QM + 64 * hd + 64, :]
        kt = zt_ref[ZT_KM + 64 * hd:ZT_KM + 64 * hd + 64, :] * (HEAD_DIM ** -0.5)
        nt = nt_ref[hd]
        no_ref[hd] = inter * nt + dgate * kt
        mo_ref[hd:hd + 1, :] = m_t
        qk = jnp.sum(qt * kt, axis=0, keepdims=True)
        nq = jnp.sum(nt * qt, axis=0, keepdims=True)
        w = dgate * qk
        den = inter * nq + w
        scale = 1.0 / jnp.maximum(jnp.abs(den), jnp.exp(-m_t))
        for base, val in ((SC_INTER, inter), (SC_W, w), (SC_SCALE, scale), (SC_D, dgate)):
            table = jnp.where(r16 == base + hd, val, table)
    full = jnp.concatenate([table, jnp.zeros((LANES - 16, LANES), F32)], axis=0)
    scal_ref[...] = full.T


def _sample_projection(x_ref, g_ref, wt_ref, z_ref, zt_ref):
    h32 = _rms(x_ref[...], g_ref[...])
    h = h32.astype(BF16)
    ht = h32.T.astype(BF16)
    for a, b, dst in ((W_QA, 768, QA), (768, 1536, 768), (1536, W_IF, 1536),
                      (W_GA, W_GM, GA), (W_GM, W_END, GM)):
        z_ref[:, dst:dst + (b - a)] = _dot_nt(h, wt_ref[a:b, :].astype(BF16))
    w_if = jnp.concatenate([wt_ref[W_IF:W_GA, :], jnp.zeros((LANES - 8, wt_ref.shape[1]), F32)],
                           axis=0).astype(BF16)
    z_ref[:, IF:IF + LANES] = _dot_nt(h, w_if)
    zt_ref[ZT_KA:ZT_QM, :] = _dot(wt_ref[W_KA:W_QM, :].astype(BF16), ht)
    zt_ref[ZT_QM:ZT_IF, :] = _dot(wt_ref[W_QM:W_VM, :].astype(BF16), ht)
    zt_ref[ZT_IF:ZT_ROWS, :] = _dot(w_if, ht)[0:8, :]


def _sample_inproj_call(x3, gattn, wt, nt, mt, bif2):
    n, _, d = x3.shape
    assert n == LANES
    full = lambda s: pl.BlockSpec(tuple(s), lambda i: (0,) * len(s))
    return pl.pallas_call(
        _sample_inproj_kernel,
        grid=(1,),
        in_specs=[pl.BlockSpec((n, None, d), lambda i: (0, 0, 0)),
                  pl.BlockSpec((1, d), lambda i: (0, 0)),
                  pl.BlockSpec(wt.shape, lambda i: (0, 0), pipeline_mode=pl.Buffered(1)),
                  full(nt.shape), full(mt.shape), _smem()],
        out_specs=(full((n, N_CAT)), full((ZT_ROWS, n)), full(nt.shape), full(mt.shape),
                   full((n, LANES))),
        out_shape=(jax.ShapeDtypeStruct((n, N_CAT), F32), jax.ShapeDtypeStruct((ZT_ROWS, n), F32),
                   jax.ShapeDtypeStruct(nt.shape, F32), jax.ShapeDtypeStruct(mt.shape, F32),
                   jax.ShapeDtypeStruct((n, LANES), F32)),
        compiler_params=pltpu.CompilerParams(dimension_semantics=("arbitrary",),
                                             vmem_limit_bytes=VMEM_LIMIT_BYTES),
        name="sample_inproj",
    )(x3, gattn, wt, nt, mt, bif2)


SK2_BB = 8
SC_INTER, SC_W, SC_SCALE, SC_D, SC_COLS = 0, 4, 8, 12, 16
HEAD_ROW_ORDER = (0, 2, 4, 6, 1, 3, 5, 7)


def _sample_mixer_kernel(z_ref, zt_ref, kt_ref, vt_ref, ct_ref, sbuck_ref,
                         scal_ref, relb_ref, sinks_ref,
                         yatt_ref, cq_ref, ko_ref, vo_ref, co_ref,
                         sbias_scr):
    i = pl.program_id(0)
    r8 = lax.broadcasted_iota(jnp.int32, (8, LANES), 0)
    l8 = lax.broadcasted_iota(jnp.int32, (8, LANES), 1)
    r8c = lax.broadcasted_iota(jnp.int32, (8, 1), 0)

    @pl.when(i == 0)
    def _prologue():
        sb = sbuck_ref[...]
        acc = jnp.zeros((8, LANES), F32)
        for rrow, hd in enumerate(HEAD_ROW_ORDER):
            acc = jnp.where(r8 == rrow, _bias_from_buckets(sb, relb_ref, hd), acc)
        sbias_scr[...] = acc

    sink = jnp.zeros((8, 1), F32)
    for rrow, hd in enumerate(HEAD_ROW_ORDER):
        sink = jnp.where(r8c == rrow, sinks_ref[hd], sink)
    sbias = sbias_scr[...]
    lane_w = lax.broadcasted_iota(jnp.int32, (WINDOW, LANES), 1)
    last_lane = lane_w == WINDOW - 1
    r256 = lax.broadcasted_iota(jnp.int32, (8, 2 * LANES), 0)
    l256 = lax.broadcasted_iota(jnp.int32, (8, 2 * LANES), 1)
    own64 = (l256 // HEAD_DIM) == r256
    l512r = lax.broadcasted_iota(jnp.int32, (1, 4 * LANES), 1) // LANES
    lo1 = lax.broadcasted_iota(jnp.int32, (1, LANES), 1) < HEAD_DIM
    row_b = lax.broadcasted_iota(jnp.int32, (LANES, 2 * LANES), 0)
    ktm = [(zt_ref[ZT_KM + 128 * pr:ZT_KM + 128 * pr + 128, :] * (HEAD_DIM ** -0.5)).astype(BF16)
           for pr in range(2)]

    nrow = 8 * SK2_BB
    lst = lax.broadcasted_iota(jnp.int32, (nrow, LANES), 1)
    z1 = jnp.zeros((1, LANES), F32)
    qm_l, so_l, kn_l, vn_l = [], [], [], []
    for bi in range(SK2_BB):
        qp = [z_ref[bi:bi + 1, QA + 128 * p:QA + 128 * p + 128] * (HEAD_DIM ** -0.5) for p in range(4)]
        qpr = [pltpu.roll(x, HEAD_DIM, 1) for x in qp]
        rows = [jnp.where(lo1, qp[0], z1), jnp.where(lo1, qp[1], z1),
                jnp.where(lo1, z1, qpr[2]), jnp.where(lo1, z1, qpr[3]),
                jnp.where(lo1, qpr[0], z1), jnp.where(lo1, qpr[1], z1),
                jnp.where(lo1, z1, qp[2]), jnp.where(lo1, z1, qp[3])]
        qm = jnp.zeros((8, LANES), F32)
        for r in range(8):
            qm = jnp.where(r8 == r, rows[r], qm)
        qm_l.append(qm)
        so_l.append(_dot(qm.astype(BF16), kt_ref[bi].astype(BF16)))
        kn_l.append(jnp.broadcast_to(z_ref[bi:bi + 1, KA:KA + 128], (8, LANES)))
        vn_l.append(jnp.broadcast_to(z_ref[bi:bi + 1, VA:VA + 128], (8, LANES)))
    qm_all = jnp.concatenate(qm_l, axis=0)
    s_old = jnp.concatenate(so_l, axis=0)
    vn_all = jnp.concatenate(vn_l, axis=0)
    s_new = jnp.sum(qm_all * jnp.concatenate(kn_l, axis=0), axis=1, keepdims=True)
    sbias_all = jnp.concatenate([sbias] * SK2_BB, axis=0)
    sink_all = jnp.concatenate([sink] * SK2_BB, axis=0)
    s = jnp.where(lst == WINDOW - 1, s_new, pltpu.roll(s_old, WINDOW - 1, 1)) + sbias_all
    mx = jnp.maximum(jnp.max(s, axis=-1, keepdims=True), sink_all)
    pe = jnp.exp(s - mx)
    den = jnp.sum(pe, axis=-1, keepdims=True) + jnp.exp(sink_all - mx)
    pn = pe * (1.0 / den)
    p_old = jnp.where(lst == 0, 0.0, pltpu.roll(pn, 1, 1))
    oa_l = [_dot_nt(p_old[8 * bi:8 * bi + 8].astype(BF16), vt_ref[bi].astype(BF16))
            for bi in range(SK2_BB)]
    oa = jnp.concatenate(oa_l, axis=0) + pn[:, WINDOW - 1:WINDOW] * vn_all
    oar = pltpu.roll(oa, HEAD_DIM, 1)
    for bi in range(SK2_BB):
        r0 = 8 * bi
        pairs = [jnp.where(lo1, oa[r0:r0 + 1], oar[r0 + 4:r0 + 5]),
                 jnp.where(lo1, oa[r0 + 1:r0 + 2], oar[r0 + 5:r0 + 6]),
                 jnp.where(lo1, oar[r0 + 2:r0 + 3], oa[r0 + 6:r0 + 7]),
                 jnp.where(lo1, oar[r0 + 3:r0 + 4], oa[r0 + 7:r0 + 8])]
        for p in range(4):
            yatt_ref[bi:bi + 1, 128 * p:128 * p + 128] = pairs[p]

    for bi in range(SK2_BB):
        b = i * SK2_BB + bi
        shift = WINDOW - 1 - b
        kcol = pltpu.roll(zt_ref[ZT_KA:ZT_KA + 128, :], shift, 1)
        vcol = pltpu.roll(zt_ref[ZT_VA:ZT_VA + 128, :], shift, 1)
        ko_ref[bi] = jnp.where(last_lane, kcol, pltpu.roll(kt_ref[bi], WINDOW - 1, 1))
        vo_ref[bi] = jnp.where(last_lane, vcol, pltpu.roll(vt_ref[bi], WINDOW - 1, 1))

        qm_row = z_ref[bi:bi + 1, QM:QM + 256]
        vm_row = z_ref[bi:bi + 1, VM:VM + 512]
        qbd = jnp.where(own64, qm_row, 0.0)
        ct_all = jnp.concatenate([ct_ref[bi, hd] for hd in range(M_HEADS)], axis=0)
        cq = _dot(qbd.astype(BF16), ct_all.astype(BF16))
        for hd in range(M_HEADS):
            cq_ref[bi:bi + 1, 128 * hd:128 * hd + 128] = cq[hd:hd + 1, :]
        dsel = jnp.zeros((1, 4 * LANES), F32)
        for hd in range(M_HEADS):
            dsel = jnp.where(l512r == hd, scal_ref[b, SC_D + hd], dsel)
        vs = vm_row * dsel
        for pr in range(2):
            vsel = jnp.where(row_b == b, vs[:, 256 * pr:256 * pr + 256], 0.0).astype(BF16)
            upd = _dot(ktm[pr], vsel)
            for hh in range(2):
                hd = 2 * pr + hh
                co_ref[bi, hd] = (scal_ref[b, SC_INTER + hd] * ct_ref[bi, hd]
                                  + upd[64 * hh:64 * hh + 64, 128 * hh:128 * hh + 128])


def _sample_mixer_call(z, zt, kt, vt, ct, sbuck, scal_small, relb_t, sinks):
    nb = z.shape[0]
    assert nb % SK2_BB == 0 and nb == LANES
    bb = SK2_BB
    out_shapes = (
        jax.ShapeDtypeStruct((nb, 512), F32),
        jax.ShapeDtypeStruct((nb, 512), F32),
        jax.ShapeDtypeStruct(kt.shape, F32),
        jax.ShapeDtypeStruct(vt.shape, F32),
        jax.ShapeDtypeStruct(ct.shape, F32),
    )
    blk = lambda s: pl.BlockSpec((bb,) + tuple(s[1:]), lambda i: (i,) + (0,) * (len(s) - 1))
    full = lambda s: pl.BlockSpec(tuple(s), lambda i: (0,) * len(s))
    in_specs = [blk(z.shape), full(zt.shape), blk(kt.shape), blk(vt.shape), blk(ct.shape),
                full(sbuck.shape), _smem(), _smem(), _smem()]
    out_specs = (blk((nb, 512)), blk((nb, 512)), blk(kt.shape), blk(vt.shape), blk(ct.shape))
    return pl.pallas_call(
        _sample_mixer_kernel,
        grid=(nb // bb,),
        in_specs=in_specs,
        out_specs=out_specs,
        out_shape=out_shapes,
        scratch_shapes=[pltpu.VMEM((8, LANES), F32)],
        compiler_params=pltpu.CompilerParams(dimension_semantics=("arbitrary",)),
        name="sample_mixers",
    )(z, zt, kt, vt, ct, sbuck, scal_small, relb_t, sinks)


def kernel(x_prompt, x_sample, cache_k_win, cache_v_win, state_mlstm_C, state_mlstm_n, state_mlstm_m,
           rel_bias, w_in, b_if, sinks, g_attn_norm, g_head, w_att_out, w_mlstm_out, w_out,
           g_ffn_norm, w_gate, w_up, w_down, g_final):
    depth = w_in.shape[0]
    assert depth == 1
    bsz, s_len, d = x_prompt.shape
    nb = x_sample.shape[0]
    assert x_sample.shape[1] == 1 and cache_k_win.shape[2] == WINDOW

    w = w_in[0]
    wt = w.T
    wcat = jnp.concatenate([wt[:2312], jnp.zeros((N_CAT - 4360, d), w.dtype), wt[2312:]],
                           axis=0).astype(BF16)
    wao = w_att_out[0].astype(BF16)
    wmo = w_mlstm_out[0].astype(BF16)
    wout = w_out[0].astype(BF16)
    wg = w_gate[0].astype(BF16)
    wu = w_up[0].astype(BF16)
    wd = w_down[0].astype(BF16)
    gattn = g_attn_norm[0].reshape(1, d)
    gffn = g_ffn_norm[0].reshape(1, d)
    gfin = g_final.reshape(1, d)
    ghead = g_head[0].reshape(1, 512)
    bif = jnp.concatenate([b_if[0].reshape(1, 2 * M_HEADS), jnp.zeros((1, LANES - 2 * M_HEADS), F32)], axis=1)
    sink_v = sinks[0]
    relb_t = rel_bias.T

    qi = jnp.arange(WINDOW)[:, None]
    kj = jnp.arange(WINDOW)[None, :]
    fbuck = _t5_bucket(jnp.where(kj <= qi, qi - kj, WINDOW + qi - kj)).astype(jnp.int32)
    sbuck = jnp.broadcast_to(_t5_bucket(WINDOW - 1 - kj), (8, WINDOW)).astype(jnp.int32)

    feat = N_KV_HEADS * HEAD_DIM
    kt_in = jnp.transpose(cache_k_win[0], (0, 2, 3, 1)).reshape(nb, feat, WINDOW)
    vt_in = jnp.transpose(cache_v_win[0], (0, 2, 3, 1)).reshape(nb, feat, WINDOW)
    ct_in = jnp.transpose(state_mlstm_C[0], (0, 1, 3, 2))
    nt_in = jnp.transpose(state_mlstm_n[0], (1, 2, 0))
    mt_in = state_mlstm_m[0].T
    z_s, zt_s, nt_s, mt_s, scal_s = _sample_inproj_call(x_sample, gattn, wt, nt_in, mt_in, b_if[0])
    yatt_s, cq_s, kt_s, vt_s, ct_s = _sample_mixer_call(
        z_s, zt_s, kt_in, vt_in, ct_in, sbuck, scal_s[:, :SC_COLS], relb_t, sink_v)
    y_p, kt_p, vt_p, ct_p, n_p, m_p, y_s = _layer_call(
        x_prompt, wcat, wao, wmo, wout, wg, wu, wd, gattn, gffn, gfin, ghead, bif, fbuck, relb_t, sink_v,
        x_sample, z_s, yatt_s, cq_s, scal_s)

    def window_out(t, n):
        return jnp.transpose(t.reshape(n, N_KV_HEADS, HEAD_DIM, WINDOW), (0, 3, 1, 2))[None]

    return (y_p, y_s,
            window_out(kt_p, bsz), window_out(vt_p, bsz),
            jnp.transpose(ct_p, (0, 1, 3, 2))[None], n_p[None], m_p.reshape(1, bsz, M_HEADS),
            window_out(kt_s, nb), window_out(vt_s, nb),
            jnp.transpose(ct_s, (0, 1, 3, 2))[None], jnp.transpose(nt_s, (2, 0, 1))[None], mt_s.T[None])
```

```python
import functools
import math

import jax
import jax.numpy as jnp
from jax import lax
from jax.experimental import pallas as pl
from jax.experimental.pallas import tpu as pltpu

F32 = jnp.float32
BF16 = jnp.bfloat16

HEAD_DIM = 64
N_Q_HEADS = 8
N_KV_HEADS = 2
WINDOW = 128
N_BUCKETS = 32
MAX_DISTANCE = WINDOW
M_HEADS = 4
EPS = 1e-6
NEG = -1e30

LANES = 128
VMEM_LIMIT_BYTES = 61 * 1024 * 1024

QA, KA, VA, QM, KM, VM, OM, IF, GA, GM = 0, 512, 640, 768, 1024, 1280, 1792, 2304, 2432, 3456
N_CAT = 4480
Z_GROUPS = ((768, 2432), (0, 768), (2432, 4480))
T_BLK = 256
N_SUB = 2


def _dot(a, b):
    return jnp.dot(a, b, preferred_element_type=F32)


def _dot_nt(a, b):
    return lax.dot_general(a, b, (((1,), (1,)), ((), ())), preferred_element_type=F32)


def _sigmoid(x):
    return 1.0 / (1.0 + jnp.exp(-x))


def _log_sigmoid(x):
    return jnp.minimum(x, 0.0) - jnp.log1p(jnp.exp(-jnp.abs(x)))


def _rms(x, g):
    ms = jnp.mean(x * x, axis=-1, keepdims=True)
    return x * lax.rsqrt(ms + EPS) * g


def _t5_bucket(dist):
    n = jnp.maximum(dist, 0)
    max_exact = N_BUCKETS // 2
    nf = jnp.maximum(n, 1).astype(F32)
    large = max_exact + jnp.floor(jnp.log(nf / max_exact) / math.log(MAX_DISTANCE / max_exact)
                                  * (N_BUCKETS - max_exact)).astype(jnp.int32)
    large = jnp.minimum(large, N_BUCKETS - 1)
    return jnp.where(n < max_exact, n, large)


def _bias_from_buckets(buckets, relb_ref, head):
    acc = jnp.zeros(buckets.shape, F32)
    for k in range(N_BUCKETS):
        acc = jnp.where(buckets == k, relb_ref[head, k], acc)
    return acc


def _layer_kernel(nj, x_ref, wt_hbm, wao_hbm, wmo_hbm, wout_hbm, wg_hbm, wu_hbm, wd_hbm,
                  gattn_ref, gffn_ref, gfin_ref, ghead_ref, bif_ref, fbuck_ref, relb_ref, sinks_ref,
                  xs_ref, zs_ref, yatts_ref, cqs_ref, scals_ref,
                  y_ref, kwin_ref, vwin_ref, c_ref, n_ref, m_ref, ys_ref,
                  z_scr, yatt_scr, ym_scr, a_scr, bias_scr, kprev_scr, vprev_scr, cbd_scr, st_scr,
                  wcat_ref, wao_ref, wmo_ref, wout_ref, wg_ref, wu_ref, wd_ref, dma_sem):
    T = T_BLK
    t = pl.program_id(0)
    n_prompt = pl.num_programs(0) - 1
    is_prompt = t < n_prompt
    j = lax.rem(t, nj)

    @pl.when(t == 0)
    def _first_step():
        fb = fbuck_ref[...]
        for h in range(N_Q_HEADS):
            bias_scr[h] = _bias_from_buckets(fb, relb_ref, h)
        chunks = _weight_chunks(wt_hbm, wcat_ref, ((wao_hbm, wao_ref), (wmo_hbm, wmo_ref),
                                                   (wout_hbm, wout_ref), (wg_hbm, wg_ref),
                                                   (wu_hbm, wu_ref), (wd_hbm, wd_ref)))
        _cast_weights(chunks, z_scr, dma_sem)

    @pl.when(jnp.logical_and(is_prompt, j == 0))
    def _reset_state():
        kprev_scr[...] = jnp.zeros_like(kprev_scr)
        vprev_scr[...] = jnp.zeros_like(vprev_scr)
        cbd_scr[...] = jnp.zeros_like(cbd_scr)
        st_scr[...] = jnp.zeros_like(st_scr)

    @pl.when(is_prompt)
    def _prompt_step():
        for s in range(N_SUB):
            _in_projection(x_ref.at[s * T:(s + 1) * T, :], gattn_ref, wcat_ref, z_scr.at[s])
        carry = _load_carry(kprev_scr, vprev_scr, cbd_scr, st_scr)
        for s in range(N_SUB):
            carry = _mixers(z_scr.at[s], yatt_scr.at[s], ym_scr.at[s], carry, ghead_ref, bif_ref,
                            bias_scr, sinks_ref, first_of_sequence=(j == 0) if s == 0 else None)
        _store_carry(carry, kprev_scr, vprev_scr, cbd_scr, st_scr)
        for s in range(N_SUB):
            _tail(x_ref.at[s * T:(s + 1) * T, :], z_scr.at[s], yatt_scr.at[s], ym_scr.at[s], a_scr,
                  wao_ref, wmo_ref, wout_ref, wg_ref, wu_ref, wd_ref, gffn_ref, gfin_ref,
                  y_ref.at[s * T:(s + 1) * T, :])

    @pl.when(jnp.logical_not(is_prompt))
    def _sample_step():
        nb = xs_ref.shape[0]
        scal = scals_ref[...]
        yatt_scr[0, 0:nb, :] = yatts_ref[...].astype(BF16)
        for hd in range(M_HEADS):
            inter = scal[:, SC_INTER + hd:SC_INTER + hd + 1]
            w = scal[:, SC_W + hd:SC_W + hd + 1]
            scale = scal[:, SC_SCALE + hd:SC_SCALE + hd + 1]
            hv = (inter * cqs_ref[:, 128 * hd:128 * hd + 128]
                  + w * zs_ref[:, VM + 128 * hd:VM + 128 * hd + 128]) * scale
            hn = hv * lax.rsqrt(jnp.mean(hv * hv, axis=-1, keepdims=True) + EPS)
            hn = hn * ghead_ref[:, 128 * hd:128 * hd + 128]
            om = zs_ref[:, OM + 128 * hd:OM + 128 * hd + 128]
            ym_scr[0, 0:nb, 128 * hd:128 * hd + 128] = (hn * _sigmoid(om)).astype(BF16)
        _tail(xs_ref, zs_ref, yatt_scr.at[0, 0:nb, :], ym_scr.at[0, 0:nb, :], a_scr.at[0:nb, :],
              wao_ref, wmo_ref, wout_ref, wg_ref, wu_ref, wd_ref, gffn_ref, gfin_ref, ys_ref)

    @pl.when(jnp.logical_and(is_prompt, j == nj - 1))
    def _write_state():
        kwin_ref[...] = kprev_scr[...].T
        vwin_ref[...] = vprev_scr[...].T
        stn = st_scr[...]
        cts = [cbd_scr[0].T, cbd_scr[1].T]
        for hd in range(M_HEADS):
            p, hh = hd // 2, hd % 2
            c_ref[hd] = cts[p][64 * hh:64 * hh + 64, 128 * hh:128 * hh + 128]
            n_ref[hd:hd + 1, :] = stn[p:p + 1, 64 * hh:64 * hh + 64]
            m_ref[0:1, hd:hd + 1] = stn[2 + hd:3 + hd, 0:1]


def _in_projection(x_ref, gattn_ref, wcat_ref, z_ref):
    h = _rms(x_ref[...], gattn_ref[...]).astype(BF16)
    for c0, c1 in Z_GROUPS:
        z_ref[:, c0:c1] = _dot_nt(h, wcat_ref[c0:c1, :])


def _weight_chunks(wt_hbm, wcat_ref, plain):
    r = T_BLK
    chunks = [(wt_hbm.at[a:a + r, :], wcat_ref.at[a:a + r, :]) for a in range(0, W_IF, r)]
    chunks.append((wt_hbm.at[W_IF:W_GA, :], wcat_ref.at[IF:IF + LANES, :]))
    chunks += [(wt_hbm.at[a:a + r, :], wcat_ref.at[GA + a - W_GA:GA + a - W_GA + r, :])
               for a in range(W_GA, W_END, r)]
    for src, dst in plain:
        chunks += [(src.at[a:a + r, :], dst.at[a:a + r, :]) for a in range(0, src.shape[0], r)]
    return chunks


def _cast_weights(chunks, stage_ref, sem):
    def copy(i):
        src = chunks[i][0]
        n, c = src.shape
        return pltpu.make_async_copy(src, stage_ref.at[i % 2, 0:n, 0:c], sem.at[i % 2])

    copy(0).start()
    for i, (src, dst) in enumerate(chunks):
        if i + 1 < len(chunks):
            copy(i + 1).start()
        copy(i).wait()
        n, c = src.shape
        val = stage_ref[i % 2, 0:n, 0:c]
        if dst.shape[0] != n:
            val = jnp.concatenate([val, jnp.zeros((dst.shape[0] - n, c), F32)], axis=0)
        dst[...] = val.astype(BF16)


def _load_carry(kprev_scr, vprev_scr, cbd_scr, st_scr):
    st = st_scr[...]
    return dict(kp=kprev_scr[...], vp=vprev_scr[...], cbd=[cbd_scr[0], cbd_scr[1]],
                n=[st[0:1, :], st[1:2, :]], m=[st[2 + hd:3 + hd, 0:1] for hd in range(M_HEADS)])


def _store_carry(carry, kprev_scr, vprev_scr, cbd_scr, st_scr):
    kprev_scr[...] = carry["kp"]
    vprev_scr[...] = carry["vp"]
    for p in range(2):
        cbd_scr[p] = carry["cbd"][p]
        st_scr[p:p + 1, :] = carry["n"][p]
    for hd in range(M_HEADS):
        st_scr[2 + hd:3 + hd, :] = jnp.broadcast_to(carry["m"][hd], (1, LANES))


def _tail(x_ref, z_ref, yatt_ref, ym_ref, a_ref, wao_ref, wmo_ref, wout_ref, wg_ref, wu_ref, wd_ref,
          gffn_ref, gfin_ref, y_ref):
    d_ff = wg_ref.shape[1]
    ya = _dot(yatt_ref[...], wao_ref[...])
    ymm = _dot(ym_ref[...], wmo_ref[...])
    mixed = _sigmoid(z_ref[:, GA:GA + 1024]) * ya + _sigmoid(z_ref[:, GM:GM + 1024]) * ymm
    x1 = x_ref[...] + _dot(mixed.astype(BF16), wout_ref[...])
    h2 = _rms(x1, gffn_ref[...]).astype(BF16)
    for c0 in range(0, d_ff, 256):
        g = _dot(h2, wg_ref[:, c0:c0 + 256])
        u = _dot(h2, wu_ref[:, c0:c0 + 256])
        a_ref[:, c0:c0 + 256] = (g * _sigmoid(g) * u).astype(BF16)
    x2 = x1 + _dot(a_ref[...], wd_ref[...])
    y_ref[...] = _rms(x2, gfin_ref[...])


def _mixers(z_ref, yatt_ref, ym_ref, carry, ghead_ref, bif_ref, bias_scr, sinks_ref, first_of_sequence):
    T = T_BLK
    lane = lax.broadcasted_iota(jnp.int32, (T, LANES), 1)
    lane_lo = lane < HEAD_DIM
    row = lax.broadcasted_iota(jnp.int32, (T, LANES), 0)
    qi = jnp.where(row < 128, row, row - 128)
    tri2 = lane <= qi
    if first_of_sequence is not None:
        valid0 = lane <= qi + jnp.where(first_of_sequence, 0, 2 * LANES)
    rowc = lax.broadcasted_iota(jnp.int32, (T, 1), 0)
    k_all = z_ref[:, KA:KA + 128]
    v_all = z_ref[:, VA:VA + 128]
    for sb in range(2):
        r0 = 128 * sb
        if sb == 0:
            kp, vp = carry["kp"], carry["vp"]
        else:
            kp, vp = k_all[0:128], v_all[0:128]
        kcat = jnp.concatenate([kp, k_all[r0:r0 + 128]], axis=0)
        vcat = jnp.concatenate([vp, v_all[r0:r0 + 128]], axis=0)
        kroll = pltpu.roll(kcat, HEAD_DIM, 1)
        vroll = pltpu.roll(vcat, HEAD_DIM, 1)
        zero = jnp.zeros_like(kcat)
        k_mats = [jnp.where(lane_lo, kcat, zero), jnp.where(lane_lo, zero, kroll),
                  jnp.where(lane_lo, kroll, zero), jnp.where(lane_lo, zero, kcat)]
        v_mats = [jnp.where(lane_lo, vcat, zero), jnp.where(lane_lo, zero, vroll),
                  jnp.where(lane_lo, vroll, zero), jnp.where(lane_lo, zero, vcat)]
        q = (z_ref[r0:r0 + 128, QA:QA + 512] * (HEAD_DIM ** -0.5)).astype(BF16)
        lhs_a = jnp.concatenate([q[:, 0:128], q[:, 128:256]], axis=0)
        lhs_b = jnp.concatenate([q[:, 256:384], q[:, 384:512]], axis=0)
        groups = [(lhs_a, 0, 0, 2), (lhs_a, 1, 1, 3), (lhs_b, 2, 4, 6), (lhs_b, 3, 5, 7)]
        pcs = []
        for lhs, mi, ha, hb in groups:
            s = _dot_nt(lhs, k_mats[mi].astype(BF16))
            sf = jnp.where(tri2, s[:, 128:256], s[:, 0:128])
            sf = sf + jnp.concatenate([bias_scr[ha], bias_scr[hb]], axis=0)
            if sb == 0 and first_of_sequence is not None:
                sf = jnp.where(valid0, sf, NEG)
            sink = jnp.where(rowc < 128, sinks_ref[ha], sinks_ref[hb])
            mx = jnp.maximum(jnp.max(sf, axis=-1, keepdims=True), sink)
            p = jnp.exp(sf - mx)
            den = jnp.sum(p, axis=-1, keepdims=True) + jnp.exp(sink - mx)
            pn = p * (1.0 / den)
            zp = jnp.zeros_like(pn)
            pcs.append(jnp.concatenate([jnp.where(tri2, zp, pn), jnp.where(tri2, pn, zp)],
                                       axis=1).astype(BF16))
        o_a = _dot(jnp.concatenate([pcs[0], pcs[1]], axis=1),
                   jnp.concatenate([v_mats[0], v_mats[1]], axis=0).astype(BF16))
        o_b = _dot(jnp.concatenate([pcs[2], pcs[3]], axis=1),
                   jnp.concatenate([v_mats[2], v_mats[3]], axis=0).astype(BF16))
        yatt_ref[r0:r0 + 128, 0:128] = o_a[0:128].astype(BF16)
        yatt_ref[r0:r0 + 128, 128:256] = o_a[128:256].astype(BF16)
        yatt_ref[r0:r0 + 128, 256:384] = o_b[0:128].astype(BF16)
        yatt_ref[r0:r0 + 128, 384:512] = o_b[128:256].astype(BF16)
    new_carry = dict(kp=k_all[128:256], vp=v_all[128:256], cbd=[None, None], n=[None, None],
                     m=[None] * M_HEADS)

    zif = z_ref[:, IF:IF + 128] + bif_ref[...]
    gl = jnp.where(lane < M_HEADS, zif, _log_sigmoid(zif))
    gl_t = gl.T
    tr = lax.broadcasted_iota(jnp.int32, (T, T), 0)
    ts = lax.broadcasted_iota(jnp.int32, (T, T), 1)
    tril = ts <= tr
    triu = tr <= ts
    row2 = lax.broadcasted_iota(jnp.int32, (2 * LANES, LANES), 0)
    lane2 = lax.broadcasted_iota(jnp.int32, (2 * LANES, LANES), 1)
    bd_mask = (row2 < LANES) == (lane2 < HEAD_DIM)
    for p in range(2):
        q_pair = z_ref[:, QM + 128 * p:QM + 128 * p + 128]
        k_pair = z_ref[:, KM + 128 * p:KM + 128 * p + 128] * (HEAD_DIM ** -0.5)
        v_pair = z_ref[:, VM + 256 * p:VM + 256 * p + 256]
        cbd = carry["cbd"][p]
        n_pair = carry["n"][p]
        q_bf = q_pair.astype(BF16)
        qc = _dot_nt(q_bf, cbd.astype(BF16))
        qn_prod = q_pair * n_pair
        ws, decays, m_ends = [], [], []
        for hh in range(2):
            hd = 2 * p + hh
            hmask = lane_lo if hh == 0 else jnp.logical_not(lane_lo)
            ig_c = gl[:, hd:hd + 1]
            lf_c = gl[:, M_HEADS + hd:M_HEADS + hd + 1]
            ig_r = gl_t[hd:hd + 1, :]
            lf_r = gl_t[M_HEADS + hd:M_HEADS + hd + 1, :]
            m_prev = carry["m"][hd]
            b_c = jnp.sum(jnp.where(tril, lf_r, 0.0), axis=1, keepdims=True)
            b_r = jnp.sum(jnp.where(triu, lf_c, 0.0), axis=0, keepdims=True)
            a_r = ig_r - b_r
            cm_c = jnp.max(jnp.where(tril, a_r, NEG), axis=1, keepdims=True)
            mt_c = b_c + jnp.maximum(m_prev, cm_c)
            g_c = b_c - mt_c
            dm = jnp.exp(jnp.where(tril, a_r + g_c, NEG))
            inter_c = jnp.exp(m_prev + g_c)
            k_h = jnp.where(hmask, k_pair, 0.0).astype(BF16)
            w = dm * _dot_nt(q_bf, k_h)
            v_h = v_pair[:, 128 * hh:128 * hh + 128]
            num = inter_c * qc[:, 128 * hh:128 * hh + 128] + _dot(w.astype(BF16), v_h.astype(BF16))
            qn = jnp.sum(jnp.where(hmask, qn_prod, 0.0), axis=1, keepdims=True)
            den = inter_c * qn + jnp.sum(w, axis=1, keepdims=True)
            hv = num / jnp.maximum(jnp.abs(den), jnp.exp(-mt_c))
            hn = hv * lax.rsqrt(jnp.mean(hv * hv, axis=-1, keepdims=True) + EPS)
            hn = hn * ghead_ref[:, 128 * hd:128 * hd + 128]
            om = z_ref[:, OM + 128 * hd:OM + 128 * hd + 128]
            ym_ref[:, 128 * hd:128 * hd + 128] = (hn * _sigmoid(om)).astype(BF16)
            b_end = b_c[T - 1:T, :]
            m_end = mt_c[T - 1:T, :]
            ws.append(jnp.exp((ig_c - b_c) + b_end - m_end))
            decays.append(jnp.exp(m_prev + b_end - m_end))
            m_ends.append(m_end)
        kw = k_pair * jnp.where(lane_lo, ws[0], ws[1])
        upd = _dot(v_pair.T.astype(BF16), kw.astype(BF16))
        dec_rows = jnp.where(row2[:, 0:1] < LANES, decays[0], decays[1])
        new_carry["cbd"][p] = dec_rows * cbd + jnp.where(bd_mask, upd, 0.0)
        dec_lanes = jnp.where(lane_lo[0:1, :], decays[0], decays[1])
        new_carry["n"][p] = dec_lanes * n_pair + jnp.sum(kw, axis=0, keepdims=True)
        for hh in range(2):
            new_carry["m"][2 * p + hh] = m_ends[hh]
    return new_carry


def _resident(shape):
    zeros = (0,) * len(shape)
    return pl.BlockSpec(shape, lambda t: zeros, pipeline_mode=pl.Buffered(1))


def _smem():
    return pl.BlockSpec(memory_space=pltpu.SMEM)


def _layer_call(x, wt, wao, wmo, wout, wg, wu, wd, gattn, gffn, gfin, ghead, bif, fbuck, relb, sinks,
                xs3, zs, yatts, cqs, scals):
    bsz, s_len, d = x.shape
    nb = xs3.shape[0]
    t_step = T_BLK * N_SUB
    assert s_len % t_step == 0 and d == 1024 and nb <= T_BLK
    assert wt.shape == (W_END, d) and N_SUB >= 2
    d_ff = wg.shape[1]
    assert all(w.shape[0] % T_BLK == 0 and w.shape[1] <= N_CAT for w in (wao, wmo, wout, wg, wu, wd))
    hbm = pl.BlockSpec(memory_space=pl.ANY)
    nj = s_len // t_step
    n_prompt = bsz * nj
    out_shapes = (
        jax.ShapeDtypeStruct((bsz, s_len, d), F32),
        jax.ShapeDtypeStruct((bsz, WINDOW, 128), F32),
        jax.ShapeDtypeStruct((bsz, WINDOW, 128), F32),
        jax.ShapeDtypeStruct((bsz, M_HEADS, 64, 128), F32),
        jax.ShapeDtypeStruct((bsz, M_HEADS, 64), F32),
        jax.ShapeDtypeStruct((bsz, 1, M_HEADS), F32),
        jax.ShapeDtypeStruct((nb, 1, d), F32),
    )

    def seq(t):
        return jnp.minimum(t, n_prompt - 1) // nj

    def blk(t):
        return jnp.minimum(t, n_prompt - 1) % nj

    in_specs = [
        pl.BlockSpec((None, t_step, d), lambda t: (seq(t), blk(t), 0)),
        hbm, hbm, hbm, hbm, hbm, hbm, hbm,
        _resident(gattn.shape), _resident(gffn.shape), _resident(gfin.shape), _resident(ghead.shape),
        _resident(bif.shape), _resident(fbuck.shape), _smem(), _smem(),
        pl.BlockSpec((nb, None, d), lambda t: (0, 0, 0), pipeline_mode=pl.Buffered(1)),
        _resident(zs.shape), _resident(yatts.shape), _resident(cqs.shape), _resident(scals.shape),
    ]
    out_specs = (
        pl.BlockSpec((None, t_step, d), lambda t: (seq(t), blk(t), 0)),
        pl.BlockSpec((None, WINDOW, 128), lambda t: (seq(t), 0, 0)),
        pl.BlockSpec((None, WINDOW, 128), lambda t: (seq(t), 0, 0)),
        pl.BlockSpec((None, M_HEADS, 64, 128), lambda t: (seq(t), 0, 0, 0)),
        pl.BlockSpec((None, M_HEADS, 64), lambda t: (seq(t), 0, 0)),
        pl.BlockSpec((None, 1, M_HEADS), lambda t: (seq(t), 0, 0)),
        pl.BlockSpec((nb, None, d), lambda t: (0, 0, 0)),
    )
    scratch = [
        pltpu.VMEM((N_SUB, T_BLK, N_CAT), F32),
        pltpu.VMEM((N_SUB, T_BLK, 512), BF16),
        pltpu.VMEM((N_SUB, T_BLK, 512), BF16),
        pltpu.VMEM((T_BLK, d_ff), BF16),
        pltpu.VMEM((N_Q_HEADS, 128, 128), F32),
        pltpu.VMEM((128, 128), F32),
        pltpu.VMEM((128, 128), F32),
        pltpu.VMEM((2, 256, 128), F32),
        pltpu.VMEM((8, 128), F32),
        pltpu.VMEM((N_CAT, d), BF16),
        pltpu.VMEM(wao.shape, BF16), pltpu.VMEM(wmo.shape, BF16), pltpu.VMEM(wout.shape, BF16),
        pltpu.VMEM(wg.shape, BF16), pltpu.VMEM(wu.shape, BF16), pltpu.VMEM(wd.shape, BF16),
        pltpu.SemaphoreType.DMA((2,)),
    ]
    return pl.pallas_call(
        functools.partial(_layer_kernel, nj),
        grid=(n_prompt + 1,),
        in_specs=in_specs,
        out_specs=out_specs,
        out_shape=out_shapes,
        scratch_shapes=scratch,
        compiler_params=pltpu.CompilerParams(
            dimension_semantics=("arbitrary",),
            vmem_limit_bytes=VMEM_LIMIT_BYTES),
        name="layer",
    )(x, wt, wao, wmo, wout, wg, wu, wd, gattn, gffn, gfin, ghead, bif, fbuck, relb, sinks,
      xs3, zs, yatts, cqs, scals)


W_QA, W_KA, W_VA, W_QM, W_KM, W_VM, W_OM, W_IF, W_GA, W_GM, W_END = (
    0, 512, 640, 768, 1024, 1280, 1792, 2304, 2312, 3336, 4360)
ZT_KA, ZT_VA, ZT_QM, ZT_KM, ZT_IF, ZT_ROWS = 0, 128, 256, 512, 768, 776


def _sample_inproj_kernel(x_ref, g_ref, wt_ref, nt_ref, mt_ref, bif_ref,
                          z_ref, zt_ref, no_ref, mo_ref, scal_ref):
    _sample_projection(x_ref, g_ref, wt_ref, z_ref, zt_ref)
    r8c = lax.broadcasted_iota(jnp.int32, (8, 1), 0)
    bias_col = jnp.zeros((8, 1), F32)
    for r in range(8):
        bias_col = jnp.where(r8c == r, bif_ref[r // M_HEADS, r % M_HEADS], bias_col)
    ift = zt_ref[ZT_IF:ZT_ROWS, :] + bias_col
    m0 = mt_ref[...]
    r16 = lax.broadcasted_iota(jnp.int32, (16, LANES), 0)
    table = jnp.zeros((16, LANES), F32)
    for hd in range(M_HEADS):
        ig = ift[hd:hd + 1, :]
        lf = _log_sigmoid(ift[M_HEADS + hd:M_HEADS + hd + 1, :])
        m_prev = m0[hd:hd + 1, :]
        a = ig - lf
        m_t = lf + jnp.maximum(m_prev, a)
        dgate = jnp.exp(a + lf - m_t)
        inter = jnp.exp(m_prev + lf - m_t)
        qt = zt_ref[ZT_QM + 64 * hd:ZT_QM + 64 * hd + 64, :]
        kt = zt_ref[ZT_KM + 64 * hd:ZT_KM + 64 * hd + 64, :] * (HEAD_DIM ** -0.5)
        nt = nt_ref[hd]
        no_ref[hd] = inter * nt + dgate * kt
        mo_ref[hd:hd + 1, :] = m_t
        qk = jnp.sum(qt * kt, axis=0, keepdims=True)
        nq = jnp.sum(nt * qt, axis=0, keepdims=True)
        w = dgate * qk
        den = inter * nq + w
        scale = 1.0 / jnp.maximum(jnp.abs(den), jnp.exp(-m_t))
        for base, val in ((SC_INTER, inter), (SC_W, w), (SC_SCALE, scale), (SC_D, dgate)):
            table = jnp.where(r16 == base + hd, val, table)
    full = jnp.concatenate([table, jnp.zeros((LANES - 16, LANES), F32)], axis=0)
    scal_ref[...] = full.T


def _sample_projection(x_ref, g_ref, wt_ref, z_ref, zt_ref):
    h32 = _rms(x_ref[...], g_ref[...])
    h = h32.astype(BF16)
    ht = h32.T.astype(BF16)
    for a, b, dst in ((W_QA, 768, QA), (768, 1536, 768), (1536, W_IF, 1536),
                      (W_GA, W_GM, GA), (W_GM, W_END, GM)):
        z_ref[:, dst:dst + (b - a)] = _dot_nt(h, wt_ref[a:b, :].astype(BF16))
    w_if = jnp.concatenate([wt_ref[W_IF:W_GA, :], jnp.zeros((LANES - 8, wt_ref.shape[1]), F32)],
                           axis=0).astype(BF16)
    z_ref[:, IF:IF + LANES] = _dot_nt(h, w_if)
    zt_ref[ZT_KA:ZT_QM, :] = _dot(wt_ref[W_KA:W_QM, :].astype(BF16), ht)
    zt_ref[ZT_QM:ZT_IF, :] = _dot(wt_ref[W_QM:W_VM, :].astype(BF16), ht)
    zt_ref[ZT_IF:ZT_ROWS, :] = _dot(w_if, ht)[0:8, :]


def _sample_inproj_call(x3, gattn, wt, nt, mt, bif2):
    n, _, d = x3.shape
    assert n == LANES
    full = lambda s: pl.BlockSpec(tuple(s), lambda i: (0,) * len(s))
    return pl.pallas_call(
        _sample_inproj_kernel,
        grid=(1,),
        in_specs=[pl.BlockSpec((n, None, d), lambda i: (0, 0, 0)),
                  pl.BlockSpec((1, d), lambda i: (0, 0)),
                  pl.BlockSpec(wt.shape, lambda i: (0, 0), pipeline_mode=pl.Buffered(1)),
                  full(nt.shape), full(mt.shape), _smem()],
        out_specs=(full((n, N_CAT)), full((ZT_ROWS, n)), full(nt.shape), full(mt.shape),
                   full((n, LANES))),
        out_shape=(jax.ShapeDtypeStruct((n, N_CAT), F32), jax.ShapeDtypeStruct((ZT_ROWS, n), F32),
                   jax.ShapeDtypeStruct(nt.shape, F32), jax.ShapeDtypeStruct(mt.shape, F32),
                   jax.ShapeDtypeStruct((n, LANES), F32)),
        compiler_params=pltpu.CompilerParams(dimension_semantics=("arbitrary",),
                                             vmem_limit_bytes=VMEM_LIMIT_BYTES),
        name="sample_inproj",
    )(x3, gattn, wt, nt, mt, bif2)


SK2_BB = 8
SC_INTER, SC_W, SC_SCALE, SC_D, SC_COLS = 0, 4, 8, 12, 16
HEAD_ROW_ORDER = (0, 2, 4, 6, 1, 3, 5, 7)


def _sample_mixer_kernel(z_ref, zt_ref, kt_ref, vt_ref, ct_ref, sbuck_ref,
                         scal_ref, relb_ref, sinks_ref,
                         yatt_ref, cq_ref, ko_ref, vo_ref, co_ref,
                         sbias_scr):
    i = pl.program_id(0)
    r8 = lax.broadcasted_iota(jnp.int32, (8, LANES), 0)
    l8 = lax.broadcasted_iota(jnp.int32, (8, LANES), 1)
    r8c = lax.broadcasted_iota(jnp.int32, (8, 1), 0)

    @pl.when(i == 0)
    def _prologue():
        sb = sbuck_ref[...]
        acc = jnp.zeros((8, LANES), F32)
        for rrow, hd in enumerate(HEAD_ROW_ORDER):
            acc = jnp.where(r8 == rrow, _bias_from_buckets(sb, relb_ref, hd), acc)
        sbias_scr[...] = acc

    sink = jnp.zeros((8, 1), F32)
    for rrow, hd in enumerate(HEAD_ROW_ORDER):
        sink = jnp.where(r8c == rrow, sinks_ref[hd], sink)
    sbias = sbias_scr[...]
    lane_w = lax.broadcasted_iota(jnp.int32, (WINDOW, LANES), 1)
    last_lane = lane_w == WINDOW - 1
    r256 = lax.broadcasted_iota(jnp.int32, (8, 2 * LANES), 0)
    l256 = lax.broadcasted_iota(jnp.int32, (8, 2 * LANES), 1)
    own64 = (l256 // HEAD_DIM) == r256
    l512r = lax.broadcasted_iota(jnp.int32, (1, 4 * LANES), 1) // LANES
    lo1 = lax.broadcasted_iota(jnp.int32, (1, LANES), 1) < HEAD_DIM
    row_b = lax.broadcasted_iota(jnp.int32, (LANES, 2 * LANES), 0)
    ktm = [(zt_ref[ZT_KM + 128 * pr:ZT_KM + 128 * pr + 128, :] * (HEAD_DIM ** -0.5)).astype(BF16)
           for pr in range(2)]

    nrow = 8 * SK2_BB
    lst = lax.broadcasted_iota(jnp.int32, (nrow, LANES), 1)
    z1 = jnp.zeros((1, LANES), F32)
    qm_l, so_l, kn_l, vn_l = [], [], [], []
    for bi in range(SK2_BB):
        qp = [z_ref[bi:bi + 1, QA + 128 * p:QA + 128 * p + 128] * (HEAD_DIM ** -0.5) for p in range(4)]
        qpr = [pltpu.roll(x, HEAD_DIM, 1) for x in qp]
        rows = [jnp.where(lo1, qp[0], z1), jnp.where(lo1, qp[1], z1),
                jnp.where(lo1, z1, qpr[2]), jnp.where(lo1, z1, qpr[3]),
                jnp.where(lo1, qpr[0], z1), jnp.where(lo1, qpr[1], z1),
                jnp.where(lo1, z1, qp[2]), jnp.where(lo1, z1, qp[3])]
        qm = jnp.zeros((8, LANES), F32)
        for r in range(8):
            qm = jnp.where(r8 == r, rows[r], qm)
        qm_l.append(qm)
        so_l.append(_dot(qm.astype(BF16), kt_ref[bi].astype(BF16)))
        kn_l.append(jnp.broadcast_to(z_ref[bi:bi + 1, KA:KA + 128], (8, LANES)))
        vn_l.append(jnp.broadcast_to(z_ref[bi:bi + 1, VA:VA + 128], (8, LANES)))
    qm_all = jnp.concatenate(qm_l, axis=0)
    s_old = jnp.concatenate(so_l, axis=0)
    vn_all = jnp.concatenate(vn_l, axis=0)
    s_new = jnp.sum(qm_all * jnp.concatenate(kn_l, axis=0), axis=1, keepdims=True)
    sbias_all = jnp.concatenate([sbias] * SK2_BB, axis=0)
    sink_all = jnp.concatenate([sink] * SK2_BB, axis=0)
    s = jnp.where(lst == WINDOW - 1, s_new, pltpu.roll(s_old, WINDOW - 1, 1)) + sbias_all
    mx = jnp.maximum(jnp.max(s, axis=-1, keepdims=True), sink_all)
    pe = jnp.exp(s - mx)
    den = jnp.sum(pe, axis=-1, keepdims=True) + jnp.exp(sink_all - mx)
    pn = pe * (1.0 / den)
    p_old = jnp.where(lst == 0, 0.0, pltpu.roll(pn, 1, 1))
    oa_l = [_dot_nt(p_old[8 * bi:8 * bi + 8].astype(BF16), vt_ref[bi].astype(BF16))
            for bi in range(SK2_BB)]
    oa = jnp.concatenate(oa_l, axis=0) + pn[:, WINDOW - 1:WINDOW] * vn_all
    oar = pltpu.roll(oa, HEAD_DIM, 1)
    for bi in range(SK2_BB):
        r0 = 8 * bi
        pairs = [jnp.where(lo1, oa[r0:r0 + 1], oar[r0 + 4:r0 + 5]),
                 jnp.where(lo1, oa[r0 + 1:r0 + 2], oar[r0 + 5:r0 + 6]),
                 jnp.where(lo1, oar[r0 + 2:r0 + 3], oa[r0 + 6:r0 + 7]),
                 jnp.where(lo1, oar[r0 + 3:r0 + 4], oa[r0 + 7:r0 + 8])]
        for p in range(4):
            yatt_ref[bi:bi + 1, 128 * p:128 * p + 128] = pairs[p]

    for bi in range(SK2_BB):
        b = i * SK2_BB + bi
        shift = WINDOW - 1 - b
        kcol = pltpu.roll(zt_ref[ZT_KA:ZT_KA + 128, :], shift, 1)
        vcol = pltpu.roll(zt_ref[ZT_VA:ZT_VA + 128, :], shift, 1)
        ko_ref[bi] = jnp.where(last_lane, kcol, pltpu.roll(kt_ref[bi], WINDOW - 1, 1))
        vo_ref[bi] = jnp.where(last_lane, vcol, pltpu.roll(vt_ref[bi], WINDOW - 1, 1))

        qm_row = z_ref[bi:bi + 1, QM:QM + 256]
        vm_row = z_ref[bi:bi + 1, VM:VM + 512]
        qbd = jnp.where(own64, qm_row, 0.0)
        ct_all = jnp.concatenate([ct_ref[bi, hd] for hd in range(M_HEADS)], axis=0)
        cq = _dot(qbd.astype(BF16), ct_all.astype(BF16))
        for hd in range(M_HEADS):
            cq_ref[bi:bi + 1, 128 * hd:128 * hd + 128] = cq[hd:hd + 1, :]
        dsel = jnp.zeros((1, 4 * LANES), F32)
        for hd in range(M_HEADS):
            dsel = jnp.where(l512r == hd, scal_ref[b, SC_D + hd], dsel)
        vs = vm_row * dsel
        for pr in range(2):
            vsel = jnp.where(row_b == b, vs[:, 256 * pr:256 * pr + 256], 0.0).astype(BF16)
            upd = _dot(ktm[pr], vsel)
            for hh in range(2):
                hd = 2 * pr + hh
                co_ref[bi, hd] = (scal_ref[b, SC_INTER + hd] * ct_ref[bi, hd]
                                  + upd[64 * hh:64 * hh + 64, 128 * hh:128 * hh + 128])


def _sample_mixer_call(z, zt, kt, vt, ct, sbuck, scal_small, relb_t, sinks):
    nb = z.shape[0]
    assert nb % SK2_BB == 0 and nb == LANES
    bb = SK2_BB
    out_shapes = (
        jax.ShapeDtypeStruct((nb, 512), F32),
        jax.ShapeDtypeStruct((nb, 512), F32),
        jax.ShapeDtypeStruct(kt.shape, F32),
        jax.ShapeDtypeStruct(vt.shape, F32),
        jax.ShapeDtypeStruct(ct.shape, F32),
    )
    blk = lambda s: pl.BlockSpec((bb,) + tuple(s[1:]), lambda i: (i,) + (0,) * (len(s) - 1))
    full = lambda s: pl.BlockSpec(tuple(s), lambda i: (0,) * len(s))
    in_specs = [blk(z.shape), full(zt.shape), blk(kt.shape), blk(vt.shape), blk(ct.shape),
                full(sbuck.shape), _smem(), _smem(), _smem()]
    out_specs = (blk((nb, 512)), blk((nb, 512)), blk(kt.shape), blk(vt.shape), blk(ct.shape))
    return pl.pallas_call(
        _sample_mixer_kernel,
        grid=(nb // bb,),
        in_specs=in_specs,
        out_specs=out_specs,
        out_shape=out_shapes,
        scratch_shapes=[pltpu.VMEM((8, LANES), F32)],
        compiler_params=pltpu.CompilerParams(dimension_semantics=("arbitrary",)),
        name="sample_mixers",
    )(z, zt, kt, vt, ct, sbuck, scal_small, relb_t, sinks)


def kernel(x_prompt, x_sample, cache_k_win, cache_v_win, state_mlstm_C, state_mlstm_n, state_mlstm_m,
           rel_bias, w_in, b_if, sinks, g_attn_norm, g_head, w_att_out, w_mlstm_out, w_out,
           g_ffn_norm, w_gate, w_up, w_down, g_final):
    depth = w_in.shape[0]
    assert depth == 1
    bsz, s_len, d = x_prompt.shape
    nb = x_sample.shape[0]
    assert x_sample.shape[1] == 1 and cache_k_win.shape[2] == WINDOW

    wt = w_in[0].T
    gattn = g_attn_norm[0].reshape(1, d)
    gffn = g_ffn_norm[0].reshape(1, d)
    gfin = g_final.reshape(1, d)
    ghead = g_head[0].reshape(1, 512)
    bif = jnp.concatenate([b_if[0].reshape(1, 2 * M_HEADS), jnp.zeros((1, LANES - 2 * M_HEADS), F32)], axis=1)
    sink_v = sinks[0]
    relb_t = rel_bias.T

    qi = jnp.arange(WINDOW)[:, None]
    kj = jnp.arange(WINDOW)[None, :]
    fbuck = _t5_bucket(jnp.where(kj <= qi, qi - kj, WINDOW + qi - kj)).astype(jnp.int32)
    sbuck = jnp.broadcast_to(_t5_bucket(WINDOW - 1 - kj), (8, WINDOW)).astype(jnp.int32)

    feat = N_KV_HEADS * HEAD_DIM
    kt_in = jnp.transpose(cache_k_win[0], (0, 2, 3, 1)).reshape(nb, feat, WINDOW)
    vt_in = jnp.transpose(cache_v_win[0], (0, 2, 3, 1)).reshape(nb, feat, WINDOW)
    ct_in = jnp.transpose(state_mlstm_C[0], (0, 1, 3, 2))
    nt_in = jnp.transpose(state_mlstm_n[0], (1, 2, 0))
    mt_in = state_mlstm_m[0].T
    z_s, zt_s, nt_s, mt_s, scal_s = _sample_inproj_call(x_sample, gattn, wt, nt_in, mt_in, b_if[0])
    yatt_s, cq_s, kt_s, vt_s, ct_s = _sample_mixer_call(
        z_s, zt_s, kt_in, vt_in, ct_in, sbuck, scal_s[:, :SC_COLS], relb_t, sink_v)
    y_p, kt_p, vt_p, ct_p, n_p, m_p, y_s = _layer_call(
        x_prompt, wt, w_att_out[0], w_mlstm_out[0], w_out[0], w_gate[0], w_up[0], w_down[0],
        gattn, gffn, gfin, ghead, bif, fbuck, relb_t, sink_v,
        x_sample, z_s, yatt_s, cq_s, scal_s)

    def window_out(t, n):
        return jnp.transpose(t.reshape(n, N_KV_HEADS, HEAD_DIM, WINDOW), (0, 3, 1, 2))[None]

    return (y_p, y_s,
            window_out(kt_p, bsz), window_out(vt_p, bsz),
            jnp.transpose(ct_p, (0, 1, 3, 2))[None], n_p[None], m_p.reshape(1, bsz, M_HEADS),
            window_out(kt_s, nb), window_out(vt_s, nb),
            jnp.transpose(ct_s, (0, 1, 3, 2))[None], jnp.transpose(nt_s, (2, 0, 1))[None], mt_s.T[None])
```

```python
import functools
import math

import jax
import jax.numpy as jnp
from jax import lax
from jax.experimental import pallas as pl
from jax.experimental.pallas import tpu as pltpu

F32 = jnp.float32
BF16 = jnp.bfloat16

HEAD_DIM = 64
N_Q_HEADS = 8
N_KV_HEADS = 2
WINDOW = 128
N_BUCKETS = 32
MAX_DISTANCE = WINDOW
M_HEADS = 4
EPS = 1e-6
NEG = -1e30

LANES = 128
VMEM_LIMIT_BYTES = 61 * 1024 * 1024

QA, KA, VA, QM, KM, VM, OM, IF, GA, GM = 0, 512, 640, 768, 1024, 1280, 1792, 2304, 2432, 3456
N_CAT = 4480
Z_GROUPS = ((768, 2432), (0, 768), (2432, 4480))
T_BLK = 256
N_SUB = 2


def _dot(a, b):
    return jnp.dot(a, b, preferred_element_type=F32)


def _dot_nt(a, b):
    return lax.dot_general(a, b, (((1,), (1,)), ((), ())), preferred_element_type=F32)


def _sigmoid(x):
    return 1.0 / (1.0 + jnp.exp(-x))


def _log_sigmoid(x):
    return jnp.minimum(x, 0.0) - jnp.log1p(jnp.exp(-jnp.abs(x)))


def _rms(x, g):
    ms = jnp.mean(x * x, axis=-1, keepdims=True)
    return x * lax.rsqrt(ms + EPS) * g


def _t5_bucket(dist):
    n = jnp.maximum(dist, 0)
    max_exact = N_BUCKETS // 2
    nf = jnp.maximum(n, 1).astype(F32)
    large = max_exact + jnp.floor(jnp.log(nf / max_exact) / math.log(MAX_DISTANCE / max_exact)
                                  * (N_BUCKETS - max_exact)).astype(jnp.int32)
    large = jnp.minimum(large, N_BUCKETS - 1)
    return jnp.where(n < max_exact, n, large)


def _bias_from_buckets(buckets, relb_ref, head):
    acc = jnp.zeros(buckets.shape, F32)
    for k in range(N_BUCKETS):
        acc = jnp.where(buckets == k, relb_ref[head, k], acc)
    return acc


def _layer_kernel(nj, x_ref, wt_hbm, wao_hbm, wmo_hbm, wout_hbm, wg_hbm, wu_hbm, wd_hbm,
                  gattn_ref, gffn_ref, gfin_ref, ghead_ref, bif_ref, fbuck_ref, relb_ref, sinks_ref,
                  xs_ref, zs_ref, yatts_ref, cqs_ref, scals_ref,
                  y_ref, kwin_ref, vwin_ref, c_ref, n_ref, m_ref, ys_ref,
                  z_scr, yatt_scr, ym_scr, a_scr, bias_scr, kprev_scr, vprev_scr, cbd_scr, st_scr,
                  wcat_ref, wao_ref, wmo_ref, wout_ref, wg_ref, wu_ref, wd_ref, dma_sem):
    T = T_BLK
    t = pl.program_id(0)
    n_prompt = pl.num_programs(0) - 1
    is_prompt = t < n_prompt
    j = lax.rem(t, nj)

    @pl.when(t == 0)
    def _first_step():
        fb = fbuck_ref[...]
        for h in range(N_Q_HEADS):
            bias_scr[h] = _bias_from_buckets(fb, relb_ref, h)
        chunks = _weight_chunks(wt_hbm, wcat_ref, ((wao_hbm, wao_ref), (wmo_hbm, wmo_ref),
                                                   (wout_hbm, wout_ref), (wg_hbm, wg_ref),
                                                   (wu_hbm, wu_ref), (wd_hbm, wd_ref)))
        _cast_weights(chunks, z_scr, dma_sem)

    @pl.when(jnp.logical_and(is_prompt, j == 0))
    def _reset_state():
        kprev_scr[...] = jnp.zeros_like(kprev_scr)
        vprev_scr[...] = jnp.zeros_like(vprev_scr)
        cbd_scr[...] = jnp.zeros_like(cbd_scr)
        st_scr[...] = jnp.zeros_like(st_scr)

    @pl.when(is_prompt)
    def _prompt_step():
        for s in range(N_SUB):
            _in_projection(x_ref.at[s * T:(s + 1) * T, :], gattn_ref, wcat_ref, z_scr.at[s])
        carry = _load_carry(kprev_scr, vprev_scr, cbd_scr, st_scr)
        for s in range(N_SUB):
            carry = _mixers(z_scr.at[s], yatt_scr.at[s], ym_scr.at[s], carry, ghead_ref, bif_ref,
                            bias_scr, sinks_ref, first_of_sequence=(j == 0) if s == 0 else None)
        _store_carry(carry, kprev_scr, vprev_scr, cbd_scr, st_scr)
        for s in range(N_SUB):
            _tail(x_ref.at[s * T:(s + 1) * T, :], z_scr.at[s], yatt_scr.at[s], ym_scr.at[s], a_scr,
                  wao_ref, wmo_ref, wout_ref, wg_ref, wu_ref, wd_ref, gffn_ref, gfin_ref,
                  y_ref.at[s * T:(s + 1) * T, :])

    @pl.when(jnp.logical_not(is_prompt))
    def _sample_step():
        nb = xs_ref.shape[0]
        scal = scals_ref[...]
        yatt_scr[0, 0:nb, :] = yatts_ref[...].astype(BF16)
        for hd in range(M_HEADS):
            inter = scal[:, SC_INTER + hd:SC_INTER + hd + 1]
            w = scal[:, SC_W + hd:SC_W + hd + 1]
            scale = scal[:, SC_SCALE + hd:SC_SCALE + hd + 1]
            hv = (inter * cqs_ref[:, 128 * hd:128 * hd + 128]
                  + w * zs_ref[:, VM + 128 * hd:VM + 128 * hd + 128]) * scale
            hn = hv * lax.rsqrt(jnp.mean(hv * hv, axis=-1, keepdims=True) + EPS)
            hn = hn * ghead_ref[:, 128 * hd:128 * hd + 128]
            om = zs_ref[:, OM + 128 * hd:OM + 128 * hd + 128]
            ym_scr[0, 0:nb, 128 * hd:128 * hd + 128] = (hn * _sigmoid(om)).astype(BF16)
        _tail(xs_ref, zs_ref, yatt_scr.at[0, 0:nb, :], ym_scr.at[0, 0:nb, :], a_scr.at[0:nb, :],
              wao_ref, wmo_ref, wout_ref, wg_ref, wu_ref, wd_ref, gffn_ref, gfin_ref, ys_ref)

    @pl.when(jnp.logical_and(is_prompt, j == nj - 1))
    def _write_state():
        kwin_ref[...] = kprev_scr[...].T
        vwin_ref[...] = vprev_scr[...].T
        stn = st_scr[...]
        cts = [cbd_scr[0].T, cbd_scr[1].T]
        for hd in range(M_HEADS):
            p, hh = hd // 2, hd % 2
            c_ref[hd] = cts[p][64 * hh:64 * hh + 64, 128 * hh:128 * hh + 128]
            n_ref[hd:hd + 1, :] = stn[p:p + 1, 64 * hh:64 * hh + 64]
            m_ref[0:1, hd:hd + 1] = stn[2 + hd:3 + hd, 0:1]


def _in_projection(x_ref, gattn_ref, wcat_ref, z_ref):
    h = _rms(x_ref[...], gattn_ref[...]).astype(BF16)
    for c0, c1 in Z_GROUPS:
        z_ref[:, c0:c1] = _dot(h, wcat_ref[:, c0:c1])


def _weight_chunks(wt_hbm, wcat_ref, plain):
    r = T_BLK
    chunks = [(wt_hbm.at[a:a + r, :], wcat_ref.at[:, a:a + r], True) for a in range(0, W_IF, r)]
    chunks.append((wt_hbm.at[W_IF:W_GA, :], wcat_ref.at[:, IF:IF + LANES], True))
    chunks += [(wt_hbm.at[a:a + r, :], wcat_ref.at[:, GA + a - W_GA:GA + a - W_GA + r], True)
               for a in range(W_GA, W_END, r)]
    for src, dst in plain:
        chunks += [(src.at[a:a + r, :], dst.at[a:a + r, :], False) for a in range(0, src.shape[0], r)]
    return chunks


def _cast_weights(chunks, stage_ref, sem):
    def copy(i):
        src = chunks[i][0]
        n, c = src.shape
        return pltpu.make_async_copy(src, stage_ref.at[i % 2, 0:n, 0:c], sem.at[i % 2])

    copy(0).start()
    for i, (src, dst, transpose) in enumerate(chunks):
        if i + 1 < len(chunks):
            copy(i + 1).start()
        copy(i).wait()
        n, c = src.shape
        val = stage_ref[i % 2, 0:n, 0:c]
        rows = dst.shape[1] if transpose else dst.shape[0]
        if rows != n:
            val = jnp.concatenate([val, jnp.zeros((rows - n, c), F32)], axis=0)
        dst[...] = (val.T if transpose else val).astype(BF16)


def _load_carry(kprev_scr, vprev_scr, cbd_scr, st_scr):
    st = st_scr[...]
    return dict(kp=kprev_scr[...], vp=vprev_scr[...], cbd=[cbd_scr[0], cbd_scr[1]],
                n=[st[0:1, :], st[1:2, :]], m=[st[2 + hd:3 + hd, 0:1] for hd in range(M_HEADS)])


def _store_carry(carry, kprev_scr, vprev_scr, cbd_scr, st_scr):
    kprev_scr[...] = carry["kp"]
    vprev_scr[...] = carry["vp"]
    for p in range(2):
        cbd_scr[p] = carry["cbd"][p]
        st_scr[p:p + 1, :] = carry["n"][p]
    for hd in range(M_HEADS):
        st_scr[2 + hd:3 + hd, :] = jnp.broadcast_to(carry["m"][hd], (1, LANES))


def _tail(x_ref, z_ref, yatt_ref, ym_ref, a_ref, wao_ref, wmo_ref, wout_ref, wg_ref, wu_ref, wd_ref,
          gffn_ref, gfin_ref, y_ref):
    d_ff = wg_ref.shape[1]
    ya = _dot(yatt_ref[...], wao_ref[...])
    ymm = _dot(ym_ref[...], wmo_ref[...])
    mixed = _sigmoid(z_ref[:, GA:GA + 1024]) * ya + _sigmoid(z_ref[:, GM:GM + 1024]) * ymm
    x1 = x_ref[...] + _dot(mixed.astype(BF16), wout_ref[...])
    h2 = _rms(x1, gffn_ref[...]).astype(BF16)
    for c0 in range(0, d_ff, 256):
        g = _dot(h2, wg_ref[:, c0:c0 + 256])
        u = _dot(h2, wu_ref[:, c0:c0 + 256])
        a_ref[:, c0:c0 + 256] = (g * _sigmoid(g) * u).astype(BF16)
    x2 = x1 + _dot(a_ref[...], wd_ref[...])
    y_ref[...] = _rms(x2, gfin_ref[...])


def _mixers(z_ref, yatt_ref, ym_ref, carry, ghead_ref, bif_ref, bias_scr, sinks_ref, first_of_sequence):
    T = T_BLK
    lane = lax.broadcasted_iota(jnp.int32, (T, LANES), 1)
    lane_lo = lane < HEAD_DIM
    row = lax.broadcasted_iota(jnp.int32, (T, LANES), 0)
    qi = jnp.where(row < 128, row, row - 128)
    tri2 = lane <= qi
    if first_of_sequence is not None:
        valid0 = lane <= qi + jnp.where(first_of_sequence, 0, 2 * LANES)
    rowc = lax.broadcasted_iota(jnp.int32, (T, 1), 0)
    k_all = z_ref[:, KA:KA + 128]
    v_all = z_ref[:, VA:VA + 128]
    for sb in range(2):
        r0 = 128 * sb
        if sb == 0:
            kp, vp = carry["kp"], carry["vp"]
        else:
            kp, vp = k_all[0:128], v_all[0:128]
        kcat = jnp.concatenate([kp, k_all[r0:r0 + 128]], axis=0)
        vcat = jnp.concatenate([vp, v_all[r0:r0 + 128]], axis=0)
        kroll = pltpu.roll(kcat, HEAD_DIM, 1)
        vroll = pltpu.roll(vcat, HEAD_DIM, 1)
        zero = jnp.zeros_like(kcat)
        k_mats = [jnp.where(lane_lo, kcat, zero), jnp.where(lane_lo, zero, kroll),
                  jnp.where(lane_lo, kroll, zero), jnp.where(lane_lo, zero, kcat)]
        v_mats = [jnp.where(lane_lo, vcat, zero), jnp.where(lane_lo, zero, vroll),
                  jnp.where(lane_lo, vroll, zero), jnp.where(lane_lo, zero, vcat)]
        q = (z_ref[r0:r0 + 128, QA:QA + 512] * (HEAD_DIM ** -0.5)).astype(BF16)
        lhs_a = jnp.concatenate([q[:, 0:128], q[:, 128:256]], axis=0)
        lhs_b = jnp.concatenate([q[:, 256:384], q[:, 384:512]], axis=0)
        groups = [(lhs_a, 0, 0, 2), (lhs_a, 1, 1, 3), (lhs_b, 2, 4, 6), (lhs_b, 3, 5, 7)]
        pcs = []
        for lhs, mi, ha, hb in groups:
            s = _dot_nt(lhs, k_mats[mi].astype(BF16))
            sf = jnp.where(tri2, s[:, 128:256], s[:, 0:128])
            sf = sf + jnp.concatenate([bias_scr[ha], bias_scr[hb]], axis=0)
            if sb == 0 and first_of_sequence is not None:
                sf = jnp.where(valid0, sf, NEG)
            sink = jnp.where(rowc < 128, sinks_ref[ha], sinks_ref[hb])
            mx = jnp.maximum(jnp.max(sf, axis=-1, keepdims=True), sink)
            p = jnp.exp(sf - mx)
            den = jnp.sum(p, axis=-1, keepdims=True) + jnp.exp(sink - mx)
            pn = p * (1.0 / den)
            zp = jnp.zeros_like(pn)
            pcs.append(jnp.concatenate([jnp.where(tri2, zp, pn), jnp.where(tri2, pn, zp)],
                                       axis=1).astype(BF16))
        o_a = _dot(jnp.concatenate([pcs[0], pcs[1]], axis=1),
                   jnp.concatenate([v_mats[0], v_mats[1]], axis=0).astype(BF16))
        o_b = _dot(jnp.concatenate([pcs[2], pcs[3]], axis=1),
                   jnp.concatenate([v_mats[2], v_mats[3]], axis=0).astype(BF16))
        yatt_ref[r0:r0 + 128, 0:128] = o_a[0:128].astype(BF16)
        yatt_ref[r0:r0 + 128, 128:256] = o_a[128:256].astype(BF16)
        yatt_ref[r0:r0 + 128, 256:384] = o_b[0:128].astype(BF16)
        yatt_ref[r0:r0 + 128, 384:512] = o_b[128:256].astype(BF16)
    new_carry = dict(kp=k_all[128:256], vp=v_all[128:256], cbd=[None, None], n=[None, None],
                     m=[None] * M_HEADS)

    zif = z_ref[:, IF:IF + 128] + bif_ref[...]
    gl = jnp.where(lane < M_HEADS, zif, _log_sigmoid(zif))
    gl_t = gl.T
    tr = lax.broadcasted_iota(jnp.int32, (T, T), 0)
    ts = lax.broadcasted_iota(jnp.int32, (T, T), 1)
    tril = ts <= tr
    triu = tr <= ts
    row2 = lax.broadcasted_iota(jnp.int32, (2 * LANES, LANES), 0)
    lane2 = lax.broadcasted_iota(jnp.int32, (2 * LANES, LANES), 1)
    bd_mask = (row2 < LANES) == (lane2 < HEAD_DIM)
    for p in range(2):
        q_pair = z_ref[:, QM + 128 * p:QM + 128 * p + 128]
        k_pair = z_ref[:, KM + 128 * p:KM + 128 * p + 128] * (HEAD_DIM ** -0.5)
        v_pair = z_ref[:, VM + 256 * p:VM + 256 * p + 256]
        cbd = carry["cbd"][p]
        n_pair = carry["n"][p]
        q_bf = q_pair.astype(BF16)
        qc = _dot_nt(q_bf, cbd.astype(BF16))
        qn_prod = q_pair * n_pair
        ws, decays, m_ends = [], [], []
        for hh in range(2):
            hd = 2 * p + hh
            hmask = lane_lo if hh == 0 else jnp.logical_not(lane_lo)
            ig_c = gl[:, hd:hd + 1]
            lf_c = gl[:, M_HEADS + hd:M_HEADS + hd + 1]
            ig_r = gl_t[hd:hd + 1, :]
            lf_r = gl_t[M_HEADS + hd:M_HEADS + hd + 1, :]
            m_prev = carry["m"][hd]
            b_c = jnp.sum(jnp.where(tril, lf_r, 0.0), axis=1, keepdims=True)
            b_r = jnp.sum(jnp.where(triu, lf_c, 0.0), axis=0, keepdims=True)
            a_r = ig_r - b_r
            cm_c = jnp.max(jnp.where(tril, a_r, NEG), axis=1, keepdims=True)
            mt_c = b_c + jnp.maximum(m_prev, cm_c)
            g_c = b_c - mt_c
            dm = jnp.exp(jnp.where(tril, a_r + g_c, NEG))
            inter_c = jnp.exp(m_prev + g_c)
            k_h = jnp.where(hmask, k_pair, 0.0).astype(BF16)
            w = dm * _dot_nt(q_bf, k_h)
            v_h = v_pair[:, 128 * hh:128 * hh + 128]
            num = inter_c * qc[:, 128 * hh:128 * hh + 128] + _dot(w.astype(BF16), v_h.astype(BF16))
            qn = jnp.sum(jnp.where(hmask, qn_prod, 0.0), axis=1, keepdims=True)
            den = inter_c * qn + jnp.sum(w, axis=1, keepdims=True)
            hv = num / jnp.maximum(jnp.abs(den), jnp.exp(-mt_c))
            hn = hv * lax.rsqrt(jnp.mean(hv * hv, axis=-1, keepdims=True) + EPS)
            hn = hn * ghead_ref[:, 128 * hd:128 * hd + 128]
            om = z_ref[:, OM + 128 * hd:OM + 128 * hd + 128]
            ym_ref[:, 128 * hd:128 * hd + 128] = (hn * _sigmoid(om)).astype(BF16)
            b_end = b_c[T - 1:T, :]
            m_end = mt_c[T - 1:T, :]
            ws.append(jnp.exp((ig_c - b_c) + b_end - m_end))
            decays.append(jnp.exp(m_prev + b_end - m_end))
            m_ends.append(m_end)
        kw = k_pair * jnp.where(lane_lo, ws[0], ws[1])
        upd = _dot(v_pair.T.astype(BF16), kw.astype(BF16))
        dec_rows = jnp.where(row2[:, 0:1] < LANES, decays[0], decays[1])
        new_carry["cbd"][p] = dec_rows * cbd + jnp.where(bd_mask, upd, 0.0)
        dec_lanes = jnp.where(lane_lo[0:1, :], decays[0], decays[1])
        new_carry["n"][p] = dec_lanes * n_pair + jnp.sum(kw, axis=0, keepdims=True)
        for hh in range(2):
            new_carry["m"][2 * p + hh] = m_ends[hh]
    return new_carry


def _resident(shape):
    zeros = (0,) * len(shape)
    return pl.BlockSpec(shape, lambda t: zeros, pipeline_mode=pl.Buffered(1))


def _smem():
    return pl.BlockSpec(memory_space=pltpu.SMEM)


def _layer_call(x, wt, wao, wmo, wout, wg, wu, wd, gattn, gffn, gfin, ghead, bif, fbuck, relb, sinks,
                xs3, zs, yatts, cqs, scals):
    bsz, s_len, d = x.shape
    nb = xs3.shape[0]
    t_step = T_BLK * N_SUB
    assert s_len % t_step == 0 and d == 1024 and nb <= T_BLK
    assert wt.shape == (W_END, d) and N_SUB >= 2
    d_ff = wg.shape[1]
    assert all(w.shape[0] % T_BLK == 0 and w.shape[1] <= N_CAT for w in (wao, wmo, wout, wg, wu, wd))
    hbm = pl.BlockSpec(memory_space=pl.ANY)
    nj = s_len // t_step
    n_prompt = bsz * nj
    out_shapes = (
        jax.ShapeDtypeStruct((bsz, s_len, d), F32),
        jax.ShapeDtypeStruct((bsz, WINDOW, 128), F32),
        jax.ShapeDtypeStruct((bsz, WINDOW, 128), F32),
        jax.ShapeDtypeStruct((bsz, M_HEADS, 64, 128), F32),
        jax.ShapeDtypeStruct((bsz, M_HEADS, 64), F32),
        jax.ShapeDtypeStruct((bsz, 1, M_HEADS), F32),
        jax.ShapeDtypeStruct((nb, 1, d), F32),
    )

    def seq(t):
        return jnp.minimum(t, n_prompt - 1) // nj

    def blk(t):
        return jnp.minimum(t, n_prompt - 1) % nj

    in_specs = [
        pl.BlockSpec((None, t_step, d), lambda t: (seq(t), blk(t), 0)),
        hbm, hbm, hbm, hbm, hbm, hbm, hbm,
        _resident(gattn.shape), _resident(gffn.shape), _resident(gfin.shape), _resident(ghead.shape),
        _resident(bif.shape), _resident(fbuck.shape), _smem(), _smem(),
        pl.BlockSpec((nb, None, d), lambda t: (0, 0, 0), pipeline_mode=pl.Buffered(1)),
        _resident(zs.shape), _resident(yatts.shape), _resident(cqs.shape), _resident(scals.shape),
    ]
    out_specs = (
        pl.BlockSpec((None, t_step, d), lambda t: (seq(t), blk(t), 0)),
        pl.BlockSpec((None, WINDOW, 128), lambda t: (seq(t), 0, 0)),
        pl.BlockSpec((None, WINDOW, 128), lambda t: (seq(t), 0, 0)),
        pl.BlockSpec((None, M_HEADS, 64, 128), lambda t: (seq(t), 0, 0, 0)),
        pl.BlockSpec((None, M_HEADS, 64), lambda t: (seq(t), 0, 0)),
        pl.BlockSpec((None, 1, M_HEADS), lambda t: (seq(t), 0, 0)),
        pl.BlockSpec((nb, None, d), lambda t: (0, 0, 0)),
    )
    scratch = [
        pltpu.VMEM((N_SUB, T_BLK, N_CAT), F32),
        pltpu.VMEM((N_SUB, T_BLK, 512), BF16),
        pltpu.VMEM((N_SUB, T_BLK, 512), BF16),
        pltpu.VMEM((T_BLK, d_ff), BF16),
        pltpu.VMEM((N_Q_HEADS, 128, 128), F32),
        pltpu.VMEM((128, 128), F32),
        pltpu.VMEM((128, 128), F32),
        pltpu.VMEM((2, 256, 128), F32),
        pltpu.VMEM((8, 128), F32),
        pltpu.VMEM((d, N_CAT), BF16),
        pltpu.VMEM(wao.shape, BF16), pltpu.VMEM(wmo.shape, BF16), pltpu.VMEM(wout.shape, BF16),
        pltpu.VMEM(wg.shape, BF16), pltpu.VMEM(wu.shape, BF16), pltpu.VMEM(wd.shape, BF16),
        pltpu.SemaphoreType.DMA((2,)),
    ]
    return pl.pallas_call(
        functools.partial(_layer_kernel, nj),
        grid=(n_prompt + 1,),
        in_specs=in_specs,
        out_specs=out_specs,
        out_shape=out_shapes,
        scratch_shapes=scratch,
        compiler_params=pltpu.CompilerParams(
            dimension_semantics=("arbitrary",),
            vmem_limit_bytes=VMEM_LIMIT_BYTES),
        name="layer",
    )(x, wt, wao, wmo, wout, wg, wu, wd, gattn, gffn, gfin, ghead, bif, fbuck, relb, sinks,
      xs3, zs, yatts, cqs, scals)


W_QA, W_KA, W_VA, W_QM, W_KM, W_VM, W_OM, W_IF, W_GA, W_GM, W_END = (
    0, 512, 640, 768, 1024, 1280, 1792, 2304, 2312, 3336, 4360)
ZT_KA, ZT_VA, ZT_QM, ZT_KM, ZT_IF, ZT_ROWS = 0, 128, 256, 512, 768, 776


def _sample_inproj_kernel(x_ref, g_ref, wt_hbm, nt_ref, mt_ref, bif_ref,
                          z_ref, zt_ref, no_ref, mo_ref, scal_ref,
                          wt_ref, sems):
    _sample_projection(x_ref, g_ref, wt_hbm, wt_ref, sems, z_ref, zt_ref)
    r8c = lax.broadcasted_iota(jnp.int32, (8, 1), 0)
    bias_col = jnp.zeros((8, 1), F32)
    for r in range(8):
        bias_col = jnp.where(r8c == r, bif_ref[r // M_HEADS, r % M_HEADS], bias_col)
    ift = zt_ref[ZT_IF:ZT_ROWS, :] + bias_col
    m0 = mt_ref[...]
    r16 = lax.broadcasted_iota(jnp.int32, (16, LANES), 0)
    table = jnp.zeros((16, LANES), F32)
    for hd in range(M_HEADS):
        ig = ift[hd:hd + 1, :]
        lf = _log_sigmoid(ift[M_HEADS + hd:M_HEADS + hd + 1, :])
        m_prev = m0[hd:hd + 1, :]
        a = ig - lf
        m_t = lf + jnp.maximum(m_prev, a)
        dgate = jnp.exp(a + lf - m_t)
        inter = jnp.exp(m_prev + lf - m_t)
        qt = zt_ref[ZT_QM + 64 * hd:ZT_QM + 64 * hd + 64, :]
        kt = zt_ref[ZT_KM + 64 * hd:ZT_KM + 64 * hd + 64, :] * (HEAD_DIM ** -0.5)
        nt = nt_ref[hd]
        no_ref[hd] = inter * nt + dgate * kt
        mo_ref[hd:hd + 1, :] = m_t
        qk = jnp.sum(qt * kt, axis=0, keepdims=True)
        nq = jnp.sum(nt * qt, axis=0, keepdims=True)
        w = dgate * qk
        den = inter * nq + w
        scale = 1.0 / jnp.maximum(jnp.abs(den), jnp.exp(-m_t))
        for base, val in ((SC_INTER, inter), (SC_W, w), (SC_SCALE, scale), (SC_D, dgate)):
            table = jnp.where(r16 == base + hd, val, table)
    full = jnp.concatenate([table, jnp.zeros((LANES - 16, LANES), F32)], axis=0)
    scal_ref[...] = full.T


WT_CHUNKS = ((W_QA, 768), (768, 1536), (1536, W_IF), (W_IF, W_GA), (W_GA, W_GM), (W_GM, W_END))


def _sample_projection(x_ref, g_ref, wt_hbm, wt_ref, sems, z_ref, zt_ref):
    copies = [pltpu.make_async_copy(wt_hbm.at[a:b, :], wt_ref.at[a:b, :], sems.at[k])
              for k, (a, b) in enumerate(WT_CHUNKS)]
    for c in copies:
        c.start()
    h32 = _rms(x_ref[...], g_ref[...])
    h = h32.astype(BF16)
    ht = h32.T.astype(BF16)
    copies[0].wait()
    z_ref[:, QA:768] = _dot_nt(h, wt_ref[W_QA:768, :].astype(BF16))
    zt_ref[ZT_KA:ZT_QM, :] = _dot(wt_ref[W_KA:W_QM, :].astype(BF16), ht)
    copies[1].wait()
    z_ref[:, 768:1536] = _dot_nt(h, wt_ref[768:1536, :].astype(BF16))
    zt_ref[ZT_QM:ZT_IF, :] = _dot(wt_ref[W_QM:W_VM, :].astype(BF16), ht)
    copies[2].wait()
    z_ref[:, 1536:W_IF] = _dot_nt(h, wt_ref[1536:W_IF, :].astype(BF16))
    copies[3].wait()
    w_if = jnp.concatenate([wt_ref[W_IF:W_GA, :], jnp.zeros((LANES - 8, wt_ref.shape[1]), F32)],
                           axis=0).astype(BF16)
    z_ref[:, IF:IF + LANES] = _dot_nt(h, w_if)
    zt_ref[ZT_IF:ZT_ROWS, :] = _dot(w_if, ht)[0:8, :]
    copies[4].wait()
    z_ref[:, GA:GM] = _dot_nt(h, wt_ref[W_GA:W_GM, :].astype(BF16))
    copies[5].wait()
    z_ref[:, GM:N_CAT] = _dot_nt(h, wt_ref[W_GM:W_END, :].astype(BF16))


def _sample_inproj_call(x3, gattn, wt, nt, mt, bif2):
    n, _, d = x3.shape
    assert n == LANES
    full = lambda s: pl.BlockSpec(tuple(s), lambda i: (0,) * len(s))
    return pl.pallas_call(
        _sample_inproj_kernel,
        grid=(1,),
        in_specs=[pl.BlockSpec((n, None, d), lambda i: (0, 0, 0)),
                  pl.BlockSpec((1, d), lambda i: (0, 0)),
                  pl.BlockSpec(memory_space=pl.ANY),
                  full(nt.shape), full(mt.shape), _smem()],
        out_specs=(full((n, N_CAT)), full((ZT_ROWS, n)), full(nt.shape), full(mt.shape),
                   full((n, LANES))),
        out_shape=(jax.ShapeDtypeStruct((n, N_CAT), F32), jax.ShapeDtypeStruct((ZT_ROWS, n), F32),
                   jax.ShapeDtypeStruct(nt.shape, F32), jax.ShapeDtypeStruct(mt.shape, F32),
                   jax.ShapeDtypeStruct((n, LANES), F32)),
        scratch_shapes=[pltpu.VMEM(wt.shape, F32), pltpu.SemaphoreType.DMA((len(WT_CHUNKS),))],
        compiler_params=pltpu.CompilerParams(dimension_semantics=("arbitrary",),
                                             vmem_limit_bytes=VMEM_LIMIT_BYTES),
        name="sample_inproj",
    )(x3, gattn, wt, nt, mt, bif2)


SK2_BB = 16
SC_INTER, SC_W, SC_SCALE, SC_D, SC_COLS = 0, 4, 8, 12, 16
HEAD_ROW_ORDER = (0, 2, 4, 6, 1, 3, 5, 7)


def _sample_mixer_kernel(z_ref, zt_ref, kt_ref, vt_ref, ct_ref, sbuck_ref,
                         scal_ref, relb_ref, sinks_ref,
                         yatt_ref, cq_ref, ko_ref, vo_ref, co_ref,
                         sbias_scr):
    i = pl.program_id(0)
    r8 = lax.broadcasted_iota(jnp.int32, (8, LANES), 0)
    l8 = lax.broadcasted_iota(jnp.int32, (8, LANES), 1)
    r8c = lax.broadcasted_iota(jnp.int32, (8, 1), 0)

    @pl.when(i == 0)
    def _prologue():
        sb = sbuck_ref[...]
        acc = jnp.zeros((8, LANES), F32)
        for rrow, hd in enumerate(HEAD_ROW_ORDER):
            acc = jnp.where(r8 == rrow, _bias_from_buckets(sb, relb_ref, hd), acc)
        sbias_scr[...] = acc

    sink = jnp.zeros((8, 1), F32)
    for rrow, hd in enumerate(HEAD_ROW_ORDER):
        sink = jnp.where(r8c == rrow, sinks_ref[hd], sink)
    sbias = sbias_scr[...]
    lane_w = lax.broadcasted_iota(jnp.int32, (WINDOW, LANES), 1)
    last_lane = lane_w == WINDOW - 1
    r256 = lax.broadcasted_iota(jnp.int32, (8, 2 * LANES), 0)
    l256 = lax.broadcasted_iota(jnp.int32, (8, 2 * LANES), 1)
    own64 = (l256 // HEAD_DIM) == r256
    l512r = lax.broadcasted_iota(jnp.int32, (1, 4 * LANES), 1) // LANES
    lo1 = lax.broadcasted_iota(jnp.int32, (1, LANES), 1) < HEAD_DIM
    row_b = lax.broadcasted_iota(jnp.int32, (LANES, 2 * LANES), 0)
    ktm = [(zt_ref[ZT_KM + 128 * pr:ZT_KM + 128 * pr + 128, :] * (HEAD_DIM ** -0.5)).astype(BF16)
           for pr in range(2)]

    nrow = 8 * SK2_BB
    lst = lax.broadcasted_iota(jnp.int32, (nrow, LANES), 1)
    z1 = jnp.zeros((1, LANES), F32)
    qm_l, so_l, kn_l, vn_l = [], [], [], []
    for bi in range(SK2_BB):
        qp = [z_ref[bi:bi + 1, QA + 128 * p:QA + 128 * p + 128] * (HEAD_DIM ** -0.5) for p in range(4)]
        qpr = [pltpu.roll(x, HEAD_DIM, 1) for x in qp]
        rows = [jnp.where(lo1, qp[0], z1), jnp.where(lo1, qp[1], z1),
                jnp.where(lo1, z1, qpr[2]), jnp.where(lo1, z1, qpr[3]),
                jnp.where(lo1, qpr[0], z1), jnp.where(lo1, qpr[1], z1),
                jnp.where(lo1, z1, qp[2]), jnp.where(lo1, z1, qp[3])]
        qm = jnp.zeros((8, LANES), F32)
        for r in range(8):
            qm = jnp.where(r8 == r, rows[r], qm)
        qm_l.append(qm)
        so_l.append(_dot(qm.astype(BF16), kt_ref[bi].astype(BF16)))
        kn_l.append(jnp.broadcast_to(z_ref[bi:bi + 1, KA:KA + 128], (8, LANES)))
        vn_l.append(jnp.broadcast_to(z_ref[bi:bi + 1, VA:VA + 128], (8, LANES)))
    qm_all = jnp.concatenate(qm_l, axis=0)
    s_old = jnp.concatenate(so_l, axis=0)
    vn_all = jnp.concatenate(vn_l, axis=0)
    s_new = jnp.sum(qm_all * jnp.concatenate(kn_l, axis=0), axis=1, keepdims=True)
    sbias_all = jnp.concatenate([sbias] * SK2_BB, axis=0)
    sink_all = jnp.concatenate([sink] * SK2_BB, axis=0)
    s = jnp.where(lst == WINDOW - 1, s_new, pltpu.roll(s_old, WINDOW - 1, 1)) + sbias_all
    mx = jnp.maximum(jnp.max(s, axis=-1, keepdims=True), sink_all)
    pe = jnp.exp(s - mx)
    den = jnp.sum(pe, axis=-1, keepdims=True) + jnp.exp(sink_all - mx)
    pn = pe * (1.0 / den)
    p_old = jnp.where(lst == 0, 0.0, pltpu.roll(pn, 1, 1))
    oa_l = [_dot_nt(p_old[8 * bi:8 * bi + 8].astype(BF16), vt_ref[bi].astype(BF16))
            for bi in range(SK2_BB)]
    oa = jnp.concatenate(oa_l, axis=0) + pn[:, WINDOW - 1:WINDOW] * vn_all
    oar = pltpu.roll(oa, HEAD_DIM, 1)
    for bi in range(SK2_BB):
        r0 = 8 * bi
        pairs = [jnp.where(lo1, oa[r0:r0 + 1], oar[r0 + 4:r0 + 5]),
                 jnp.where(lo1, oa[r0 + 1:r0 + 2], oar[r0 + 5:r0 + 6]),
                 jnp.where(lo1, oar[r0 + 2:r0 + 3], oa[r0 + 6:r0 + 7]),
                 jnp.where(lo1, oar[r0 + 3:r0 + 4], oa[r0 + 7:r0 + 8])]
        for p in range(4):
            yatt_ref[bi:bi + 1, 128 * p:128 * p + 128] = pairs[p]

    for bi in range(SK2_BB):
        b = i * SK2_BB + bi
        shift = WINDOW - 1 - b
        kcol = pltpu.roll(zt_ref[ZT_KA:ZT_KA + 128, :], shift, 1)
        vcol = pltpu.roll(zt_ref[ZT_VA:ZT_VA + 128, :], shift, 1)
        ko_ref[bi] = jnp.where(last_lane, kcol, pltpu.roll(kt_ref[bi], WINDOW - 1, 1))
        vo_ref[bi] = jnp.where(last_lane, vcol, pltpu.roll(vt_ref[bi], WINDOW - 1, 1))

        qm_row = z_ref[bi:bi + 1, QM:QM + 256]
        vm_row = z_ref[bi:bi + 1, VM:VM + 512]
        qbd = jnp.where(own64, qm_row, 0.0)
        ct_all = jnp.concatenate([ct_ref[bi, hd] for hd in range(M_HEADS)], axis=0)
        cq = _dot(qbd.astype(BF16), ct_all.astype(BF16))
        for hd in range(M_HEADS):
            cq_ref[bi:bi + 1, 128 * hd:128 * hd + 128] = cq[hd:hd + 1, :]
        dsel = jnp.zeros((1, 4 * LANES), F32)
        for hd in range(M_HEADS):
            dsel = jnp.where(l512r == hd, scal_ref[b, SC_D + hd], dsel)
        vs = vm_row * dsel
        for pr in range(2):
            vsel = jnp.where(row_b == b, vs[:, 256 * pr:256 * pr + 256], 0.0).astype(BF16)
            upd = _dot(ktm[pr], vsel)
            for hh in range(2):
                hd = 2 * pr + hh
                co_ref[bi, hd] = (scal_ref[b, SC_INTER + hd] * ct_ref[bi, hd]
                                  + upd[64 * hh:64 * hh + 64, 128 * hh:128 * hh + 128])


def _sample_mixer_call(z, zt, kt, vt, ct, sbuck, scal_small, relb_t, sinks):
    nb = z.shape[0]
    assert nb % SK2_BB == 0 and nb == LANES
    bb = SK2_BB
    out_shapes = (
        jax.ShapeDtypeStruct((nb, 512), F32),
        jax.ShapeDtypeStruct((nb, 512), F32),
        jax.ShapeDtypeStruct(kt.shape, F32),
        jax.ShapeDtypeStruct(vt.shape, F32),
        jax.ShapeDtypeStruct(ct.shape, F32),
    )
    blk = lambda s: pl.BlockSpec((bb,) + tuple(s[1:]), lambda i: (i,) + (0,) * (len(s) - 1))
    full = lambda s: pl.BlockSpec(tuple(s), lambda i: (0,) * len(s))
    in_specs = [blk(z.shape), full(zt.shape), blk(kt.shape), blk(vt.shape), blk(ct.shape),
                full(sbuck.shape), _smem(), _smem(), _smem()]
    out_specs = (blk((nb, 512)), blk((nb, 512)), blk(kt.shape), blk(vt.shape), blk(ct.shape))
    return pl.pallas_call(
        _sample_mixer_kernel,
        grid=(nb // bb,),
        in_specs=in_specs,
        out_specs=out_specs,
        out_shape=out_shapes,
        scratch_shapes=[pltpu.VMEM((8, LANES), F32)],
        compiler_params=pltpu.CompilerParams(dimension_semantics=("arbitrary",)),
        name="sample_mixers",
    )(z, zt, kt, vt, ct, sbuck, scal_small, relb_t, sinks)


def kernel(x_prompt, x_sample, cache_k_win, cache_v_win, state_mlstm_C, state_mlstm_n, state_mlstm_m,
           rel_bias, w_in, b_if, sinks, g_attn_norm, g_head, w_att_out, w_mlstm_out, w_out,
           g_ffn_norm, w_gate, w_up, w_down, g_final):
    depth = w_in.shape[0]
    assert depth == 1
    bsz, s_len, d = x_prompt.shape
    nb = x_sample.shape[0]
    assert x_sample.shape[1] == 1 and cache_k_win.shape[2] == WINDOW

    wt = w_in[0].T
    gattn = g_attn_norm[0].reshape(1, d)
    gffn = g_ffn_norm[0].reshape(1, d)
    gfin = g_final.reshape(1, d)
    ghead = g_head[0].reshape(1, 512)
    bif = jnp.concatenate([b_if[0].reshape(1, 2 * M_HEADS), jnp.zeros((1, LANES - 2 * M_HEADS), F32)], axis=1)
    sink_v = sinks[0]
    relb_t = rel_bias.T

    qi = jnp.arange(WINDOW)[:, None]
    kj = jnp.arange(WINDOW)[None, :]
    fbuck = _t5_bucket(jnp.where(kj <= qi, qi - kj, WINDOW + qi - kj)).astype(jnp.int32)
    sbuck = jnp.broadcast_to(_t5_bucket(WINDOW - 1 - kj), (8, WINDOW)).astype(jnp.int32)

    feat = N_KV_HEADS * HEAD_DIM
    kt_in = jnp.transpose(cache_k_win[0], (0, 2, 3, 1)).reshape(nb, feat, WINDOW)
    vt_in = jnp.transpose(cache_v_win[0], (0, 2, 3, 1)).reshape(nb, feat, WINDOW)
    ct_in = jnp.transpose(state_mlstm_C[0], (0, 1, 3, 2))
    nt_in = jnp.transpose(state_mlstm_n[0], (1, 2, 0))
    mt_in = state_mlstm_m[0].T
    z_s, zt_s, nt_s, mt_s, scal_s = _sample_inproj_call(x_sample, gattn, wt, nt_in, mt_in, b_if[0])
    yatt_s, cq_s, kt_s, vt_s, ct_s = _sample_mixer_call(
        z_s, zt_s, kt_in, vt_in, ct_in, sbuck, scal_s[:, :SC_COLS], relb_t, sink_v)
    y_p, kt_p, vt_p, ct_p, n_p, m_p, y_s = _layer_call(
        x_prompt, wt, w_att_out[0], w_mlstm_out[0], w_out[0], w_gate[0], w_up[0], w_down[0],
        gattn, gffn, gfin, ghead, bif, fbuck, relb_t, sink_v,
        x_sample, z_s, yatt_s, cq_s, scal_s)

    def window_out(t, n):
        return jnp.transpose(t.reshape(n, N_KV_HEADS, HEAD_DIM, WINDOW), (0, 3, 1, 2))[None]

    return (y_p, y_s,
            window_out(kt_p, bsz), window_out(vt_p, bsz),
            jnp.transpose(ct_p, (0, 1, 3, 2))[None], n_p[None], m_p.reshape(1, bsz, M_HEADS),
            window_out(kt_s, nb), window_out(vt_s, nb),
            jnp.transpose(ct_s, (0, 1, 3, 2))[None], jnp.transpose(nt_s, (2, 0, 1))[None], mt_s.T[None])
```

```python
import functools
import math

import jax
import jax.numpy as jnp
from jax import lax
from jax.experimental import pallas as pl
from jax.experimental.pallas import tpu as pltpu

F32 = jnp.float32
BF16 = jnp.bfloat16

HEAD_DIM = 64
N_Q_HEADS = 8
N_KV_HEADS = 2
WINDOW = 128
N_BUCKETS = 32
MAX_DISTANCE = WINDOW
M_HEADS = 4
EPS = 1e-6
NEG = -1e30

LANES = 128
VMEM_LIMIT_BYTES = 61 * 1024 * 1024

QA, KA, VA, QM, KM, VM, OM, IF, GA, GM = 0, 512, 640, 768, 1024, 1280, 1792, 2304, 2432, 3456
N_CAT = 4480
Z_GROUPS = ((768, 2432), (0, 768), (2432, 4480))
T_BLK = 256
N_SUB = 2


def _dot(a, b):
    return jnp.dot(a, b, preferred_element_type=F32)


def _dot_nt(a, b):
    return lax.dot_general(a, b, (((1,), (1,)), ((), ())), preferred_element_type=F32)


def _sigmoid(x):
    return 1.0 / (1.0 + jnp.exp(-x))


def _log_sigmoid(x):
    return jnp.minimum(x, 0.0) - jnp.log1p(jnp.exp(-jnp.abs(x)))


def _rms(x, g):
    ms = jnp.mean(x * x, axis=-1, keepdims=True)
    return x * lax.rsqrt(ms + EPS) * g


def _t5_bucket(dist):
    n = jnp.maximum(dist, 0)
    max_exact = N_BUCKETS // 2
    nf = jnp.maximum(n, 1).astype(F32)
    large = max_exact + jnp.floor(jnp.log(nf / max_exact) / math.log(MAX_DISTANCE / max_exact)
                                  * (N_BUCKETS - max_exact)).astype(jnp.int32)
    large = jnp.minimum(large, N_BUCKETS - 1)
    return jnp.where(n < max_exact, n, large)


def _bias_from_buckets(buckets, relb_ref, head):
    acc = jnp.zeros(buckets.shape, F32)
    for k in range(N_BUCKETS):
        acc = jnp.where(buckets == k, relb_ref[head, k], acc)
    return acc


def _layer_kernel(nj, x_ref, wt_hbm, wao_hbm, wmo_hbm, wout_hbm, wg_hbm, wu_hbm, wd_hbm,
                  gattn_ref, gffn_ref, gfin_ref, ghead_ref, bif_ref, fbuck_ref, relb_ref, sinks_ref,
                  xs_ref, zs_ref, yatts_ref, cqs_ref, scals_ref,
                  y_ref, kwin_ref, vwin_ref, c_ref, n_ref, m_ref, ys_ref,
                  z_scr, yatt_scr, ym_scr, a_scr, bias_scr, kprev_scr, vprev_scr, cbd_scr, st_scr,
                  wcat_ref, wao_ref, wmo_ref, wout_ref, wg_ref, wu_ref, wd_ref, dma_sem):
    T = T_BLK
    t = pl.program_id(0)
    n_prompt = pl.num_programs(0) - 1
    is_prompt = t < n_prompt
    j = lax.rem(t, nj)

    @pl.when(t == 0)
    def _first_step():
        def build_bias():
            fb = fbuck_ref[...]
            for h in range(N_Q_HEADS):
                bias_scr[h] = _bias_from_buckets(fb, relb_ref, h)

        chunks = _weight_chunks(wt_hbm, wcat_ref, ((wao_hbm, wao_ref), (wmo_hbm, wmo_ref),
                                                   (wout_hbm, wout_ref), (wg_hbm, wg_ref),
                                                   (wu_hbm, wu_ref), (wd_hbm, wd_ref)))
        _cast_weights(chunks, z_scr, dma_sem, while_first_copy=build_bias)

    @pl.when(jnp.logical_and(is_prompt, j == 0))
    def _reset_state():
        kprev_scr[...] = jnp.zeros_like(kprev_scr)
        vprev_scr[...] = jnp.zeros_like(vprev_scr)
        cbd_scr[...] = jnp.zeros_like(cbd_scr)
        st_scr[...] = jnp.zeros_like(st_scr)

    @pl.when(is_prompt)
    def _prompt_step():
        for s in range(N_SUB):
            _in_projection(x_ref.at[s * T:(s + 1) * T, :], gattn_ref, wcat_ref, z_scr.at[s])
        carry = _load_carry(kprev_scr, vprev_scr, cbd_scr, st_scr)
        for s in range(N_SUB):
            carry = _mixers(z_scr.at[s], yatt_scr.at[s], ym_scr.at[s], carry, ghead_ref, bif_ref,
                            bias_scr, sinks_ref, first_of_sequence=(j == 0) if s == 0 else None)
        _store_carry(carry, kprev_scr, vprev_scr, cbd_scr, st_scr)
        for s in range(N_SUB):
            _tail(x_ref.at[s * T:(s + 1) * T, :], z_scr.at[s], yatt_scr.at[s], ym_scr.at[s], a_scr,
                  wao_ref, wmo_ref, wout_ref, wg_ref, wu_ref, wd_ref, gffn_ref, gfin_ref,
                  y_ref.at[s * T:(s + 1) * T, :])

    @pl.when(jnp.logical_not(is_prompt))
    def _sample_step():
        nb = xs_ref.shape[0]
        scal = scals_ref[...]
        yatt_scr[0, 0:nb, :] = yatts_ref[...].astype(BF16)
        for hd in range(M_HEADS):
            inter = scal[:, SC_INTER + hd:SC_INTER + hd + 1]
            w = scal[:, SC_W + hd:SC_W + hd + 1]
            scale = scal[:, SC_SCALE + hd:SC_SCALE + hd + 1]
            hv = (inter * cqs_ref[:, 128 * hd:128 * hd + 128]
                  + w * zs_ref[:, VM + 128 * hd:VM + 128 * hd + 128]) * scale
            hn = hv * lax.rsqrt(jnp.mean(hv * hv, axis=-1, keepdims=True) + EPS)
            hn = hn * ghead_ref[:, 128 * hd:128 * hd + 128]
            om = zs_ref[:, OM + 128 * hd:OM + 128 * hd + 128]
            ym_scr[0, 0:nb, 128 * hd:128 * hd + 128] = (hn * _sigmoid(om)).astype(BF16)
        _tail(xs_ref, zs_ref, yatt_scr.at[0, 0:nb, :], ym_scr.at[0, 0:nb, :], a_scr.at[0:nb, :],
              wao_ref, wmo_ref, wout_ref, wg_ref, wu_ref, wd_ref, gffn_ref, gfin_ref, ys_ref)

    @pl.when(jnp.logical_and(is_prompt, j == nj - 1))
    def _write_state():
        kwin_ref[...] = kprev_scr[...].T
        vwin_ref[...] = vprev_scr[...].T
        stn = st_scr[...]
        cts = [cbd_scr[0].T, cbd_scr[1].T]
        for hd in range(M_HEADS):
            p, hh = hd // 2, hd % 2
            c_ref[hd] = cts[p][64 * hh:64 * hh + 64, 128 * hh:128 * hh + 128]
            n_ref[hd:hd + 1, :] = stn[p:p + 1, 64 * hh:64 * hh + 64]
            m_ref[0:1, hd:hd + 1] = stn[2 + hd:3 + hd, 0:1]


def _in_projection(x_ref, gattn_ref, wcat_ref, z_ref):
    h = _rms(x_ref[...], gattn_ref[...]).astype(BF16)
    for c0, c1 in Z_GROUPS:
        z_ref[:, c0:c1] = _dot(h, wcat_ref[:, c0:c1])


def _weight_chunks(wt_hbm, wcat_ref, plain):
    r = T_BLK
    chunks = [(wt_hbm.at[a:a + r, :], wcat_ref.at[:, a:a + r], True) for a in range(0, W_IF, r)]
    chunks.append((wt_hbm.at[W_IF:W_GA, :], wcat_ref.at[:, IF:IF + LANES], True))
    chunks += [(wt_hbm.at[a:a + r, :], wcat_ref.at[:, GA + a - W_GA:GA + a - W_GA + r], True)
               for a in range(W_GA, W_END, r)]
    for src, dst in plain:
        chunks += [(src.at[a:a + r, :], dst.at[a:a + r, :], False) for a in range(0, src.shape[0], r)]
    return chunks


def _cast_weights(chunks, stage_ref, sem, while_first_copy):
    def copy(i):
        src = chunks[i][0]
        n, c = src.shape
        return pltpu.make_async_copy(src, stage_ref.at[i % 2, 0:n, 0:c], sem.at[i % 2])

    copy(0).start()
    for i, (src, dst, transpose) in enumerate(chunks):
        if i + 1 < len(chunks):
            copy(i + 1).start()
        if i == 0:
            while_first_copy()
        copy(i).wait()
        n, c = src.shape
        val = stage_ref[i % 2, 0:n, 0:c]
        rows = dst.shape[1] if transpose else dst.shape[0]
        if rows != n:
            val = jnp.concatenate([val, jnp.zeros((rows - n, c), F32)], axis=0)
        dst[...] = (val.T if transpose else val).astype(BF16)


def _load_carry(kprev_scr, vprev_scr, cbd_scr, st_scr):
    st = st_scr[...]
    return dict(kp=kprev_scr[...], vp=vprev_scr[...], cbd=[cbd_scr[0], cbd_scr[1]],
                n=[st[0:1, :], st[1:2, :]], m=[st[2 + hd:3 + hd, 0:1] for hd in range(M_HEADS)])


def _store_carry(carry, kprev_scr, vprev_scr, cbd_scr, st_scr):
    kprev_scr[...] = carry["kp"]
    vprev_scr[...] = carry["vp"]
    for p in range(2):
        cbd_scr[p] = carry["cbd"][p]
        st_scr[p:p + 1, :] = carry["n"][p]
    for hd in range(M_HEADS):
        st_scr[2 + hd:3 + hd, :] = jnp.broadcast_to(carry["m"][hd], (1, LANES))


def _tail(x_ref, z_ref, yatt_ref, ym_ref, a_ref, wao_ref, wmo_ref, wout_ref, wg_ref, wu_ref, wd_ref,
          gffn_ref, gfin_ref, y_ref):
    d_ff = wg_ref.shape[1]
    ya = _dot(yatt_ref[...], wao_ref[...])
    ymm = _dot(ym_ref[...], wmo_ref[...])
    mixed = _sigmoid(z_ref[:, GA:GA + 1024]) * ya + _sigmoid(z_ref[:, GM:GM + 1024]) * ymm
    x1 = x_ref[...] + _dot(mixed.astype(BF16), wout_ref[...])
    h2 = _rms(x1, gffn_ref[...]).astype(BF16)
    for c0 in range(0, d_ff, 256):
        g = _dot(h2, wg_ref[:, c0:c0 + 256])
        u = _dot(h2, wu_ref[:, c0:c0 + 256])
        a_ref[:, c0:c0 + 256] = (g * _sigmoid(g) * u).astype(BF16)
    x2 = x1 + _dot(a_ref[...], wd_ref[...])
    y_ref[...] = _rms(x2, gfin_ref[...])


def _mixers(z_ref, yatt_ref, ym_ref, carry, ghead_ref, bif_ref, bias_scr, sinks_ref, first_of_sequence):
    T = T_BLK
    lane = lax.broadcasted_iota(jnp.int32, (T, LANES), 1)
    lane_lo = lane < HEAD_DIM
    row = lax.broadcasted_iota(jnp.int32, (T, LANES), 0)
    qi = jnp.where(row < 128, row, row - 128)
    tri2 = lane <= qi
    if first_of_sequence is not None:
        valid0 = lane <= qi + jnp.where(first_of_sequence, 0, 2 * LANES)
    rowc = lax.broadcasted_iota(jnp.int32, (T, 1), 0)
    k_all = z_ref[:, KA:KA + 128]
    v_all = z_ref[:, VA:VA + 128]
    for sb in range(2):
        r0 = 128 * sb
        if sb == 0:
            kp, vp = carry["kp"], carry["vp"]
        else:
            kp, vp = k_all[0:128], v_all[0:128]
        kcat = jnp.concatenate([kp, k_all[r0:r0 + 128]], axis=0)
        vcat = jnp.concatenate([vp, v_all[r0:r0 + 128]], axis=0)
        kroll = pltpu.roll(kcat, HEAD_DIM, 1)
        vroll = pltpu.roll(vcat, HEAD_DIM, 1)
        zero = jnp.zeros_like(kcat)
        k_mats = [jnp.where(lane_lo, kcat, zero), jnp.where(lane_lo, zero, kroll),
                  jnp.where(lane_lo, kroll, zero), jnp.where(lane_lo, zero, kcat)]
        v_mats = [jnp.where(lane_lo, vcat, zero), jnp.where(lane_lo, zero, vroll),
                  jnp.where(lane_lo, vroll, zero), jnp.where(lane_lo, zero, vcat)]
        q = (z_ref[r0:r0 + 128, QA:QA + 512] * (HEAD_DIM ** -0.5)).astype(BF16)
        lhs_a = jnp.concatenate([q[:, 0:128], q[:, 128:256]], axis=0)
        lhs_b = jnp.concatenate([q[:, 256:384], q[:, 384:512]], axis=0)
        groups = [(lhs_a, 0, 0, 2), (lhs_a, 1, 1, 3), (lhs_b, 2, 4, 6), (lhs_b, 3, 5, 7)]
        pcs = []
        for lhs, mi, ha, hb in groups:
            s = _dot_nt(lhs, k_mats[mi].astype(BF16))
            sf = jnp.where(tri2, s[:, 128:256], s[:, 0:128])
            sf = sf + jnp.concatenate([bias_scr[ha], bias_scr[hb]], axis=0)
            if sb == 0 and first_of_sequence is not None:
                sf = jnp.where(valid0, sf, NEG)
            sink = jnp.where(rowc < 128, sinks_ref[ha], sinks_ref[hb])
            mx = jnp.maximum(jnp.max(sf, axis=-1, keepdims=True), sink)
            p = jnp.exp(sf - mx)
            den = jnp.sum(p, axis=-1, keepdims=True) + jnp.exp(sink - mx)
            pn = p * (1.0 / den)
            zp = jnp.zeros_like(pn)
            pcs.append(jnp.concatenate([jnp.where(tri2, zp, pn), jnp.where(tri2, pn, zp)],
                                       axis=1).astype(BF16))
        o_a = _dot(jnp.concatenate([pcs[0], pcs[1]], axis=1),
                   jnp.concatenate([v_mats[0], v_mats[1]], axis=0).astype(BF16))
        o_b = _dot(jnp.concatenate([pcs[2], pcs[3]], axis=1),
                   jnp.concatenate([v_mats[2], v_mats[3]], axis=0).astype(BF16))
        yatt_ref[r0:r0 + 128, 0:128] = o_a[0:128].astype(BF16)
        yatt_ref[r0:r0 + 128, 128:256] = o_a[128:256].astype(BF16)
        yatt_ref[r0:r0 + 128, 256:384] = o_b[0:128].astype(BF16)
        yatt_ref[r0:r0 + 128, 384:512] = o_b[128:256].astype(BF16)
    new_carry = dict(kp=k_all[128:256], vp=v_all[128:256], cbd=[None, None], n=[None, None],
                     m=[None] * M_HEADS)

    zif = z_ref[:, IF:IF + 128] + bif_ref[...]
    gl = jnp.where(lane < M_HEADS, zif, _log_sigmoid(zif))
    gl_t = gl.T
    tr = lax.broadcasted_iota(jnp.int32, (T, T), 0)
    ts = lax.broadcasted_iota(jnp.int32, (T, T), 1)
    tril = ts <= tr
    triu = tr <= ts
    row2 = lax.broadcasted_iota(jnp.int32, (2 * LANES, LANES), 0)
    lane2 = lax.broadcasted_iota(jnp.int32, (2 * LANES, LANES), 1)
    bd_mask = (row2 < LANES) == (lane2 < HEAD_DIM)
    for p in range(2):
        q_pair = z_ref[:, QM + 128 * p:QM + 128 * p + 128]
        k_pair = z_ref[:, KM + 128 * p:KM + 128 * p + 128] * (HEAD_DIM ** -0.5)
        v_pair = z_ref[:, VM + 256 * p:VM + 256 * p + 256]
        cbd = carry["cbd"][p]
        n_pair = carry["n"][p]
        q_bf = q_pair.astype(BF16)
        qc = _dot_nt(q_bf, cbd.astype(BF16))
        qn_prod = q_pair * n_pair
        ws, decays, m_ends = [], [], []
        for hh in range(2):
            hd = 2 * p + hh
            hmask = lane_lo if hh == 0 else jnp.logical_not(lane_lo)
            ig_c = gl[:, hd:hd + 1]
            lf_c = gl[:, M_HEADS + hd:M_HEADS + hd + 1]
            ig_r = gl_t[hd:hd + 1, :]
            lf_r = gl_t[M_HEADS + hd:M_HEADS + hd + 1, :]
            m_prev = carry["m"][hd]
            b_c = jnp.sum(jnp.where(tril, lf_r, 0.0), axis=1, keepdims=True)
            b_r = jnp.sum(jnp.where(triu, lf_c, 0.0), axis=0, keepdims=True)
            a_r = ig_r - b_r
            cm_c = jnp.max(jnp.where(tril, a_r, NEG), axis=1, keepdims=True)
            mt_c = b_c + jnp.maximum(m_prev, cm_c)
            g_c = b_c - mt_c
            dm = jnp.exp(jnp.where(tril, a_r + g_c, NEG))
            inter_c = jnp.exp(m_prev + g_c)
            k_h = jnp.where(hmask, k_pair, 0.0).astype(BF16)
            w = dm * _dot_nt(q_bf, k_h)
            v_h = v_pair[:, 128 * hh:128 * hh + 128]
            num = inter_c * qc[:, 128 * hh:128 * hh + 128] + _dot(w.astype(BF16), v_h.astype(BF16))
            qn = jnp.sum(jnp.where(hmask, qn_prod, 0.0), axis=1, keepdims=True)
            den = inter_c * qn + jnp.sum(w, axis=1, keepdims=True)
            hv = num / jnp.maximum(jnp.abs(den), jnp.exp(-mt_c))
            hn = hv * lax.rsqrt(jnp.mean(hv * hv, axis=-1, keepdims=True) + EPS)
            hn = hn * ghead_ref[:, 128 * hd:128 * hd + 128]
            om = z_ref[:, OM + 128 * hd:OM + 128 * hd + 128]
            ym_ref[:, 128 * hd:128 * hd + 128] = (hn * _sigmoid(om)).astype(BF16)
            b_end = b_c[T - 1:T, :]
            m_end = mt_c[T - 1:T, :]
            ws.append(jnp.exp((ig_c - b_c) + b_end - m_end))
            decays.append(jnp.exp(m_prev + b_end - m_end))
            m_ends.append(m_end)
        kw = k_pair * jnp.where(lane_lo, ws[0], ws[1])
        upd = _dot(v_pair.T.astype(BF16), kw.astype(BF16))
        dec_rows = jnp.where(row2[:, 0:1] < LANES, decays[0], decays[1])
        new_carry["cbd"][p] = dec_rows * cbd + jnp.where(bd_mask, upd, 0.0)
        dec_lanes = jnp.where(lane_lo[0:1, :], decays[0], decays[1])
        new_carry["n"][p] = dec_lanes * n_pair + jnp.sum(kw, axis=0, keepdims=True)
        for hh in range(2):
            new_carry["m"][2 * p + hh] = m_ends[hh]
    return new_carry


def _resident(shape):
    zeros = (0,) * len(shape)
    return pl.BlockSpec(shape, lambda t: zeros, pipeline_mode=pl.Buffered(1))


def _smem():
    return pl.BlockSpec(memory_space=pltpu.SMEM)


def _layer_call(x, wt, wao, wmo, wout, wg, wu, wd, gattn, gffn, gfin, ghead, bif, fbuck, relb, sinks,
                xs3, zs, yatts, cqs, scals):
    bsz, s_len, d = x.shape
    nb = xs3.shape[0]
    t_step = T_BLK * N_SUB
    assert s_len % t_step == 0 and d == 1024 and nb <= T_BLK
    assert wt.shape == (W_END, d) and N_SUB >= 2
    d_ff = wg.shape[1]
    assert all(w.shape[0] % T_BLK == 0 and w.shape[1] <= N_CAT for w in (wao, wmo, wout, wg, wu, wd))
    hbm = pl.BlockSpec(memory_space=pl.ANY)
    nj = s_len // t_step
    n_prompt = bsz * nj
    out_shapes = (
        jax.ShapeDtypeStruct((bsz, s_len, d), F32),
        jax.ShapeDtypeStruct((bsz, WINDOW, 128), F32),
        jax.ShapeDtypeStruct((bsz, WINDOW, 128), F32),
        jax.ShapeDtypeStruct((bsz, M_HEADS, 64, 128), F32),
        jax.ShapeDtypeStruct((bsz, M_HEADS, 64), F32),
        jax.ShapeDtypeStruct((bsz, 1, M_HEADS), F32),
        jax.ShapeDtypeStruct((nb, 1, d), F32),
    )

    def seq(t):
        return jnp.minimum(t, n_prompt - 1) // nj

    def blk(t):
        return jnp.minimum(t, n_prompt - 1) % nj

    in_specs = [
        pl.BlockSpec((None, t_step, d), lambda t: (seq(t), blk(t), 0)),
        hbm, hbm, hbm, hbm, hbm, hbm, hbm,
        _resident(gattn.shape), _resident(gffn.shape), _resident(gfin.shape), _resident(ghead.shape),
        _resident(bif.shape), _resident(fbuck.shape), _smem(), _smem(),
        pl.BlockSpec((nb, None, d), lambda t: (0, 0, 0), pipeline_mode=pl.Buffered(1)),
        _resident(zs.shape), _resident(yatts.shape), _resident(cqs.shape), _resident(scals.shape),
    ]
    out_specs = (
        pl.BlockSpec((None, t_step, d), lambda t: (seq(t), blk(t), 0)),
        pl.BlockSpec((None, WINDOW, 128), lambda t: (seq(t), 0, 0)),
        pl.BlockSpec((None, WINDOW, 128), lambda t: (seq(t), 0, 0)),
        pl.BlockSpec((None, M_HEADS, 64, 128), lambda t: (seq(t), 0, 0, 0)),
        pl.BlockSpec((None, M_HEADS, 64), lambda t: (seq(t), 0, 0)),
        pl.BlockSpec((None, 1, M_HEADS), lambda t: (seq(t), 0, 0)),
        pl.BlockSpec((nb, None, d), lambda t: (0, 0, 0)),
    )
    scratch = [
        pltpu.VMEM((N_SUB, T_BLK, N_CAT), F32),
        pltpu.VMEM((N_SUB, T_BLK, 512), BF16),
        pltpu.VMEM((N_SUB, T_BLK, 512), BF16),
        pltpu.VMEM((T_BLK, d_ff), BF16),
        pltpu.VMEM((N_Q_HEADS, 128, 128), F32),
        pltpu.VMEM((128, 128), F32),
        pltpu.VMEM((128, 128), F32),
        pltpu.VMEM((2, 256, 128), F32),
        pltpu.VMEM((8, 128), F32),
        pltpu.VMEM((d, N_CAT), BF16),
        pltpu.VMEM(wao.shape, BF16), pltpu.VMEM(wmo.shape, BF16), pltpu.VMEM(wout.shape, BF16),
        pltpu.VMEM(wg.shape, BF16), pltpu.VMEM(wu.shape, BF16), pltpu.VMEM(wd.shape, BF16),
        pltpu.SemaphoreType.DMA((2,)),
    ]
    return pl.pallas_call(
        functools.partial(_layer_kernel, nj),
        grid=(n_prompt + 1,),
        in_specs=in_specs,
        out_specs=out_specs,
        out_shape=out_shapes,
        scratch_shapes=scratch,
        compiler_params=pltpu.CompilerParams(
            dimension_semantics=("arbitrary",),
            vmem_limit_bytes=VMEM_LIMIT_BYTES),
        name="layer",
    )(x, wt, wao, wmo, wout, wg, wu, wd, gattn, gffn, gfin, ghead, bif, fbuck, relb, sinks,
      xs3, zs, yatts, cqs, scals)


W_QA, W_KA, W_VA, W_QM, W_KM, W_VM, W_OM, W_IF, W_GA, W_GM, W_END = (
    0, 512, 640, 768, 1024, 1280, 1792, 2304, 2312, 3336, 4360)
ZT_KA, ZT_VA, ZT_QM, ZT_KM, ZT_IF, ZT_ROWS = 0, 128, 256, 512, 768, 776


def _sample_inproj_kernel(x_ref, g_ref, wt_hbm, nt_ref, mt_ref, bif_ref,
                          z_ref, zt_ref, no_ref, mo_ref, scal_ref,
                          wt_ref, sems):
    _sample_projection(x_ref, g_ref, wt_hbm, wt_ref, sems, z_ref, zt_ref)
    r8c = lax.broadcasted_iota(jnp.int32, (8, 1), 0)
    bias_col = jnp.zeros((8, 1), F32)
    for r in range(8):
        bias_col = jnp.where(r8c == r, bif_ref[r // M_HEADS, r % M_HEADS], bias_col)
    ift = zt_ref[ZT_IF:ZT_ROWS, :] + bias_col
    m0 = mt_ref[...]
    r16 = lax.broadcasted_iota(jnp.int32, (16, LANES), 0)
    table = jnp.zeros((16, LANES), F32)
    for hd in range(M_HEADS):
        ig = ift[hd:hd + 1, :]
        lf = _log_sigmoid(ift[M_HEADS + hd:M_HEADS + hd + 1, :])
        m_prev = m0[hd:hd + 1, :]
        a = ig - lf
        m_t = lf + jnp.maximum(m_prev, a)
        dgate = jnp.exp(a + lf - m_t)
        inter = jnp.exp(m_prev + lf - m_t)
        qt = zt_ref[ZT_QM + 64 * hd:ZT_QM + 64 * hd + 64, :]
        kt = zt_ref[ZT_KM + 64 * hd:ZT_KM + 64 * hd + 64, :] * (HEAD_DIM ** -0.5)
        nt = nt_ref[hd]
        no_ref[hd] = inter * nt + dgate * kt
        mo_ref[hd:hd + 1, :] = m_t
        qk = jnp.sum(qt * kt, axis=0, keepdims=True)
        nq = jnp.sum(nt * qt, axis=0, keepdims=True)
        w = dgate * qk
        den = inter * nq + w
        scale = 1.0 / jnp.maximum(jnp.abs(den), jnp.exp(-m_t))
        for base, val in ((SC_INTER, inter), (SC_W, w), (SC_SCALE, scale), (SC_D, dgate)):
            table = jnp.where(r16 == base + hd, val, table)
    full = jnp.concatenate([table, jnp.zeros((LANES - 16, LANES), F32)], axis=0)
    scal_ref[...] = full.T


WT_CHUNKS = ((W_QA, 768), (768, 1536), (1536, W_IF), (W_IF, W_GA), (W_GA, W_GM), (W_GM, W_END))


def _sample_projection(x_ref, g_ref, wt_hbm, wt_ref, sems, z_ref, zt_ref):
    copies = [pltpu.make_async_copy(wt_hbm.at[a:b, :], wt_ref.at[a:b, :], sems.at[k])
              for k, (a, b) in enumerate(WT_CHUNKS)]
    for c in copies:
        c.start()
    h32 = _rms(x_ref[...], g_ref[...])
    h = h32.astype(BF16)
    ht = h32.T.astype(BF16)
    copies[0].wait()
    z_ref[:, QA:768] = _dot_nt(h, wt_ref[W_QA:768, :].astype(BF16))
    zt_ref[ZT_KA:ZT_QM, :] = _dot(wt_ref[W_KA:W_QM, :].astype(BF16), ht)
    copies[1].wait()
    z_ref[:, 768:1536] = _dot_nt(h, wt_ref[768:1536, :].astype(BF16))
    zt_ref[ZT_QM:ZT_IF, :] = _dot(wt_ref[W_QM:W_VM, :].astype(BF16), ht)
    copies[2].wait()
    z_ref[:, 1536:W_IF] = _dot_nt(h, wt_ref[1536:W_IF, :].astype(BF16))
    copies[3].wait()
    w_if = jnp.concatenate([wt_ref[W_IF:W_GA, :], jnp.zeros((LANES - 8, wt_ref.shape[1]), F32)],
                           axis=0).astype(BF16)
    z_ref[:, IF:IF + LANES] = _dot_nt(h, w_if)
    zt_ref[ZT_IF:ZT_ROWS, :] = _dot(w_if, ht)[0:8, :]
    copies[4].wait()
    z_ref[:, GA:GM] = _dot_nt(h, wt_ref[W_GA:W_GM, :].astype(BF16))
    copies[5].wait()
    z_ref[:, GM:N_CAT] = _dot_nt(h, wt_ref[W_GM:W_END, :].astype(BF16))


def _sample_inproj_call(x3, gattn, wt, nt, mt, bif2):
    n, _, d = x3.shape
    assert n == LANES
    full = lambda s: pl.BlockSpec(tuple(s), lambda i: (0,) * len(s))
    return pl.pallas_call(
        _sample_inproj_kernel,
        grid=(1,),
        in_specs=[pl.BlockSpec((n, None, d), lambda i: (0, 0, 0)),
                  pl.BlockSpec((1, d), lambda i: (0, 0)),
                  pl.BlockSpec(memory_space=pl.ANY),
                  full(nt.shape), full(mt.shape), _smem()],
        out_specs=(full((n, N_CAT)), full((ZT_ROWS, n)), full(nt.shape), full(mt.shape),
                   full((n, LANES))),
        out_shape=(jax.ShapeDtypeStruct((n, N_CAT), F32), jax.ShapeDtypeStruct((ZT_ROWS, n), F32),
                   jax.ShapeDtypeStruct(nt.shape, F32), jax.ShapeDtypeStruct(mt.shape, F32),
                   jax.ShapeDtypeStruct((n, LANES), F32)),
        scratch_shapes=[pltpu.VMEM(wt.shape, F32), pltpu.SemaphoreType.DMA((len(WT_CHUNKS),))],
        compiler_params=pltpu.CompilerParams(dimension_semantics=("arbitrary",),
                                             vmem_limit_bytes=VMEM_LIMIT_BYTES),
        name="sample_inproj",
    )(x3, gattn, wt, nt, mt, bif2)


SK2_BB = 16
SC_INTER, SC_W, SC_SCALE, SC_D, SC_COLS = 0, 4, 8, 12, 16
HEAD_ROW_ORDER = (0, 2, 4, 6, 1, 3, 5, 7)


def _sample_mixer_kernel(z_ref, zt_ref, kt_ref, vt_ref, ct_ref, sbuck_ref,
                         scal_ref, relb_ref, sinks_ref,
                         yatt_ref, cq_ref, ko_ref, vo_ref, co_ref,
                         sbias_scr):
    i = pl.program_id(0)
    r8 = lax.broadcasted_iota(jnp.int32, (8, LANES), 0)
    l8 = lax.broadcasted_iota(jnp.int32, (8, LANES), 1)
    r8c = lax.broadcasted_iota(jnp.int32, (8, 1), 0)

    @pl.when(i == 0)
    def _prologue():
        sb = sbuck_ref[...]
        acc = jnp.zeros((8, LANES), F32)
        for rrow, hd in enumerate(HEAD_ROW_ORDER):
            acc = jnp.where(r8 == rrow, _bias_from_buckets(sb, relb_ref, hd), acc)
        sbias_scr[...] = acc

    sink = jnp.zeros((8, 1), F32)
    for rrow, hd in enumerate(HEAD_ROW_ORDER):
        sink = jnp.where(r8c == rrow, sinks_ref[hd], sink)
    sbias = sbias_scr[...]
    lane_w = lax.broadcasted_iota(jnp.int32, (WINDOW, LANES), 1)
    last_lane = lane_w == WINDOW - 1
    r256 = lax.broadcasted_iota(jnp.int32, (8, 2 * LANES), 0)
    l256 = lax.broadcasted_iota(jnp.int32, (8, 2 * LANES), 1)
    own64 = (l256 // HEAD_DIM) == r256
    l512r = lax.broadcasted_iota(jnp.int32, (1, 4 * LANES), 1) // LANES
    lo1 = lax.broadcasted_iota(jnp.int32, (1, LANES), 1) < HEAD_DIM
    row_b = lax.broadcasted_iota(jnp.int32, (LANES, 2 * LANES), 0)
    ktm = [(zt_ref[ZT_KM + 128 * pr:ZT_KM + 128 * pr + 128, :] * (HEAD_DIM ** -0.5)).astype(BF16)
           for pr in range(2)]

    nrow = 8 * SK2_BB
    lst = lax.broadcasted_iota(jnp.int32, (nrow, LANES), 1)
    z1 = jnp.zeros((1, LANES), F32)
    qm_l, so_l, kn_l, vn_l = [], [], [], []
    for bi in range(SK2_BB):
        qp = [z_ref[bi:bi + 1, QA + 128 * p:QA + 128 * p + 128] * (HEAD_DIM ** -0.5) for p in range(4)]
        qpr = [pltpu.roll(x, HEAD_DIM, 1) for x in qp]
        rows = [jnp.where(lo1, qp[0], z1), jnp.where(lo1, qp[1], z1),
                jnp.where(lo1, z1, qpr[2]), jnp.where(lo1, z1, qpr[3]),
                jnp.where(lo1, qpr[0], z1), jnp.where(lo1, qpr[1], z1),
                jnp.where(lo1, z1, qp[2]), jnp.where(lo1, z1, qp[3])]
        qm = jnp.zeros((8, LANES), F32)
        for r in range(8):
            qm = jnp.where(r8 == r, rows[r], qm)
        qm_l.append(qm)
        so_l.append(_dot(qm.astype(BF16), kt_ref[bi].astype(BF16)))
        kn_l.append(jnp.broadcast_to(z_ref[bi:bi + 1, KA:KA + 128], (8, LANES)))
        vn_l.append(jnp.broadcast_to(z_ref[bi:bi + 1, VA:VA + 128], (8, LANES)))
    qm_all = jnp.concatenate(qm_l, axis=0)
    s_old = jnp.concatenate(so_l, axis=0)
    vn_all = jnp.concatenate(vn_l, axis=0)
    s_new = jnp.sum(qm_all * jnp.concatenate(kn_l, axis=0), axis=1, keepdims=True)
    sbias_all = jnp.concatenate([sbias] * SK2_BB, axis=0)
    sink_all = jnp.concatenate([sink] * SK2_BB, axis=0)
    s = jnp.where(lst == WINDOW - 1, s_new, pltpu.roll(s_old, WINDOW - 1, 1)) + sbias_all
    mx = jnp.maximum(jnp.max(s, axis=-1, keepdims=True), sink_all)
    pe = jnp.exp(s - mx)
    den = jnp.sum(pe, axis=-1, keepdims=True) + jnp.exp(sink_all - mx)
    pn = pe * (1.0 / den)
    p_old = jnp.where(lst == 0, 0.0, pltpu.roll(pn, 1, 1))
    oa_l = [_dot_nt(p_old[8 * bi:8 * bi + 8].astype(BF16), vt_ref[bi].astype(BF16))
            for bi in range(SK2_BB)]
    oa = jnp.concatenate(oa_l, axis=0) + pn[:, WINDOW - 1:WINDOW] * vn_all
    oar = pltpu.roll(oa, HEAD_DIM, 1)
    for bi in range(SK2_BB):
        r0 = 8 * bi
        pairs = [jnp.where(lo1, oa[r0:r0 + 1], oar[r0 + 4:r0 + 5]),
                 jnp.where(lo1, oa[r0 + 1:r0 + 2], oar[r0 + 5:r0 + 6]),
                 jnp.where(lo1, oar[r0 + 2:r0 + 3], oa[r0 + 6:r0 + 7]),
                 jnp.where(lo1, oar[r0 + 3:r0 + 4], oa[r0 + 7:r0 + 8])]
        for p in range(4):
            yatt_ref[bi:bi + 1, 128 * p:128 * p + 128] = pairs[p]

    for bi in range(SK2_BB):
        b = i * SK2_BB + bi
        shift = WINDOW - 1 - b
        kcol = pltpu.roll(zt_ref[ZT_KA:ZT_KA + 128, :], shift, 1)
        vcol = pltpu.roll(zt_ref[ZT_VA:ZT_VA + 128, :], shift, 1)
        ko_ref[bi] = jnp.where(last_lane, kcol, pltpu.roll(kt_ref[bi], WINDOW - 1, 1))
        vo_ref[bi] = jnp.where(last_lane, vcol, pltpu.roll(vt_ref[bi], WINDOW - 1, 1))

        qm_row = z_ref[bi:bi + 1, QM:QM + 256]
        vm_row = z_ref[bi:bi + 1, VM:VM + 512]
        qbd = jnp.where(own64, qm_row, 0.0)
        ct_all = jnp.concatenate([ct_ref[bi, hd] for hd in range(M_HEADS)], axis=0)
        cq = _dot(qbd.astype(BF16), ct_all.astype(BF16))
        for hd in range(M_HEADS):
            cq_ref[bi:bi + 1, 128 * hd:128 * hd + 128] = cq[hd:hd + 1, :]
        dsel = jnp.zeros((1, 4 * LANES), F32)
        for hd in range(M_HEADS):
            dsel = jnp.where(l512r == hd, scal_ref[b, SC_D + hd], dsel)
        vs = vm_row * dsel
        for pr in range(2):
            vsel = jnp.where(row_b == b, vs[:, 256 * pr:256 * pr + 256], 0.0).astype(BF16)
            upd = _dot(ktm[pr], vsel)
            for hh in range(2):
                hd = 2 * pr + hh
                co_ref[bi, hd] = (scal_ref[b, SC_INTER + hd] * ct_ref[bi, hd]
                                  + upd[64 * hh:64 * hh + 64, 128 * hh:128 * hh + 128])


def _sample_mixer_call(z, zt, kt, vt, ct, sbuck, scal_small, relb_t, sinks):
    nb = z.shape[0]
    assert nb % SK2_BB == 0 and nb == LANES
    bb = SK2_BB
    out_shapes = (
        jax.ShapeDtypeStruct((nb, 512), F32),
        jax.ShapeDtypeStruct((nb, 512), F32),
        jax.ShapeDtypeStruct(kt.shape, F32),
        jax.ShapeDtypeStruct(vt.shape, F32),
        jax.ShapeDtypeStruct(ct.shape, F32),
    )
    blk = lambda s: pl.BlockSpec((bb,) + tuple(s[1:]), lambda i: (i,) + (0,) * (len(s) - 1))
    full = lambda s: pl.BlockSpec(tuple(s), lambda i: (0,) * len(s))
    in_specs = [blk(z.shape), full(zt.shape), blk(kt.shape), blk(vt.shape), blk(ct.shape),
                full(sbuck.shape), _smem(), _smem(), _smem()]
    out_specs = (blk((nb, 512)), blk((nb, 512)), blk(kt.shape), blk(vt.shape), blk(ct.shape))
    return pl.pallas_call(
        _sample_mixer_kernel,
        grid=(nb // bb,),
        in_specs=in_specs,
        out_specs=out_specs,
        out_shape=out_shapes,
        scratch_shapes=[pltpu.VMEM((8, LANES), F32)],
        compiler_params=pltpu.CompilerParams(dimension_semantics=("arbitrary",)),
        name="sample_mixers",
    )(z, zt, kt, vt, ct, sbuck, scal_small, relb_t, sinks)


def kernel(x_prompt, x_sample, cache_k_win, cache_v_win, state_mlstm_C, state_mlstm_n, state_mlstm_m,
           rel_bias, w_in, b_if, sinks, g_attn_norm, g_head, w_att_out, w_mlstm_out, w_out,
           g_ffn_norm, w_gate, w_up, w_down, g_final):
    depth = w_in.shape[0]
    assert depth == 1
    bsz, s_len, d = x_prompt.shape
    nb = x_sample.shape[0]
    assert x_sample.shape[1] == 1 and cache_k_win.shape[2] == WINDOW

    wt = w_in[0].T
    gattn = g_attn_norm[0].reshape(1, d)
    gffn = g_ffn_norm[0].reshape(1, d)
    gfin = g_final.reshape(1, d)
    ghead = g_head[0].reshape(1, 512)
    bif = jnp.concatenate([b_if[0].reshape(1, 2 * M_HEADS), jnp.zeros((1, LANES - 2 * M_HEADS), F32)], axis=1)
    sink_v = sinks[0]
    relb_t = rel_bias.T

    qi = jnp.arange(WINDOW)[:, None]
    kj = jnp.arange(WINDOW)[None, :]
    fbuck = _t5_bucket(jnp.where(kj <= qi, qi - kj, WINDOW + qi - kj)).astype(jnp.int32)
    sbuck = jnp.broadcast_to(_t5_bucket(WINDOW - 1 - kj), (8, WINDOW)).astype(jnp.int32)

    feat = N_KV_HEADS * HEAD_DIM
    kt_in = jnp.transpose(cache_k_win[0], (0, 2, 3, 1)).reshape(nb, feat, WINDOW)
    vt_in = jnp.transpose(cache_v_win[0], (0, 2, 3, 1)).reshape(nb, feat, WINDOW)
    ct_in = jnp.transpose(state_mlstm_C[0], (0, 1, 3, 2))
    nt_in = jnp.transpose(state_mlstm_n[0], (1, 2, 0))
    mt_in = state_mlstm_m[0].T
    z_s, zt_s, nt_s, mt_s, scal_s = _sample_inproj_call(x_sample, gattn, wt, nt_in, mt_in, b_if[0])
    yatt_s, cq_s, kt_s, vt_s, ct_s = _sample_mixer_call(
        z_s, zt_s, kt_in, vt_in, ct_in, sbuck, scal_s[:, :SC_COLS], relb_t, sink_v)
    y_p, kt_p, vt_p, ct_p, n_p, m_p, y_s = _layer_call(
        x_prompt, wt, w_att_out[0], w_mlstm_out[0], w_out[0], w_gate[0], w_up[0], w_down[0],
        gattn, gffn, gfin, ghead, bif, fbuck, relb_t, sink_v,
        x_sample, z_s, yatt_s, cq_s, scal_s)

    def window_out(t, n):
        return jnp.transpose(t.reshape(n, N_KV_HEADS, HEAD_DIM, WINDOW), (0, 3, 1, 2))[None]

    return (y_p, y_s,
            window_out(kt_p, bsz), window_out(vt_p, bsz),
            jnp.transpose(ct_p, (0, 1, 3, 2))[None], n_p[None], m_p.reshape(1, bsz, M_HEADS),
            window_out(kt_s, nb), window_out(vt_s, nb),
            jnp.transpose(ct_s, (0, 1, 3, 2))[None], jnp.transpose(nt_s, (2, 0, 1))[None], mt_s.T[None])
```

```python
import functools
import math

import jax
import jax.numpy as jnp
from jax import lax
from jax.experimental import pallas as pl
from jax.experimental.pallas import tpu as pltpu

F32 = jnp.float32
BF16 = jnp.bfloat16

HEAD_DIM = 64
N_Q_HEADS = 8
N_KV_HEADS = 2
WINDOW = 128
N_BUCKETS = 32
MAX_DISTANCE = WINDOW
M_HEADS = 4
EPS = 1e-6
NEG = -1e30

LANES = 128
VMEM_LIMIT_BYTES = 61 * 1024 * 1024

QA, KA, VA, QM, KM, VM, OM, IF, GA, GM = 0, 512, 640, 768, 1024, 1280, 1792, 2304, 2432, 3456
N_CAT = 4480
Z_GROUPS = ((768, 2432), (0, 768), (2432, 4480))
T_BLK = 256
N_SUB = 2


def _dot(a, b):
    return jnp.dot(a, b, preferred_element_type=F32)


def _dot_nt(a, b):
    return lax.dot_general(a, b, (((1,), (1,)), ((), ())), preferred_element_type=F32)


def _sigmoid(x):
    return 1.0 / (1.0 + jnp.exp(-x))


def _log_sigmoid(x):
    return jnp.minimum(x, 0.0) - jnp.log1p(jnp.exp(-jnp.abs(x)))


def _rms(x, g):
    ms = jnp.mean(x * x, axis=-1, keepdims=True)
    return x * lax.rsqrt(ms + EPS) * g


def _t5_bucket(dist):
    n = jnp.maximum(dist, 0)
    max_exact = N_BUCKETS // 2
    nf = jnp.maximum(n, 1).astype(F32)
    large = max_exact + jnp.floor(jnp.log(nf / max_exact) / math.log(MAX_DISTANCE / max_exact)
                                  * (N_BUCKETS - max_exact)).astype(jnp.int32)
    large = jnp.minimum(large, N_BUCKETS - 1)
    return jnp.where(n < max_exact, n, large)


def _bias_from_buckets(buckets, relb_ref, head):
    acc = jnp.zeros(buckets.shape, F32)
    for k in range(N_BUCKETS):
        acc = jnp.where(buckets == k, relb_ref[head, k], acc)
    return acc


def _layer_kernel(nj, x_ref, wt_hbm, wao_hbm, wmo_hbm, wout_hbm, wg_hbm, wu_hbm, wd_hbm,
                  gattn_ref, gffn_ref, gfin_ref, ghead_ref, bif_ref, fbuck_ref, relb_ref, sinks_ref,
                  xs_ref, zs_ref, yatts_ref, cqs_ref, scals_ref,
                  y_ref, kwin_ref, vwin_ref, c_ref, n_ref, m_ref, ys_ref,
                  z_scr, yatt_scr, ym_scr, a_scr, bias_scr, kprev_scr, vprev_scr, cbd_scr, st_scr,
                  wcat_ref, wao_ref, wmo_ref, wout_ref, wg_ref, wu_ref, wd_ref, dma_sem):
    T = T_BLK
    t = pl.program_id(0)
    n_prompt = pl.num_programs(0) - 1
    is_prompt = t < n_prompt
    j = lax.rem(t, nj)

    @pl.when(t == 0)
    def _first_step():
        def build_bias():
            fb = fbuck_ref[...]
            for h in range(N_Q_HEADS):
                bias_scr[h] = _bias_from_buckets(fb, relb_ref, h)

        chunks = _weight_chunks(wt_hbm, wcat_ref, ((wao_hbm, wao_ref), (wmo_hbm, wmo_ref),
                                                   (wout_hbm, wout_ref), (wg_hbm, wg_ref),
                                                   (wu_hbm, wu_ref), (wd_hbm, wd_ref)))
        _cast_weights(chunks, z_scr, dma_sem, while_first_copy=build_bias)

    @pl.when(jnp.logical_and(is_prompt, j == 0))
    def _reset_state():
        kprev_scr[...] = jnp.zeros_like(kprev_scr)
        vprev_scr[...] = jnp.zeros_like(vprev_scr)
        cbd_scr[...] = jnp.zeros_like(cbd_scr)
        st_scr[...] = jnp.zeros_like(st_scr)

    @pl.when(is_prompt)
    def _prompt_step():
        for s in range(N_SUB):
            _in_projection(x_ref.at[s * T:(s + 1) * T, :], gattn_ref, wcat_ref, z_scr.at[s])
        carry = _load_carry(kprev_scr, vprev_scr, cbd_scr, st_scr)
        for s in range(N_SUB):
            carry = _mixers(z_scr.at[s], yatt_scr.at[s], ym_scr.at[s], carry, ghead_ref, bif_ref,
                            bias_scr, sinks_ref, first_of_sequence=(j == 0) if s == 0 else None)
        _store_carry(carry, kprev_scr, vprev_scr, cbd_scr, st_scr)
        for s in range(N_SUB):
            _tail(x_ref.at[s * T:(s + 1) * T, :], z_scr.at[s], yatt_scr.at[s], ym_scr.at[s], a_scr,
                  wao_ref, wmo_ref, wout_ref, wg_ref, wu_ref, wd_ref, gffn_ref, gfin_ref,
                  y_ref.at[s * T:(s + 1) * T, :])

    @pl.when(jnp.logical_not(is_prompt))
    def _sample_step():
        nb = xs_ref.shape[0]
        scal = scals_ref[...]
        yatt_scr[0, 0:nb, :] = yatts_ref[...].astype(BF16)
        for hd in range(M_HEADS):
            inter = scal[:, SC_INTER + hd:SC_INTER + hd + 1]
            w = scal[:, SC_W + hd:SC_W + hd + 1]
            scale = scal[:, SC_SCALE + hd:SC_SCALE + hd + 1]
            hv = (inter * cqs_ref[:, 128 * hd:128 * hd + 128]
                  + w * zs_ref[:, VM + 128 * hd:VM + 128 * hd + 128]) * scale
            hn = hv * lax.rsqrt(jnp.mean(hv * hv, axis=-1, keepdims=True) + EPS)
            hn = hn * ghead_ref[:, 128 * hd:128 * hd + 128]
            om = zs_ref[:, OM + 128 * hd:OM + 128 * hd + 128]
            ym_scr[0, 0:nb, 128 * hd:128 * hd + 128] = (hn * _sigmoid(om)).astype(BF16)
        _tail(xs_ref, zs_ref, yatt_scr.at[0, 0:nb, :], ym_scr.at[0, 0:nb, :], a_scr.at[0:nb, :],
              wao_ref, wmo_ref, wout_ref, wg_ref, wu_ref, wd_ref, gffn_ref, gfin_ref, ys_ref)

    @pl.when(jnp.logical_and(is_prompt, j == nj - 1))
    def _write_state():
        kwin_ref[...] = kprev_scr[...].T
        vwin_ref[...] = vprev_scr[...].T
        stn = st_scr[...]
        cts = [cbd_scr[0].T, cbd_scr[1].T]
        for hd in range(M_HEADS):
            p, hh = hd // 2, hd % 2
            c_ref[hd] = cts[p][64 * hh:64 * hh + 64, 128 * hh:128 * hh + 128]
            n_ref[hd:hd + 1, :] = stn[p:p + 1, 64 * hh:64 * hh + 64]
            m_ref[0:1, hd:hd + 1] = stn[2 + hd:3 + hd, 0:1]


def _in_projection(x_ref, gattn_ref, wcat_ref, z_ref):
    h = _rms(x_ref[...], gattn_ref[...]).astype(BF16)
    for c0, c1 in Z_GROUPS:
        z_ref[:, c0:c1] = _dot(h, wcat_ref[:, c0:c1])


def _weight_chunks(wt_hbm, wcat_ref, plain):
    r = T_BLK
    chunks = [(wt_hbm.at[a:a + r, :], wcat_ref.at[:, a:a + r], True) for a in range(0, W_IF, r)]
    chunks.append((wt_hbm.at[W_IF:W_GA, :], wcat_ref.at[:, IF:IF + LANES], True))
    chunks += [(wt_hbm.at[a:a + r, :], wcat_ref.at[:, GA + a - W_GA:GA + a - W_GA + r], True)
               for a in range(W_GA, W_END, r)]
    for src, dst in plain:
        chunks += [(src.at[a:a + r, :], dst.at[a:a + r, :], False) for a in range(0, src.shape[0], r)]
    return chunks


def _cast_weights(chunks, stage_ref, sem, while_first_copy):
    def copy(i):
        src = chunks[i][0]
        n, c = src.shape
        return pltpu.make_async_copy(src, stage_ref.at[i % 2, 0:n, 0:c], sem.at[i % 2])

    copy(0).start()
    for i, (src, dst, transpose) in enumerate(chunks):
        if i + 1 < len(chunks):
            copy(i + 1).start()
        if i == 0:
            while_first_copy()
        copy(i).wait()
        n, c = src.shape
        val = stage_ref[i % 2, 0:n, 0:c]
        rows = dst.shape[1] if transpose else dst.shape[0]
        if rows != n:
            val = jnp.concatenate([val, jnp.zeros((rows - n, c), F32)], axis=0)
        dst[...] = (val.T if transpose else val).astype(BF16)


def _load_carry(kprev_scr, vprev_scr, cbd_scr, st_scr):
    st = st_scr[...]
    return dict(kp=kprev_scr[...], vp=vprev_scr[...], cbd=[cbd_scr[0], cbd_scr[1]],
                n=[st[0:1, :], st[1:2, :]], m=[st[2 + hd:3 + hd, 0:1] for hd in range(M_HEADS)])


def _store_carry(carry, kprev_scr, vprev_scr, cbd_scr, st_scr):
    kprev_scr[...] = carry["kp"]
    vprev_scr[...] = carry["vp"]
    for p in range(2):
        cbd_scr[p] = carry["cbd"][p]
        st_scr[p:p + 1, :] = carry["n"][p]
    for hd in range(M_HEADS):
        st_scr[2 + hd:3 + hd, :] = jnp.broadcast_to(carry["m"][hd], (1, LANES))


def _tail(x_ref, z_ref, yatt_ref, ym_ref, a_ref, wao_ref, wmo_ref, wout_ref, wg_ref, wu_ref, wd_ref,
          gffn_ref, gfin_ref, y_ref):
    d_ff = wg_ref.shape[1]
    ya = _dot(yatt_ref[...], wao_ref[...])
    ymm = _dot(ym_ref[...], wmo_ref[...])
    mixed = _sigmoid(z_ref[:, GA:GA + 1024]) * ya + _sigmoid(z_ref[:, GM:GM + 1024]) * ymm
    x1 = x_ref[...] + _dot(mixed.astype(BF16), wout_ref[...])
    h2 = _rms(x1, gffn_ref[...]).astype(BF16)
    for c0 in range(0, d_ff, 256):
        g = _dot(h2, wg_ref[:, c0:c0 + 256])
        u = _dot(h2, wu_ref[:, c0:c0 + 256])
        a_ref[:, c0:c0 + 256] = (g * _sigmoid(g) * u).astype(BF16)
    x2 = x1 + _dot(a_ref[...], wd_ref[...])
    y_ref[...] = _rms(x2, gfin_ref[...])


def _mixers(z_ref, yatt_ref, ym_ref, carry, ghead_ref, bif_ref, bias_scr, sinks_ref, first_of_sequence):
    T = T_BLK
    lane = lax.broadcasted_iota(jnp.int32, (T, LANES), 1)
    lane_lo = lane < HEAD_DIM
    k_all = z_ref[:, KA:KA + 128]
    v_all = z_ref[:, VA:VA + 128]
    jk = lax.broadcasted_iota(jnp.int32, (128, T), 0)
    iq = lax.broadcasted_iota(jnp.int32, (128, T), 1)
    iq = jnp.where(iq < 128, iq, iq - 128)
    tri_t = jk <= iq
    if first_of_sequence is not None:
        valid_t = jk <= iq + jnp.where(first_of_sequence, 0, 2 * LANES)
    lane_q = lax.broadcasted_iota(jnp.int32, (1, T), 1)
    feat_lo = lax.broadcasted_iota(jnp.int32, (128, T), 0) < HEAD_DIM
    vt_all = v_all.T
    for sb in range(2):
        r0 = 128 * sb
        if sb == 0:
            kp, vpt = carry["kp"], carry["vp"].T
        else:
            kp, vpt = k_all[0:128], vt_all[:, 0:128]
        kcat = jnp.concatenate([kp, k_all[r0:r0 + 128]], axis=0)
        kroll = pltpu.roll(kcat, HEAD_DIM, 1)
        zero = jnp.zeros_like(kcat)
        k_mats = [jnp.where(lane_lo, kcat, zero), jnp.where(lane_lo, zero, kroll),
                  jnp.where(lane_lo, kroll, zero), jnp.where(lane_lo, zero, kcat)]
        vcat_t = jnp.concatenate([vpt, vt_all[:, r0:r0 + 128]], axis=1)
        vroll_t = pltpu.roll(vcat_t, HEAD_DIM, 0)
        zero_t = jnp.zeros_like(vcat_t)
        v_mats_t = [jnp.where(feat_lo, vcat_t, zero_t), jnp.where(feat_lo, zero_t, vroll_t),
                    jnp.where(feat_lo, vroll_t, zero_t), jnp.where(feat_lo, zero_t, vcat_t)]
        q = (z_ref[r0:r0 + 128, QA:QA + 512] * (HEAD_DIM ** -0.5)).astype(BF16)
        lhs_a = jnp.concatenate([q[:, 0:128], q[:, 128:256]], axis=0)
        lhs_b = jnp.concatenate([q[:, 256:384], q[:, 384:512]], axis=0)
        groups = [(lhs_a, 0, 0, 2), (lhs_a, 1, 1, 3), (lhs_b, 2, 4, 6), (lhs_b, 3, 5, 7)]
        pts = []
        for lhs, mi, ha, hb in groups:
            st = _dot_nt(k_mats[mi].astype(BF16), lhs)
            sf = jnp.where(tri_t, st[128:256, :], st[0:128, :])
            sf = sf + jnp.concatenate([bias_scr[ha], bias_scr[hb]], axis=1)
            if sb == 0 and first_of_sequence is not None:
                sf = jnp.where(valid_t, sf, NEG)
            sink = jnp.where(lane_q < 128, sinks_ref[ha], sinks_ref[hb])
            mx = jnp.maximum(jnp.max(sf, axis=0, keepdims=True), sink)
            p = jnp.exp(sf - mx)
            den = jnp.sum(p, axis=0, keepdims=True) + jnp.exp(sink - mx)
            pn = p * (1.0 / den)
            zp = jnp.zeros_like(pn)
            pts.append(jnp.concatenate([jnp.where(tri_t, zp, pn), jnp.where(tri_t, pn, zp)],
                                       axis=0).astype(BF16))
        ot_a = _dot(jnp.concatenate([v_mats_t[0], v_mats_t[1]], axis=1).astype(BF16),
                    jnp.concatenate([pts[0], pts[1]], axis=0))
        ot_b = _dot(jnp.concatenate([v_mats_t[2], v_mats_t[3]], axis=1).astype(BF16),
                    jnp.concatenate([pts[2], pts[3]], axis=0))
        yatt_ref[r0:r0 + 128, 0:128] = ot_a[:, 0:128].T.astype(BF16)
        yatt_ref[r0:r0 + 128, 128:256] = ot_a[:, 128:256].T.astype(BF16)
        yatt_ref[r0:r0 + 128, 256:384] = ot_b[:, 0:128].T.astype(BF16)
        yatt_ref[r0:r0 + 128, 384:512] = ot_b[:, 128:256].T.astype(BF16)
    new_carry = dict(kp=k_all[128:256], vp=v_all[128:256], cbd=[None, None], n=[None, None],
                     m=[None] * M_HEADS)

    zif = z_ref[:, IF:IF + 128] + bif_ref[...]
    gl = jnp.where(lane < M_HEADS, zif, _log_sigmoid(zif))
    gl_t = gl.T
    tr = lax.broadcasted_iota(jnp.int32, (T, T), 0)
    ts = lax.broadcasted_iota(jnp.int32, (T, T), 1)
    tril = ts <= tr
    triu = tr <= ts
    row2 = lax.broadcasted_iota(jnp.int32, (2 * LANES, LANES), 0)
    lane2 = lax.broadcasted_iota(jnp.int32, (2 * LANES, LANES), 1)
    bd_mask = (row2 < LANES) == (lane2 < HEAD_DIM)
    for p in range(2):
        q_pair = z_ref[:, QM + 128 * p:QM + 128 * p + 128]
        k_pair = z_ref[:, KM + 128 * p:KM + 128 * p + 128] * (HEAD_DIM ** -0.5)
        v_pair = z_ref[:, VM + 256 * p:VM + 256 * p + 256]
        cbd = carry["cbd"][p]
        n_pair = carry["n"][p]
        q_bf = q_pair.astype(BF16)
        qc = _dot_nt(q_bf, cbd.astype(BF16))
        qn_prod = q_pair * n_pair
        ws, decays, m_ends = [], [], []
        for hh in range(2):
            hd = 2 * p + hh
            hmask = lane_lo if hh == 0 else jnp.logical_not(lane_lo)
            ig_c = gl[:, hd:hd + 1]
            lf_c = gl[:, M_HEADS + hd:M_HEADS + hd + 1]
            ig_r = gl_t[hd:hd + 1, :]
            lf_r = gl_t[M_HEADS + hd:M_HEADS + hd + 1, :]
            m_prev = carry["m"][hd]
            b_c = jnp.sum(jnp.where(tril, lf_r, 0.0), axis=1, keepdims=True)
            b_r = jnp.sum(jnp.where(triu, lf_c, 0.0), axis=0, keepdims=True)
            a_r = ig_r - b_r
            cm_c = jnp.max(jnp.where(tril, a_r, NEG), axis=1, keepdims=True)
            mt_c = b_c + jnp.maximum(m_prev, cm_c)
            g_c = b_c - mt_c
            dm = jnp.exp(jnp.where(tril, a_r + g_c, NEG))
            inter_c = jnp.exp(m_prev + g_c)
            k_h = jnp.where(hmask, k_pair, 0.0).astype(BF16)
            w = dm * _dot_nt(q_bf, k_h)
            v_h = v_pair[:, 128 * hh:128 * hh + 128]
            num = inter_c * qc[:, 128 * hh:128 * hh + 128] + _dot(w.astype(BF16), v_h.astype(BF16))
            qn = jnp.sum(jnp.where(hmask, qn_prod, 0.0), axis=1, keepdims=True)
            den = inter_c * qn + jnp.sum(w, axis=1, keepdims=True)
            hv = num / jnp.maximum(jnp.abs(den), jnp.exp(-mt_c))
            hn = hv * lax.rsqrt(jnp.mean(hv * hv, axis=-1, keepdims=True) + EPS)
            hn = hn * ghead_ref[:, 128 * hd:128 * hd + 128]
            om = z_ref[:, OM + 128 * hd:OM + 128 * hd + 128]
            ym_ref[:, 128 * hd:128 * hd + 128] = (hn * _sigmoid(om)).astype(BF16)
            b_end = b_c[T - 1:T, :]
            m_end = mt_c[T - 1:T, :]
            ws.append(jnp.exp((ig_c - b_c) + b_end - m_end))
            decays.append(jnp.exp(m_prev + b_end - m_end))
            m_ends.append(m_end)
        kw = k_pair * jnp.where(lane_lo, ws[0], ws[1])
        upd = _dot(v_pair.T.astype(BF16), kw.astype(BF16))
        dec_rows = jnp.where(row2[:, 0:1] < LANES, decays[0], decays[1])
        new_carry["cbd"][p] = dec_rows * cbd + jnp.where(bd_mask, upd, 0.0)
        dec_lanes = jnp.where(lane_lo[0:1, :], decays[0], decays[1])
        new_carry["n"][p] = dec_lanes * n_pair + jnp.sum(kw, axis=0, keepdims=True)
        for hh in range(2):
            new_carry["m"][2 * p + hh] = m_ends[hh]
    return new_carry


def _resident(shape):
    zeros = (0,) * len(shape)
    return pl.BlockSpec(shape, lambda t: zeros, pipeline_mode=pl.Buffered(1))


def _smem():
    return pl.BlockSpec(memory_space=pltpu.SMEM)


def _layer_call(x, wt, wao, wmo, wout, wg, wu, wd, gattn, gffn, gfin, ghead, bif, fbuck, relb, sinks,
                xs3, zs, yatts, cqs, scals):
    bsz, s_len, d = x.shape
    nb = xs3.shape[0]
    t_step = T_BLK * N_SUB
    assert s_len % t_step == 0 and d == 1024 and nb <= T_BLK
    assert wt.shape == (W_END, d) and N_SUB >= 2
    d_ff = wg.shape[1]
    assert all(w.shape[0] % T_BLK == 0 and w.shape[1] <= N_CAT for w in (wao, wmo, wout, wg, wu, wd))
    hbm = pl.BlockSpec(memory_space=pl.ANY)
    nj = s_len // t_step
    n_prompt = bsz * nj
    out_shapes = (
        jax.ShapeDtypeStruct((bsz, s_len, d), F32),
        jax.ShapeDtypeStruct((bsz, WINDOW, 128), F32),
        jax.ShapeDtypeStruct((bsz, WINDOW, 128), F32),
        jax.ShapeDtypeStruct((bsz, M_HEADS, 64, 128), F32),
        jax.ShapeDtypeStruct((bsz, M_HEADS, 64), F32),
        jax.ShapeDtypeStruct((bsz, 1, M_HEADS), F32),
        jax.ShapeDtypeStruct((nb, 1, d), F32),
    )

    def seq(t):
        return jnp.minimum(t, n_prompt - 1) // nj

    def blk(t):
        return jnp.minimum(t, n_prompt - 1) % nj

    in_specs = [
        pl.BlockSpec((None, t_step, d), lambda t: (seq(t), blk(t), 0)),
        hbm, hbm, hbm, hbm, hbm, hbm, hbm,
        _resident(gattn.shape), _resident(gffn.shape), _resident(gfin.shape), _resident(ghead.shape),
        _resident(bif.shape), _resident(fbuck.shape), _smem(), _smem(),
        pl.BlockSpec((nb, None, d), lambda t: (0, 0, 0), pipeline_mode=pl.Buffered(1)),
        _resident(zs.shape), _resident(yatts.shape), _resident(cqs.shape), _resident(scals.shape),
    ]
    out_specs = (
        pl.BlockSpec((None, t_step, d), lambda t: (seq(t), blk(t), 0)),
        pl.BlockSpec((None, WINDOW, 128), lambda t: (seq(t), 0, 0)),
        pl.BlockSpec((None, WINDOW, 128), lambda t: (seq(t), 0, 0)),
        pl.BlockSpec((None, M_HEADS, 64, 128), lambda t: (seq(t), 0, 0, 0)),
        pl.BlockSpec((None, M_HEADS, 64), lambda t: (seq(t), 0, 0)),
        pl.BlockSpec((None, 1, M_HEADS), lambda t: (seq(t), 0, 0)),
        pl.BlockSpec((nb, None, d), lambda t: (0, 0, 0)),
    )
    scratch = [
        pltpu.VMEM((N_SUB, T_BLK, N_CAT), F32),
        pltpu.VMEM((N_SUB, T_BLK, 512), BF16),
        pltpu.VMEM((N_SUB, T_BLK, 512), BF16),
        pltpu.VMEM((T_BLK, d_ff), BF16),
        pltpu.VMEM((N_Q_HEADS, 128, 128), F32),
        pltpu.VMEM((128, 128), F32),
        pltpu.VMEM((128, 128), F32),
        pltpu.VMEM((2, 256, 128), F32),
        pltpu.VMEM((8, 128), F32),
        pltpu.VMEM((d, N_CAT), BF16),
        pltpu.VMEM(wao.shape, BF16), pltpu.VMEM(wmo.shape, BF16), pltpu.VMEM(wout.shape, BF16),
        pltpu.VMEM(wg.shape, BF16), pltpu.VMEM(wu.shape, BF16), pltpu.VMEM(wd.shape, BF16),
        pltpu.SemaphoreType.DMA((2,)),
    ]
    return pl.pallas_call(
        functools.partial(_layer_kernel, nj),
        grid=(n_prompt + 1,),
        in_specs=in_specs,
        out_specs=out_specs,
        out_shape=out_shapes,
        scratch_shapes=scratch,
        compiler_params=pltpu.CompilerParams(
            dimension_semantics=("arbitrary",),
            vmem_limit_bytes=VMEM_LIMIT_BYTES),
        name="layer",
    )(x, wt, wao, wmo, wout, wg, wu, wd, gattn, gffn, gfin, ghead, bif, fbuck, relb, sinks,
      xs3, zs, yatts, cqs, scals)


W_QA, W_KA, W_VA, W_QM, W_KM, W_VM, W_OM, W_IF, W_GA, W_GM, W_END = (
    0, 512, 640, 768, 1024, 1280, 1792, 2304, 2312, 3336, 4360)
ZT_KA, ZT_VA, ZT_QM, ZT_KM, ZT_IF, ZT_ROWS = 0, 128, 256, 512, 768, 776


def _sample_inproj_kernel(x_ref, g_ref, wt_hbm, nt_ref, mt_ref, bif_ref,
                          z_ref, zt_ref, no_ref, mo_ref, scal_ref,
                          wt_ref, sems):
    _sample_projection(x_ref, g_ref, wt_hbm, wt_ref, sems, z_ref, zt_ref)
    r8c = lax.broadcasted_iota(jnp.int32, (8, 1), 0)
    bias_col = jnp.zeros((8, 1), F32)
    for r in range(8):
        bias_col = jnp.where(r8c == r, bif_ref[r // M_HEADS, r % M_HEADS], bias_col)
    ift = zt_ref[ZT_IF:ZT_ROWS, :] + bias_col
    m0 = mt_ref[...]
    r16 = lax.broadcasted_iota(jnp.int32, (16, LANES), 0)
    table = jnp.zeros((16, LANES), F32)
    for hd in range(M_HEADS):
        ig = ift[hd:hd + 1, :]
        lf = _log_sigmoid(ift[M_HEADS + hd:M_HEADS + hd + 1, :])
        m_prev = m0[hd:hd + 1, :]
        a = ig - lf
        m_t = lf + jnp.maximum(m_prev, a)
        dgate = jnp.exp(a + lf - m_t)
        inter = jnp.exp(m_prev + lf - m_t)
        qt = zt_ref[ZT_QM + 64 * hd:ZT_QM + 64 * hd + 64, :]
        kt = zt_ref[ZT_KM + 64 * hd:ZT_KM + 64 * hd + 64, :] * (HEAD_DIM ** -0.5)
        nt = nt_ref[hd]
        no_ref[hd] = inter * nt + dgate * kt
        mo_ref[hd:hd + 1, :] = m_t
        qk = jnp.sum(qt * kt, axis=0, keepdims=True)
        nq = jnp.sum(nt * qt, axis=0, keepdims=True)
        w = dgate * qk
        den = inter * nq + w
        scale = 1.0 / jnp.maximum(jnp.abs(den), jnp.exp(-m_t))
        for base, val in ((SC_INTER, inter), (SC_W, w), (SC_SCALE, scale), (SC_D, dgate)):
            table = jnp.where(r16 == base + hd, val, table)
    full = jnp.concatenate([table, jnp.zeros((LANES - 16, LANES), F32)], axis=0)
    scal_ref[...] = full.T


WT_CHUNKS = ((W_QA, 768), (768, 1536), (1536, W_IF), (W_IF, W_GA), (W_GA, W_GM), (W_GM, W_END))


def _sample_projection(x_ref, g_ref, wt_hbm, wt_ref, sems, z_ref, zt_ref):
    copies = [pltpu.make_async_copy(wt_hbm.at[a:b, :], wt_ref.at[a:b, :], sems.at[k])
              for k, (a, b) in enumerate(WT_CHUNKS)]
    for c in copies:
        c.start()
    h32 = _rms(x_ref[...], g_ref[...])
    h = h32.astype(BF16)
    ht = h32.T.astype(BF16)
    copies[0].wait()
    z_ref[:, QA:768] = _dot_nt(h, wt_ref[W_QA:768, :].astype(BF16))
    zt_ref[ZT_KA:ZT_QM, :] = _dot(wt_ref[W_KA:W_QM, :].astype(BF16), ht)
    copies[1].wait()
    z_ref[:, 768:1536] = _dot_nt(h, wt_ref[768:1536, :].astype(BF16))
    zt_ref[ZT_QM:ZT_IF, :] = _dot(wt_ref[W_QM:W_VM, :].astype(BF16), ht)
    copies[2].wait()
    z_ref[:, 1536:W_IF] = _dot_nt(h, wt_ref[1536:W_IF, :].astype(BF16))
    copies[3].wait()
    w_if = jnp.concatenate([wt_ref[W_IF:W_GA, :], jnp.zeros((LANES - 8, wt_ref.shape[1]), F32)],
                           axis=0).astype(BF16)
    z_ref[:, IF:IF + LANES] = _dot_nt(h, w_if)
    zt_ref[ZT_IF:ZT_ROWS, :] = _dot(w_if, ht)[0:8, :]
    copies[4].wait()
    z_ref[:, GA:GM] = _dot_nt(h, wt_ref[W_GA:W_GM, :].astype(BF16))
    copies[5].wait()
    z_ref[:, GM:N_CAT] = _dot_nt(h, wt_ref[W_GM:W_END, :].astype(BF16))


def _sample_inproj_call(x3, gattn, wt, nt, mt, bif2):
    n, _, d = x3.shape
    assert n == LANES
    full = lambda s: pl.BlockSpec(tuple(s), lambda i: (0,) * len(s))
    return pl.pallas_call(
        _sample_inproj_kernel,
        grid=(1,),
        in_specs=[pl.BlockSpec((n, None, d), lambda i: (0, 0, 0)),
                  pl.BlockSpec((1, d), lambda i: (0, 0)),
                  pl.BlockSpec(memory_space=pl.ANY),
                  full(nt.shape), full(mt.shape), _smem()],
        out_specs=(full((n, N_CAT)), full((ZT_ROWS, n)), full(nt.shape), full(mt.shape),
                   full((n, LANES))),
        out_shape=(jax.ShapeDtypeStruct((n, N_CAT), F32), jax.ShapeDtypeStruct((ZT_ROWS, n), F32),
                   jax.ShapeDtypeStruct(nt.shape, F32), jax.ShapeDtypeStruct(mt.shape, F32),
                   jax.ShapeDtypeStruct((n, LANES), F32)),
        scratch_shapes=[pltpu.VMEM(wt.shape, F32), pltpu.SemaphoreType.DMA((len(WT_CHUNKS),))],
        compiler_params=pltpu.CompilerParams(dimension_semantics=("arbitrary",),
                                             vmem_limit_bytes=VMEM_LIMIT_BYTES),
        name="sample_inproj",
    )(x3, gattn, wt, nt, mt, bif2)


SK2_BB = 16
SC_INTER, SC_W, SC_SCALE, SC_D, SC_COLS = 0, 4, 8, 12, 16
HEAD_ROW_ORDER = (0, 2, 4, 6, 1, 3, 5, 7)


def _sample_mixer_kernel(z_ref, zt_ref, kt_ref, vt_ref, ct_ref, sbuck_ref,
                         scal_ref, relb_ref, sinks_ref,
                         yatt_ref, cq_ref, ko_ref, vo_ref, co_ref,
                         sbias_scr):
    i = pl.program_id(0)
    r8 = lax.broadcasted_iota(jnp.int32, (8, LANES), 0)
    l8 = lax.broadcasted_iota(jnp.int32, (8, LANES), 1)
    r8c = lax.broadcasted_iota(jnp.int32, (8, 1), 0)

    @pl.when(i == 0)
    def _prologue():
        sb = sbuck_ref[...]
        acc = jnp.zeros((8, LANES), F32)
        for rrow, hd in enumerate(HEAD_ROW_ORDER):
            acc = jnp.where(r8 == rrow, _bias_from_buckets(sb, relb_ref, hd), acc)
        sbias_scr[...] = acc

    sink = jnp.zeros((8, 1), F32)
    for rrow, hd in enumerate(HEAD_ROW_ORDER):
        sink = jnp.where(r8c == rrow, sinks_ref[hd], sink)
    sbias = sbias_scr[...]
    lane_w = lax.broadcasted_iota(jnp.int32, (WINDOW, LANES), 1)
    last_lane = lane_w == WINDOW - 1
    r256 = lax.broadcasted_iota(jnp.int32, (8, 2 * LANES), 0)
    l256 = lax.broadcasted_iota(jnp.int32, (8, 2 * LANES), 1)
    own64 = (l256 // HEAD_DIM) == r256
    l512r = lax.broadcasted_iota(jnp.int32, (1, 4 * LANES), 1) // LANES
    lo1 = lax.broadcasted_iota(jnp.int32, (1, LANES), 1) < HEAD_DIM
    row_b = lax.broadcasted_iota(jnp.int32, (LANES, 2 * LANES), 0)
    ktm = [(zt_ref[ZT_KM + 128 * pr:ZT_KM + 128 * pr + 128, :] * (HEAD_DIM ** -0.5)).astype(BF16)
           for pr in range(2)]

    nrow = 8 * SK2_BB
    lst = lax.broadcasted_iota(jnp.int32, (nrow, LANES), 1)
    z1 = jnp.zeros((1, LANES), F32)
    qm_l, so_l, kn_l, vn_l = [], [], [], []
    for bi in range(SK2_BB):
        qp = [z_ref[bi:bi + 1, QA + 128 * p:QA + 128 * p + 128] * (HEAD_DIM ** -0.5) for p in range(4)]
        qpr = [pltpu.roll(x, HEAD_DIM, 1) for x in qp]
        rows = [jnp.where(lo1, qp[0], z1), jnp.where(lo1, qp[1], z1),
                jnp.where(lo1, z1, qpr[2]), jnp.where(lo1, z1, qpr[3]),
                jnp.where(lo1, qpr[0], z1), jnp.where(lo1, qpr[1], z1),
                jnp.where(lo1, z1, qp[2]), jnp.where(lo1, z1, qp[3])]
        qm = jnp.zeros((8, LANES), F32)
        for r in range(8):
            qm = jnp.where(r8 == r, rows[r], qm)
        qm_l.append(qm)
        so_l.append(_dot(qm.astype(BF16), kt_ref[bi].astype(BF16)))
        kn_l.append(jnp.broadcast_to(z_ref[bi:bi + 1, KA:KA + 128], (8, LANES)))
        vn_l.append(jnp.broadcast_to(z_ref[bi:bi + 1, VA:VA + 128], (8, LANES)))
    qm_all = jnp.concatenate(qm_l, axis=0)
    s_old = jnp.concatenate(so_l, axis=0)
    vn_all = jnp.concatenate(vn_l, axis=0)
    s_new = jnp.sum(qm_all * jnp.concatenate(kn_l, axis=0), axis=1, keepdims=True)
    sbias_all = jnp.concatenate([sbias] * SK2_BB, axis=0)
    sink_all = jnp.concatenate([sink] * SK2_BB, axis=0)
    s = jnp.where(lst == WINDOW - 1, s_new, pltpu.roll(s_old, WINDOW - 1, 1)) + sbias_all
    mx = jnp.maximum(jnp.max(s, axis=-1, keepdims=True), sink_all)
    pe = jnp.exp(s - mx)
    den = jnp.sum(pe, axis=-1, keepdims=True) + jnp.exp(sink_all - mx)
    pn = pe * (1.0 / den)
    p_old = jnp.where(lst == 0, 0.0, pltpu.roll(pn, 1, 1))
    oa_l = [_dot_nt(p_old[8 * bi:8 * bi + 8].astype(BF16), vt_ref[bi].astype(BF16))
            for bi in range(SK2_BB)]
    oa = jnp.concatenate(oa_l, axis=0) + pn[:, WINDOW - 1:WINDOW] * vn_all
    oar = pltpu.roll(oa, HEAD_DIM, 1)
    for bi in range(SK2_BB):
        r0 = 8 * bi
        pairs = [jnp.where(lo1, oa[r0:r0 + 1], oar[r0 + 4:r0 + 5]),
                 jnp.where(lo1, oa[r0 + 1:r0 + 2], oar[r0 + 5:r0 + 6]),
                 jnp.where(lo1, oar[r0 + 2:r0 + 3], oa[r0 + 6:r0 + 7]),
                 jnp.where(lo1, oar[r0 + 3:r0 + 4], oa[r0 + 7:r0 + 8])]
        for p in range(4):
            yatt_ref[bi:bi + 1, 128 * p:128 * p + 128] = pairs[p]

    for bi in range(SK2_BB):
        b = i * SK2_BB + bi
        shift = WINDOW - 1 - b
        kcol = pltpu.roll(zt_ref[ZT_KA:ZT_KA + 128, :], shift, 1)
        vcol = pltpu.roll(zt_ref[ZT_VA:ZT_VA + 128, :], shift, 1)
        ko_ref[bi] = jnp.where(last_lane, kcol, pltpu.roll(kt_ref[bi], WINDOW - 1, 1))
        vo_ref[bi] = jnp.where(last_lane, vcol, pltpu.roll(vt_ref[bi], WINDOW - 1, 1))

        qm_row = z_ref[bi:bi + 1, QM:QM + 256]
        vm_row = z_ref[bi:bi + 1, VM:VM + 512]
        qbd = jnp.where(own64, qm_row, 0.0)
        ct_all = jnp.concatenate([ct_ref[bi, hd] for hd in range(M_HEADS)], axis=0)
        cq = _dot(qbd.astype(BF16), ct_all.astype(BF16))
        for hd in range(M_HEADS):
            cq_ref[bi:bi + 1, 128 * hd:128 * hd + 128] = cq[hd:hd + 1, :]
        dsel = jnp.zeros((1, 4 * LANES), F32)
        for hd in range(M_HEADS):
            dsel = jnp.where(l512r == hd, scal_ref[b, SC_D + hd], dsel)
        vs = vm_row * dsel
        for pr in range(2):
            vsel = jnp.where(row_b == b, vs[:, 256 * pr:256 * pr + 256], 0.0).astype(BF16)
            upd = _dot(ktm[pr], vsel)
            for hh in range(2):
                hd = 2 * pr + hh
                co_ref[bi, hd] = (scal_ref[b, SC_INTER + hd] * ct_ref[bi, hd]
                                  + upd[64 * hh:64 * hh + 64, 128 * hh:128 * hh + 128])


def _sample_mixer_call(z, zt, kt, vt, ct, sbuck, scal_small, relb_t, sinks):
    nb = z.shape[0]
    assert nb % SK2_BB == 0 and nb == LANES
    bb = SK2_BB
    out_shapes = (
        jax.ShapeDtypeStruct((nb, 512), F32),
        jax.ShapeDtypeStruct((nb, 512), F32),
        jax.ShapeDtypeStruct(kt.shape, F32),
        jax.ShapeDtypeStruct(vt.shape, F32),
        jax.ShapeDtypeStruct(ct.shape, F32),
    )
    blk = lambda s: pl.BlockSpec((bb,) + tuple(s[1:]), lambda i: (i,) + (0,) * (len(s) - 1))
    full = lambda s: pl.BlockSpec(tuple(s), lambda i: (0,) * len(s))
    in_specs = [blk(z.shape), full(zt.shape), blk(kt.shape), blk(vt.shape), blk(ct.shape),
                full(sbuck.shape), _smem(), _smem(), _smem()]
    out_specs = (blk((nb, 512)), blk((nb, 512)), blk(kt.shape), blk(vt.shape), blk(ct.shape))
    return pl.pallas_call(
        _sample_mixer_kernel,
        grid=(nb // bb,),
        in_specs=in_specs,
        out_specs=out_specs,
        out_shape=out_shapes,
        scratch_shapes=[pltpu.VMEM((8, LANES), F32)],
        compiler_params=pltpu.CompilerParams(dimension_semantics=("arbitrary",)),
        name="sample_mixers",
    )(z, zt, kt, vt, ct, sbuck, scal_small, relb_t, sinks)


def kernel(x_prompt, x_sample, cache_k_win, cache_v_win, state_mlstm_C, state_mlstm_n, state_mlstm_m,
           rel_bias, w_in, b_if, sinks, g_attn_norm, g_head, w_att_out, w_mlstm_out, w_out,
           g_ffn_norm, w_gate, w_up, w_down, g_final):
    depth = w_in.shape[0]
    assert depth == 1
    bsz, s_len, d = x_prompt.shape
    nb = x_sample.shape[0]
    assert x_sample.shape[1] == 1 and cache_k_win.shape[2] == WINDOW

    wt = w_in[0].T
    gattn = g_attn_norm[0].reshape(1, d)
    gffn = g_ffn_norm[0].reshape(1, d)
    gfin = g_final.reshape(1, d)
    ghead = g_head[0].reshape(1, 512)
    bif = jnp.concatenate([b_if[0].reshape(1, 2 * M_HEADS), jnp.zeros((1, LANES - 2 * M_HEADS), F32)], axis=1)
    sink_v = sinks[0]
    relb_t = rel_bias.T

    qi = jnp.arange(WINDOW)[:, None]
    kj = jnp.arange(WINDOW)[None, :]
    fbuck = _t5_bucket(jnp.where(kj <= qi, qi - kj, WINDOW + qi - kj)).astype(jnp.int32).T
    sbuck = jnp.broadcast_to(_t5_bucket(WINDOW - 1 - kj), (8, WINDOW)).astype(jnp.int32)

    feat = N_KV_HEADS * HEAD_DIM
    kt_in = jnp.transpose(cache_k_win[0], (0, 2, 3, 1)).reshape(nb, feat, WINDOW)
    vt_in = jnp.transpose(cache_v_win[0], (0, 2, 3, 1)).reshape(nb, feat, WINDOW)
    ct_in = jnp.transpose(state_mlstm_C[0], (0, 1, 3, 2))
    nt_in = jnp.transpose(state_mlstm_n[0], (1, 2, 0))
    mt_in = state_mlstm_m[0].T
    z_s, zt_s, nt_s, mt_s, scal_s = _sample_inproj_call(x_sample, gattn, wt, nt_in, mt_in, b_if[0])
    yatt_s, cq_s, kt_s, vt_s, ct_s = _sample_mixer_call(
        z_s, zt_s, kt_in, vt_in, ct_in, sbuck, scal_s[:, :SC_COLS], relb_t, sink_v)
    y_p, kt_p, vt_p, ct_p, n_p, m_p, y_s = _layer_call(
        x_prompt, wt, w_att_out[0], w_mlstm_out[0], w_out[0], w_gate[0], w_up[0], w_down[0],
        gattn, gffn, gfin, ghead, bif, fbuck, relb_t, sink_v,
        x_sample, z_s, yatt_s, cq_s, scal_s)

    def window_out(t, n):
        return jnp.transpose(t.reshape(n, N_KV_HEADS, HEAD_DIM, WINDOW), (0, 3, 1, 2))[None]

    return (y_p, y_s,
            window_out(kt_p, bsz), window_out(vt_p, bsz),
            jnp.transpose(ct_p, (0, 1, 3, 2))[None], n_p[None], m_p.reshape(1, bsz, M_HEADS),
            window_out(kt_s, nb), window_out(vt_s, nb),
            jnp.transpose(ct_s, (0, 1, 3, 2))[None], jnp.transpose(nt_s, (2, 0, 1))[None], mt_s.T[None])
```

```python
import functools
import math

import jax
import jax.numpy as jnp
from jax import lax
from jax.experimental import pallas as pl
from jax.experimental.pallas import tpu as pltpu

F32 = jnp.float32
BF16 = jnp.bfloat16

HEAD_DIM = 64
N_Q_HEADS = 8
N_KV_HEADS = 2
WINDOW = 128
N_BUCKETS = 32
MAX_DISTANCE = WINDOW
M_HEADS = 4
EPS = 1e-6
NEG = -1e30

LANES = 128
VMEM_LIMIT_BYTES = 61 * 1024 * 1024

QA, KA, VA, QM, KM, VM, OM, IF, GA, GM = 0, 512, 640, 768, 1024, 1280, 1792, 2304, 2432, 3456
N_CAT = 4480
Z_GROUPS = ((768, 2432), (0, 768), (2432, 4480))
T_BLK = 256
N_SUB = 2


def _dot(a, b):
    return jnp.dot(a, b, preferred_element_type=F32)


def _dot_nt(a, b):
    return lax.dot_general(a, b, (((1,), (1,)), ((), ())), preferred_element_type=F32)


def _sigmoid(x):
    return 1.0 / (1.0 + jnp.exp(-x))


def _log_sigmoid(x):
    return jnp.minimum(x, 0.0) - jnp.log1p(jnp.exp(-jnp.abs(x)))


def _rms(x, g):
    ms = jnp.mean(x * x, axis=-1, keepdims=True)
    return x * lax.rsqrt(ms + EPS) * g


def _t5_bucket(dist):
    n = jnp.maximum(dist, 0)
    max_exact = N_BUCKETS // 2
    nf = jnp.maximum(n, 1).astype(F32)
    large = max_exact + jnp.floor(jnp.log(nf / max_exact) / math.log(MAX_DISTANCE / max_exact)
                                  * (N_BUCKETS - max_exact)).astype(jnp.int32)
    large = jnp.minimum(large, N_BUCKETS - 1)
    return jnp.where(n < max_exact, n, large)


def _bias_from_buckets(buckets, relb_ref, head):
    acc = jnp.zeros(buckets.shape, F32)
    for k in range(N_BUCKETS):
        acc = jnp.where(buckets == k, relb_ref[head, k], acc)
    return acc


def _layer_kernel(nj, x_ref, wt_hbm, wao_hbm, wmo_hbm, wout_hbm, wg_hbm, wu_hbm, wd_hbm,
                  gattn_ref, gffn_ref, gfin_ref, ghead_ref, bif_ref, fbuck_ref, relb_ref, sinks_ref,
                  xs_ref, zs_ref, yatts_ref, cqs_ref, scals_ref,
                  y_ref, kwin_ref, vwin_ref, c_ref, n_ref, m_ref, ys_ref,
                  z_scr, yatt_scr, ym_scr, a_scr, bias_scr, kprev_scr, vprev_scr, cbd_scr, st_scr,
                  wcat_ref, wao_ref, wmo_ref, wout_ref, wg_ref, wu_ref, wd_ref, dma_sem):
    T = T_BLK
    t = pl.program_id(0)
    n_prompt = pl.num_programs(0) - 1
    is_prompt = t < n_prompt
    j = lax.rem(t, nj)

    @pl.when(t == 0)
    def _first_step():
        def build_bias():
            fb = fbuck_ref[...]
            for h in range(N_Q_HEADS):
                bias_scr[h] = _bias_from_buckets(fb, relb_ref, h)

        chunks = _weight_chunks(wt_hbm, wcat_ref, ((wao_hbm, wao_ref), (wmo_hbm, wmo_ref),
                                                   (wout_hbm, wout_ref), (wg_hbm, wg_ref),
                                                   (wu_hbm, wu_ref), (wd_hbm, wd_ref)))
        _cast_weights(chunks, z_scr, dma_sem, while_first_copy=build_bias)

    @pl.when(jnp.logical_and(is_prompt, j == 0))
    def _reset_state():
        kprev_scr[...] = jnp.zeros_like(kprev_scr)
        vprev_scr[...] = jnp.zeros_like(vprev_scr)
        cbd_scr[...] = jnp.zeros_like(cbd_scr)
        st_scr[...] = jnp.zeros_like(st_scr)

    @pl.when(is_prompt)
    def _prompt_step():
        for s in range(N_SUB):
            _in_projection(x_ref.at[s * T:(s + 1) * T, :], gattn_ref, wcat_ref, z_scr.at[s])
        carry = _load_carry(kprev_scr, vprev_scr, cbd_scr, st_scr)
        for s in range(N_SUB):
            carry = _mixers(z_scr.at[s], yatt_scr.at[s], ym_scr.at[s], carry, ghead_ref, bif_ref,
                            bias_scr, sinks_ref, first_of_sequence=(j == 0) if s == 0 else None)
        _store_carry(carry, kprev_scr, vprev_scr, cbd_scr, st_scr)
        for s in range(N_SUB):
            _merge(x_ref.at[s * T:(s + 1) * T, :], z_scr.at[s], yatt_scr.at[s], ym_scr.at[s],
                   wao_ref, wmo_ref, wout_ref, gffn_ref, y_ref.at[s * T:(s + 1) * T, :])
        for s in range(N_SUB):
            _ffn(yatt_scr.at[s], ym_scr.at[s], a_scr, wg_ref, wu_ref, wd_ref, gfin_ref,
                 y_ref.at[s * T:(s + 1) * T, :])

    @pl.when(jnp.logical_not(is_prompt))
    def _sample_step():
        nb = xs_ref.shape[0]
        scal = scals_ref[...]
        yatt_scr[0, 0:nb, :] = yatts_ref[...].astype(BF16)
        for hd in range(M_HEADS):
            inter = scal[:, SC_INTER + hd:SC_INTER + hd + 1]
            w = scal[:, SC_W + hd:SC_W + hd + 1]
            scale = scal[:, SC_SCALE + hd:SC_SCALE + hd + 1]
            hv = (inter * cqs_ref[:, 128 * hd:128 * hd + 128]
                  + w * zs_ref[:, VM + 128 * hd:VM + 128 * hd + 128]) * scale
            hn = hv * lax.rsqrt(jnp.mean(hv * hv, axis=-1, keepdims=True) + EPS)
            hn = hn * ghead_ref[:, 128 * hd:128 * hd + 128]
            om = zs_ref[:, OM + 128 * hd:OM + 128 * hd + 128]
            ym_scr[0, 0:nb, 128 * hd:128 * hd + 128] = (hn * _sigmoid(om)).astype(BF16)
        _merge(xs_ref, zs_ref, yatt_scr.at[0, 0:nb, :], ym_scr.at[0, 0:nb, :],
               wao_ref, wmo_ref, wout_ref, gffn_ref, ys_ref)
        _ffn(yatt_scr.at[0, 0:nb, :], ym_scr.at[0, 0:nb, :], a_scr.at[0:nb, :], wg_ref, wu_ref, wd_ref,
             gfin_ref, ys_ref)

    @pl.when(jnp.logical_and(is_prompt, j == nj - 1))
    def _write_state():
        kwin_ref[...] = kprev_scr[...].T
        vwin_ref[...] = vprev_scr[...].T
        stn = st_scr[...]
        cts = [cbd_scr[0].T, cbd_scr[1].T]
        for hd in range(M_HEADS):
            p, hh = hd // 2, hd % 2
            c_ref[hd] = cts[p][64 * hh:64 * hh + 64, 128 * hh:128 * hh + 128]
            n_ref[hd:hd + 1, :] = stn[p:p + 1, 64 * hh:64 * hh + 64]
            m_ref[0:1, hd:hd + 1] = stn[2 + hd:3 + hd, 0:1]


def _in_projection(x_ref, gattn_ref, wcat_ref, z_ref):
    h = _rms(x_ref[...], gattn_ref[...]).astype(BF16)
    for c0, c1 in Z_GROUPS:
        z_ref[:, c0:c1] = _dot(h, wcat_ref[:, c0:c1])


def _weight_chunks(wt_hbm, wcat_ref, plain):
    r = T_BLK
    chunks = [(wt_hbm.at[a:a + r, :], wcat_ref.at[:, a:a + r], True) for a in range(0, W_IF, r)]
    chunks.append((wt_hbm.at[W_IF:W_GA, :], wcat_ref.at[:, IF:IF + LANES], True))
    chunks += [(wt_hbm.at[a:a + r, :], wcat_ref.at[:, GA + a - W_GA:GA + a - W_GA + r], True)
               for a in range(W_GA, W_END, r)]
    for src, dst in plain:
        chunks += [(src.at[a:a + r, :], dst.at[a:a + r, :], False) for a in range(0, src.shape[0], r)]
    return chunks


def _cast_weights(chunks, stage_ref, sem, while_first_copy):
    def copy(i):
        src = chunks[i][0]
        n, c = src.shape
        return pltpu.make_async_copy(src, stage_ref.at[i % 2, 0:n, 0:c], sem.at[i % 2])

    copy(0).start()
    for i, (src, dst, transpose) in enumerate(chunks):
        if i + 1 < len(chunks):
            copy(i + 1).start()
        if i == 0:
            while_first_copy()
        copy(i).wait()
        n, c = src.shape
        val = stage_ref[i % 2, 0:n, 0:c]
        rows = dst.shape[1] if transpose else dst.shape[0]
        if rows != n:
            val = jnp.concatenate([val, jnp.zeros((rows - n, c), F32)], axis=0)
        dst[...] = (val.T if transpose else val).astype(BF16)


def _load_carry(kprev_scr, vprev_scr, cbd_scr, st_scr):
    st = st_scr[...]
    return dict(kp=kprev_scr[...], vp=vprev_scr[...], cbd=[cbd_scr[0], cbd_scr[1]],
                n=[st[0:1, :], st[1:2, :]], m=[st[2 + hd:3 + hd, 0:1] for hd in range(M_HEADS)])


def _store_carry(carry, kprev_scr, vprev_scr, cbd_scr, st_scr):
    kprev_scr[...] = carry["kp"]
    vprev_scr[...] = carry["vp"]
    for p in range(2):
        cbd_scr[p] = carry["cbd"][p]
        st_scr[p:p + 1, :] = carry["n"][p]
    for hd in range(M_HEADS):
        st_scr[2 + hd:3 + hd, :] = jnp.broadcast_to(carry["m"][hd], (1, LANES))


def _merge(x_ref, z_ref, yatt_ref, ym_ref, wao_ref, wmo_ref, wout_ref, gffn_ref, y_ref):
    half = yatt_ref.shape[1]
    ya = _dot(yatt_ref[...], wao_ref[...])
    ymm = _dot(ym_ref[...], wmo_ref[...])
    mixed = _sigmoid(z_ref[:, GA:GA + 1024]) * ya + _sigmoid(z_ref[:, GM:GM + 1024]) * ymm
    x1 = x_ref[...] + _dot(mixed.astype(BF16), wout_ref[...])
    y_ref[...] = x1
    h2 = _rms(x1, gffn_ref[...]).astype(BF16)
    yatt_ref[...] = h2[:, :half]
    ym_ref[...] = h2[:, half:]


def _ffn(h2a_ref, h2b_ref, a_ref, wg_ref, wu_ref, wd_ref, gfin_ref, y_ref):
    d_ff = wg_ref.shape[1]
    h2 = jnp.concatenate([h2a_ref[...], h2b_ref[...]], axis=1)
    for c0 in range(0, d_ff, 256):
        g = _dot(h2, wg_ref[:, c0:c0 + 256])
        u = _dot(h2, wu_ref[:, c0:c0 + 256])
        a_ref[:, c0:c0 + 256] = (g * _sigmoid(g) * u).astype(BF16)
    x2 = y_ref[...] + _dot(a_ref[...], wd_ref[...])
    y_ref[...] = _rms(x2, gfin_ref[...])


def _mixers(z_ref, yatt_ref, ym_ref, carry, ghead_ref, bif_ref, bias_scr, sinks_ref, first_of_sequence):
    T = T_BLK
    lane = lax.broadcasted_iota(jnp.int32, (T, LANES), 1)
    lane_lo = lane < HEAD_DIM
    k_all = z_ref[:, KA:KA + 128]
    v_all = z_ref[:, VA:VA + 128]
    jk = lax.broadcasted_iota(jnp.int32, (128, T), 0)
    iq = lax.broadcasted_iota(jnp.int32, (128, T), 1)
    iq = jnp.where(iq < 128, iq, iq - 128)
    tri_t = jk <= iq
    if first_of_sequence is not None:
        valid_t = jk <= iq + jnp.where(first_of_sequence, 0, 2 * LANES)
    lane_q = lax.broadcasted_iota(jnp.int32, (1, T), 1)
    feat_lo = lax.broadcasted_iota(jnp.int32, (128, T), 0) < HEAD_DIM
    vt_all = v_all.T
    for sb in range(2):
        r0 = 128 * sb
        if sb == 0:
            kp, vpt = carry["kp"], carry["vp"].T
        else:
            kp, vpt = k_all[0:128], vt_all[:, 0:128]
        kcat = jnp.concatenate([kp, k_all[r0:r0 + 128]], axis=0)
        kroll = pltpu.roll(kcat, HEAD_DIM, 1)
        zero = jnp.zeros_like(kcat)
        k_mats = [jnp.where(lane_lo, kcat, zero), jnp.where(lane_lo, zero, kroll),
                  jnp.where(lane_lo, kroll, zero), jnp.where(lane_lo, zero, kcat)]
        vcat_t = jnp.concatenate([vpt, vt_all[:, r0:r0 + 128]], axis=1)
        vroll_t = pltpu.roll(vcat_t, HEAD_DIM, 0)
        zero_t = jnp.zeros_like(vcat_t)
        v_mats_t = [jnp.where(feat_lo, vcat_t, zero_t), jnp.where(feat_lo, zero_t, vroll_t),
                    jnp.where(feat_lo, vroll_t, zero_t), jnp.where(feat_lo, zero_t, vcat_t)]
        q = (z_ref[r0:r0 + 128, QA:QA + 512] * (HEAD_DIM ** -0.5)).astype(BF16)
        lhs_a = jnp.concatenate([q[:, 0:128], q[:, 128:256]], axis=0)
        lhs_b = jnp.concatenate([q[:, 256:384], q[:, 384:512]], axis=0)
        groups = [(lhs_a, 0, 0, 2), (lhs_a, 1, 1, 3), (lhs_b, 2, 4, 6), (lhs_b, 3, 5, 7)]
        pts = []
        for lhs, mi, ha, hb in groups:
            st = _dot_nt(k_mats[mi].astype(BF16), lhs)
            sf = jnp.where(tri_t, st[128:256, :], st[0:128, :])
            sf = sf + jnp.concatenate([bias_scr[ha], bias_scr[hb]], axis=1)
            if sb == 0 and first_of_sequence is not None:
                sf = jnp.where(valid_t, sf, NEG)
            sink = jnp.where(lane_q < 128, sinks_ref[ha], sinks_ref[hb])
            mx = jnp.maximum(jnp.max(sf, axis=0, keepdims=True), sink)
            p = jnp.exp(sf - mx)
            den = jnp.sum(p, axis=0, keepdims=True) + jnp.exp(sink - mx)
            pn = p * (1.0 / den)
            zp = jnp.zeros_like(pn)
            pts.append(jnp.concatenate([jnp.where(tri_t, zp, pn), jnp.where(tri_t, pn, zp)],
                                       axis=0).astype(BF16))
        ot_a = _dot(jnp.concatenate([v_mats_t[0], v_mats_t[1]], axis=1).astype(BF16),
                    jnp.concatenate([pts[0], pts[1]], axis=0))
        ot_b = _dot(jnp.concatenate([v_mats_t[2], v_mats_t[3]], axis=1).astype(BF16),
                    jnp.concatenate([pts[2], pts[3]], axis=0))
        yatt_ref[r0:r0 + 128, 0:128] = ot_a[:, 0:128].T.astype(BF16)
        yatt_ref[r0:r0 + 128, 128:256] = ot_a[:, 128:256].T.astype(BF16)
        yatt_ref[r0:r0 + 128, 256:384] = ot_b[:, 0:128].T.astype(BF16)
        yatt_ref[r0:r0 + 128, 384:512] = ot_b[:, 128:256].T.astype(BF16)
    new_carry = dict(kp=k_all[128:256], vp=v_all[128:256], cbd=[None, None], n=[None, None],
                     m=[None] * M_HEADS)

    zif = z_ref[:, IF:IF + 128] + bif_ref[...]
    gl = jnp.where(lane < M_HEADS, zif, _log_sigmoid(zif))
    gl_t = gl.T
    tr = lax.broadcasted_iota(jnp.int32, (T, T), 0)
    ts = lax.broadcasted_iota(jnp.int32, (T, T), 1)
    tril = ts <= tr
    triu = tr <= ts
    row2 = lax.broadcasted_iota(jnp.int32, (2 * LANES, LANES), 0)
    lane2 = lax.broadcasted_iota(jnp.int32, (2 * LANES, LANES), 1)
    bd_mask = (row2 < LANES) == (lane2 < HEAD_DIM)
    for p in range(2):
        q_pair = z_ref[:, QM + 128 * p:QM + 128 * p + 128]
        k_pair = z_ref[:, KM + 128 * p:KM + 128 * p + 128] * (HEAD_DIM ** -0.5)
        v_pair = z_ref[:, VM + 256 * p:VM + 256 * p + 256]
        cbd = carry["cbd"][p]
        n_pair = carry["n"][p]
        q_bf = q_pair.astype(BF16)
        qc = _dot_nt(q_bf, cbd.astype(BF16))
        qn_prod = q_pair * n_pair
        ws, decays, m_ends = [], [], []
        for hh in range(2):
            hd = 2 * p + hh
            hmask = lane_lo if hh == 0 else jnp.logical_not(lane_lo)
            ig_c = gl[:, hd:hd + 1]
            lf_c = gl[:, M_HEADS + hd:M_HEADS + hd + 1]
            ig_r = gl_t[hd:hd + 1, :]
            lf_r = gl_t[M_HEADS + hd:M_HEADS + hd + 1, :]
            m_prev = carry["m"][hd]
            b_c = jnp.sum(jnp.where(tril, lf_r, 0.0), axis=1, keepdims=True)
            b_r = jnp.sum(jnp.where(triu, lf_c, 0.0), axis=0, keepdims=True)
            a_r = ig_r - b_r
            cm_c = jnp.max(jnp.where(tril, a_r, NEG), axis=1, keepdims=True)
            mt_c = b_c + jnp.maximum(m_prev, cm_c)
            g_c = b_c - mt_c
            dm = jnp.exp(jnp.where(tril, a_r + g_c, NEG))
            inter_c = jnp.exp(m_prev + g_c)
            k_h = jnp.where(hmask, k_pair, 0.0).astype(BF16)
            w = dm * _dot_nt(q_bf, k_h)
            v_h = v_pair[:, 128 * hh:128 * hh + 128]
            num = inter_c * qc[:, 128 * hh:128 * hh + 128] + _dot(w.astype(BF16), v_h.astype(BF16))
            qn = jnp.sum(jnp.where(hmask, qn_prod, 0.0), axis=1, keepdims=True)
            den = inter_c * qn + jnp.sum(w, axis=1, keepdims=True)
            hv = num / jnp.maximum(jnp.abs(den), jnp.exp(-mt_c))
            hn = hv * lax.rsqrt(jnp.mean(hv * hv, axis=-1, keepdims=True) + EPS)
            hn = hn * ghead_ref[:, 128 * hd:128 * hd + 128]
            om = z_ref[:, OM + 128 * hd:OM + 128 * hd + 128]
            ym_ref[:, 128 * hd:128 * hd + 128] = (hn * _sigmoid(om)).astype(BF16)
            b_end = b_c[T - 1:T, :]
            m_end = mt_c[T - 1:T, :]
            ws.append(jnp.exp((ig_c - b_c) + b_end - m_end))
            decays.append(jnp.exp(m_prev + b_end - m_end))
            m_ends.append(m_end)
        kw = k_pair * jnp.where(lane_lo, ws[0], ws[1])
        upd = _dot(v_pair.T.astype(BF16), kw.astype(BF16))
        dec_rows = jnp.where(row2[:, 0:1] < LANES, decays[0], decays[1])
        new_carry["cbd"][p] = dec_rows * cbd + jnp.where(bd_mask, upd, 0.0)
        dec_lanes = jnp.where(lane_lo[0:1, :], decays[0], decays[1])
        new_carry["n"][p] = dec_lanes * n_pair + jnp.sum(kw, axis=0, keepdims=True)
        for hh in range(2):
            new_carry["m"][2 * p + hh] = m_ends[hh]
    return new_carry


def _resident(shape):
    zeros = (0,) * len(shape)
    return pl.BlockSpec(shape, lambda t: zeros, pipeline_mode=pl.Buffered(1))


def _smem():
    return pl.BlockSpec(memory_space=pltpu.SMEM)


def _layer_call(x, wt, wao, wmo, wout, wg, wu, wd, gattn, gffn, gfin, ghead, bif, fbuck, relb, sinks,
                xs3, zs, yatts, cqs, scals):
    bsz, s_len, d = x.shape
    nb = xs3.shape[0]
    t_step = T_BLK * N_SUB
    assert s_len % t_step == 0 and d == 1024 and nb <= T_BLK
    assert wt.shape == (W_END, d) and N_SUB >= 2
    d_ff = wg.shape[1]
    assert all(w.shape[0] % T_BLK == 0 and w.shape[1] <= N_CAT for w in (wao, wmo, wout, wg, wu, wd))
    hbm = pl.BlockSpec(memory_space=pl.ANY)
    nj = s_len // t_step
    n_prompt = bsz * nj
    out_shapes = (
        jax.ShapeDtypeStruct((bsz, s_len, d), F32),
        jax.ShapeDtypeStruct((bsz, WINDOW, 128), F32),
        jax.ShapeDtypeStruct((bsz, WINDOW, 128), F32),
        jax.ShapeDtypeStruct((bsz, M_HEADS, 64, 128), F32),
        jax.ShapeDtypeStruct((bsz, M_HEADS, 64), F32),
        jax.ShapeDtypeStruct((bsz, 1, M_HEADS), F32),
        jax.ShapeDtypeStruct((nb, 1, d), F32),
    )

    def seq(t):
        return jnp.minimum(t, n_prompt - 1) // nj

    def blk(t):
        return jnp.minimum(t, n_prompt - 1) % nj

    in_specs = [
        pl.BlockSpec((None, t_step, d), lambda t: (seq(t), blk(t), 0)),
        hbm, hbm, hbm, hbm, hbm, hbm, hbm,
        _resident(gattn.shape), _resident(gffn.shape), _resident(gfin.shape), _resident(ghead.shape),
        _resident(bif.shape), _resident(fbuck.shape), _smem(), _smem(),
        pl.BlockSpec((nb, None, d), lambda t: (0, 0, 0), pipeline_mode=pl.Buffered(1)),
        _resident(zs.shape), _resident(yatts.shape), _resident(cqs.shape), _resident(scals.shape),
    ]
    out_specs = (
        pl.BlockSpec((None, t_step, d), lambda t: (seq(t), blk(t), 0)),
        pl.BlockSpec((None, WINDOW, 128), lambda t: (seq(t), 0, 0)),
        pl.BlockSpec((None, WINDOW, 128), lambda t: (seq(t), 0, 0)),
        pl.BlockSpec((None, M_HEADS, 64, 128), lambda t: (seq(t), 0, 0, 0)),
        pl.BlockSpec((None, M_HEADS, 64), lambda t: (seq(t), 0, 0)),
        pl.BlockSpec((None, 1, M_HEADS), lambda t: (seq(t), 0, 0)),
        pl.BlockSpec((nb, None, d), lambda t: (0, 0, 0)),
    )
    scratch = [
        pltpu.VMEM((N_SUB, T_BLK, N_CAT), F32),
        pltpu.VMEM((N_SUB, T_BLK, 512), BF16),
        pltpu.VMEM((N_SUB, T_BLK, 512), BF16),
        pltpu.VMEM((T_BLK, d_ff), BF16),
        pltpu.VMEM((N_Q_HEADS, 128, 128), F32),
        pltpu.VMEM((128, 128), F32),
        pltpu.VMEM((128, 128), F32),
        pltpu.VMEM((2, 256, 128), F32),
        pltpu.VMEM((8, 128), F32),
        pltpu.VMEM((d, N_CAT), BF16),
        pltpu.VMEM(wao.shape, BF16), pltpu.VMEM(wmo.shape, BF16), pltpu.VMEM(wout.shape, BF16),
        pltpu.VMEM(wg.shape, BF16), pltpu.VMEM(wu.shape, BF16), pltpu.VMEM(wd.shape, BF16),
        pltpu.SemaphoreType.DMA((2,)),
    ]
    return pl.pallas_call(
        functools.partial(_layer_kernel, nj),
        grid=(n_prompt + 1,),
        in_specs=in_specs,
        out_specs=out_specs,
        out_shape=out_shapes,
        scratch_shapes=scratch,
        compiler_params=pltpu.CompilerParams(
            dimension_semantics=("arbitrary",),
            vmem_limit_bytes=VMEM_LIMIT_BYTES),
        name="layer",
    )(x, wt, wao, wmo, wout, wg, wu, wd, gattn, gffn, gfin, ghead, bif, fbuck, relb, sinks,
      xs3, zs, yatts, cqs, scals)


W_QA, W_KA, W_VA, W_QM, W_KM, W_VM, W_OM, W_IF, W_GA, W_GM, W_END = (
    0, 512, 640, 768, 1024, 1280, 1792, 2304, 2312, 3336, 4360)
ZT_KA, ZT_VA, ZT_QM, ZT_KM, ZT_IF, ZT_ROWS = 0, 128, 256, 512, 768, 776


def _sample_inproj_kernel(x_ref, g_ref, wt_hbm, nt_ref, mt_ref, bif_ref,
                          z_ref, zt_ref, no_ref, mo_ref, scal_ref,
                          wt_ref, sems):
    _sample_projection(x_ref, g_ref, wt_hbm, wt_ref, sems, z_ref, zt_ref)
    r8c = lax.broadcasted_iota(jnp.int32, (8, 1), 0)
    bias_col = jnp.zeros((8, 1), F32)
    for r in range(8):
        bias_col = jnp.where(r8c == r, bif_ref[r // M_HEADS, r % M_HEADS], bias_col)
    ift = zt_ref[ZT_IF:ZT_ROWS, :] + bias_col
    m0 = mt_ref[...]
    r16 = lax.broadcasted_iota(jnp.int32, (16, LANES), 0)
    table = jnp.zeros((16, LANES), F32)
    for hd in range(M_HEADS):
        ig = ift[hd:hd + 1, :]
        lf = _log_sigmoid(ift[M_HEADS + hd:M_HEADS + hd + 1, :])
        m_prev = m0[hd:hd + 1, :]
        a = ig - lf
        m_t = lf + jnp.maximum(m_prev, a)
        dgate = jnp.exp(a + lf - m_t)
        inter = jnp.exp(m_prev + lf - m_t)
        qt = zt_ref[ZT_QM + 64 * hd:ZT_QM + 64 * hd + 64, :]
        kt = zt_ref[ZT_KM + 64 * hd:ZT_KM + 64 * hd + 64, :] * (HEAD_DIM ** -0.5)
        nt = nt_ref[hd]
        no_ref[hd] = inter * nt + dgate * kt
        mo_ref[hd:hd + 1, :] = m_t
        qk = jnp.sum(qt * kt, axis=0, keepdims=True)
        nq = jnp.sum(nt * qt, axis=0, keepdims=True)
        w = dgate * qk
        den = inter * nq + w
        scale = 1.0 / jnp.maximum(jnp.abs(den), jnp.exp(-m_t))
        for base, val in ((SC_INTER, inter), (SC_W, w), (SC_SCALE, scale), (SC_D, dgate)):
            table = jnp.where(r16 == base + hd, val, table)
    full = jnp.concatenate([table, jnp.zeros((LANES - 16, LANES), F32)], axis=0)
    scal_ref[...] = full.T


WT_CHUNKS = ((W_QA, 768), (768, 1536), (1536, W_IF), (W_IF, W_GA), (W_GA, W_GM), (W_GM, W_END))


def _sample_projection(x_ref, g_ref, wt_hbm, wt_ref, sems, z_ref, zt_ref):
    copies = [pltpu.make_async_copy(wt_hbm.at[a:b, :], wt_ref.at[a:b, :], sems.at[k])
              for k, (a, b) in enumerate(WT_CHUNKS)]
    for c in copies:
        c.start()
    h32 = _rms(x_ref[...], g_ref[...])
    h = h32.astype(BF16)
    ht = h32.T.astype(BF16)
    copies[0].wait()
    z_ref[:, QA:768] = _dot_nt(h, wt_ref[W_QA:768, :].astype(BF16))
    zt_ref[ZT_KA:ZT_QM, :] = _dot(wt_ref[W_KA:W_QM, :].astype(BF16), ht)
    copies[1].wait()
    z_ref[:, 768:1536] = _dot_nt(h, wt_ref[768:1536, :].astype(BF16))
    zt_ref[ZT_QM:ZT_IF, :] = _dot(wt_ref[W_QM:W_VM, :].astype(BF16), ht)
    copies[2].wait()
    z_ref[:, 1536:W_IF] = _dot_nt(h, wt_ref[1536:W_IF, :].astype(BF16))
    copies[3].wait()
    w_if = jnp.concatenate([wt_ref[W_IF:W_GA, :], jnp.zeros((LANES - 8, wt_ref.shape[1]), F32)],
                           axis=0).astype(BF16)
    z_ref[:, IF:IF + LANES] = _dot_nt(h, w_if)
    zt_ref[ZT_IF:ZT_ROWS, :] = _dot(w_if, ht)[0:8, :]
    copies[4].wait()
    z_ref[:, GA:GM] = _dot_nt(h, wt_ref[W_GA:W_GM, :].astype(BF16))
    copies[5].wait()
    z_ref[:, GM:N_CAT] = _dot_nt(h, wt_ref[W_GM:W_END, :].astype(BF16))


def _sample_inproj_call(x3, gattn, wt, nt, mt, bif2):
    n, _, d = x3.shape
    assert n == LANES
    full = lambda s: pl.BlockSpec(tuple(s), lambda i: (0,) * len(s))
    return pl.pallas_call(
        _sample_inproj_kernel,
        grid=(1,),
        in_specs=[pl.BlockSpec((n, None, d), lambda i: (0, 0, 0)),
                  pl.BlockSpec((1, d), lambda i: (0, 0)),
                  pl.BlockSpec(memory_space=pl.ANY),
                  full(nt.shape), full(mt.shape), _smem()],
        out_specs=(full((n, N_CAT)), full((ZT_ROWS, n)), full(nt.shape), full(mt.shape),
                   full((n, LANES))),
        out_shape=(jax.ShapeDtypeStruct((n, N_CAT), F32), jax.ShapeDtypeStruct((ZT_ROWS, n), F32),
                   jax.ShapeDtypeStruct(nt.shape, F32), jax.ShapeDtypeStruct(mt.shape, F32),
                   jax.ShapeDtypeStruct((n, LANES), F32)),
        scratch_shapes=[pltpu.VMEM(wt.shape, F32), pltpu.SemaphoreType.DMA((len(WT_CHUNKS),))],
        compiler_params=pltpu.CompilerParams(dimension_semantics=("arbitrary",),
                                             vmem_limit_bytes=VMEM_LIMIT_BYTES),
        name="sample_inproj",
    )(x3, gattn, wt, nt, mt, bif2)


SK2_BB = 16
SC_INTER, SC_W, SC_SCALE, SC_D, SC_COLS = 0, 4, 8, 12, 16
HEAD_ROW_ORDER = (0, 2, 4, 6, 1, 3, 5, 7)


def _sample_mixer_kernel(z_ref, zt_ref, kt_ref, vt_ref, ct_ref, sbuck_ref,
                         scal_ref, relb_ref, sinks_ref,
                         yatt_ref, cq_ref, ko_ref, vo_ref, co_ref,
                         sbias_scr):
    i = pl.program_id(0)
    r8 = lax.broadcasted_iota(jnp.int32, (8, LANES), 0)
    l8 = lax.broadcasted_iota(jnp.int32, (8, LANES), 1)
    r8c = lax.broadcasted_iota(jnp.int32, (8, 1), 0)

    @pl.when(i == 0)
    def _prologue():
        sb = sbuck_ref[...]
        acc = jnp.zeros((8, LANES), F32)
        for rrow, hd in enumerate(HEAD_ROW_ORDER):
            acc = jnp.where(r8 == rrow, _bias_from_buckets(sb, relb_ref, hd), acc)
        sbias_scr[...] = acc

    sink = jnp.zeros((8, 1), F32)
    for rrow, hd in enumerate(HEAD_ROW_ORDER):
        sink = jnp.where(r8c == rrow, sinks_ref[hd], sink)
    sbias = sbias_scr[...]
    lane_w = lax.broadcasted_iota(jnp.int32, (WINDOW, LANES), 1)
    last_lane = lane_w == WINDOW - 1
    r256 = lax.broadcasted_iota(jnp.int32, (8, 2 * LANES), 0)
    l256 = lax.broadcasted_iota(jnp.int32, (8, 2 * LANES), 1)
    own64 = (l256 // HEAD_DIM) == r256
    l512r = lax.broadcasted_iota(jnp.int32, (1, 4 * LANES), 1) // LANES
    lo1 = lax.broadcasted_iota(jnp.int32, (1, LANES), 1) < HEAD_DIM
    row_b = lax.broadcasted_iota(jnp.int32, (LANES, 2 * LANES), 0)
    ktm = [(zt_ref[ZT_KM + 128 * pr:ZT_KM + 128 * pr + 128, :] * (HEAD_DIM ** -0.5)).astype(BF16)
           for pr in range(2)]

    nrow = 8 * SK2_BB
    lst = lax.broadcasted_iota(jnp.int32, (nrow, LANES), 1)
    z1 = jnp.zeros((1, LANES), F32)
    qm_l, so_l, kn_l, vn_l = [], [], [], []
    for bi in range(SK2_BB):
        qp = [z_ref[bi:bi + 1, QA + 128 * p:QA + 128 * p + 128] * (HEAD_DIM ** -0.5) for p in range(4)]
        qpr = [pltpu.roll(x, HEAD_DIM, 1) for x in qp]
        rows = [jnp.where(lo1, qp[0], z1), jnp.where(lo1, qp[1], z1),
                jnp.where(lo1, z1, qpr[2]), jnp.where(lo1, z1, qpr[3]),
                jnp.where(lo1, qpr[0], z1), jnp.where(lo1, qpr[1], z1),
                jnp.where(lo1, z1, qp[2]), jnp.where(lo1, z1, qp[3])]
        qm = jnp.zeros((8, LANES), F32)
        for r in range(8):
            qm = jnp.where(r8 == r, rows[r], qm)
        qm_l.append(qm)
        so_l.append(_dot(qm.astype(BF16), kt_ref[bi].astype(BF16)))
        kn_l.append(jnp.broadcast_to(z_ref[bi:bi + 1, KA:KA + 128], (8, LANES)))
        vn_l.append(jnp.broadcast_to(z_ref[bi:bi + 1, VA:VA + 128], (8, LANES)))
    qm_all = jnp.concatenate(qm_l, axis=0)
    s_old = jnp.concatenate(so_l, axis=0)
    vn_all = jnp.concatenate(vn_l, axis=0)
    s_new = jnp.sum(qm_all * jnp.concatenate(kn_l, axis=0), axis=1, keepdims=True)
    sbias_all = jnp.concatenate([sbias] * SK2_BB, axis=0)
    sink_all = jnp.concatenate([sink] * SK2_BB, axis=0)
    s = jnp.where(lst == WINDOW - 1, s_new, pltpu.roll(s_old, WINDOW - 1, 1)) + sbias_all
    mx = jnp.maximum(jnp.max(s, axis=-1, keepdims=True), sink_all)
    pe = jnp.exp(s - mx)
    den = jnp.sum(pe, axis=-1, keepdims=True) + jnp.exp(sink_all - mx)
    pn = pe * (1.0 / den)
    p_old = jnp.where(lst == 0, 0.0, pltpu.roll(pn, 1, 1))
    oa_l = [_dot_nt(p_old[8 * bi:8 * bi + 8].astype(BF16), vt_ref[bi].astype(BF16))
            for bi in range(SK2_BB)]
    oa = jnp.concatenate(oa_l, axis=0) + pn[:, WINDOW - 1:WINDOW] * vn_all
    oar = pltpu.roll(oa, HEAD_DIM, 1)
    for bi in range(SK2_BB):
        r0 = 8 * bi
        pairs = [jnp.where(lo1, oa[r0:r0 + 1], oar[r0 + 4:r0 + 5]),
                 jnp.where(lo1, oa[r0 + 1:r0 + 2], oar[r0 + 5:r0 + 6]),
                 jnp.where(lo1, oar[r0 + 2:r0 + 3], oa[r0 + 6:r0 + 7]),
                 jnp.where(lo1, oar[r0 + 3:r0 + 4], oa[r0 + 7:r0 + 8])]
        for p in range(4):
            yatt_ref[bi:bi + 1, 128 * p:128 * p + 128] = pairs[p]

    for bi in range(SK2_BB):
        b = i * SK2_BB + bi
        shift = WINDOW - 1 - b
        kcol = pltpu.roll(zt_ref[ZT_KA:ZT_KA + 128, :], shift, 1)
        vcol = pltpu.roll(zt_ref[ZT_VA:ZT_VA + 128, :], shift, 1)
        ko_ref[bi] = jnp.where(last_lane, kcol, pltpu.roll(kt_ref[bi], WINDOW - 1, 1))
        vo_ref[bi] = jnp.where(last_lane, vcol, pltpu.roll(vt_ref[bi], WINDOW - 1, 1))

        qm_row = z_ref[bi:bi + 1, QM:QM + 256]
        vm_row = z_ref[bi:bi + 1, VM:VM + 512]
        qbd = jnp.where(own64, qm_row, 0.0)
        ct_all = jnp.concatenate([ct_ref[bi, hd] for hd in range(M_HEADS)], axis=0)
        cq = _dot(qbd.astype(BF16), ct_all.astype(BF16))
        for hd in range(M_HEADS):
            cq_ref[bi:bi + 1, 128 * hd:128 * hd + 128] = cq[hd:hd + 1, :]
        dsel = jnp.zeros((1, 4 * LANES), F32)
        for hd in range(M_HEADS):
            dsel = jnp.where(l512r == hd, scal_ref[b, SC_D + hd], dsel)
        vs = vm_row * dsel
        for pr in range(2):
            vsel = jnp.where(row_b == b, vs[:, 256 * pr:256 * pr + 256], 0.0).astype(BF16)
            upd = _dot(ktm[pr], vsel)
            for hh in range(2):
                hd = 2 * pr + hh
                co_ref[bi, hd] = (scal_ref[b, SC_INTER + hd] * ct_ref[bi, hd]
                                  + upd[64 * hh:64 * hh + 64, 128 * hh:128 * hh + 128])


def _sample_mixer_call(z, zt, kt, vt, ct, sbuck, scal_small, relb_t, sinks):
    nb = z.shape[0]
    assert nb % SK2_BB == 0 and nb == LANES
    bb = SK2_BB
    out_shapes = (
        jax.ShapeDtypeStruct((nb, 512), F32),
        jax.ShapeDtypeStruct((nb, 512), F32),
        jax.ShapeDtypeStruct(kt.shape, F32),
        jax.ShapeDtypeStruct(vt.shape, F32),
        jax.ShapeDtypeStruct(ct.shape, F32),
    )
    blk = lambda s: pl.BlockSpec((bb,) + tuple(s[1:]), lambda i: (i,) + (0,) * (len(s) - 1))
    full = lambda s: pl.BlockSpec(tuple(s), lambda i: (0,) * len(s))
    in_specs = [blk(z.shape), full(zt.shape), blk(kt.shape), blk(vt.shape), blk(ct.shape),
                full(sbuck.shape), _smem(), _smem(), _smem()]
    out_specs = (blk((nb, 512)), blk((nb, 512)), blk(kt.shape), blk(vt.shape), blk(ct.shape))
    return pl.pallas_call(
        _sample_mixer_kernel,
        grid=(nb // bb,),
        in_specs=in_specs,
        out_specs=out_specs,
        out_shape=out_shapes,
        scratch_shapes=[pltpu.VMEM((8, LANES), F32)],
        compiler_params=pltpu.CompilerParams(dimension_semantics=("arbitrary",)),
        name="sample_mixers",
    )(z, zt, kt, vt, ct, sbuck, scal_small, relb_t, sinks)


def kernel(x_prompt, x_sample, cache_k_win, cache_v_win, state_mlstm_C, state_mlstm_n, state_mlstm_m,
           rel_bias, w_in, b_if, sinks, g_attn_norm, g_head, w_att_out, w_mlstm_out, w_out,
           g_ffn_norm, w_gate, w_up, w_down, g_final):
    depth = w_in.shape[0]
    assert depth == 1
    bsz, s_len, d = x_prompt.shape
    nb = x_sample.shape[0]
    assert x_sample.shape[1] == 1 and cache_k_win.shape[2] == WINDOW

    wt = w_in[0].T
    gattn = g_attn_norm[0].reshape(1, d)
    gffn = g_ffn_norm[0].reshape(1, d)
    gfin = g_final.reshape(1, d)
    ghead = g_head[0].reshape(1, 512)
    bif = jnp.concatenate([b_if[0].reshape(1, 2 * M_HEADS), jnp.zeros((1, LANES - 2 * M_HEADS), F32)], axis=1)
    sink_v = sinks[0]
    relb_t = rel_bias.T

    qi = jnp.arange(WINDOW)[:, None]
    kj = jnp.arange(WINDOW)[None, :]
    fbuck = _t5_bucket(jnp.where(kj <= qi, qi - kj, WINDOW + qi - kj)).astype(jnp.int32).T
    sbuck = jnp.broadcast_to(_t5_bucket(WINDOW - 1 - kj), (8, WINDOW)).astype(jnp.int32)

    feat = N_KV_HEADS * HEAD_DIM
    kt_in = jnp.transpose(cache_k_win[0], (0, 2, 3, 1)).reshape(nb, feat, WINDOW)
    vt_in = jnp.transpose(cache_v_win[0], (0, 2, 3, 1)).reshape(nb, feat, WINDOW)
    ct_in = jnp.transpose(state_mlstm_C[0], (0, 1, 3, 2))
    nt_in = jnp.transpose(state_mlstm_n[0], (1, 2, 0))
    mt_in = state_mlstm_m[0].T
    z_s, zt_s, nt_s, mt_s, scal_s = _sample_inproj_call(x_sample, gattn, wt, nt_in, mt_in, b_if[0])
    yatt_s, cq_s, kt_s, vt_s, ct_s = _sample_mixer_call(
        z_s, zt_s, kt_in, vt_in, ct_in, sbuck, scal_s[:, :SC_COLS], relb_t, sink_v)
    y_p, kt_p, vt_p, ct_p, n_p, m_p, y_s = _layer_call(
        x_prompt, wt, w_att_out[0], w_mlstm_out[0], w_out[0], w_gate[0], w_up[0], w_down[0],
        gattn, gffn, gfin, ghead, bif, fbuck, relb_t, sink_v,
        x_sample, z_s, yatt_s, cq_s, scal_s)

    def window_out(t, n):
        return jnp.transpose(t.reshape(n, N_KV_HEADS, HEAD_DIM, WINDOW), (0, 3, 1, 2))[None]

    return (y_p, y_s,
            window_out(kt_p, bsz), window_out(vt_p, bsz),
            jnp.transpose(ct_p, (0, 1, 3, 2))[None], n_p[None], m_p.reshape(1, bsz, M_HEADS),
            window_out(kt_s, nb), window_out(vt_s, nb),
            jnp.transpose(ct_s, (0, 1, 3, 2))[None], jnp.transpose(nt_s, (2, 0, 1))[None], mt_s.T[None])
```

```python
import functools
import math

import jax
import jax.numpy as jnp
from jax import lax
from jax.experimental import pallas as pl
from jax.experimental.pallas import tpu as pltpu

F32 = jnp.float32
BF16 = jnp.bfloat16

HEAD_DIM = 64
N_Q_HEADS = 8
N_KV_HEADS = 2
WINDOW = 128
N_BUCKETS = 32
MAX_DISTANCE = WINDOW
M_HEADS = 4
EPS = 1e-6
NEG = -1e30

LANES = 128
VMEM_LIMIT_BYTES = 61 * 1024 * 1024

QA, KA, VA, QM, KM, VM, OM, IF, GA, GM = 0, 512, 640, 768, 1024, 1280, 1792, 2304, 2432, 3456
N_CAT = 4480
Z_GROUPS = ((768, 2432), (0, 768), (2432, 4480))
T_BLK = 256
N_SUB = 2


def _dot(a, b):
    return jnp.dot(a, b, preferred_element_type=F32)


def _dot_nt(a, b):
    return lax.dot_general(a, b, (((1,), (1,)), ((), ())), preferred_element_type=F32)


def _sigmoid(x):
    return 1.0 / (1.0 + jnp.exp(-x))


def _log_sigmoid(x):
    return jnp.minimum(x, 0.0) - jnp.log1p(jnp.exp(-jnp.abs(x)))


def _rms(x, g):
    ms = jnp.mean(x * x, axis=-1, keepdims=True)
    return x * lax.rsqrt(ms + EPS) * g


def _t5_bucket(dist):
    n = jnp.maximum(dist, 0)
    max_exact = N_BUCKETS // 2
    nf = jnp.maximum(n, 1).astype(F32)
    large = max_exact + jnp.floor(jnp.log(nf / max_exact) / math.log(MAX_DISTANCE / max_exact)
                                  * (N_BUCKETS - max_exact)).astype(jnp.int32)
    large = jnp.minimum(large, N_BUCKETS - 1)
    return jnp.where(n < max_exact, n, large)


def _bias_from_buckets(buckets, relb_ref, head):
    acc = jnp.zeros(buckets.shape, F32)
    for k in range(N_BUCKETS):
        acc = jnp.where(buckets == k, relb_ref[head, k], acc)
    return acc


def _layer_kernel(nj, x_ref, wt_hbm, wao_hbm, wmo_hbm, wout_hbm, wg_hbm, wu_hbm, wd_hbm,
                  gattn_ref, gffn_ref, gfin_ref, ghead_ref, bif_ref, fbuck_ref, relb_ref, sinks_ref,
                  xs_ref, zs_ref, yatts_ref, cqs_ref, scals_ref,
                  y_ref, kwin_ref, vwin_ref, c_ref, n_ref, m_ref, ys_ref,
                  z_scr, yatt_scr, ym_scr, a_scr, bias_scr, kprev_scr, vprev_scr, cbd_scr, st_scr,
                  wcat_ref, wao_ref, wmo_ref, wout_ref, wg_ref, wu_ref, wd_ref, dma_sem):
    T = T_BLK
    t = pl.program_id(0)
    n_prompt = pl.num_programs(0) - 1
    is_prompt = t < n_prompt
    j = lax.rem(t, nj)

    @pl.when(t == 0)
    def _first_step():
        def build_bias():
            fb = fbuck_ref[...]
            for h in range(N_Q_HEADS):
                bias_scr[h] = _bias_from_buckets(fb, relb_ref, h)

        chunks = _weight_chunks(wt_hbm, wcat_ref, ((wao_hbm, wao_ref), (wmo_hbm, wmo_ref),
                                                   (wout_hbm, wout_ref), (wg_hbm, wg_ref),
                                                   (wu_hbm, wu_ref), (wd_hbm, wd_ref)))
        _cast_weights(chunks, z_scr, dma_sem, while_first_copy=build_bias)

    @pl.when(jnp.logical_and(is_prompt, j == 0))
    def _reset_state():
        kprev_scr[...] = jnp.zeros_like(kprev_scr)
        vprev_scr[...] = jnp.zeros_like(vprev_scr)
        cbd_scr[...] = jnp.zeros_like(cbd_scr)
        st_scr[...] = jnp.zeros_like(st_scr)

    @pl.when(is_prompt)
    def _prompt_step():
        for s in range(N_SUB):
            _in_projection(x_ref.at[s * T:(s + 1) * T, :], gattn_ref, wcat_ref, z_scr.at[s])
        carry = _load_carry(kprev_scr, vprev_scr, cbd_scr, st_scr)
        for s in range(N_SUB):
            carry = _mixers(z_scr.at[s], yatt_scr.at[s], ym_scr.at[s], carry, ghead_ref, bif_ref,
                            bias_scr, sinks_ref, first_of_sequence=(j == 0) if s == 0 else None)
        _store_carry(carry, kprev_scr, vprev_scr, cbd_scr, st_scr)
        for s in range(N_SUB):
            _merge(x_ref.at[s * T:(s + 1) * T, :], z_scr.at[s], yatt_scr.at[s], ym_scr.at[s],
                   wao_ref, wmo_ref, wout_ref, gffn_ref, y_ref.at[s * T:(s + 1) * T, :])
        for s in range(N_SUB):
            _ffn(yatt_scr.at[s], ym_scr.at[s], a_scr, wg_ref, wu_ref, wd_ref, gfin_ref,
                 y_ref.at[s * T:(s + 1) * T, :])

    @pl.when(jnp.logical_not(is_prompt))
    def _sample_step():
        nb = xs_ref.shape[0]
        scal = scals_ref[...]
        yatt_scr[0, 0:nb, :] = yatts_ref[...].astype(BF16)
        for hd in range(M_HEADS):
            inter = scal[:, SC_INTER + hd:SC_INTER + hd + 1]
            w = scal[:, SC_W + hd:SC_W + hd + 1]
            scale = scal[:, SC_SCALE + hd:SC_SCALE + hd + 1]
            hv = (inter * cqs_ref[:, 128 * hd:128 * hd + 128]
                  + w * zs_ref[:, VM + 128 * hd:VM + 128 * hd + 128]) * scale
            hn = hv * lax.rsqrt(jnp.mean(hv * hv, axis=-1, keepdims=True) + EPS)
            hn = hn * ghead_ref[:, 128 * hd:128 * hd + 128]
            om = zs_ref[:, OM + 128 * hd:OM + 128 * hd + 128]
            ym_scr[0, 0:nb, 128 * hd:128 * hd + 128] = (hn * _sigmoid(om)).astype(BF16)
        _merge(xs_ref, zs_ref, yatt_scr.at[0, 0:nb, :], ym_scr.at[0, 0:nb, :],
               wao_ref, wmo_ref, wout_ref, gffn_ref, ys_ref)
        _ffn(yatt_scr.at[0, 0:nb, :], ym_scr.at[0, 0:nb, :], a_scr.at[0:nb, :], wg_ref, wu_ref, wd_ref,
             gfin_ref, ys_ref)

    @pl.when(jnp.logical_and(is_prompt, j == nj - 1))
    def _write_state():
        kwin_ref[...] = kprev_scr[...].T
        vwin_ref[...] = vprev_scr[...].T
        stn = st_scr[...]
        cts = [cbd_scr[0].T, cbd_scr[1].T]
        for hd in range(M_HEADS):
            p, hh = hd // 2, hd % 2
            c_ref[hd] = cts[p][64 * hh:64 * hh + 64, 128 * hh:128 * hh + 128]
            n_ref[hd:hd + 1, :] = stn[p:p + 1, 64 * hh:64 * hh + 64]
            m_ref[0:1, hd:hd + 1] = stn[2 + hd:3 + hd, 0:1]


def _in_projection(x_ref, gattn_ref, wcat_ref, z_ref):
    h = _rms(x_ref[...], gattn_ref[...]).astype(BF16)
    for c0, c1 in Z_GROUPS:
        z_ref[:, c0:c1] = _dot(h, wcat_ref[:, c0:c1])


def _weight_chunks(wt_hbm, wcat_ref, plain):
    r = T_BLK
    chunks = [(wt_hbm.at[a:a + r, :], wcat_ref.at[:, a:a + r], True) for a in range(0, W_IF, r)]
    chunks.append((wt_hbm.at[W_IF:W_GA, :], wcat_ref.at[:, IF:IF + LANES], True))
    chunks += [(wt_hbm.at[a:a + r, :], wcat_ref.at[:, GA + a - W_GA:GA + a - W_GA + r], True)
               for a in range(W_GA, W_END, r)]
    for src, dst in plain:
        chunks += [(src.at[a:a + r, :], dst.at[a:a + r, :], False) for a in range(0, src.shape[0], r)]
    return chunks


def _cast_weights(chunks, stage_ref, sem, while_first_copy):
    def copy(i):
        src = chunks[i][0]
        n, c = src.shape
        return pltpu.make_async_copy(src, stage_ref.at[i % 2, 0:n, 0:c], sem.at[i % 2])

    copy(0).start()
    for i, (src, dst, transpose) in enumerate(chunks):
        if i + 1 < len(chunks):
            copy(i + 1).start()
        if i == 0:
            while_first_copy()
        copy(i).wait()
        n, c = src.shape
        val = stage_ref[i % 2, 0:n, 0:c]
        rows = dst.shape[1] if transpose else dst.shape[0]
        if rows != n:
            val = jnp.concatenate([val, jnp.zeros((rows - n, c), F32)], axis=0)
        dst[...] = (val.T if transpose else val).astype(BF16)


def _load_carry(kprev_scr, vprev_scr, cbd_scr, st_scr):
    st = st_scr[...]
    return dict(kp=kprev_scr[...], vp=vprev_scr[...], cbd=[cbd_scr[0], cbd_scr[1]],
                n=[st[0:1, :], st[1:2, :]], m=[st[2 + hd:3 + hd, 0:1] for hd in range(M_HEADS)])


def _store_carry(carry, kprev_scr, vprev_scr, cbd_scr, st_scr):
    kprev_scr[...] = carry["kp"]
    vprev_scr[...] = carry["vp"]
    for p in range(2):
        cbd_scr[p] = carry["cbd"][p]
        st_scr[p:p + 1, :] = carry["n"][p]
    for hd in range(M_HEADS):
        st_scr[2 + hd:3 + hd, :] = jnp.broadcast_to(carry["m"][hd], (1, LANES))


def _merge(x_ref, z_ref, yatt_ref, ym_ref, wao_ref, wmo_ref, wout_ref, gffn_ref, y_ref):
    half = yatt_ref.shape[1]
    ya = _dot(yatt_ref[...], wao_ref[...])
    ymm = _dot(ym_ref[...], wmo_ref[...])
    mixed = _sigmoid(z_ref[:, GA:GA + 1024]) * ya + _sigmoid(z_ref[:, GM:GM + 1024]) * ymm
    x1 = x_ref[...] + _dot(mixed.astype(BF16), wout_ref[...])
    y_ref[...] = x1
    h2 = _rms(x1, gffn_ref[...]).astype(BF16)
    yatt_ref[...] = h2[:, :half]
    ym_ref[...] = h2[:, half:]


def _ffn(h2a_ref, h2b_ref, a_ref, wg_ref, wu_ref, wd_ref, gfin_ref, y_ref):
    d_ff = wg_ref.shape[1]
    h2 = jnp.concatenate([h2a_ref[...], h2b_ref[...]], axis=1)
    for c0 in range(0, d_ff, 256):
        g = _dot(h2, wg_ref[:, c0:c0 + 256])
        u = _dot(h2, wu_ref[:, c0:c0 + 256])
        a_ref[:, c0:c0 + 256] = (g * _sigmoid(g) * u).astype(BF16)
    x2 = y_ref[...] + _dot(a_ref[...], wd_ref[...])
    y_ref[...] = _rms(x2, gfin_ref[...])


def _mixers(z_ref, yatt_ref, ym_ref, carry, ghead_ref, bif_ref, bias_scr, sinks_ref, first_of_sequence):
    T = T_BLK
    lane = lax.broadcasted_iota(jnp.int32, (T, LANES), 1)
    lane_lo = lane < HEAD_DIM
    k_all = z_ref[:, KA:KA + 128]
    v_all = z_ref[:, VA:VA + 128]
    jk = lax.broadcasted_iota(jnp.int32, (128, T), 0)
    iq = lax.broadcasted_iota(jnp.int32, (128, T), 1)
    iq = jnp.where(iq < 128, iq, iq - 128)
    tri_t = jk <= iq
    if first_of_sequence is not None:
        valid_t = jk <= iq + jnp.where(first_of_sequence, 0, 2 * LANES)
    lane_q = lax.broadcasted_iota(jnp.int32, (1, T), 1)
    feat_lo = lax.broadcasted_iota(jnp.int32, (128, T), 0) < HEAD_DIM
    vt_all = v_all.T
    for sb in range(2):
        r0 = 128 * sb
        if sb == 0:
            kp, vpt = carry["kp"], carry["vp"].T
        else:
            kp, vpt = k_all[0:128], vt_all[:, 0:128]
        kcat = jnp.concatenate([kp, k_all[r0:r0 + 128]], axis=0)
        kroll = pltpu.roll(kcat, HEAD_DIM, 1)
        zero = jnp.zeros_like(kcat)
        k_mats = [jnp.where(lane_lo, kcat, zero), jnp.where(lane_lo, zero, kroll),
                  jnp.where(lane_lo, kroll, zero), jnp.where(lane_lo, zero, kcat)]
        vcat_t = jnp.concatenate([vpt, vt_all[:, r0:r0 + 128]], axis=1)
        vroll_t = pltpu.roll(vcat_t, HEAD_DIM, 0)
        zero_t = jnp.zeros_like(vcat_t)
        v_mats_t = [jnp.where(feat_lo, vcat_t, zero_t), jnp.where(feat_lo, zero_t, vroll_t),
                    jnp.where(feat_lo, vroll_t, zero_t), jnp.where(feat_lo, zero_t, vcat_t)]
        q = (z_ref[r0:r0 + 128, QA:QA + 512] * (HEAD_DIM ** -0.5)).astype(BF16)
        lhs_a = jnp.concatenate([q[:, 0:128], q[:, 128:256]], axis=0)
        lhs_b = jnp.concatenate([q[:, 256:384], q[:, 384:512]], axis=0)
        groups = [(lhs_a, 0, 0, 2), (lhs_a, 1, 1, 3), (lhs_b, 2, 4, 6), (lhs_b, 3, 5, 7)]
        pts = []
        for lhs, mi, ha, hb in groups:
            st = _dot_nt(k_mats[mi].astype(BF16), lhs)
            sf = jnp.where(tri_t, st[128:256, :], st[0:128, :])
            sf = sf + jnp.concatenate([bias_scr[ha], bias_scr[hb]], axis=1)
            if sb == 0 and first_of_sequence is not None:
                sf = jnp.where(valid_t, sf, NEG)
            sink = jnp.where(lane_q < 128, sinks_ref[ha], sinks_ref[hb])
            mx = jnp.maximum(jnp.max(sf, axis=0, keepdims=True), sink)
            p = jnp.exp(sf - mx)
            den = jnp.sum(p, axis=0, keepdims=True) + jnp.exp(sink - mx)
            pn = p * (1.0 / den)
            zp = jnp.zeros_like(pn)
            pts.append(jnp.concatenate([jnp.where(tri_t, zp, pn), jnp.where(tri_t, pn, zp)],
                                       axis=0).astype(BF16))
        ot_a = _dot(jnp.concatenate([v_mats_t[0], v_mats_t[1]], axis=1).astype(BF16),
                    jnp.concatenate([pts[0], pts[1]], axis=0))
        ot_b = _dot(jnp.concatenate([v_mats_t[2], v_mats_t[3]], axis=1).astype(BF16),
                    jnp.concatenate([pts[2], pts[3]], axis=0))
        yatt_ref[r0:r0 + 128, 0:128] = ot_a[:, 0:128].T.astype(BF16)
        yatt_ref[r0:r0 + 128, 128:256] = ot_a[:, 128:256].T.astype(BF16)
        yatt_ref[r0:r0 + 128, 256:384] = ot_b[:, 0:128].T.astype(BF16)
        yatt_ref[r0:r0 + 128, 384:512] = ot_b[:, 128:256].T.astype(BF16)
    new_carry = dict(kp=k_all[128:256], vp=v_all[128:256], cbd=[None, None], n=[None, None],
                     m=[None] * M_HEADS)

    zif = z_ref[:, IF:IF + 128] + bif_ref[...]
    gl = jnp.where(lane < M_HEADS, zif, _log_sigmoid(zif))
    gl_t = gl.T
    tr = lax.broadcasted_iota(jnp.int32, (T, T), 0)
    ts = lax.broadcasted_iota(jnp.int32, (T, T), 1)
    tril = ts <= tr
    triu = tr <= ts
    row2 = lax.broadcasted_iota(jnp.int32, (2 * LANES, LANES), 0)
    lane2 = lax.broadcasted_iota(jnp.int32, (2 * LANES, LANES), 1)
    bd_mask = (row2 < LANES) == (lane2 < HEAD_DIM)
    for p in range(2):
        q_pair = z_ref[:, QM + 128 * p:QM + 128 * p + 128]
        k_pair = z_ref[:, KM + 128 * p:KM + 128 * p + 128] * (HEAD_DIM ** -0.5)
        v_pair = z_ref[:, VM + 256 * p:VM + 256 * p + 256]
        cbd = carry["cbd"][p]
        n_pair = carry["n"][p]
        q_bf = q_pair.astype(BF16)
        qc = _dot_nt(q_bf, cbd.astype(BF16))
        qn_prod = q_pair * n_pair
        ws, decays, m_ends = [], [], []
        for hh in range(2):
            hd = 2 * p + hh
            hmask = lane_lo if hh == 0 else jnp.logical_not(lane_lo)
            ig_c = gl[:, hd:hd + 1]
            lf_c = gl[:, M_HEADS + hd:M_HEADS + hd + 1]
            ig_r = gl_t[hd:hd + 1, :]
            lf_r = gl_t[M_HEADS + hd:M_HEADS + hd + 1, :]
            m_prev = carry["m"][hd]
            b_c = jnp.sum(jnp.where(tril, lf_r, 0.0), axis=1, keepdims=True)
            b_r = jnp.sum(jnp.where(triu, lf_c, 0.0), axis=0, keepdims=True)
            a_r = ig_r - b_r
            cm_c = jnp.max(jnp.where(tril, a_r, NEG), axis=1, keepdims=True)
            mt_c = b_c + jnp.maximum(m_prev, cm_c)
            g_c = b_c - mt_c
            dm = jnp.exp(jnp.where(tril, a_r + g_c, NEG))
            inter_c = jnp.exp(m_prev + g_c)
            k_h = jnp.where(hmask, k_pair, 0.0).astype(BF16)
            w = dm * _dot_nt(q_bf, k_h)
            v_h = v_pair[:, 128 * hh:128 * hh + 128]
            num = inter_c * qc[:, 128 * hh:128 * hh + 128] + _dot(w.astype(BF16), v_h.astype(BF16))
            qn = jnp.sum(jnp.where(hmask, qn_prod, 0.0), axis=1, keepdims=True)
            den = inter_c * qn + jnp.sum(w, axis=1, keepdims=True)
            hv = num / jnp.maximum(jnp.abs(den), jnp.exp(-mt_c))
            hn = hv * lax.rsqrt(jnp.mean(hv * hv, axis=-1, keepdims=True) + EPS)
            hn = hn * ghead_ref[:, 128 * hd:128 * hd + 128]
            om = z_ref[:, OM + 128 * hd:OM + 128 * hd + 128]
            ym_ref[:, 128 * hd:128 * hd + 128] = (hn * _sigmoid(om)).astype(BF16)
            b_end = b_c[T - 1:T, :]
            m_end = mt_c[T - 1:T, :]
            ws.append(jnp.exp((ig_c - b_c) + b_end - m_end))
            decays.append(jnp.exp(m_prev + b_end - m_end))
            m_ends.append(m_end)
        kw = k_pair * jnp.where(lane_lo, ws[0], ws[1])
        upd = _dot(v_pair.T.astype(BF16), kw.astype(BF16))
        dec_rows = jnp.where(row2[:, 0:1] < LANES, decays[0], decays[1])
        new_carry["cbd"][p] = dec_rows * cbd + jnp.where(bd_mask, upd, 0.0)
        dec_lanes = jnp.where(lane_lo[0:1, :], decays[0], decays[1])
        new_carry["n"][p] = dec_lanes * n_pair + jnp.sum(kw, axis=0, keepdims=True)
        for hh in range(2):
            new_carry["m"][2 * p + hh] = m_ends[hh]
    return new_carry


def _resident(shape):
    zeros = (0,) * len(shape)
    return pl.BlockSpec(shape, lambda t: zeros, pipeline_mode=pl.Buffered(1))


def _smem():
    return pl.BlockSpec(memory_space=pltpu.SMEM)


def _layer_call(x, wt, wao, wmo, wout, wg, wu, wd, gattn, gffn, gfin, ghead, bif, fbuck, relb, sinks,
                xs3, zs, yatts, cqs, scals):
    bsz, s_len, d = x.shape
    nb = xs3.shape[0]
    t_step = T_BLK * N_SUB
    assert s_len % t_step == 0 and d == 1024 and nb <= T_BLK
    assert wt.shape == (W_END, d) and N_SUB >= 2
    d_ff = wg.shape[1]
    assert all(w.shape[0] % T_BLK == 0 and w.shape[1] <= N_CAT for w in (wao, wmo, wout, wg, wu, wd))
    hbm = pl.BlockSpec(memory_space=pl.ANY)
    nj = s_len // t_step
    n_prompt = bsz * nj
    out_shapes = (
        jax.ShapeDtypeStruct((bsz, s_len, d), F32),
        jax.ShapeDtypeStruct((bsz, WINDOW, 128), F32),
        jax.ShapeDtypeStruct((bsz, WINDOW, 128), F32),
        jax.ShapeDtypeStruct((bsz, M_HEADS, 64, 128), F32),
        jax.ShapeDtypeStruct((bsz, M_HEADS, 64), F32),
        jax.ShapeDtypeStruct((bsz, 1, M_HEADS), F32),
        jax.ShapeDtypeStruct((nb, 1, d), F32),
    )

    def seq(t):
        return jnp.minimum(t, n_prompt - 1) // nj

    def blk(t):
        return jnp.minimum(t, n_prompt - 1) % nj

    in_specs = [
        pl.BlockSpec((None, t_step, d), lambda t: (seq(t), blk(t), 0)),
        hbm, hbm, hbm, hbm, hbm, hbm, hbm,
        _resident(gattn.shape), _resident(gffn.shape), _resident(gfin.shape), _resident(ghead.shape),
        _resident(bif.shape), _resident(fbuck.shape), _smem(), _smem(),
        pl.BlockSpec((nb, None, d), lambda t: (0, 0, 0), pipeline_mode=pl.Buffered(1)),
        _resident(zs.shape), _resident(yatts.shape), _resident(cqs.shape), _resident(scals.shape),
    ]
    out_specs = (
        pl.BlockSpec((None, t_step, d), lambda t: (seq(t), blk(t), 0)),
        pl.BlockSpec((None, WINDOW, 128), lambda t: (seq(t), 0, 0)),
        pl.BlockSpec((None, WINDOW, 128), lambda t: (seq(t), 0, 0)),
        pl.BlockSpec((None, M_HEADS, 64, 128), lambda t: (seq(t), 0, 0, 0)),
        pl.BlockSpec((None, M_HEADS, 64), lambda t: (seq(t), 0, 0)),
        pl.BlockSpec((None, 1, M_HEADS), lambda t: (seq(t), 0, 0)),
        pl.BlockSpec((nb, None, d), lambda t: (0, 0, 0)),
    )
    scratch = [
        pltpu.VMEM((N_SUB, T_BLK, N_CAT), F32),
        pltpu.VMEM((N_SUB, T_BLK, 512), BF16),
        pltpu.VMEM((N_SUB, T_BLK, 512), BF16),
        pltpu.VMEM((T_BLK, d_ff), BF16),
        pltpu.VMEM((N_Q_HEADS, 128, 128), F32),
        pltpu.VMEM((128, 128), F32),
        pltpu.VMEM((128, 128), F32),
        pltpu.VMEM((2, 256, 128), F32),
        pltpu.VMEM((8, 128), F32),
        pltpu.VMEM((d, N_CAT), BF16),
        pltpu.VMEM(wao.shape, BF16), pltpu.VMEM(wmo.shape, BF16), pltpu.VMEM(wout.shape, BF16),
        pltpu.VMEM(wg.shape, BF16), pltpu.VMEM(wu.shape, BF16), pltpu.VMEM(wd.shape, BF16),
        pltpu.SemaphoreType.DMA((2,)),
    ]
    return pl.pallas_call(
        functools.partial(_layer_kernel, nj),
        grid=(n_prompt + 1,),
        in_specs=in_specs,
        out_specs=out_specs,
        out_shape=out_shapes,
        scratch_shapes=scratch,
        compiler_params=pltpu.CompilerParams(
            dimension_semantics=("arbitrary",),
            vmem_limit_bytes=VMEM_LIMIT_BYTES),
        name="layer",
    )(x, wt, wao, wmo, wout, wg, wu, wd, gattn, gffn, gfin, ghead, bif, fbuck, relb, sinks,
      xs3, zs, yatts, cqs, scals)


W_QA, W_KA, W_VA, W_QM, W_KM, W_VM, W_OM, W_IF, W_GA, W_GM, W_END = (
    0, 512, 640, 768, 1024, 1280, 1792, 2304, 2312, 3336, 4360)
ZT_KA, ZT_VA, ZT_QM, ZT_KM, ZT_IF, ZT_ROWS = 0, 128, 256, 512, 768, 776


def _sample_inproj_kernel(x_ref, g_ref, wt_hbm, nt_ref, mt_ref, bif_ref,
                          z_ref, zt_ref, no_ref, mo_ref, scal_ref,
                          wt_ref, sems):
    _sample_projection(x_ref, g_ref, wt_hbm, wt_ref, sems, z_ref, zt_ref)
    r8c = lax.broadcasted_iota(jnp.int32, (8, 1), 0)
    bias_col = jnp.zeros((8, 1), F32)
    for r in range(8):
        bias_col = jnp.where(r8c == r, bif_ref[r // M_HEADS, r % M_HEADS], bias_col)
    ift = zt_ref[ZT_IF:ZT_ROWS, :] + bias_col
    m0 = mt_ref[...]
    r16 = lax.broadcasted_iota(jnp.int32, (16, LANES), 0)
    table = jnp.zeros((16, LANES), F32)
    for hd in range(M_HEADS):
        ig = ift[hd:hd + 1, :]
        lf = _log_sigmoid(ift[M_HEADS + hd:M_HEADS + hd + 1, :])
        m_prev = m0[hd:hd + 1, :]
        a = ig - lf
        m_t = lf + jnp.maximum(m_prev, a)
        dgate = jnp.exp(a + lf - m_t)
        inter = jnp.exp(m_prev + lf - m_t)
        qt = zt_ref[ZT_QM + 64 * hd:ZT_QM + 64 * hd + 64, :]
        kt = zt_ref[ZT_KM + 64 * hd:ZT_KM + 64 * hd + 64, :] * (HEAD_DIM ** -0.5)
        nt = nt_ref[hd]
        no_ref[hd] = inter * nt + dgate * kt
        mo_ref[hd:hd + 1, :] = m_t
        qk = jnp.sum(qt * kt, axis=0, keepdims=True)
        nq = jnp.sum(nt * qt, axis=0, keepdims=True)
        w = dgate * qk
        den = inter * nq + w
        scale = 1.0 / jnp.maximum(jnp.abs(den), jnp.exp(-m_t))
        for base, val in ((SC_INTER, inter), (SC_W, w), (SC_SCALE, scale), (SC_D, dgate)):
            table = jnp.where(r16 == base + hd, val, table)
    full = jnp.concatenate([table, jnp.zeros((LANES - 16, LANES), F32)], axis=0)
    scal_ref[...] = full.T


WT_CHUNKS = ((W_QA, 768), (768, 1536), (1536, W_IF), (W_IF, W_GA), (W_GA, W_GM), (W_GM, W_END))


def _sample_projection(x_ref, g_ref, wt_hbm, wt_ref, sems, z_ref, zt_ref):
    copies = [pltpu.make_async_copy(wt_hbm.at[a:b, :], wt_ref.at[a:b, :], sems.at[k])
              for k, (a, b) in enumerate(WT_CHUNKS)]
    in_flight = 2

    def wait(k):
        copies[k].wait()
        if k + in_flight < len(copies):
            copies[k + in_flight].start()

    for c in copies[:in_flight]:
        c.start()
    h32 = _rms(x_ref[...], g_ref[...])
    h = h32.astype(BF16)
    ht = h32.T.astype(BF16)
    wait(0)
    z_ref[:, QA:768] = _dot_nt(h, wt_ref[W_QA:768, :].astype(BF16))
    zt_ref[ZT_KA:ZT_QM, :] = _dot(wt_ref[W_KA:W_QM, :].astype(BF16), ht)
    wait(1)
    z_ref[:, 768:1536] = _dot_nt(h, wt_ref[768:1536, :].astype(BF16))
    zt_ref[ZT_QM:ZT_IF, :] = _dot(wt_ref[W_QM:W_VM, :].astype(BF16), ht)
    wait(2)
    z_ref[:, 1536:W_IF] = _dot_nt(h, wt_ref[1536:W_IF, :].astype(BF16))
    wait(3)
    w_if = jnp.concatenate([wt_ref[W_IF:W_GA, :], jnp.zeros((LANES - 8, wt_ref.shape[1]), F32)],
                           axis=0).astype(BF16)
    z_ref[:, IF:IF + LANES] = _dot_nt(h, w_if)
    zt_ref[ZT_IF:ZT_ROWS, :] = _dot(w_if, ht)[0:8, :]
    wait(4)
    z_ref[:, GA:GM] = _dot_nt(h, wt_ref[W_GA:W_GM, :].astype(BF16))
    wait(5)
    z_ref[:, GM:N_CAT] = _dot_nt(h, wt_ref[W_GM:W_END, :].astype(BF16))


def _sample_inproj_call(x3, gattn, wt, nt, mt, bif2):
    n, _, d = x3.shape
    assert n == LANES
    full = lambda s: pl.BlockSpec(tuple(s), lambda i: (0,) * len(s))
    return pl.pallas_call(
        _sample_inproj_kernel,
        grid=(1,),
        in_specs=[pl.BlockSpec((n, None, d), lambda i: (0, 0, 0)),
                  pl.BlockSpec((1, d), lambda i: (0, 0)),
                  pl.BlockSpec(memory_space=pl.ANY),
                  full(nt.shape), full(mt.shape), _smem()],
        out_specs=(full((n, N_CAT)), full((ZT_ROWS, n)), full(nt.shape), full(mt.shape),
                   full((n, LANES))),
        out_shape=(jax.ShapeDtypeStruct((n, N_CAT), F32), jax.ShapeDtypeStruct((ZT_ROWS, n), F32),
                   jax.ShapeDtypeStruct(nt.shape, F32), jax.ShapeDtypeStruct(mt.shape, F32),
                   jax.ShapeDtypeStruct((n, LANES), F32)),
        scratch_shapes=[pltpu.VMEM(wt.shape, F32), pltpu.SemaphoreType.DMA((len(WT_CHUNKS),))],
        compiler_params=pltpu.CompilerParams(dimension_semantics=("arbitrary",),
                                             vmem_limit_bytes=VMEM_LIMIT_BYTES),
        name="sample_inproj",
    )(x3, gattn, wt, nt, mt, bif2)


SK2_BB = 16
SC_INTER, SC_W, SC_SCALE, SC_D, SC_COLS = 0, 4, 8, 12, 16
HEAD_ROW_ORDER = (0, 2, 4, 6, 1, 3, 5, 7)


def _sample_mixer_kernel(z_ref, zt_ref, kt_ref, vt_ref, ct_ref, sbuck_ref,
                         scal_ref, relb_ref, sinks_ref,
                         yatt_ref, cq_ref, ko_ref, vo_ref, co_ref,
                         sbias_scr):
    i = pl.program_id(0)
    r8 = lax.broadcasted_iota(jnp.int32, (8, LANES), 0)
    l8 = lax.broadcasted_iota(jnp.int32, (8, LANES), 1)
    r8c = lax.broadcasted_iota(jnp.int32, (8, 1), 0)

    @pl.when(i == 0)
    def _prologue():
        sb = sbuck_ref[...]
        acc = jnp.zeros((8, LANES), F32)
        for rrow, hd in enumerate(HEAD_ROW_ORDER):
            acc = jnp.where(r8 == rrow, _bias_from_buckets(sb, relb_ref, hd), acc)
        sbias_scr[...] = acc

    sink = jnp.zeros((8, 1), F32)
    for rrow, hd in enumerate(HEAD_ROW_ORDER):
        sink = jnp.where(r8c == rrow, sinks_ref[hd], sink)
    sbias = sbias_scr[...]
    lane_w = lax.broadcasted_iota(jnp.int32, (WINDOW, LANES), 1)
    last_lane = lane_w == WINDOW - 1
    r256 = lax.broadcasted_iota(jnp.int32, (8, 2 * LANES), 0)
    l256 = lax.broadcasted_iota(jnp.int32, (8, 2 * LANES), 1)
    own64 = (l256 // HEAD_DIM) == r256
    l512r = lax.broadcasted_iota(jnp.int32, (1, 4 * LANES), 1) // LANES
    lo1 = lax.broadcasted_iota(jnp.int32, (1, LANES), 1) < HEAD_DIM
    row_b = lax.broadcasted_iota(jnp.int32, (LANES, 2 * LANES), 0)
    ktm = [(zt_ref[ZT_KM + 128 * pr:ZT_KM + 128 * pr + 128, :] * (HEAD_DIM ** -0.5)).astype(BF16)
           for pr in range(2)]

    nrow = 8 * SK2_BB
    lst = lax.broadcasted_iota(jnp.int32, (nrow, LANES), 1)
    z1 = jnp.zeros((1, LANES), F32)
    qm_l, so_l, kn_l, vn_l = [], [], [], []
    for bi in range(SK2_BB):
        qp = [z_ref[bi:bi + 1, QA + 128 * p:QA + 128 * p + 128] * (HEAD_DIM ** -0.5) for p in range(4)]
        qpr = [pltpu.roll(x, HEAD_DIM, 1) for x in qp]
        rows = [jnp.where(lo1, qp[0], z1), jnp.where(lo1, qp[1], z1),
                jnp.where(lo1, z1, qpr[2]), jnp.where(lo1, z1, qpr[3]),
                jnp.where(lo1, qpr[0], z1), jnp.where(lo1, qpr[1], z1),
                jnp.where(lo1, z1, qp[2]), jnp.where(lo1, z1, qp[3])]
        qm = jnp.zeros((8, LANES), F32)
        for r in range(8):
            qm = jnp.where(r8 == r, rows[r], qm)
        qm_l.append(qm)
        so_l.append(_dot(qm.astype(BF16), kt_ref[bi].astype(BF16)))
        kn_l.append(jnp.broadcast_to(z_ref[bi:bi + 1, KA:KA + 128], (8, LANES)))
        vn_l.append(jnp.broadcast_to(z_ref[bi:bi + 1, VA:VA + 128], (8, LANES)))
    qm_all = jnp.concatenate(qm_l, axis=0)
    s_old = jnp.concatenate(so_l, axis=0)
    vn_all = jnp.concatenate(vn_l, axis=0)
    s_new = jnp.sum(qm_all * jnp.concatenate(kn_l, axis=0), axis=1, keepdims=True)
    sbias_all = jnp.concatenate([sbias] * SK2_BB, axis=0)
    sink_all = jnp.concatenate([sink] * SK2_BB, axis=0)
    s = jnp.where(lst == WINDOW - 1, s_new, pltpu.roll(s_old, WINDOW - 1, 1)) + sbias_all
    mx = jnp.maximum(jnp.max(s, axis=-1, keepdims=True), sink_all)
    pe = jnp.exp(s - mx)
    den = jnp.sum(pe, axis=-1, keepdims=True) + jnp.exp(sink_all - mx)
    pn = pe * (1.0 / den)
    p_old = jnp.where(lst == 0, 0.0, pltpu.roll(pn, 1, 1))
    oa_l = [_dot_nt(p_old[8 * bi:8 * bi + 8].astype(BF16), vt_ref[bi].astype(BF16))
            for bi in range(SK2_BB)]
    oa = jnp.concatenate(oa_l, axis=0) + pn[:, WINDOW - 1:WINDOW] * vn_all
    oar = pltpu.roll(oa, HEAD_DIM, 1)
    for bi in range(SK2_BB):
        r0 = 8 * bi
        pairs = [jnp.where(lo1, oa[r0:r0 + 1], oar[r0 + 4:r0 + 5]),
                 jnp.where(lo1, oa[r0 + 1:r0 + 2], oar[r0 + 5:r0 + 6]),
                 jnp.where(lo1, oar[r0 + 2:r0 + 3], oa[r0 + 6:r0 + 7]),
                 jnp.where(lo1, oar[r0 + 3:r0 + 4], oa[r0 + 7:r0 + 8])]
        for p in range(4):
            yatt_ref[bi:bi + 1, 128 * p:128 * p + 128] = pairs[p]

    for bi in range(SK2_BB):
        b = i * SK2_BB + bi
        shift = WINDOW - 1 - b
        kcol = pltpu.roll(zt_ref[ZT_KA:ZT_KA + 128, :], shift, 1)
        vcol = pltpu.roll(zt_ref[ZT_VA:ZT_VA + 128, :], shift, 1)
        ko_ref[bi] = jnp.where(last_lane, kcol, pltpu.roll(kt_ref[bi], WINDOW - 1, 1))
        vo_ref[bi] = jnp.where(last_lane, vcol, pltpu.roll(vt_ref[bi], WINDOW - 1, 1))

        qm_row = z_ref[bi:bi + 1, QM:QM + 256]
        vm_row = z_ref[bi:bi + 1, VM:VM + 512]
        qbd = jnp.where(own64, qm_row, 0.0)
        ct_all = jnp.concatenate([ct_ref[bi, hd] for hd in range(M_HEADS)], axis=0)
        cq = _dot(qbd.astype(BF16), ct_all.astype(BF16))
        for hd in range(M_HEADS):
            cq_ref[bi:bi + 1, 128 * hd:128 * hd + 128] = cq[hd:hd + 1, :]
        dsel = jnp.zeros((1, 4 * LANES), F32)
        for hd in range(M_HEADS):
            dsel = jnp.where(l512r == hd, scal_ref[b, SC_D + hd], dsel)
        vs = vm_row * dsel
        for pr in range(2):
            vsel = jnp.where(row_b == b, vs[:, 256 * pr:256 * pr + 256], 0.0).astype(BF16)
            upd = _dot(ktm[pr], vsel)
            for hh in range(2):
                hd = 2 * pr + hh
                co_ref[bi, hd] = (scal_ref[b, SC_INTER + hd] * ct_ref[bi, hd]
                                  + upd[64 * hh:64 * hh + 64, 128 * hh:128 * hh + 128])


def _sample_mixer_call(z, zt, kt, vt, ct, sbuck, scal_small, relb_t, sinks):
    nb = z.shape[0]
    assert nb % SK2_BB == 0 and nb == LANES
    bb = SK2_BB
    out_shapes = (
        jax.ShapeDtypeStruct((nb, 512), F32),
        jax.ShapeDtypeStruct((nb, 512), F32),
        jax.ShapeDtypeStruct(kt.shape, F32),
        jax.ShapeDtypeStruct(vt.shape, F32),
        jax.ShapeDtypeStruct(ct.shape, F32),
    )
    blk = lambda s: pl.BlockSpec((bb,) + tuple(s[1:]), lambda i: (i,) + (0,) * (len(s) - 1))
    full = lambda s: pl.BlockSpec(tuple(s), lambda i: (0,) * len(s))
    in_specs = [blk(z.shape), full(zt.shape), blk(kt.shape), blk(vt.shape), blk(ct.shape),
                full(sbuck.shape), _smem(), _smem(), _smem()]
    out_specs = (blk((nb, 512)), blk((nb, 512)), blk(kt.shape), blk(vt.shape), blk(ct.shape))
    return pl.pallas_call(
        _sample_mixer_kernel,
        grid=(nb // bb,),
        in_specs=in_specs,
        out_specs=out_specs,
        out_shape=out_shapes,
        scratch_shapes=[pltpu.VMEM((8, LANES), F32)],
        compiler_params=pltpu.CompilerParams(dimension_semantics=("arbitrary",)),
        name="sample_mixers",
    )(z, zt, kt, vt, ct, sbuck, scal_small, relb_t, sinks)


def kernel(x_prompt, x_sample, cache_k_win, cache_v_win, state_mlstm_C, state_mlstm_n, state_mlstm_m,
           rel_bias, w_in, b_if, sinks, g_attn_norm, g_head, w_att_out, w_mlstm_out, w_out,
           g_ffn_norm, w_gate, w_up, w_down, g_final):
    depth = w_in.shape[0]
    assert depth == 1
    bsz, s_len, d = x_prompt.shape
    nb = x_sample.shape[0]
    assert x_sample.shape[1] == 1 and cache_k_win.shape[2] == WINDOW

    wt = w_in[0].T
    gattn = g_attn_norm[0].reshape(1, d)
    gffn = g_ffn_norm[0].reshape(1, d)
    gfin = g_final.reshape(1, d)
    ghead = g_head[0].reshape(1, 512)
    bif = jnp.concatenate([b_if[0].reshape(1, 2 * M_HEADS), jnp.zeros((1, LANES - 2 * M_HEADS), F32)], axis=1)
    sink_v = sinks[0]
    relb_t = rel_bias.T

    qi = jnp.arange(WINDOW)[:, None]
    kj = jnp.arange(WINDOW)[None, :]
    fbuck = _t5_bucket(jnp.where(kj <= qi, qi - kj, WINDOW + qi - kj)).astype(jnp.int32).T
    sbuck = jnp.broadcast_to(_t5_bucket(WINDOW - 1 - kj), (8, WINDOW)).astype(jnp.int32)

    feat = N_KV_HEADS * HEAD_DIM
    kt_in = jnp.transpose(cache_k_win[0], (0, 2, 3, 1)).reshape(nb, feat, WINDOW)
    vt_in = jnp.transpose(cache_v_win[0], (0, 2, 3, 1)).reshape(nb, feat, WINDOW)
    ct_in = jnp.transpose(state_mlstm_C[0], (0, 1, 3, 2))
    nt_in = jnp.transpose(state_mlstm_n[0], (1, 2, 0))
    mt_in = state_mlstm_m[0].T
    z_s, zt_s, nt_s, mt_s, scal_s = _sample_inproj_call(x_sample, gattn, wt, nt_in, mt_in, b_if[0])
    yatt_s, cq_s, kt_s, vt_s, ct_s = _sample_mixer_call(
        z_s, zt_s, kt_in, vt_in, ct_in, sbuck, scal_s[:, :SC_COLS], relb_t, sink_v)
    y_p, kt_p, vt_p, ct_p, n_p, m_p, y_s = _layer_call(
        x_prompt, wt, w_att_out[0], w_mlstm_out[0], w_out[0], w_gate[0], w_up[0], w_down[0],
        gattn, gffn, gfin, ghead, bif, fbuck, relb_t, sink_v,
        x_sample, z_s, yatt_s, cq_s, scal_s)

    def window_out(t, n):
        return jnp.transpose(t.reshape(n, N_KV_HEADS, HEAD_DIM, WINDOW), (0, 3, 1, 2))[None]

    return (y_p, y_s,
            window_out(kt_p, bsz), window_out(vt_p, bsz),
            jnp.transpose(ct_p, (0, 1, 3, 2))[None], n_p[None], m_p.reshape(1, bsz, M_HEADS),
            window_out(kt_s, nb), window_out(vt_s, nb),
            jnp.transpose(ct_s, (0, 1, 3, 2))[None], jnp.transpose(nt_s, (2, 0, 1))[None], mt_s.T[None])
```

```python
import functools
import math

import jax
import jax.numpy as jnp
from jax import lax
from jax.experimental import pallas as pl
from jax.experimental.pallas import tpu as pltpu

F32 = jnp.float32
BF16 = jnp.bfloat16

HEAD_DIM = 64
N_Q_HEADS = 8
N_KV_HEADS = 2
WINDOW = 128
N_BUCKETS = 32
MAX_DISTANCE = WINDOW
M_HEADS = 4
EPS = 1e-6
NEG = -1e30

LANES = 128
VMEM_LIMIT_BYTES = 61 * 1024 * 1024

QA, KA, VA, QM, KM, VM, OM, IF, GA, GM = 0, 512, 640, 768, 1024, 1280, 1792, 2304, 2432, 3456
N_CAT = 4480
Z_GROUPS = ((768, 2432), (0, 768), (2432, 4480))
T_BLK = 256
N_SUB = 2
STAGE_COLS = 1408
STAGE_SLOTS = N_SUB * (N_CAT // STAGE_COLS)


def _dot(a, b):
    return jnp.dot(a, b, preferred_element_type=F32)


def _dot_nt(a, b):
    return lax.dot_general(a, b, (((1,), (1,)), ((), ())), preferred_element_type=F32)


def _sigmoid(x):
    return 1.0 / (1.0 + jnp.exp(-x))


def _log_sigmoid(x):
    return jnp.minimum(x, 0.0) - jnp.log1p(jnp.exp(-jnp.abs(x)))


def _rms(x, g):
    ms = jnp.mean(x * x, axis=-1, keepdims=True)
    return x * lax.rsqrt(ms + EPS) * g


def _t5_bucket(dist):
    n = jnp.maximum(dist, 0)
    max_exact = N_BUCKETS // 2
    nf = jnp.maximum(n, 1).astype(F32)
    large = max_exact + jnp.floor(jnp.log(nf / max_exact) / math.log(MAX_DISTANCE / max_exact)
                                  * (N_BUCKETS - max_exact)).astype(jnp.int32)
    large = jnp.minimum(large, N_BUCKETS - 1)
    return jnp.where(n < max_exact, n, large)


def _bias_from_buckets(buckets, relb_ref, head):
    acc = jnp.zeros(buckets.shape, F32)
    for k in range(N_BUCKETS):
        acc = jnp.where(buckets == k, relb_ref[head, k], acc)
    return acc


def _layer_kernel(nj, x_ref, wt_hbm, wao_hbm, wmo_hbm, wout_hbm, wg_hbm, wu_hbm, wd_hbm,
                  gattn_ref, gffn_ref, gfin_ref, ghead_ref, bif_ref, fbuck_ref, relb_ref, sinks_ref,
                  xs_ref, zs_ref, yatts_ref, cqs_ref, scals_ref,
                  y_ref, kwin_ref, vwin_ref, c_ref, n_ref, m_ref, ys_ref,
                  z_scr, yatt_scr, ym_scr, a_scr, bias_scr, kprev_scr, vprev_scr, cbd_scr, st_scr,
                  wcat_ref, wao_ref, wmo_ref, wout_ref, wg_ref, wu_ref, wd_ref, dma_sem):
    T = T_BLK
    t = pl.program_id(0)
    n_prompt = pl.num_programs(0) - 1
    is_prompt = t < n_prompt
    j = lax.rem(t, nj)

    @pl.when(t == 0)
    def _first_step():
        def build_bias():
            fb = fbuck_ref[...]
            for h in range(N_Q_HEADS):
                bias_scr[h] = _bias_from_buckets(fb, relb_ref, h)

        chunks = _weight_chunks(wt_hbm, wcat_ref, ((wao_hbm, wao_ref), (wmo_hbm, wmo_ref),
                                                   (wout_hbm, wout_ref), (wg_hbm, wg_ref),
                                                   (wu_hbm, wu_ref), (wd_hbm, wd_ref)))
        _cast_weights(chunks, z_scr, dma_sem, while_first_copy=build_bias)

    @pl.when(jnp.logical_and(is_prompt, j == 0))
    def _reset_state():
        kprev_scr[...] = jnp.zeros_like(kprev_scr)
        vprev_scr[...] = jnp.zeros_like(vprev_scr)
        cbd_scr[...] = jnp.zeros_like(cbd_scr)
        st_scr[...] = jnp.zeros_like(st_scr)

    @pl.when(is_prompt)
    def _prompt_step():
        for s in range(N_SUB):
            _in_projection(x_ref.at[s * T:(s + 1) * T, :], gattn_ref, wcat_ref, z_scr.at[s])
        carry = _load_carry(kprev_scr, vprev_scr, cbd_scr, st_scr)
        for s in range(N_SUB):
            carry = _mixers(z_scr.at[s], yatt_scr.at[s], ym_scr.at[s], carry, ghead_ref, bif_ref,
                            bias_scr, sinks_ref, first_of_sequence=(j == 0) if s == 0 else None)
        _store_carry(carry, kprev_scr, vprev_scr, cbd_scr, st_scr)
        for s in range(N_SUB):
            _merge(x_ref.at[s * T:(s + 1) * T, :], z_scr.at[s], yatt_scr.at[s], ym_scr.at[s],
                   wao_ref, wmo_ref, wout_ref, gffn_ref, y_ref.at[s * T:(s + 1) * T, :])
        for s in range(N_SUB):
            _ffn(yatt_scr.at[s], ym_scr.at[s], a_scr, wg_ref, wu_ref, wd_ref, gfin_ref,
                 y_ref.at[s * T:(s + 1) * T, :])

    @pl.when(jnp.logical_not(is_prompt))
    def _sample_step():
        nb = xs_ref.shape[0]
        scal = scals_ref[...]
        yatt_scr[0, 0:nb, :] = yatts_ref[...].astype(BF16)
        for hd in range(M_HEADS):
            inter = scal[:, SC_INTER + hd:SC_INTER + hd + 1]
            w = scal[:, SC_W + hd:SC_W + hd + 1]
            scale = scal[:, SC_SCALE + hd:SC_SCALE + hd + 1]
            hv = (inter * cqs_ref[:, 128 * hd:128 * hd + 128]
                  + w * zs_ref[:, VM + 128 * hd:VM + 128 * hd + 128]) * scale
            hn = hv * lax.rsqrt(jnp.mean(hv * hv, axis=-1, keepdims=True) + EPS)
            hn = hn * ghead_ref[:, 128 * hd:128 * hd + 128]
            om = zs_ref[:, OM + 128 * hd:OM + 128 * hd + 128]
            ym_scr[0, 0:nb, 128 * hd:128 * hd + 128] = (hn * _sigmoid(om)).astype(BF16)
        _merge(xs_ref, zs_ref, yatt_scr.at[0, 0:nb, :], ym_scr.at[0, 0:nb, :],
               wao_ref, wmo_ref, wout_ref, gffn_ref, ys_ref)
        _ffn(yatt_scr.at[0, 0:nb, :], ym_scr.at[0, 0:nb, :], a_scr.at[0:nb, :], wg_ref, wu_ref, wd_ref,
             gfin_ref, ys_ref)

    @pl.when(jnp.logical_and(is_prompt, j == nj - 1))
    def _write_state():
        kwin_ref[...] = kprev_scr[...].T
        vwin_ref[...] = vprev_scr[...].T
        stn = st_scr[...]
        cts = [cbd_scr[0].T, cbd_scr[1].T]
        for hd in range(M_HEADS):
            p, hh = hd // 2, hd % 2
            c_ref[hd] = cts[p][64 * hh:64 * hh + 64, 128 * hh:128 * hh + 128]
            n_ref[hd:hd + 1, :] = stn[p:p + 1, 64 * hh:64 * hh + 64]
            m_ref[0:1, hd:hd + 1] = stn[2 + hd:3 + hd, 0:1]


def _in_projection(x_ref, gattn_ref, wcat_ref, z_ref):
    h = _rms(x_ref[...], gattn_ref[...]).astype(BF16)
    for c0, c1 in Z_GROUPS:
        z_ref[:, c0:c1] = _dot(h, wcat_ref[:, c0:c1])


def _weight_chunks(wt_hbm, wcat_ref, plain):
    r = T_BLK
    chunks = [(wt_hbm.at[a:a + r, :], wcat_ref.at[:, a:a + r], True) for a in range(0, W_IF, r)]
    chunks.append((wt_hbm.at[W_IF:W_GA, :], wcat_ref.at[:, IF:IF + LANES], True))
    chunks += [(wt_hbm.at[a:a + r, :], wcat_ref.at[:, GA + a - W_GA:GA + a - W_GA + r], True)
               for a in range(W_GA, W_END, r)]
    for src, dst in plain:
        cols = src.shape[1]
        widths = [cols] if cols <= STAGE_COLS else [STAGE_COLS] * (cols // STAGE_COLS)
        assert sum(widths) == cols
        chunks += [(src.at[a:a + r, c0:c0 + w], dst.at[a:a + r, c0:c0 + w], False)
                   for a in range(0, src.shape[0], r) for c0, w in zip(range(0, cols, STAGE_COLS), widths)]
    return chunks


def _cast_weights(chunks, stage_ref, sem, while_first_copy):
    def window(i, n, c):
        slot = i % STAGE_SLOTS
        off = (slot // N_SUB) * STAGE_COLS
        return stage_ref.at[slot % N_SUB, 0:n, off:off + c]

    def copy(i):
        src = chunks[i][0]
        return pltpu.make_async_copy(src, window(i, *src.shape), sem.at[i % STAGE_SLOTS])

    ahead = STAGE_SLOTS - 1
    for i in range(min(ahead, len(chunks))):
        copy(i).start()
    for i, (src, dst, transpose) in enumerate(chunks):
        if i + ahead < len(chunks):
            copy(i + ahead).start()
        if i == 0:
            while_first_copy()
        copy(i).wait()
        n, c = src.shape
        val = window(i, n, c)[...]
        rows = dst.shape[1] if transpose else dst.shape[0]
        if rows != n:
            val = jnp.concatenate([val, jnp.zeros((rows - n, c), F32)], axis=0)
        dst[...] = (val.T if transpose else val).astype(BF16)


def _load_carry(kprev_scr, vprev_scr, cbd_scr, st_scr):
    st = st_scr[...]
    return dict(kp=kprev_scr[...], vp=vprev_scr[...], cbd=[cbd_scr[0], cbd_scr[1]],
                n=[st[0:1, :], st[1:2, :]], m=[st[2 + hd:3 + hd, 0:1] for hd in range(M_HEADS)])


def _store_carry(carry, kprev_scr, vprev_scr, cbd_scr, st_scr):
    kprev_scr[...] = carry["kp"]
    vprev_scr[...] = carry["vp"]
    for p in range(2):
        cbd_scr[p] = carry["cbd"][p]
        st_scr[p:p + 1, :] = carry["n"][p]
    for hd in range(M_HEADS):
        st_scr[2 + hd:3 + hd, :] = jnp.broadcast_to(carry["m"][hd], (1, LANES))


def _merge(x_ref, z_ref, yatt_ref, ym_ref, wao_ref, wmo_ref, wout_ref, gffn_ref, y_ref):
    half = yatt_ref.shape[1]
    ya = _dot(yatt_ref[...], wao_ref[...])
    ymm = _dot(ym_ref[...], wmo_ref[...])
    mixed = _sigmoid(z_ref[:, GA:GA + 1024]) * ya + _sigmoid(z_ref[:, GM:GM + 1024]) * ymm
    x1 = x_ref[...] + _dot(mixed.astype(BF16), wout_ref[...])
    y_ref[...] = x1
    h2 = _rms(x1, gffn_ref[...]).astype(BF16)
    yatt_ref[...] = h2[:, :half]
    ym_ref[...] = h2[:, half:]


def _ffn(h2a_ref, h2b_ref, a_ref, wg_ref, wu_ref, wd_ref, gfin_ref, y_ref):
    d_ff = wg_ref.shape[1]
    h2 = jnp.concatenate([h2a_ref[...], h2b_ref[...]], axis=1)
    for c0 in range(0, d_ff, 256):
        g = _dot(h2, wg_ref[:, c0:c0 + 256])
        u = _dot(h2, wu_ref[:, c0:c0 + 256])
        a_ref[:, c0:c0 + 256] = (g * _sigmoid(g) * u).astype(BF16)
    x2 = y_ref[...] + _dot(a_ref[...], wd_ref[...])
    y_ref[...] = _rms(x2, gfin_ref[...])


def _mixers(z_ref, yatt_ref, ym_ref, carry, ghead_ref, bif_ref, bias_scr, sinks_ref, first_of_sequence):
    T = T_BLK
    lane = lax.broadcasted_iota(jnp.int32, (T, LANES), 1)
    lane_lo = lane < HEAD_DIM
    k_all = z_ref[:, KA:KA + 128]
    v_all = z_ref[:, VA:VA + 128]
    jk = lax.broadcasted_iota(jnp.int32, (128, T), 0)
    iq = lax.broadcasted_iota(jnp.int32, (128, T), 1)
    iq = jnp.where(iq < 128, iq, iq - 128)
    tri_t = jk <= iq
    if first_of_sequence is not None:
        valid_t = jk <= iq + jnp.where(first_of_sequence, 0, 2 * LANES)
    lane_q = lax.broadcasted_iota(jnp.int32, (1, T), 1)
    feat_lo = lax.broadcasted_iota(jnp.int32, (128, T), 0) < HEAD_DIM
    vt_all = v_all.T
    for sb in range(2):
        r0 = 128 * sb
        if sb == 0:
            kp, vpt = carry["kp"], carry["vp"].T
        else:
            kp, vpt = k_all[0:128], vt_all[:, 0:128]
        kcat = jnp.concatenate([kp, k_all[r0:r0 + 128]], axis=0)
        kroll = pltpu.roll(kcat, HEAD_DIM, 1)
        zero = jnp.zeros_like(kcat)
        k_mats = [jnp.where(lane_lo, kcat, zero), jnp.where(lane_lo, zero, kroll),
                  jnp.where(lane_lo, kroll, zero), jnp.where(lane_lo, zero, kcat)]
        vcat_t = jnp.concatenate([vpt, vt_all[:, r0:r0 + 128]], axis=1)
        vroll_t = pltpu.roll(vcat_t, HEAD_DIM, 0)
        zero_t = jnp.zeros_like(vcat_t)
        v_mats_t = [jnp.where(feat_lo, vcat_t, zero_t), jnp.where(feat_lo, zero_t, vroll_t),
                    jnp.where(feat_lo, vroll_t, zero_t), jnp.where(feat_lo, zero_t, vcat_t)]
        q = (z_ref[r0:r0 + 128, QA:QA + 512] * (HEAD_DIM ** -0.5)).astype(BF16)
        lhs_a = jnp.concatenate([q[:, 0:128], q[:, 128:256]], axis=0)
        lhs_b = jnp.concatenate([q[:, 256:384], q[:, 384:512]], axis=0)
        groups = [(lhs_a, 0, 0, 2), (lhs_a, 1, 1, 3), (lhs_b, 2, 4, 6), (lhs_b, 3, 5, 7)]
        pts = []
        for lhs, mi, ha, hb in groups:
            st = _dot_nt(k_mats[mi].astype(BF16), lhs)
            sf = jnp.where(tri_t, st[128:256, :], st[0:128, :])
            sf = sf + jnp.concatenate([bias_scr[ha], bias_scr[hb]], axis=1)
            if sb == 0 and first_of_sequence is not None:
                sf = jnp.where(valid_t, sf, NEG)
            sink = jnp.where(lane_q < 128, sinks_ref[ha], sinks_ref[hb])
            mx = jnp.maximum(jnp.max(sf, axis=0, keepdims=True), sink)
            p = jnp.exp(sf - mx)
            den = jnp.sum(p, axis=0, keepdims=True) + jnp.exp(sink - mx)
            pn = p * (1.0 / den)
            zp = jnp.zeros_like(pn)
            pts.append(jnp.concatenate([jnp.where(tri_t, zp, pn), jnp.where(tri_t, pn, zp)],
                                       axis=0).astype(BF16))
        ot_a = _dot(jnp.concatenate([v_mats_t[0], v_mats_t[1]], axis=1).astype(BF16),
                    jnp.concatenate([pts[0], pts[1]], axis=0))
        ot_b = _dot(jnp.concatenate([v_mats_t[2], v_mats_t[3]], axis=1).astype(BF16),
                    jnp.concatenate([pts[2], pts[3]], axis=0))
        yatt_ref[r0:r0 + 128, 0:128] = ot_a[:, 0:128].T.astype(BF16)
        yatt_ref[r0:r0 + 128, 128:256] = ot_a[:, 128:256].T.astype(BF16)
        yatt_ref[r0:r0 + 128, 256:384] = ot_b[:, 0:128].T.astype(BF16)
        yatt_ref[r0:r0 + 128, 384:512] = ot_b[:, 128:256].T.astype(BF16)
    new_carry = dict(kp=k_all[128:256], vp=v_all[128:256], cbd=[None, None], n=[None, None],
                     m=[None] * M_HEADS)

    zif = z_ref[:, IF:IF + 128] + bif_ref[...]
    gl = jnp.where(lane < M_HEADS, zif, _log_sigmoid(zif))
    gl_t = gl.T
    tr = lax.broadcasted_iota(jnp.int32, (T, T), 0)
    ts = lax.broadcasted_iota(jnp.int32, (T, T), 1)
    tril = ts <= tr
    triu = tr <= ts
    row2 = lax.broadcasted_iota(jnp.int32, (2 * LANES, LANES), 0)
    lane2 = lax.broadcasted_iota(jnp.int32, (2 * LANES, LANES), 1)
    bd_mask = (row2 < LANES) == (lane2 < HEAD_DIM)
    for p in range(2):
        q_pair = z_ref[:, QM + 128 * p:QM + 128 * p + 128]
        k_pair = z_ref[:, KM + 128 * p:KM + 128 * p + 128] * (HEAD_DIM ** -0.5)
        v_pair = z_ref[:, VM + 256 * p:VM + 256 * p + 256]
        cbd = carry["cbd"][p]
        n_pair = carry["n"][p]
        q_bf = q_pair.astype(BF16)
        qc = _dot_nt(q_bf, cbd.astype(BF16))
        qn_prod = q_pair * n_pair
        ws, decays, m_ends = [], [], []
        for hh in range(2):
            hd = 2 * p + hh
            hmask = lane_lo if hh == 0 else jnp.logical_not(lane_lo)
            ig_c = gl[:, hd:hd + 1]
            lf_c = gl[:, M_HEADS + hd:M_HEADS + hd + 1]
            ig_r = gl_t[hd:hd + 1, :]
            lf_r = gl_t[M_HEADS + hd:M_HEADS + hd + 1, :]
            m_prev = carry["m"][hd]
            b_c = jnp.sum(jnp.where(tril, lf_r, 0.0), axis=1, keepdims=True)
            b_r = jnp.sum(jnp.where(triu, lf_c, 0.0), axis=0, keepdims=True)
            a_r = ig_r - b_r
            cm_c = jnp.max(jnp.where(tril, a_r, NEG), axis=1, keepdims=True)
            mt_c = b_c + jnp.maximum(m_prev, cm_c)
            g_c = b_c - mt_c
            dm = jnp.exp(jnp.where(tril, a_r + g_c, NEG))
            inter_c = jnp.exp(m_prev + g_c)
            k_h = jnp.where(hmask, k_pair, 0.0).astype(BF16)
            w = dm * _dot_nt(q_bf, k_h)
            v_h = v_pair[:, 128 * hh:128 * hh + 128]
            num = inter_c * qc[:, 128 * hh:128 * hh + 128] + _dot(w.astype(BF16), v_h.astype(BF16))
            qn = jnp.sum(jnp.where(hmask, qn_prod, 0.0), axis=1, keepdims=True)
            den = inter_c * qn + jnp.sum(w, axis=1, keepdims=True)
            hv = num / jnp.maximum(jnp.abs(den), jnp.exp(-mt_c))
            hn = hv * lax.rsqrt(jnp.mean(hv * hv, axis=-1, keepdims=True) + EPS)
            hn = hn * ghead_ref[:, 128 * hd:128 * hd + 128]
            om = z_ref[:, OM + 128 * hd:OM + 128 * hd + 128]
            ym_ref[:, 128 * hd:128 * hd + 128] = (hn * _sigmoid(om)).astype(BF16)
            b_end = b_c[T - 1:T, :]
            m_end = mt_c[T - 1:T, :]
            ws.append(jnp.exp((ig_c - b_c) + b_end - m_end))
            decays.append(jnp.exp(m_prev + b_end - m_end))
            m_ends.append(m_end)
        kw = k_pair * jnp.where(lane_lo, ws[0], ws[1])
        upd = _dot(v_pair.T.astype(BF16), kw.astype(BF16))
        dec_rows = jnp.where(row2[:, 0:1] < LANES, decays[0], decays[1])
        new_carry["cbd"][p] = dec_rows * cbd + jnp.where(bd_mask, upd, 0.0)
        dec_lanes = jnp.where(lane_lo[0:1, :], decays[0], decays[1])
        new_carry["n"][p] = dec_lanes * n_pair + jnp.sum(kw, axis=0, keepdims=True)
        for hh in range(2):
            new_carry["m"][2 * p + hh] = m_ends[hh]
    return new_carry


def _resident(shape):
    zeros = (0,) * len(shape)
    return pl.BlockSpec(shape, lambda t: zeros, pipeline_mode=pl.Buffered(1))


def _smem():
    return pl.BlockSpec(memory_space=pltpu.SMEM)


def _layer_call(x, wt, wao, wmo, wout, wg, wu, wd, gattn, gffn, gfin, ghead, bif, fbuck, relb, sinks,
                xs3, zs, yatts, cqs, scals):
    bsz, s_len, d = x.shape
    nb = xs3.shape[0]
    t_step = T_BLK * N_SUB
    assert s_len % t_step == 0 and d == 1024 and nb <= T_BLK
    assert wt.shape == (W_END, d) and N_SUB >= 2
    d_ff = wg.shape[1]
    assert all(w.shape[0] % T_BLK == 0 and w.shape[1] <= N_CAT for w in (wao, wmo, wout, wg, wu, wd))
    hbm = pl.BlockSpec(memory_space=pl.ANY)
    nj = s_len // t_step
    n_prompt = bsz * nj
    out_shapes = (
        jax.ShapeDtypeStruct((bsz, s_len, d), F32),
        jax.ShapeDtypeStruct((bsz, WINDOW, 128), F32),
        jax.ShapeDtypeStruct((bsz, WINDOW, 128), F32),
        jax.ShapeDtypeStruct((bsz, M_HEADS, 64, 128), F32),
        jax.ShapeDtypeStruct((bsz, M_HEADS, 64), F32),
        jax.ShapeDtypeStruct((bsz, 1, M_HEADS), F32),
        jax.ShapeDtypeStruct((nb, 1, d), F32),
    )

    def seq(t):
        return jnp.minimum(t, n_prompt - 1) // nj

    def blk(t):
        return jnp.minimum(t, n_prompt - 1) % nj

    in_specs = [
        pl.BlockSpec((None, t_step, d), lambda t: (seq(t), blk(t), 0)),
        hbm, hbm, hbm, hbm, hbm, hbm, hbm,
        _resident(gattn.shape), _resident(gffn.shape), _resident(gfin.shape), _resident(ghead.shape),
        _resident(bif.shape), _resident(fbuck.shape), _smem(), _smem(),
        pl.BlockSpec((nb, None, d), lambda t: (0, 0, 0), pipeline_mode=pl.Buffered(1)),
        _resident(zs.shape), _resident(yatts.shape), _resident(cqs.shape), _resident(scals.shape),
    ]
    out_specs = (
        pl.BlockSpec((None, t_step, d), lambda t: (seq(t), blk(t), 0)),
        pl.BlockSpec((None, WINDOW, 128), lambda t: (seq(t), 0, 0)),
        pl.BlockSpec((None, WINDOW, 128), lambda t: (seq(t), 0, 0)),
        pl.BlockSpec((None, M_HEADS, 64, 128), lambda t: (seq(t), 0, 0, 0)),
        pl.BlockSpec((None, M_HEADS, 64), lambda t: (seq(t), 0, 0)),
        pl.BlockSpec((None, 1, M_HEADS), lambda t: (seq(t), 0, 0)),
        pl.BlockSpec((nb, None, d), lambda t: (0, 0, 0)),
    )
    scratch = [
        pltpu.VMEM((N_SUB, T_BLK, N_CAT), F32),
        pltpu.VMEM((N_SUB, T_BLK, 512), BF16),
        pltpu.VMEM((N_SUB, T_BLK, 512), BF16),
        pltpu.VMEM((T_BLK, d_ff), BF16),
        pltpu.VMEM((N_Q_HEADS, 128, 128), F32),
        pltpu.VMEM((128, 128), F32),
        pltpu.VMEM((128, 128), F32),
        pltpu.VMEM((2, 256, 128), F32),
        pltpu.VMEM((8, 128), F32),
        pltpu.VMEM((d, N_CAT), BF16),
        pltpu.VMEM(wao.shape, BF16), pltpu.VMEM(wmo.shape, BF16), pltpu.VMEM(wout.shape, BF16),
        pltpu.VMEM(wg.shape, BF16), pltpu.VMEM(wu.shape, BF16), pltpu.VMEM(wd.shape, BF16),
        pltpu.SemaphoreType.DMA((STAGE_SLOTS,)),
    ]
    return pl.pallas_call(
        functools.partial(_layer_kernel, nj),
        grid=(n_prompt + 1,),
        in_specs=in_specs,
        out_specs=out_specs,
        out_shape=out_shapes,
        scratch_shapes=scratch,
        compiler_params=pltpu.CompilerParams(
            dimension_semantics=("arbitrary",),
            vmem_limit_bytes=VMEM_LIMIT_BYTES),
        name="layer",
    )(x, wt, wao, wmo, wout, wg, wu, wd, gattn, gffn, gfin, ghead, bif, fbuck, relb, sinks,
      xs3, zs, yatts, cqs, scals)


W_QA, W_KA, W_VA, W_QM, W_KM, W_VM, W_OM, W_IF, W_GA, W_GM, W_END = (
    0, 512, 640, 768, 1024, 1280, 1792, 2304, 2312, 3336, 4360)
ZT_KA, ZT_VA, ZT_QM, ZT_KM, ZT_IF, ZT_ROWS = 0, 128, 256, 512, 768, 776


def _sample_inproj_kernel(x_ref, g_ref, wt_hbm, nt_ref, mt_ref, bif_ref,
                          z_ref, zt_ref, no_ref, mo_ref, scal_ref,
                          wt_ref, sems):
    _sample_projection(x_ref, g_ref, wt_hbm, wt_ref, sems, z_ref, zt_ref)
    r8c = lax.broadcasted_iota(jnp.int32, (8, 1), 0)
    bias_col = jnp.zeros((8, 1), F32)
    for r in range(8):
        bias_col = jnp.where(r8c == r, bif_ref[r // M_HEADS, r % M_HEADS], bias_col)
    ift = zt_ref[ZT_IF:ZT_ROWS, :] + bias_col
    m0 = mt_ref[...]
    r16 = lax.broadcasted_iota(jnp.int32, (16, LANES), 0)
    table = jnp.zeros((16, LANES), F32)
    for hd in range(M_HEADS):
        ig = ift[hd:hd + 1, :]
        lf = _log_sigmoid(ift[M_HEADS + hd:M_HEADS + hd + 1, :])
        m_prev = m0[hd:hd + 1, :]
        a = ig - lf
        m_t = lf + jnp.maximum(m_prev, a)
        dgate = jnp.exp(a + lf - m_t)
        inter = jnp.exp(m_prev + lf - m_t)
        qt = zt_ref[ZT_QM + 64 * hd:ZT_QM + 64 * hd + 64, :]
        kt = zt_ref[ZT_KM + 64 * hd:ZT_KM + 64 * hd + 64, :] * (HEAD_DIM ** -0.5)
        nt = nt_ref[hd]
        no_ref[hd] = inter * nt + dgate * kt
        mo_ref[hd:hd + 1, :] = m_t
        qk = jnp.sum(qt * kt, axis=0, keepdims=True)
        nq = jnp.sum(nt * qt, axis=0, keepdims=True)
        w = dgate * qk
        den = inter * nq + w
        scale = 1.0 / jnp.maximum(jnp.abs(den), jnp.exp(-m_t))
        for base, val in ((SC_INTER, inter), (SC_W, w), (SC_SCALE, scale), (SC_D, dgate)):
            table = jnp.where(r16 == base + hd, val, table)
    full = jnp.concatenate([table, jnp.zeros((LANES - 16, LANES), F32)], axis=0)
    scal_ref[...] = full.T


WT_CHUNKS = ((W_QA, 768), (768, 1536), (1536, W_IF), (W_IF, W_GA), (W_GA, W_GM), (W_GM, W_END))


def _sample_projection(x_ref, g_ref, wt_hbm, wt_ref, sems, z_ref, zt_ref):
    copies = [pltpu.make_async_copy(wt_hbm.at[a:b, :], wt_ref.at[a:b, :], sems.at[k])
              for k, (a, b) in enumerate(WT_CHUNKS)]
    for c in copies:
        c.start()
    h32 = _rms(x_ref[...], g_ref[...])
    h = h32.astype(BF16)
    ht = h32.T.astype(BF16)
    copies[0].wait()
    z_ref[:, QA:768] = _dot_nt(h, wt_ref[W_QA:768, :].astype(BF16))
    zt_ref[ZT_KA:ZT_QM, :] = _dot(wt_ref[W_KA:W_QM, :].astype(BF16), ht)
    copies[1].wait()
    z_ref[:, 768:1536] = _dot_nt(h, wt_ref[768:1536, :].astype(BF16))
    zt_ref[ZT_QM:ZT_IF, :] = _dot(wt_ref[W_QM:W_VM, :].astype(BF16), ht)
    copies[2].wait()
    z_ref[:, 1536:W_IF] = _dot_nt(h, wt_ref[1536:W_IF, :].astype(BF16))
    copies[3].wait()
    w_if = jnp.concatenate([wt_ref[W_IF:W_GA, :], jnp.zeros((LANES - 8, wt_ref.shape[1]), F32)],
                           axis=0).astype(BF16)
    z_ref[:, IF:IF + LANES] = _dot_nt(h, w_if)
    zt_ref[ZT_IF:ZT_ROWS, :] = _dot(w_if, ht)[0:8, :]
    copies[4].wait()
    z_ref[:, GA:GM] = _dot_nt(h, wt_ref[W_GA:W_GM, :].astype(BF16))
    copies[5].wait()
    z_ref[:, GM:N_CAT] = _dot_nt(h, wt_ref[W_GM:W_END, :].astype(BF16))


def _sample_inproj_call(x3, gattn, wt, nt, mt, bif2):
    n, _, d = x3.shape
    assert n == LANES
    full = lambda s: pl.BlockSpec(tuple(s), lambda i: (0,) * len(s))
    return pl.pallas_call(
        _sample_inproj_kernel,
        grid=(1,),
        in_specs=[pl.BlockSpec((n, None, d), lambda i: (0, 0, 0)),
                  pl.BlockSpec((1, d), lambda i: (0, 0)),
                  pl.BlockSpec(memory_space=pl.ANY),
                  full(nt.shape), full(mt.shape), _smem()],
        out_specs=(full((n, N_CAT)), full((ZT_ROWS, n)), full(nt.shape), full(mt.shape),
                   full((n, LANES))),
        out_shape=(jax.ShapeDtypeStruct((n, N_CAT), F32), jax.ShapeDtypeStruct((ZT_ROWS, n), F32),
                   jax.ShapeDtypeStruct(nt.shape, F32), jax.ShapeDtypeStruct(mt.shape, F32),
                   jax.ShapeDtypeStruct((n, LANES), F32)),
        scratch_shapes=[pltpu.VMEM(wt.shape, F32), pltpu.SemaphoreType.DMA((len(WT_CHUNKS),))],
        compiler_params=pltpu.CompilerParams(dimension_semantics=("arbitrary",),
                                             vmem_limit_bytes=VMEM_LIMIT_BYTES),
        name="sample_inproj",
    )(x3, gattn, wt, nt, mt, bif2)


SK2_BB = 16
SC_INTER, SC_W, SC_SCALE, SC_D, SC_COLS = 0, 4, 8, 12, 16
HEAD_ROW_ORDER = (0, 2, 4, 6, 1, 3, 5, 7)


def _sample_mixer_kernel(z_ref, zt_ref, kt_ref, vt_ref, ct_ref, sbuck_ref,
                         scal_ref, relb_ref, sinks_ref,
                         yatt_ref, cq_ref, ko_ref, vo_ref, co_ref,
                         sbias_scr):
    i = pl.program_id(0)
    r8 = lax.broadcasted_iota(jnp.int32, (8, LANES), 0)
    l8 = lax.broadcasted_iota(jnp.int32, (8, LANES), 1)
    r8c = lax.broadcasted_iota(jnp.int32, (8, 1), 0)

    @pl.when(i == 0)
    def _prologue():
        sb = sbuck_ref[...]
        acc = jnp.zeros((8, LANES), F32)
        for rrow, hd in enumerate(HEAD_ROW_ORDER):
            acc = jnp.where(r8 == rrow, _bias_from_buckets(sb, relb_ref, hd), acc)
        sbias_scr[...] = acc

    sink = jnp.zeros((8, 1), F32)
    for rrow, hd in enumerate(HEAD_ROW_ORDER):
        sink = jnp.where(r8c == rrow, sinks_ref[hd], sink)
    sbias = sbias_scr[...]
    lane_w = lax.broadcasted_iota(jnp.int32, (WINDOW, LANES), 1)
    last_lane = lane_w == WINDOW - 1
    r256 = lax.broadcasted_iota(jnp.int32, (8, 2 * LANES), 0)
    l256 = lax.broadcasted_iota(jnp.int32, (8, 2 * LANES), 1)
    own64 = (l256 // HEAD_DIM) == r256
    l512r = lax.broadcasted_iota(jnp.int32, (1, 4 * LANES), 1) // LANES
    lo1 = lax.broadcasted_iota(jnp.int32, (1, LANES), 1) < HEAD_DIM
    row_b = lax.broadcasted_iota(jnp.int32, (LANES, 2 * LANES), 0)
    ktm = [(zt_ref[ZT_KM + 128 * pr:ZT_KM + 128 * pr + 128, :] * (HEAD_DIM ** -0.5)).astype(BF16)
           for pr in range(2)]

    nrow = 8 * SK2_BB
    lst = lax.broadcasted_iota(jnp.int32, (nrow, LANES), 1)
    z1 = jnp.zeros((1, LANES), F32)
    qm_l, so_l, kn_l, vn_l = [], [], [], []
    for bi in range(SK2_BB):
        qp = [z_ref[bi:bi + 1, QA + 128 * p:QA + 128 * p + 128] * (HEAD_DIM ** -0.5) for p in range(4)]
        qpr = [pltpu.roll(x, HEAD_DIM, 1) for x in qp]
        rows = [jnp.where(lo1, qp[0], z1), jnp.where(lo1, qp[1], z1),
                jnp.where(lo1, z1, qpr[2]), jnp.where(lo1, z1, qpr[3]),
                jnp.where(lo1, qpr[0], z1), jnp.where(lo1, qpr[1], z1),
                jnp.where(lo1, z1, qp[2]), jnp.where(lo1, z1, qp[3])]
        qm = jnp.zeros((8, LANES), F32)
        for r in range(8):
            qm = jnp.where(r8 == r, rows[r], qm)
        qm_l.append(qm)
        so_l.append(_dot(qm.astype(BF16), kt_ref[bi].astype(BF16)))
        kn_l.append(jnp.broadcast_to(z_ref[bi:bi + 1, KA:KA + 128], (8, LANES)))
        vn_l.append(jnp.broadcast_to(z_ref[bi:bi + 1, VA:VA + 128], (8, LANES)))
    qm_all = jnp.concatenate(qm_l, axis=0)
    s_old = jnp.concatenate(so_l, axis=0)
    vn_all = jnp.concatenate(vn_l, axis=0)
    s_new = jnp.sum(qm_all * jnp.concatenate(kn_l, axis=0), axis=1, keepdims=True)
    sbias_all = jnp.concatenate([sbias] * SK2_BB, axis=0)
    sink_all = jnp.concatenate([sink] * SK2_BB, axis=0)
    s = jnp.where(lst == WINDOW - 1, s_new, pltpu.roll(s_old, WINDOW - 1, 1)) + sbias_all
    mx = jnp.maximum(jnp.max(s, axis=-1, keepdims=True), sink_all)
    pe = jnp.exp(s - mx)
    den = jnp.sum(pe, axis=-1, keepdims=True) + jnp.exp(sink_all - mx)
    pn = pe * (1.0 / den)
    p_old = jnp.where(lst == 0, 0.0, pltpu.roll(pn, 1, 1))
    oa_l = [_dot_nt(p_old[8 * bi:8 * bi + 8].astype(BF16), vt_ref[bi].astype(BF16))
            for bi in range(SK2_BB)]
    oa = jnp.concatenate(oa_l, axis=0) + pn[:, WINDOW - 1:WINDOW] * vn_all
    oar = pltpu.roll(oa, HEAD_DIM, 1)
    for bi in range(SK2_BB):
        r0 = 8 * bi
        pairs = [jnp.where(lo1, oa[r0:r0 + 1], oar[r0 + 4:r0 + 5]),
                 jnp.where(lo1, oa[r0 + 1:r0 + 2], oar[r0 + 5:r0 + 6]),
                 jnp.where(lo1, oar[r0 + 2:r0 + 3], oa[r0 + 6:r0 + 7]),
                 jnp.where(lo1, oar[r0 + 3:r0 + 4], oa[r0 + 7:r0 + 8])]
        for p in range(4):
            yatt_ref[bi:bi + 1, 128 * p:128 * p + 128] = pairs[p]

    for bi in range(SK2_BB):
        b = i * SK2_BB + bi
        shift = WINDOW - 1 - b
        kcol = pltpu.roll(zt_ref[ZT_KA:ZT_KA + 128, :], shift, 1)
        vcol = pltpu.roll(zt_ref[ZT_VA:ZT_VA + 128, :], shift, 1)
        ko_ref[bi] = jnp.where(last_lane, kcol, pltpu.roll(kt_ref[bi], WINDOW - 1, 1))
        vo_ref[bi] = jnp.where(last_lane, vcol, pltpu.roll(vt_ref[bi], WINDOW - 1, 1))

        qm_row = z_ref[bi:bi + 1, QM:QM + 256]
        vm_row = z_ref[bi:bi + 1, VM:VM + 512]
        qbd = jnp.where(own64, qm_row, 0.0)
        ct_all = jnp.concatenate([ct_ref[bi, hd] for hd in range(M_HEADS)], axis=0)
        cq = _dot(qbd.astype(BF16), ct_all.astype(BF16))
        for hd in range(M_HEADS):
            cq_ref[bi:bi + 1, 128 * hd:128 * hd + 128] = cq[hd:hd + 1, :]
        dsel = jnp.zeros((1, 4 * LANES), F32)
        for hd in range(M_HEADS):
            dsel = jnp.where(l512r == hd, scal_ref[b, SC_D + hd], dsel)
        vs = vm_row * dsel
        for pr in range(2):
            vsel = jnp.where(row_b == b, vs[:, 256 * pr:256 * pr + 256], 0.0).astype(BF16)
            upd = _dot(ktm[pr], vsel)
            for hh in range(2):
                hd = 2 * pr + hh
                co_ref[bi, hd] = (scal_ref[b, SC_INTER + hd] * ct_ref[bi, hd]
                                  + upd[64 * hh:64 * hh + 64, 128 * hh:128 * hh + 128])


def _sample_mixer_call(z, zt, kt, vt, ct, sbuck, scal_small, relb_t, sinks):
    nb = z.shape[0]
    assert nb % SK2_BB == 0 and nb == LANES
    bb = SK2_BB
    out_shapes = (
        jax.ShapeDtypeStruct((nb, 512), F32),
        jax.ShapeDtypeStruct((nb, 512), F32),
        jax.ShapeDtypeStruct(kt.shape, F32),
        jax.ShapeDtypeStruct(vt.shape, F32),
        jax.ShapeDtypeStruct(ct.shape, F32),
    )
    blk = lambda s: pl.BlockSpec((bb,) + tuple(s[1:]), lambda i: (i,) + (0,) * (len(s) - 1))
    full = lambda s: pl.BlockSpec(tuple(s), lambda i: (0,) * len(s))
    in_specs = [blk(z.shape), full(zt.shape), blk(kt.shape), blk(vt.shape), blk(ct.shape),
                full(sbuck.shape), _smem(), _smem(), _smem()]
    out_specs = (blk((nb, 512)), blk((nb, 512)), blk(kt.shape), blk(vt.shape), blk(ct.shape))
    return pl.pallas_call(
        _sample_mixer_kernel,
        grid=(nb // bb,),
        in_specs=in_specs,
        out_specs=out_specs,
        out_shape=out_shapes,
        scratch_shapes=[pltpu.VMEM((8, LANES), F32)],
        compiler_params=pltpu.CompilerParams(dimension_semantics=("arbitrary",)),
        name="sample_mixers",
    )(z, zt, kt, vt, ct, sbuck, scal_small, relb_t, sinks)


def kernel(x_prompt, x_sample, cache_k_win, cache_v_win, state_mlstm_C, state_mlstm_n, state_mlstm_m,
           rel_bias, w_in, b_if, sinks, g_attn_norm, g_head, w_att_out, w_mlstm_out, w_out,
           g_ffn_norm, w_gate, w_up, w_down, g_final):
    depth = w_in.shape[0]
    assert depth == 1
    bsz, s_len, d = x_prompt.shape
    nb = x_sample.shape[0]
    assert x_sample.shape[1] == 1 and cache_k_win.shape[2] == WINDOW

    wt = w_in[0].T
    gattn = g_attn_norm[0].reshape(1, d)
    gffn = g_ffn_norm[0].reshape(1, d)
    gfin = g_final.reshape(1, d)
    ghead = g_head[0].reshape(1, 512)
    bif = jnp.concatenate([b_if[0].reshape(1, 2 * M_HEADS), jnp.zeros((1, LANES - 2 * M_HEADS), F32)], axis=1)
    sink_v = sinks[0]
    relb_t = rel_bias.T

    qi = jnp.arange(WINDOW)[:, None]
    kj = jnp.arange(WINDOW)[None, :]
    fbuck = _t5_bucket(jnp.where(kj <= qi, qi - kj, WINDOW + qi - kj)).astype(jnp.int32).T
    sbuck = jnp.broadcast_to(_t5_bucket(WINDOW - 1 - kj), (8, WINDOW)).astype(jnp.int32)

    feat = N_KV_HEADS * HEAD_DIM
    kt_in = jnp.transpose(cache_k_win[0], (0, 2, 3, 1)).reshape(nb, feat, WINDOW)
    vt_in = jnp.transpose(cache_v_win[0], (0, 2, 3, 1)).reshape(nb, feat, WINDOW)
    ct_in = jnp.transpose(state_mlstm_C[0], (0, 1, 3, 2))
    nt_in = jnp.transpose(state_mlstm_n[0], (1, 2, 0))
    mt_in = state_mlstm_m[0].T
    z_s, zt_s, nt_s, mt_s, scal_s = _sample_inproj_call(x_sample, gattn, wt, nt_in, mt_in, b_if[0])
    yatt_s, cq_s, kt_s, vt_s, ct_s = _sample_mixer_call(
        z_s, zt_s, kt_in, vt_in, ct_in, sbuck, scal_s[:, :SC_COLS], relb_t, sink_v)
    y_p, kt_p, vt_p, ct_p, n_p, m_p, y_s = _layer_call(
        x_prompt, wt, w_att_out[0], w_mlstm_out[0], w_out[0], w_gate[0], w_up[0], w_down[0],
        gattn, gffn, gfin, ghead, bif, fbuck, relb_t, sink_v,
        x_sample, z_s, yatt_s, cq_s, scal_s)

    def window_out(t, n):
        return jnp.transpose(t.reshape(n, N_KV_HEADS, HEAD_DIM, WINDOW), (0, 3, 1, 2))[None]

    return (y_p, y_s,
            window_out(kt_p, bsz), window_out(vt_p, bsz),
            jnp.transpose(ct_p, (0, 1, 3, 2))[None], n_p[None], m_p.reshape(1, bsz, M_HEADS),
            window_out(kt_s, nb), window_out(vt_s, nb),
            jnp.transpose(ct_s, (0, 1, 3, 2))[None], jnp.transpose(nt_s, (2, 0, 1))[None], mt_s.T[None])
```

```python
import functools
import math

import jax
import jax.numpy as jnp
from jax import lax
from jax.experimental import pallas as pl
from jax.experimental.pallas import tpu as pltpu

F32 = jnp.float32
BF16 = jnp.bfloat16

HEAD_DIM = 64
N_Q_HEADS = 8
N_KV_HEADS = 2
WINDOW = 128
N_BUCKETS = 32
MAX_DISTANCE = WINDOW
M_HEADS = 4
EPS = 1e-6
NEG = -1e30

LANES = 128
VMEM_LIMIT_BYTES = 61 * 1024 * 1024

QA, KA, VA, QM, KM, VM, OM, IF, GA, GM = 0, 512, 640, 768, 1024, 1280, 1792, 2304, 2432, 3456
N_CAT = 4480
Z_GROUPS = ((768, 2432), (0, 768), (2432, 4480))
T_BLK = 256
N_SUB = 2
STAGE_ROWS = T_BLK // 2
STAGE_COLS = 1408
STAGE_COL_WINDOWS = N_CAT // STAGE_COLS
STAGE_SLOTS = N_SUB * STAGE_COL_WINDOWS * (T_BLK // STAGE_ROWS)


def _dot(a, b):
    return jnp.dot(a, b, preferred_element_type=F32)


def _dot_nt(a, b):
    return lax.dot_general(a, b, (((1,), (1,)), ((), ())), preferred_element_type=F32)


def _sigmoid(x):
    return 1.0 / (1.0 + jnp.exp(-x))


def _log_sigmoid(x):
    return jnp.minimum(x, 0.0) - jnp.log1p(jnp.exp(-jnp.abs(x)))


def _rms(x, g):
    ms = jnp.mean(x * x, axis=-1, keepdims=True)
    return x * lax.rsqrt(ms + EPS) * g


def _t5_bucket(dist):
    n = jnp.maximum(dist, 0)
    max_exact = N_BUCKETS // 2
    nf = jnp.maximum(n, 1).astype(F32)
    large = max_exact + jnp.floor(jnp.log(nf / max_exact) / math.log(MAX_DISTANCE / max_exact)
                                  * (N_BUCKETS - max_exact)).astype(jnp.int32)
    large = jnp.minimum(large, N_BUCKETS - 1)
    return jnp.where(n < max_exact, n, large)


def _bias_from_buckets(buckets, relb_ref, head):
    acc = jnp.zeros(buckets.shape, F32)
    for k in range(N_BUCKETS):
        acc = jnp.where(buckets == k, relb_ref[head, k], acc)
    return acc


def _layer_kernel(nj, x_ref, wt_hbm, wao_hbm, wmo_hbm, wout_hbm, wg_hbm, wu_hbm, wd_hbm,
                  gattn_ref, gffn_ref, gfin_ref, ghead_ref, bif_ref, fbuck_ref, relb_ref, sinks_ref,
                  xs_ref, zs_ref, yatts_ref, cqs_ref, scals_ref,
                  y_ref, kwin_ref, vwin_ref, c_ref, n_ref, m_ref, ys_ref,
                  z_scr, yatt_scr, ym_scr, a_scr, bias_scr, kprev_scr, vprev_scr, cbd_scr, st_scr,
                  wcat_ref, wao_ref, wmo_ref, wout_ref, wg_ref, wu_ref, wd_ref, dma_sem):
    T = T_BLK
    t = pl.program_id(0)
    n_prompt = pl.num_programs(0) - 1
    is_prompt = t < n_prompt
    j = lax.rem(t, nj)

    @pl.when(t == 0)
    def _first_step():
        def build_bias():
            fb = fbuck_ref[...]
            for h in range(N_Q_HEADS):
                bias_scr[h] = _bias_from_buckets(fb, relb_ref, h)

        chunks = _weight_chunks(wt_hbm, wcat_ref, ((wao_hbm, wao_ref), (wmo_hbm, wmo_ref),
                                                   (wout_hbm, wout_ref), (wg_hbm, wg_ref),
                                                   (wu_hbm, wu_ref), (wd_hbm, wd_ref)))
        _cast_weights(chunks, z_scr, dma_sem, while_first_copy=build_bias)

    @pl.when(jnp.logical_and(is_prompt, j == 0))
    def _reset_state():
        kprev_scr[...] = jnp.zeros_like(kprev_scr)
        vprev_scr[...] = jnp.zeros_like(vprev_scr)
        cbd_scr[...] = jnp.zeros_like(cbd_scr)
        st_scr[...] = jnp.zeros_like(st_scr)

    @pl.when(is_prompt)
    def _prompt_step():
        for s in range(N_SUB):
            _in_projection(x_ref.at[s * T:(s + 1) * T, :], gattn_ref, wcat_ref, z_scr.at[s])
        carry = _load_carry(kprev_scr, vprev_scr, cbd_scr, st_scr)
        for s in range(N_SUB):
            carry = _mixers(z_scr.at[s], yatt_scr.at[s], ym_scr.at[s], carry, ghead_ref, bif_ref,
                            bias_scr, sinks_ref, first_of_sequence=(j == 0) if s == 0 else None)
        _store_carry(carry, kprev_scr, vprev_scr, cbd_scr, st_scr)
        for s in range(N_SUB):
            _merge(x_ref.at[s * T:(s + 1) * T, :], z_scr.at[s], yatt_scr.at[s], ym_scr.at[s],
                   wao_ref, wmo_ref, wout_ref, gffn_ref, y_ref.at[s * T:(s + 1) * T, :])
        for s in range(N_SUB):
            _ffn(yatt_scr.at[s], ym_scr.at[s], a_scr, wg_ref, wu_ref, wd_ref, gfin_ref,
                 y_ref.at[s * T:(s + 1) * T, :])

    @pl.when(jnp.logical_not(is_prompt))
    def _sample_step():
        nb = xs_ref.shape[0]
        scal = scals_ref[...]
        yatt_scr[0, 0:nb, :] = yatts_ref[...].astype(BF16)
        for hd in range(M_HEADS):
            inter = scal[:, SC_INTER + hd:SC_INTER + hd + 1]
            w = scal[:, SC_W + hd:SC_W + hd + 1]
            scale = scal[:, SC_SCALE + hd:SC_SCALE + hd + 1]
            hv = (inter * cqs_ref[:, 128 * hd:128 * hd + 128]
                  + w * zs_ref[:, VM + 128 * hd:VM + 128 * hd + 128]) * scale
            hn = hv * lax.rsqrt(jnp.mean(hv * hv, axis=-1, keepdims=True) + EPS)
            hn = hn * ghead_ref[:, 128 * hd:128 * hd + 128]
            om = zs_ref[:, OM + 128 * hd:OM + 128 * hd + 128]
            ym_scr[0, 0:nb, 128 * hd:128 * hd + 128] = (hn * _sigmoid(om)).astype(BF16)
        _merge(xs_ref, zs_ref, yatt_scr.at[0, 0:nb, :], ym_scr.at[0, 0:nb, :],
               wao_ref, wmo_ref, wout_ref, gffn_ref, ys_ref)
        _ffn(yatt_scr.at[0, 0:nb, :], ym_scr.at[0, 0:nb, :], a_scr.at[0:nb, :], wg_ref, wu_ref, wd_ref,
             gfin_ref, ys_ref)

    @pl.when(jnp.logical_and(is_prompt, j == nj - 1))
    def _write_state():
        kwin_ref[...] = kprev_scr[...].T
        vwin_ref[...] = vprev_scr[...].T
        stn = st_scr[...]
        cts = [cbd_scr[0].T, cbd_scr[1].T]
        for hd in range(M_HEADS):
            p, hh = hd // 2, hd % 2
            c_ref[hd] = cts[p][64 * hh:64 * hh + 64, 128 * hh:128 * hh + 128]
            n_ref[hd:hd + 1, :] = stn[p:p + 1, 64 * hh:64 * hh + 64]
            m_ref[0:1, hd:hd + 1] = stn[2 + hd:3 + hd, 0:1]


def _in_projection(x_ref, gattn_ref, wcat_ref, z_ref):
    h = _rms(x_ref[...], gattn_ref[...]).astype(BF16)
    for c0, c1 in Z_GROUPS:
        z_ref[:, c0:c1] = _dot(h, wcat_ref[:, c0:c1])


def _weight_chunks(wt_hbm, wcat_ref, plain):
    r = STAGE_ROWS
    chunks = [(wt_hbm.at[a:a + r, :], wcat_ref.at[:, a:a + r], True) for a in range(0, W_IF, r)]
    chunks.append((wt_hbm.at[W_IF:W_GA, :], wcat_ref.at[:, IF:IF + LANES], True))
    chunks += [(wt_hbm.at[a:a + r, :], wcat_ref.at[:, GA + a - W_GA:GA + a - W_GA + r], True)
               for a in range(W_GA, W_END, r)]
    for src, dst in plain:
        cols = src.shape[1]
        widths = [cols] if cols <= STAGE_COLS else [STAGE_COLS] * (cols // STAGE_COLS)
        assert sum(widths) == cols
        chunks += [(src.at[a:a + r, c0:c0 + w], dst.at[a:a + r, c0:c0 + w], False)
                   for a in range(0, src.shape[0], r) for c0, w in zip(range(0, cols, STAGE_COLS), widths)]
    return chunks


def _cast_weights(chunks, stage_ref, sem, while_first_copy):
    def window(i, n, c):
        slot = i % STAGE_SLOTS
        col = ((slot // N_SUB) % STAGE_COL_WINDOWS) * STAGE_COLS
        row = (slot // (N_SUB * STAGE_COL_WINDOWS)) * STAGE_ROWS
        return stage_ref.at[slot % N_SUB, row:row + n, col:col + c]

    def copy(i):
        src = chunks[i][0]
        return pltpu.make_async_copy(src, window(i, *src.shape), sem.at[i % STAGE_SLOTS])

    ahead = STAGE_SLOTS - 1
    for i in range(min(ahead, len(chunks))):
        copy(i).start()
    for i, (src, dst, transpose) in enumerate(chunks):
        if i + ahead < len(chunks):
            copy(i + ahead).start()
        if i == 0:
            while_first_copy()
        copy(i).wait()
        n, c = src.shape
        val = window(i, n, c)[...]
        rows = dst.shape[1] if transpose else dst.shape[0]
        if rows != n:
            val = jnp.concatenate([val, jnp.zeros((rows - n, c), F32)], axis=0)
        dst[...] = (val.T if transpose else val).astype(BF16)


def _load_carry(kprev_scr, vprev_scr, cbd_scr, st_scr):
    st = st_scr[...]
    return dict(kp=kprev_scr[...], vp=vprev_scr[...], cbd=[cbd_scr[0], cbd_scr[1]],
                n=[st[0:1, :], st[1:2, :]], m=[st[2 + hd:3 + hd, 0:1] for hd in range(M_HEADS)])


def _store_carry(carry, kprev_scr, vprev_scr, cbd_scr, st_scr):
    kprev_scr[...] = carry["kp"]
    vprev_scr[...] = carry["vp"]
    for p in range(2):
        cbd_scr[p] = carry["cbd"][p]
        st_scr[p:p + 1, :] = carry["n"][p]
    for hd in range(M_HEADS):
        st_scr[2 + hd:3 + hd, :] = jnp.broadcast_to(carry["m"][hd], (1, LANES))


def _merge(x_ref, z_ref, yatt_ref, ym_ref, wao_ref, wmo_ref, wout_ref, gffn_ref, y_ref):
    half = yatt_ref.shape[1]
    ya = _dot(yatt_ref[...], wao_ref[...])
    ymm = _dot(ym_ref[...], wmo_ref[...])
    mixed = _sigmoid(z_ref[:, GA:GA + 1024]) * ya + _sigmoid(z_ref[:, GM:GM + 1024]) * ymm
    x1 = x_ref[...] + _dot(mixed.astype(BF16), wout_ref[...])
    y_ref[...] = x1
    h2 = _rms(x1, gffn_ref[...]).astype(BF16)
    yatt_ref[...] = h2[:, :half]
    ym_ref[...] = h2[:, half:]


def _ffn(h2a_ref, h2b_ref, a_ref, wg_ref, wu_ref, wd_ref, gfin_ref, y_ref):
    d_ff = wg_ref.shape[1]
    h2 = jnp.concatenate([h2a_ref[...], h2b_ref[...]], axis=1)
    for c0 in range(0, d_ff, 256):
        g = _dot(h2, wg_ref[:, c0:c0 + 256])
        u = _dot(h2, wu_ref[:, c0:c0 + 256])
        a_ref[:, c0:c0 + 256] = (g * _sigmoid(g) * u).astype(BF16)
    x2 = y_ref[...] + _dot(a_ref[...], wd_ref[...])
    y_ref[...] = _rms(x2, gfin_ref[...])


def _mixers(z_ref, yatt_ref, ym_ref, carry, ghead_ref, bif_ref, bias_scr, sinks_ref, first_of_sequence):
    T = T_BLK
    lane = lax.broadcasted_iota(jnp.int32, (T, LANES), 1)
    lane_lo = lane < HEAD_DIM
    k_all = z_ref[:, KA:KA + 128]
    v_all = z_ref[:, VA:VA + 128]
    jk = lax.broadcasted_iota(jnp.int32, (128, T), 0)
    iq = lax.broadcasted_iota(jnp.int32, (128, T), 1)
    iq = jnp.where(iq < 128, iq, iq - 128)
    tri_t = jk <= iq
    if first_of_sequence is not None:
        valid_t = jk <= iq + jnp.where(first_of_sequence, 0, 2 * LANES)
    lane_q = lax.broadcasted_iota(jnp.int32, (1, T), 1)
    feat_lo = lax.broadcasted_iota(jnp.int32, (128, T), 0) < HEAD_DIM
    vt_all = v_all.T
    for sb in range(2):
        r0 = 128 * sb
        if sb == 0:
            kp, vpt = carry["kp"], carry["vp"].T
        else:
            kp, vpt = k_all[0:128], vt_all[:, 0:128]
        kcat = jnp.concatenate([kp, k_all[r0:r0 + 128]], axis=0)
        kroll = pltpu.roll(kcat, HEAD_DIM, 1)
        zero = jnp.zeros_like(kcat)
        k_mats = [jnp.where(lane_lo, kcat, zero), jnp.where(lane_lo, zero, kroll),
                  jnp.where(lane_lo, kroll, zero), jnp.where(lane_lo, zero, kcat)]
        vcat_t = jnp.concatenate([vpt, vt_all[:, r0:r0 + 128]], axis=1)
        vroll_t = pltpu.roll(vcat_t, HEAD_DIM, 0)
        zero_t = jnp.zeros_like(vcat_t)
        v_mats_t = [jnp.where(feat_lo, vcat_t, zero_t), jnp.where(feat_lo, zero_t, vroll_t),
                    jnp.where(feat_lo, vroll_t, zero_t), jnp.where(feat_lo, zero_t, vcat_t)]
        q = (z_ref[r0:r0 + 128, QA:QA + 512] * (HEAD_DIM ** -0.5)).astype(BF16)
        lhs_a = jnp.concatenate([q[:, 0:128], q[:, 128:256]], axis=0)
        lhs_b = jnp.concatenate([q[:, 256:384], q[:, 384:512]], axis=0)
        groups = [(lhs_a, 0, 0, 2), (lhs_a, 1, 1, 3), (lhs_b, 2, 4, 6), (lhs_b, 3, 5, 7)]
        pts = []
        for lhs, mi, ha, hb in groups:
            st = _dot_nt(k_mats[mi].astype(BF16), lhs)
            sf = jnp.where(tri_t, st[128:256, :], st[0:128, :])
            sf = sf + jnp.concatenate([bias_scr[ha], bias_scr[hb]], axis=1)
            if sb == 0 and first_of_sequence is not None:
                sf = jnp.where(valid_t, sf, NEG)
            sink = jnp.where(lane_q < 128, sinks_ref[ha], sinks_ref[hb])
            mx = jnp.maximum(jnp.max(sf, axis=0, keepdims=True), sink)
            p = jnp.exp(sf - mx)
            den = jnp.sum(p, axis=0, keepdims=True) + jnp.exp(sink - mx)
            pn = p * (1.0 / den)
            zp = jnp.zeros_like(pn)
            pts.append(jnp.concatenate([jnp.where(tri_t, zp, pn), jnp.where(tri_t, pn, zp)],
                                       axis=0).astype(BF16))
        ot_a = _dot(jnp.concatenate([v_mats_t[0], v_mats_t[1]], axis=1).astype(BF16),
                    jnp.concatenate([pts[0], pts[1]], axis=0))
        ot_b = _dot(jnp.concatenate([v_mats_t[2], v_mats_t[3]], axis=1).astype(BF16),
                    jnp.concatenate([pts[2], pts[3]], axis=0))
        yatt_ref[r0:r0 + 128, 0:128] = ot_a[:, 0:128].T.astype(BF16)
        yatt_ref[r0:r0 + 128, 128:256] = ot_a[:, 128:256].T.astype(BF16)
        yatt_ref[r0:r0 + 128, 256:384] = ot_b[:, 0:128].T.astype(BF16)
        yatt_ref[r0:r0 + 128, 384:512] = ot_b[:, 128:256].T.astype(BF16)
    new_carry = dict(kp=k_all[128:256], vp=v_all[128:256], cbd=[None, None], n=[None, None],
                     m=[None] * M_HEADS)

    zif = z_ref[:, IF:IF + 128] + bif_ref[...]
    gl = jnp.where(lane < M_HEADS, zif, _log_sigmoid(zif))
    gl_t = gl.T
    tr = lax.broadcasted_iota(jnp.int32, (T, T), 0)
    ts = lax.broadcasted_iota(jnp.int32, (T, T), 1)
    tril = ts <= tr
    triu = tr <= ts
    row2 = lax.broadcasted_iota(jnp.int32, (2 * LANES, LANES), 0)
    lane2 = lax.broadcasted_iota(jnp.int32, (2 * LANES, LANES), 1)
    bd_mask = (row2 < LANES) == (lane2 < HEAD_DIM)
    for p in range(2):
        q_pair = z_ref[:, QM + 128 * p:QM + 128 * p + 128]
        k_pair = z_ref[:, KM + 128 * p:KM + 128 * p + 128] * (HEAD_DIM ** -0.5)
        v_pair = z_ref[:, VM + 256 * p:VM + 256 * p + 256]
        cbd = carry["cbd"][p]
        n_pair = carry["n"][p]
        q_bf = q_pair.astype(BF16)
        qc = _dot_nt(q_bf, cbd.astype(BF16))
        qn_prod = q_pair * n_pair
        ws, decays, m_ends = [], [], []
        for hh in range(2):
            hd = 2 * p + hh
            hmask = lane_lo if hh == 0 else jnp.logical_not(lane_lo)
            ig_c = gl[:, hd:hd + 1]
            lf_c = gl[:, M_HEADS + hd:M_HEADS + hd + 1]
            ig_r = gl_t[hd:hd + 1, :]
            lf_r = gl_t[M_HEADS + hd:M_HEADS + hd + 1, :]
            m_prev = carry["m"][hd]
            b_c = jnp.sum(jnp.where(tril, lf_r, 0.0), axis=1, keepdims=True)
            b_r = jnp.sum(jnp.where(triu, lf_c, 0.0), axis=0, keepdims=True)
            a_r = ig_r - b_r
            cm_c = jnp.max(jnp.where(tril, a_r, NEG), axis=1, keepdims=True)
            mt_c = b_c + jnp.maximum(m_prev, cm_c)
            g_c = b_c - mt_c
            dm = jnp.exp(jnp.where(tril, a_r + g_c, NEG))
            inter_c = jnp.exp(m_prev + g_c)
            k_h = jnp.where(hmask, k_pair, 0.0).astype(BF16)
            w = dm * _dot_nt(q_bf, k_h)
            v_h = v_pair[:, 128 * hh:128 * hh + 128]
            num = inter_c * qc[:, 128 * hh:128 * hh + 128] + _dot(w.astype(BF16), v_h.astype(BF16))
            qn = jnp.sum(jnp.where(hmask, qn_prod, 0.0), axis=1, keepdims=True)
            den = inter_c * qn + jnp.sum(w, axis=1, keepdims=True)
            hv = num / jnp.maximum(jnp.abs(den), jnp.exp(-mt_c))
            hn = hv * lax.rsqrt(jnp.mean(hv * hv, axis=-1, keepdims=True) + EPS)
            hn = hn * ghead_ref[:, 128 * hd:128 * hd + 128]
            om = z_ref[:, OM + 128 * hd:OM + 128 * hd + 128]
            ym_ref[:, 128 * hd:128 * hd + 128] = (hn * _sigmoid(om)).astype(BF16)
            b_end = b_c[T - 1:T, :]
            m_end = mt_c[T - 1:T, :]
            ws.append(jnp.exp((ig_c - b_c) + b_end - m_end))
            decays.append(jnp.exp(m_prev + b_end - m_end))
            m_ends.append(m_end)
        kw = k_pair * jnp.where(lane_lo, ws[0], ws[1])
        upd = _dot(v_pair.T.astype(BF16), kw.astype(BF16))
        dec_rows = jnp.where(row2[:, 0:1] < LANES, decays[0], decays[1])
        new_carry["cbd"][p] = dec_rows * cbd + jnp.where(bd_mask, upd, 0.0)
        dec_lanes = jnp.where(lane_lo[0:1, :], decays[0], decays[1])
        new_carry["n"][p] = dec_lanes * n_pair + jnp.sum(kw, axis=0, keepdims=True)
        for hh in range(2):
            new_carry["m"][2 * p + hh] = m_ends[hh]
    return new_carry


def _resident(shape):
    zeros = (0,) * len(shape)
    return pl.BlockSpec(shape, lambda t: zeros, pipeline_mode=pl.Buffered(1))


def _smem():
    return pl.BlockSpec(memory_space=pltpu.SMEM)


def _layer_call(x, wt, wao, wmo, wout, wg, wu, wd, gattn, gffn, gfin, ghead, bif, fbuck, relb, sinks,
                xs3, zs, yatts, cqs, scals):
    bsz, s_len, d = x.shape
    nb = xs3.shape[0]
    t_step = T_BLK * N_SUB
    assert s_len % t_step == 0 and d == 1024 and nb <= T_BLK
    assert wt.shape == (W_END, d) and N_SUB >= 2
    d_ff = wg.shape[1]
    assert all(w.shape[0] % T_BLK == 0 and w.shape[1] <= N_CAT for w in (wao, wmo, wout, wg, wu, wd))
    hbm = pl.BlockSpec(memory_space=pl.ANY)
    nj = s_len // t_step
    n_prompt = bsz * nj
    out_shapes = (
        jax.ShapeDtypeStruct((bsz, s_len, d), F32),
        jax.ShapeDtypeStruct((bsz, WINDOW, 128), F32),
        jax.ShapeDtypeStruct((bsz, WINDOW, 128), F32),
        jax.ShapeDtypeStruct((bsz, M_HEADS, 64, 128), F32),
        jax.ShapeDtypeStruct((bsz, M_HEADS, 64), F32),
        jax.ShapeDtypeStruct((bsz, 1, M_HEADS), F32),
        jax.ShapeDtypeStruct((nb, 1, d), F32),
    )

    def seq(t):
        return jnp.minimum(t, n_prompt - 1) // nj

    def blk(t):
        return jnp.minimum(t, n_prompt - 1) % nj

    in_specs = [
        pl.BlockSpec((None, t_step, d), lambda t: (seq(t), blk(t), 0)),
        hbm, hbm, hbm, hbm, hbm, hbm, hbm,
        _resident(gattn.shape), _resident(gffn.shape), _resident(gfin.shape), _resident(ghead.shape),
        _resident(bif.shape), _resident(fbuck.shape), _smem(), _smem(),
        pl.BlockSpec((nb, None, d), lambda t: (0, 0, 0), pipeline_mode=pl.Buffered(1)),
        _resident(zs.shape), _resident(yatts.shape), _resident(cqs.shape), _resident(scals.shape),
    ]
    out_specs = (
        pl.BlockSpec((None, t_step, d), lambda t: (seq(t), blk(t), 0)),
        pl.BlockSpec((None, WINDOW, 128), lambda t: (seq(t), 0, 0)),
        pl.BlockSpec((None, WINDOW, 128), lambda t: (seq(t), 0, 0)),
        pl.BlockSpec((None, M_HEADS, 64, 128), lambda t: (seq(t), 0, 0, 0)),
        pl.BlockSpec((None, M_HEADS, 64), lambda t: (seq(t), 0, 0)),
        pl.BlockSpec((None, 1, M_HEADS), lambda t: (seq(t), 0, 0)),
        pl.BlockSpec((nb, None, d), lambda t: (0, 0, 0)),
    )
    scratch = [
        pltpu.VMEM((N_SUB, T_BLK, N_CAT), F32),
        pltpu.VMEM((N_SUB, T_BLK, 512), BF16),
        pltpu.VMEM((N_SUB, T_BLK, 512), BF16),
        pltpu.VMEM((T_BLK, d_ff), BF16),
        pltpu.VMEM((N_Q_HEADS, 128, 128), F32),
        pltpu.VMEM((128, 128), F32),
        pltpu.VMEM((128, 128), F32),
        pltpu.VMEM((2, 256, 128), F32),
        pltpu.VMEM((8, 128), F32),
        pltpu.VMEM((d, N_CAT), BF16),
        pltpu.VMEM(wao.shape, BF16), pltpu.VMEM(wmo.shape, BF16), pltpu.VMEM(wout.shape, BF16),
        pltpu.VMEM(wg.shape, BF16), pltpu.VMEM(wu.shape, BF16), pltpu.VMEM(wd.shape, BF16),
        pltpu.SemaphoreType.DMA((STAGE_SLOTS,)),
    ]
    return pl.pallas_call(
        functools.partial(_layer_kernel, nj),
        grid=(n_prompt + 1,),
        in_specs=in_specs,
        out_specs=out_specs,
        out_shape=out_shapes,
        scratch_shapes=scratch,
        compiler_params=pltpu.CompilerParams(
            dimension_semantics=("arbitrary",),
            vmem_limit_bytes=VMEM_LIMIT_BYTES),
        name="layer",
    )(x, wt, wao, wmo, wout, wg, wu, wd, gattn, gffn, gfin, ghead, bif, fbuck, relb, sinks,
      xs3, zs, yatts, cqs, scals)


W_QA, W_KA, W_VA, W_QM, W_KM, W_VM, W_OM, W_IF, W_GA, W_GM, W_END = (
    0, 512, 640, 768, 1024, 1280, 1792, 2304, 2312, 3336, 4360)
ZT_KA, ZT_VA, ZT_QM, ZT_KM, ZT_IF, ZT_ROWS = 0, 128, 256, 512, 768, 776


def _sample_inproj_kernel(x_ref, g_ref, wt_hbm, nt_ref, mt_ref, bif_ref,
                          z_ref, zt_ref, no_ref, mo_ref, scal_ref,
                          wt_ref, sems):
    _sample_projection(x_ref, g_ref, wt_hbm, wt_ref, sems, z_ref, zt_ref)
    r8c = lax.broadcasted_iota(jnp.int32, (8, 1), 0)
    bias_col = jnp.zeros((8, 1), F32)
    for r in range(8):
        bias_col = jnp.where(r8c == r, bif_ref[r // M_HEADS, r % M_HEADS], bias_col)
    ift = zt_ref[ZT_IF:ZT_ROWS, :] + bias_col
    m0 = mt_ref[...]
    r16 = lax.broadcasted_iota(jnp.int32, (16, LANES), 0)
    table = jnp.zeros((16, LANES), F32)
    for hd in range(M_HEADS):
        ig = ift[hd:hd + 1, :]
        lf = _log_sigmoid(ift[M_HEADS + hd:M_HEADS + hd + 1, :])
        m_prev = m0[hd:hd + 1, :]
        a = ig - lf
        m_t = lf + jnp.maximum(m_prev, a)
        dgate = jnp.exp(a + lf - m_t)
        inter = jnp.exp(m_prev + lf - m_t)
        qt = zt_ref[ZT_QM + 64 * hd:ZT_QM + 64 * hd + 64, :]
        kt = zt_ref[ZT_KM + 64 * hd:ZT_KM + 64 * hd + 64, :] * (HEAD_DIM ** -0.5)
        nt = nt_ref[hd]
        no_ref[hd] = inter * nt + dgate * kt
        mo_ref[hd:hd + 1, :] = m_t
        qk = jnp.sum(qt * kt, axis=0, keepdims=True)
        nq = jnp.sum(nt * qt, axis=0, keepdims=True)
        w = dgate * qk
        den = inter * nq + w
        scale = 1.0 / jnp.maximum(jnp.abs(den), jnp.exp(-m_t))
        for base, val in ((SC_INTER, inter), (SC_W, w), (SC_SCALE, scale), (SC_D, dgate)):
            table = jnp.where(r16 == base + hd, val, table)
    full = jnp.concatenate([table, jnp.zeros((LANES - 16, LANES), F32)], axis=0)
    scal_ref[...] = full.T


WT_CHUNKS = ((W_QA, 768), (768, 1536), (1536, W_IF), (W_IF, W_GA), (W_GA, W_GM), (W_GM, W_END))


def _sample_projection(x_ref, g_ref, wt_hbm, wt_ref, sems, z_ref, zt_ref):
    copies = [pltpu.make_async_copy(wt_hbm.at[a:b, :], wt_ref.at[a:b, :], sems.at[k])
              for k, (a, b) in enumerate(WT_CHUNKS)]
    for c in copies:
        c.start()
    h32 = _rms(x_ref[...], g_ref[...])
    h = h32.astype(BF16)
    ht = h32.T.astype(BF16)
    copies[0].wait()
    z_ref[:, QA:768] = _dot_nt(h, wt_ref[W_QA:768, :].astype(BF16))
    zt_ref[ZT_KA:ZT_QM, :] = _dot(wt_ref[W_KA:W_QM, :].astype(BF16), ht)
    copies[1].wait()
    z_ref[:, 768:1536] = _dot_nt(h, wt_ref[768:1536, :].astype(BF16))
    zt_ref[ZT_QM:ZT_IF, :] = _dot(wt_ref[W_QM:W_VM, :].astype(BF16), ht)
    copies[2].wait()
    z_ref[:, 1536:W_IF] = _dot_nt(h, wt_ref[1536:W_IF, :].astype(BF16))
    copies[3].wait()
    w_if = jnp.concatenate([wt_ref[W_IF:W_GA, :], jnp.zeros((LANES - 8, wt_ref.shape[1]), F32)],
                           axis=0).astype(BF16)
    z_ref[:, IF:IF + LANES] = _dot_nt(h, w_if)
    zt_ref[ZT_IF:ZT_ROWS, :] = _dot(w_if, ht)[0:8, :]
    copies[4].wait()
    z_ref[:, GA:GM] = _dot_nt(h, wt_ref[W_GA:W_GM, :].astype(BF16))
    copies[5].wait()
    z_ref[:, GM:N_CAT] = _dot_nt(h, wt_ref[W_GM:W_END, :].astype(BF16))


def _sample_inproj_call(x3, gattn, wt, nt, mt, bif2):
    n, _, d = x3.shape
    assert n == LANES
    full = lambda s: pl.BlockSpec(tuple(s), lambda i: (0,) * len(s))
    return pl.pallas_call(
        _sample_inproj_kernel,
        grid=(1,),
        in_specs=[pl.BlockSpec((n, None, d), lambda i: (0, 0, 0)),
                  pl.BlockSpec((1, d), lambda i: (0, 0)),
                  pl.BlockSpec(memory_space=pl.ANY),
                  full(nt.shape), full(mt.shape), _smem()],
        out_specs=(full((n, N_CAT)), full((ZT_ROWS, n)), full(nt.shape), full(mt.shape),
                   full((n, LANES))),
        out_shape=(jax.ShapeDtypeStruct((n, N_CAT), F32), jax.ShapeDtypeStruct((ZT_ROWS, n), F32),
                   jax.ShapeDtypeStruct(nt.shape, F32), jax.ShapeDtypeStruct(mt.shape, F32),
                   jax.ShapeDtypeStruct((n, LANES), F32)),
        scratch_shapes=[pltpu.VMEM(wt.shape, F32), pltpu.SemaphoreType.DMA((len(WT_CHUNKS),))],
        compiler_params=pltpu.CompilerParams(dimension_semantics=("arbitrary",),
                                             vmem_limit_bytes=VMEM_LIMIT_BYTES),
        name="sample_inproj",
    )(x3, gattn, wt, nt, mt, bif2)


SK2_BB = 16
SC_INTER, SC_W, SC_SCALE, SC_D, SC_COLS = 0, 4, 8, 12, 16
HEAD_ROW_ORDER = (0, 2, 4, 6, 1, 3, 5, 7)


def _sample_mixer_kernel(z_ref, zt_ref, kt_ref, vt_ref, ct_ref, sbuck_ref,
                         scal_ref, relb_ref, sinks_ref,
                         yatt_ref, cq_ref, ko_ref, vo_ref, co_ref,
                         sbias_scr):
    i = pl.program_id(0)
    r8 = lax.broadcasted_iota(jnp.int32, (8, LANES), 0)
    l8 = lax.broadcasted_iota(jnp.int32, (8, LANES), 1)
    r8c = lax.broadcasted_iota(jnp.int32, (8, 1), 0)

    @pl.when(i == 0)
    def _prologue():
        sb = sbuck_ref[...]
        acc = jnp.zeros((8, LANES), F32)
        for rrow, hd in enumerate(HEAD_ROW_ORDER):
            acc = jnp.where(r8 == rrow, _bias_from_buckets(sb, relb_ref, hd), acc)
        sbias_scr[...] = acc

    sink = jnp.zeros((8, 1), F32)
    for rrow, hd in enumerate(HEAD_ROW_ORDER):
        sink = jnp.where(r8c == rrow, sinks_ref[hd], sink)
    sbias = sbias_scr[...]
    lane_w = lax.broadcasted_iota(jnp.int32, (WINDOW, LANES), 1)
    last_lane = lane_w == WINDOW - 1
    r256 = lax.broadcasted_iota(jnp.int32, (8, 2 * LANES), 0)
    l256 = lax.broadcasted_iota(jnp.int32, (8, 2 * LANES), 1)
    own64 = (l256 // HEAD_DIM) == r256
    l512r = lax.broadcasted_iota(jnp.int32, (1, 4 * LANES), 1) // LANES
    lo1 = lax.broadcasted_iota(jnp.int32, (1, LANES), 1) < HEAD_DIM
    row_b = lax.broadcasted_iota(jnp.int32, (LANES, 2 * LANES), 0)
    ktm = [(zt_ref[ZT_KM + 128 * pr:ZT_KM + 128 * pr + 128, :] * (HEAD_DIM ** -0.5)).astype(BF16)
           for pr in range(2)]

    nrow = 8 * SK2_BB
    lst = lax.broadcasted_iota(jnp.int32, (nrow, LANES), 1)
    z1 = jnp.zeros((1, LANES), F32)
    qm_l, so_l, kn_l, vn_l = [], [], [], []
    for bi in range(SK2_BB):
        qp = [z_ref[bi:bi + 1, QA + 128 * p:QA + 128 * p + 128] * (HEAD_DIM ** -0.5) for p in range(4)]
        qpr = [pltpu.roll(x, HEAD_DIM, 1) for x in qp]
        rows = [jnp.where(lo1, qp[0], z1), jnp.where(lo1, qp[1], z1),
                jnp.where(lo1, z1, qpr[2]), jnp.where(lo1, z1, qpr[3]),
                jnp.where(lo1, qpr[0], z1), jnp.where(lo1, qpr[1], z1),
                jnp.where(lo1, z1, qp[2]), jnp.where(lo1, z1, qp[3])]
        qm = jnp.zeros((8, LANES), F32)
        for r in range(8):
            qm = jnp.where(r8 == r, rows[r], qm)
        qm_l.append(qm)
        so_l.append(_dot(qm.astype(BF16), kt_ref[bi].astype(BF16)))
        kn_l.append(jnp.broadcast_to(z_ref[bi:bi + 1, KA:KA + 128], (8, LANES)))
        vn_l.append(jnp.broadcast_to(z_ref[bi:bi + 1, VA:VA + 128], (8, LANES)))
    qm_all = jnp.concatenate(qm_l, axis=0)
    s_old = jnp.concatenate(so_l, axis=0)
    vn_all = jnp.concatenate(vn_l, axis=0)
    s_new = jnp.sum(qm_all * jnp.concatenate(kn_l, axis=0), axis=1, keepdims=True)
    sbias_all = jnp.concatenate([sbias] * SK2_BB, axis=0)
    sink_all = jnp.concatenate([sink] * SK2_BB, axis=0)
    s = jnp.where(lst == WINDOW - 1, s_new, pltpu.roll(s_old, WINDOW - 1, 1)) + sbias_all
    mx = jnp.maximum(jnp.max(s, axis=-1, keepdims=True), sink_all)
    pe = jnp.exp(s - mx)
    den = jnp.sum(pe, axis=-1, keepdims=True) + jnp.exp(sink_all - mx)
    pn = pe * (1.0 / den)
    p_old = jnp.where(lst == 0, 0.0, pltpu.roll(pn, 1, 1))
    oa_l = [_dot_nt(p_old[8 * bi:8 * bi + 8].astype(BF16), vt_ref[bi].astype(BF16))
            for bi in range(SK2_BB)]
    oa = jnp.concatenate(oa_l, axis=0) + pn[:, WINDOW - 1:WINDOW] * vn_all
    oar = pltpu.roll(oa, HEAD_DIM, 1)
    for bi in range(SK2_BB):
        r0 = 8 * bi
        pairs = [jnp.where(lo1, oa[r0:r0 + 1], oar[r0 + 4:r0 + 5]),
                 jnp.where(lo1, oa[r0 + 1:r0 + 2], oar[r0 + 5:r0 + 6]),
                 jnp.where(lo1, oar[r0 + 2:r0 + 3], oa[r0 + 6:r0 + 7]),
                 jnp.where(lo1, oar[r0 + 3:r0 + 4], oa[r0 + 7:r0 + 8])]
        for p in range(4):
            yatt_ref[bi:bi + 1, 128 * p:128 * p + 128] = pairs[p]

    for bi in range(SK2_BB):
        b = i * SK2_BB + bi
        shift = WINDOW - 1 - b
        kcol = pltpu.roll(zt_ref[ZT_KA:ZT_KA + 128, :], shift, 1)
        vcol = pltpu.roll(zt_ref[ZT_VA:ZT_VA + 128, :], shift, 1)
        ko_ref[bi] = jnp.where(last_lane, kcol, pltpu.roll(kt_ref[bi], WINDOW - 1, 1))
        vo_ref[bi] = jnp.where(last_lane, vcol, pltpu.roll(vt_ref[bi], WINDOW - 1, 1))

        qm_row = z_ref[bi:bi + 1, QM:QM + 256]
        vm_row = z_ref[bi:bi + 1, VM:VM + 512]
        qbd = jnp.where(own64, qm_row, 0.0)
        ct_all = jnp.concatenate([ct_ref[bi, hd] for hd in range(M_HEADS)], axis=0)
        cq = _dot(qbd.astype(BF16), ct_all.astype(BF16))
        for hd in range(M_HEADS):
            cq_ref[bi:bi + 1, 128 * hd:128 * hd + 128] = cq[hd:hd + 1, :]
        dsel = jnp.zeros((1, 4 * LANES), F32)
        for hd in range(M_HEADS):
            dsel = jnp.where(l512r == hd, scal_ref[b, SC_D + hd], dsel)
        vs = vm_row * dsel
        for pr in range(2):
            vsel = jnp.where(row_b == b, vs[:, 256 * pr:256 * pr + 256], 0.0).astype(BF16)
            upd = _dot(ktm[pr], vsel)
            for hh in range(2):
                hd = 2 * pr + hh
                co_ref[bi, hd] = (scal_ref[b, SC_INTER + hd] * ct_ref[bi, hd]
                                  + upd[64 * hh:64 * hh + 64, 128 * hh:128 * hh + 128])


def _sample_mixer_call(z, zt, kt, vt, ct, sbuck, scal_small, relb_t, sinks):
    nb = z.shape[0]
    assert nb % SK2_BB == 0 and nb == LANES
    bb = SK2_BB
    out_shapes = (
        jax.ShapeDtypeStruct((nb, 512), F32),
        jax.ShapeDtypeStruct((nb, 512), F32),
        jax.ShapeDtypeStruct(kt.shape, F32),
        jax.ShapeDtypeStruct(vt.shape, F32),
        jax.ShapeDtypeStruct(ct.shape, F32),
    )
    blk = lambda s: pl.BlockSpec((bb,) + tuple(s[1:]), lambda i: (i,) + (0,) * (len(s) - 1))
    full = lambda s: pl.BlockSpec(tuple(s), lambda i: (0,) * len(s))
    in_specs = [blk(z.shape), full(zt.shape), blk(kt.shape), blk(vt.shape), blk(ct.shape),
                full(sbuck.shape), _smem(), _smem(), _smem()]
    out_specs = (blk((nb, 512)), blk((nb, 512)), blk(kt.shape), blk(vt.shape), blk(ct.shape))
    return pl.pallas_call(
        _sample_mixer_kernel,
        grid=(nb // bb,),
        in_specs=in_specs,
        out_specs=out_specs,
        out_shape=out_shapes,
        scratch_shapes=[pltpu.VMEM((8, LANES), F32)],
        compiler_params=pltpu.CompilerParams(dimension_semantics=("arbitrary",)),
        name="sample_mixers",
    )(z, zt, kt, vt, ct, sbuck, scal_small, relb_t, sinks)


def kernel(x_prompt, x_sample, cache_k_win, cache_v_win, state_mlstm_C, state_mlstm_n, state_mlstm_m,
           rel_bias, w_in, b_if, sinks, g_attn_norm, g_head, w_att_out, w_mlstm_out, w_out,
           g_ffn_norm, w_gate, w_up, w_down, g_final):
    depth = w_in.shape[0]
    assert depth == 1
    bsz, s_len, d = x_prompt.shape
    nb = x_sample.shape[0]
    assert x_sample.shape[1] == 1 and cache_k_win.shape[2] == WINDOW

    wt = w_in[0].T
    gattn = g_attn_norm[0].reshape(1, d)
    gffn = g_ffn_norm[0].reshape(1, d)
    gfin = g_final.reshape(1, d)
    ghead = g_head[0].reshape(1, 512)
    bif = jnp.concatenate([b_if[0].reshape(1, 2 * M_HEADS), jnp.zeros((1, LANES - 2 * M_HEADS), F32)], axis=1)
    sink_v = sinks[0]
    relb_t = rel_bias.T

    qi = jnp.arange(WINDOW)[:, None]
    kj = jnp.arange(WINDOW)[None, :]
    fbuck = _t5_bucket(jnp.where(kj <= qi, qi - kj, WINDOW + qi - kj)).astype(jnp.int32).T
    sbuck = jnp.broadcast_to(_t5_bucket(WINDOW - 1 - kj), (8, WINDOW)).astype(jnp.int32)

    feat = N_KV_HEADS * HEAD_DIM
    kt_in = jnp.transpose(cache_k_win[0], (0, 2, 3, 1)).reshape(nb, feat, WINDOW)
    vt_in = jnp.transpose(cache_v_win[0], (0, 2, 3, 1)).reshape(nb, feat, WINDOW)
    ct_in = jnp.transpose(state_mlstm_C[0], (0, 1, 3, 2))
    nt_in = jnp.transpose(state_mlstm_n[0], (1, 2, 0))
    mt_in = state_mlstm_m[0].T
    z_s, zt_s, nt_s, mt_s, scal_s = _sample_inproj_call(x_sample, gattn, wt, nt_in, mt_in, b_if[0])
    yatt_s, cq_s, kt_s, vt_s, ct_s = _sample_mixer_call(
        z_s, zt_s, kt_in, vt_in, ct_in, sbuck, scal_s[:, :SC_COLS], relb_t, sink_v)
    y_p, kt_p, vt_p, ct_p, n_p, m_p, y_s = _layer_call(
        x_prompt, wt, w_att_out[0], w_mlstm_out[0], w_out[0], w_gate[0], w_up[0], w_down[0],
        gattn, gffn, gfin, ghead, bif, fbuck, relb_t, sink_v,
        x_sample, z_s, yatt_s, cq_s, scal_s)

    def window_out(t, n):
        return jnp.transpose(t.reshape(n, N_KV_HEADS, HEAD_DIM, WINDOW), (0, 3, 1, 2))[None]

    return (y_p, y_s,
            window_out(kt_p, bsz), window_out(vt_p, bsz),
            jnp.transpose(ct_p, (0, 1, 3, 2))[None], n_p[None], m_p.reshape(1, bsz, M_HEADS),
            window_out(kt_s, nb), window_out(vt_s, nb),
            jnp.transpose(ct_s, (0, 1, 3, 2))[None], jnp.transpose(nt_s, (2, 0, 1))[None], mt_s.T[None])
```

```python
import functools
import math

import jax
import jax.numpy as jnp
from jax import lax
from jax.experimental import pallas as pl
from jax.experimental.pallas import tpu as pltpu

F32 = jnp.float32
BF16 = jnp.bfloat16

HEAD_DIM = 64
N_Q_HEADS = 8
N_KV_HEADS = 2
WINDOW = 128
N_BUCKETS = 32
MAX_DISTANCE = WINDOW
M_HEADS = 4
EPS = 1e-6
NEG = -1e30

LANES = 128
VMEM_LIMIT_BYTES = 61 * 1024 * 1024

QA, KA, VA, QM, KM, VM, OM, IF, GA, GM = 0, 512, 640, 768, 1024, 1280, 1792, 2304, 2432, 3456
N_CAT = 4480
Z_GROUPS = ((768, 2432), (0, 768), (2432, 4480))
T_BLK = 256
N_SUB = 2
STAGE_COLS = 1024
STAGE_SLOTS = N_SUB * (N_CAT // STAGE_COLS)


def _dot(a, b):
    return jnp.dot(a, b, preferred_element_type=F32)


def _dot_nt(a, b):
    return lax.dot_general(a, b, (((1,), (1,)), ((), ())), preferred_element_type=F32)


def _sigmoid(x):
    return 1.0 / (1.0 + jnp.exp(-x))


def _log_sigmoid(x):
    return jnp.minimum(x, 0.0) - jnp.log1p(jnp.exp(-jnp.abs(x)))


def _rms(x, g):
    ms = jnp.mean(x * x, axis=-1, keepdims=True)
    return x * lax.rsqrt(ms + EPS) * g


def _t5_bucket(dist):
    n = jnp.maximum(dist, 0)
    max_exact = N_BUCKETS // 2
    nf = jnp.maximum(n, 1).astype(F32)
    large = max_exact + jnp.floor(jnp.log(nf / max_exact) / math.log(MAX_DISTANCE / max_exact)
                                  * (N_BUCKETS - max_exact)).astype(jnp.int32)
    large = jnp.minimum(large, N_BUCKETS - 1)
    return jnp.where(n < max_exact, n, large)


def _bias_from_buckets(buckets, relb_ref, head):
    acc = jnp.zeros(buckets.shape, F32)
    for k in range(N_BUCKETS):
        acc = jnp.where(buckets == k, relb_ref[head, k], acc)
    return acc


def _layer_kernel(nj, x_ref, wt_hbm, wao_hbm, wmo_hbm, wout_hbm, wg_hbm, wu_hbm, wd_hbm,
                  gattn_ref, gffn_ref, gfin_ref, ghead_ref, bif_ref, fbuck_ref, relb_ref, sinks_ref,
                  xs_ref, zs_ref, yatts_ref, cqs_ref, scals_ref,
                  y_ref, kwin_ref, vwin_ref, c_ref, n_ref, m_ref, ys_ref,
                  z_scr, yatt_scr, ym_scr, a_scr, bias_scr, kprev_scr, vprev_scr, cbd_scr, st_scr,
                  wcat_ref, wao_ref, wmo_ref, wout_ref, wg_ref, wu_ref, wd_ref, dma_sem):
    T = T_BLK
    t = pl.program_id(0)
    n_prompt = pl.num_programs(0) - 1
    is_prompt = t < n_prompt
    j = lax.rem(t, nj)

    @pl.when(t == 0)
    def _first_step():
        def build_bias():
            fb = fbuck_ref[...]
            for h in range(N_Q_HEADS):
                bias_scr[h] = _bias_from_buckets(fb, relb_ref, h)

        chunks = _weight_chunks(wt_hbm, wcat_ref, ((wao_hbm, wao_ref), (wmo_hbm, wmo_ref),
                                                   (wout_hbm, wout_ref), (wg_hbm, wg_ref),
                                                   (wu_hbm, wu_ref), (wd_hbm, wd_ref)))
        _cast_weights(chunks, z_scr, dma_sem, while_first_copy=build_bias)

    @pl.when(jnp.logical_and(is_prompt, j == 0))
    def _reset_state():
        kprev_scr[...] = jnp.zeros_like(kprev_scr)
        vprev_scr[...] = jnp.zeros_like(vprev_scr)
        cbd_scr[...] = jnp.zeros_like(cbd_scr)
        st_scr[...] = jnp.zeros_like(st_scr)

    @pl.when(is_prompt)
    def _prompt_step():
        for s in range(N_SUB):
            _in_projection(x_ref.at[s * T:(s + 1) * T, :], gattn_ref, wcat_ref, z_scr.at[s])
        carry = _load_carry(kprev_scr, vprev_scr, cbd_scr, st_scr)
        for s in range(N_SUB):
            carry = _mixers(z_scr.at[s], yatt_scr.at[s], ym_scr.at[s], carry, ghead_ref, bif_ref,
                            bias_scr, sinks_ref, first_of_sequence=(j == 0) if s == 0 else None)
        _store_carry(carry, kprev_scr, vprev_scr, cbd_scr, st_scr)
        for s in range(N_SUB):
            _merge(x_ref.at[s * T:(s + 1) * T, :], z_scr.at[s], yatt_scr.at[s], ym_scr.at[s],
                   wao_ref, wmo_ref, wout_ref, gffn_ref, y_ref.at[s * T:(s + 1) * T, :])
        for s in range(N_SUB):
            _ffn(yatt_scr.at[s], ym_scr.at[s], a_scr, wg_ref, wu_ref, wd_ref, gfin_ref,
                 y_ref.at[s * T:(s + 1) * T, :])

    @pl.when(jnp.logical_not(is_prompt))
    def _sample_step():
        nb = xs_ref.shape[0]
        scal = scals_ref[...]
        yatt_scr[0, 0:nb, :] = yatts_ref[...].astype(BF16)
        for hd in range(M_HEADS):
            inter = scal[:, SC_INTER + hd:SC_INTER + hd + 1]
            w = scal[:, SC_W + hd:SC_W + hd + 1]
            scale = scal[:, SC_SCALE + hd:SC_SCALE + hd + 1]
            hv = (inter * cqs_ref[:, 128 * hd:128 * hd + 128]
                  + w * zs_ref[:, VM + 128 * hd:VM + 128 * hd + 128]) * scale
            hn = hv * lax.rsqrt(jnp.mean(hv * hv, axis=-1, keepdims=True) + EPS)
            hn = hn * ghead_ref[:, 128 * hd:128 * hd + 128]
            om = zs_ref[:, OM + 128 * hd:OM + 128 * hd + 128]
            ym_scr[0, 0:nb, 128 * hd:128 * hd + 128] = (hn * _sigmoid(om)).astype(BF16)
        _merge(xs_ref, zs_ref, yatt_scr.at[0, 0:nb, :], ym_scr.at[0, 0:nb, :],
               wao_ref, wmo_ref, wout_ref, gffn_ref, ys_ref)
        _ffn(yatt_scr.at[0, 0:nb, :], ym_scr.at[0, 0:nb, :], a_scr.at[0:nb, :], wg_ref, wu_ref, wd_ref,
             gfin_ref, ys_ref)

    @pl.when(jnp.logical_and(is_prompt, j == nj - 1))
    def _write_state():
        kwin_ref[...] = kprev_scr[...].T
        vwin_ref[...] = vprev_scr[...].T
        stn = st_scr[...]
        cts = [cbd_scr[0].T, cbd_scr[1].T]
        for hd in range(M_HEADS):
            p, hh = hd // 2, hd % 2
            c_ref[hd] = cts[p][64 * hh:64 * hh + 64, 128 * hh:128 * hh + 128]
            n_ref[hd:hd + 1, :] = stn[p:p + 1, 64 * hh:64 * hh + 64]
            m_ref[0:1, hd:hd + 1] = stn[2 + hd:3 + hd, 0:1]


def _in_projection(x_ref, gattn_ref, wcat_ref, z_ref):
    h = _rms(x_ref[...], gattn_ref[...]).astype(BF16)
    for c0, c1 in Z_GROUPS:
        z_ref[:, c0:c1] = _dot(h, wcat_ref[:, c0:c1])


def _weight_chunks(wt_hbm, wcat_ref, plain):
    r = T_BLK
    chunks = [(wt_hbm.at[a:a + r, :], wcat_ref.at[:, a:a + r], True) for a in range(0, W_IF, r)]
    chunks.append((wt_hbm.at[W_IF:W_GA, :], wcat_ref.at[:, IF:IF + LANES], True))
    chunks += [(wt_hbm.at[a:a + r, :], wcat_ref.at[:, GA + a - W_GA:GA + a - W_GA + r], True)
               for a in range(W_GA, W_END, r)]
    for src, dst in plain:
        cols = src.shape[1]
        chunks += [(src.at[a:a + r, c0:min(c0 + STAGE_COLS, cols)], dst.at[a:a + r, c0:min(c0 + STAGE_COLS, cols)],
                    False)
                   for a in range(0, src.shape[0], r) for c0 in range(0, cols, STAGE_COLS)]
    return chunks


def _cast_weights(chunks, stage_ref, sem, while_first_copy):
    def window(i, n, c):
        slot = i % STAGE_SLOTS
        off = (slot // N_SUB) * STAGE_COLS
        return stage_ref.at[slot % N_SUB, 0:n, off:off + c]

    def copy(i):
        src = chunks[i][0]
        return pltpu.make_async_copy(src, window(i, *src.shape), sem.at[i % STAGE_SLOTS])

    ahead = STAGE_SLOTS - 1
    for i in range(min(ahead, len(chunks))):
        copy(i).start()
    for i, (src, dst, transpose) in enumerate(chunks):
        if i + ahead < len(chunks):
            copy(i + ahead).start()
        if i == 0:
            while_first_copy()
        copy(i).wait()
        n, c = src.shape
        val = window(i, n, c)[...]
        rows = dst.shape[1] if transpose else dst.shape[0]
        if rows != n:
            val = jnp.concatenate([val, jnp.zeros((rows - n, c), F32)], axis=0)
        dst[...] = (val.T if transpose else val).astype(BF16)


def _load_carry(kprev_scr, vprev_scr, cbd_scr, st_scr):
    st = st_scr[...]
    return dict(kp=kprev_scr[...], vp=vprev_scr[...], cbd=[cbd_scr[0], cbd_scr[1]],
                n=[st[0:1, :], st[1:2, :]], m=[st[2 + hd:3 + hd, 0:1] for hd in range(M_HEADS)])


def _store_carry(carry, kprev_scr, vprev_scr, cbd_scr, st_scr):
    kprev_scr[...] = carry["kp"]
    vprev_scr[...] = carry["vp"]
    for p in range(2):
        cbd_scr[p] = carry["cbd"][p]
        st_scr[p:p + 1, :] = carry["n"][p]
    for hd in range(M_HEADS):
        st_scr[2 + hd:3 + hd, :] = jnp.broadcast_to(carry["m"][hd], (1, LANES))


def _merge(x_ref, z_ref, yatt_ref, ym_ref, wao_ref, wmo_ref, wout_ref, gffn_ref, y_ref):
    half = yatt_ref.shape[1]
    ya = _dot(yatt_ref[...], wao_ref[...])
    ymm = _dot(ym_ref[...], wmo_ref[...])
    mixed = _sigmoid(z_ref[:, GA:GA + 1024]) * ya + _sigmoid(z_ref[:, GM:GM + 1024]) * ymm
    x1 = x_ref[...] + _dot(mixed.astype(BF16), wout_ref[...])
    y_ref[...] = x1
    h2 = _rms(x1, gffn_ref[...]).astype(BF16)
    yatt_ref[...] = h2[:, :half]
    ym_ref[...] = h2[:, half:]


def _ffn(h2a_ref, h2b_ref, a_ref, wg_ref, wu_ref, wd_ref, gfin_ref, y_ref):
    d_ff = wg_ref.shape[1]
    h2 = jnp.concatenate([h2a_ref[...], h2b_ref[...]], axis=1)
    for c0 in range(0, d_ff, 256):
        g = _dot(h2, wg_ref[:, c0:c0 + 256])
        u = _dot(h2, wu_ref[:, c0:c0 + 256])
        a_ref[:, c0:c0 + 256] = (g * _sigmoid(g) * u).astype(BF16)
    x2 = y_ref[...] + _dot(a_ref[...], wd_ref[...])
    y_ref[...] = _rms(x2, gfin_ref[...])


def _mixers(z_ref, yatt_ref, ym_ref, carry, ghead_ref, bif_ref, bias_scr, sinks_ref, first_of_sequence):
    T = T_BLK
    lane = lax.broadcasted_iota(jnp.int32, (T, LANES), 1)
    lane_lo = lane < HEAD_DIM
    k_all = z_ref[:, KA:KA + 128]
    v_all = z_ref[:, VA:VA + 128]
    jk = lax.broadcasted_iota(jnp.int32, (128, T), 0)
    iq = lax.broadcasted_iota(jnp.int32, (128, T), 1)
    iq = jnp.where(iq < 128, iq, iq - 128)
    tri_t = jk <= iq
    if first_of_sequence is not None:
        valid_t = jk <= iq + jnp.where(first_of_sequence, 0, 2 * LANES)
    lane_q = lax.broadcasted_iota(jnp.int32, (1, T), 1)
    feat_lo = lax.broadcasted_iota(jnp.int32, (128, T), 0) < HEAD_DIM
    vt_all = v_all.T
    for sb in range(2):
        r0 = 128 * sb
        if sb == 0:
            kp, vpt = carry["kp"], carry["vp"].T
        else:
            kp, vpt = k_all[0:128], vt_all[:, 0:128]
        kcat = jnp.concatenate([kp, k_all[r0:r0 + 128]], axis=0)
        kroll = pltpu.roll(kcat, HEAD_DIM, 1)
        zero = jnp.zeros_like(kcat)
        k_mats = [jnp.where(lane_lo, kcat, zero), jnp.where(lane_lo, zero, kroll),
                  jnp.where(lane_lo, kroll, zero), jnp.where(lane_lo, zero, kcat)]
        vcat_t = jnp.concatenate([vpt, vt_all[:, r0:r0 + 128]], axis=1)
        vroll_t = pltpu.roll(vcat_t, HEAD_DIM, 0)
        zero_t = jnp.zeros_like(vcat_t)
        v_mats_t = [jnp.where(feat_lo, vcat_t, zero_t), jnp.where(feat_lo, zero_t, vroll_t),
                    jnp.where(feat_lo, vroll_t, zero_t), jnp.where(feat_lo, zero_t, vcat_t)]
        q = (z_ref[r0:r0 + 128, QA:QA + 512] * (HEAD_DIM ** -0.5)).astype(BF16)
        lhs_a = jnp.concatenate([q[:, 0:128], q[:, 128:256]], axis=0)
        lhs_b = jnp.concatenate([q[:, 256:384], q[:, 384:512]], axis=0)
        groups = [(lhs_a, 0, 0, 2), (lhs_a, 1, 1, 3), (lhs_b, 2, 4, 6), (lhs_b, 3, 5, 7)]
        pts = []
        for lhs, mi, ha, hb in groups:
            st = _dot_nt(k_mats[mi].astype(BF16), lhs)
            sf = jnp.where(tri_t, st[128:256, :], st[0:128, :])
            sf = sf + jnp.concatenate([bias_scr[ha], bias_scr[hb]], axis=1)
            if sb == 0 and first_of_sequence is not None:
                sf = jnp.where(valid_t, sf, NEG)
            sink = jnp.where(lane_q < 128, sinks_ref[ha], sinks_ref[hb])
            mx = jnp.maximum(jnp.max(sf, axis=0, keepdims=True), sink)
            p = jnp.exp(sf - mx)
            den = jnp.sum(p, axis=0, keepdims=True) + jnp.exp(sink - mx)
            pn = p * (1.0 / den)
            zp = jnp.zeros_like(pn)
            pts.append(jnp.concatenate([jnp.where(tri_t, zp, pn), jnp.where(tri_t, pn, zp)],
                                       axis=0).astype(BF16))
        ot_a = _dot(jnp.concatenate([v_mats_t[0], v_mats_t[1]], axis=1).astype(BF16),
                    jnp.concatenate([pts[0], pts[1]], axis=0))
        ot_b = _dot(jnp.concatenate([v_mats_t[2], v_mats_t[3]], axis=1).astype(BF16),
                    jnp.concatenate([pts[2], pts[3]], axis=0))
        yatt_ref[r0:r0 + 128, 0:128] = ot_a[:, 0:128].T.astype(BF16)
        yatt_ref[r0:r0 + 128, 128:256] = ot_a[:, 128:256].T.astype(BF16)
        yatt_ref[r0:r0 + 128, 256:384] = ot_b[:, 0:128].T.astype(BF16)
        yatt_ref[r0:r0 + 128, 384:512] = ot_b[:, 128:256].T.astype(BF16)
    new_carry = dict(kp=k_all[128:256], vp=v_all[128:256], cbd=[None, None], n=[None, None],
                     m=[None] * M_HEADS)

    zif = z_ref[:, IF:IF + 128] + bif_ref[...]
    gl = jnp.where(lane < M_HEADS, zif, _log_sigmoid(zif))
    gl_t = gl.T
    tr = lax.broadcasted_iota(jnp.int32, (T, T), 0)
    ts = lax.broadcasted_iota(jnp.int32, (T, T), 1)
    tril = ts <= tr
    triu = tr <= ts
    row2 = lax.broadcasted_iota(jnp.int32, (2 * LANES, LANES), 0)
    lane2 = lax.broadcasted_iota(jnp.int32, (2 * LANES, LANES), 1)
    bd_mask = (row2 < LANES) == (lane2 < HEAD_DIM)
    for p in range(2):
        q_pair = z_ref[:, QM + 128 * p:QM + 128 * p + 128]
        k_pair = z_ref[:, KM + 128 * p:KM + 128 * p + 128] * (HEAD_DIM ** -0.5)
        v_pair = z_ref[:, VM + 256 * p:VM + 256 * p + 256]
        cbd = carry["cbd"][p]
        n_pair = carry["n"][p]
        q_bf = q_pair.astype(BF16)
        qc = _dot_nt(q_bf, cbd.astype(BF16))
        qn_prod = q_pair * n_pair
        ws, decays, m_ends = [], [], []
        for hh in range(2):
            hd = 2 * p + hh
            hmask = lane_lo if hh == 0 else jnp.logical_not(lane_lo)
            ig_c = gl[:, hd:hd + 1]
            lf_c = gl[:, M_HEADS + hd:M_HEADS + hd + 1]
            ig_r = gl_t[hd:hd + 1, :]
            lf_r = gl_t[M_HEADS + hd:M_HEADS + hd + 1, :]
            m_prev = carry["m"][hd]
            b_c = jnp.sum(jnp.where(tril, lf_r, 0.0), axis=1, keepdims=True)
            b_r = jnp.sum(jnp.where(triu, lf_c, 0.0), axis=0, keepdims=True)
            a_r = ig_r - b_r
            cm_c = jnp.max(jnp.where(tril, a_r, NEG), axis=1, keepdims=True)
            mt_c = b_c + jnp.maximum(m_prev, cm_c)
            g_c = b_c - mt_c
            dm = jnp.exp(jnp.where(tril, a_r + g_c, NEG))
            inter_c = jnp.exp(m_prev + g_c)
            k_h = jnp.where(hmask, k_pair, 0.0).astype(BF16)
            w = dm * _dot_nt(q_bf, k_h)
            v_h = v_pair[:, 128 * hh:128 * hh + 128]
            num = inter_c * qc[:, 128 * hh:128 * hh + 128] + _dot(w.astype(BF16), v_h.astype(BF16))
            qn = jnp.sum(jnp.where(hmask, qn_prod, 0.0), axis=1, keepdims=True)
            den = inter_c * qn + jnp.sum(w, axis=1, keepdims=True)
            hv = num / jnp.maximum(jnp.abs(den), jnp.exp(-mt_c))
            hn = hv * lax.rsqrt(jnp.mean(hv * hv, axis=-1, keepdims=True) + EPS)
            hn = hn * ghead_ref[:, 128 * hd:128 * hd + 128]
            om = z_ref[:, OM + 128 * hd:OM + 128 * hd + 128]
            ym_ref[:, 128 * hd:128 * hd + 128] = (hn * _sigmoid(om)).astype(BF16)
            b_end = b_c[T - 1:T, :]
            m_end = mt_c[T - 1:T, :]
            ws.append(jnp.exp((ig_c - b_c) + b_end - m_end))
            decays.append(jnp.exp(m_prev + b_end - m_end))
            m_ends.append(m_end)
        kw = k_pair * jnp.where(lane_lo, ws[0], ws[1])
        upd = _dot(v_pair.T.astype(BF16), kw.astype(BF16))
        dec_rows = jnp.where(row2[:, 0:1] < LANES, decays[0], decays[1])
        new_carry["cbd"][p] = dec_rows * cbd + jnp.where(bd_mask, upd, 0.0)
        dec_lanes = jnp.where(lane_lo[0:1, :], decays[0], decays[1])
        new_carry["n"][p] = dec_lanes * n_pair + jnp.sum(kw, axis=0, keepdims=True)
        for hh in range(2):
            new_carry["m"][2 * p + hh] = m_ends[hh]
    return new_carry


def _resident(shape):
    zeros = (0,) * len(shape)
    return pl.BlockSpec(shape, lambda t: zeros, pipeline_mode=pl.Buffered(1))


def _smem():
    return pl.BlockSpec(memory_space=pltpu.SMEM)


def _layer_call(x, wt, wao, wmo, wout, wg, wu, wd, gattn, gffn, gfin, ghead, bif, fbuck, relb, sinks,
                xs3, zs, yatts, cqs, scals):
    bsz, s_len, d = x.shape
    nb = xs3.shape[0]
    t_step = T_BLK * N_SUB
    assert s_len % t_step == 0 and d == 1024 and nb <= T_BLK
    assert wt.shape == (W_END, d) and N_SUB >= 2
    d_ff = wg.shape[1]
    assert all(w.shape[0] % T_BLK == 0 and w.shape[1] <= N_CAT for w in (wao, wmo, wout, wg, wu, wd))
    hbm = pl.BlockSpec(memory_space=pl.ANY)
    nj = s_len // t_step
    n_prompt = bsz * nj
    out_shapes = (
        jax.ShapeDtypeStruct((bsz, s_len, d), F32),
        jax.ShapeDtypeStruct((bsz, WINDOW, 128), F32),
        jax.ShapeDtypeStruct((bsz, WINDOW, 128), F32),
        jax.ShapeDtypeStruct((bsz, M_HEADS, 64, 128), F32),
        jax.ShapeDtypeStruct((bsz, M_HEADS, 64), F32),
        jax.ShapeDtypeStruct((bsz, 1, M_HEADS), F32),
        jax.ShapeDtypeStruct((nb, 1, d), F32),
    )

    def seq(t):
        return jnp.minimum(t, n_prompt - 1) // nj

    def blk(t):
        return jnp.minimum(t, n_prompt - 1) % nj

    in_specs = [
        pl.BlockSpec((None, t_step, d), lambda t: (seq(t), blk(t), 0)),
        hbm, hbm, hbm, hbm, hbm, hbm, hbm,
        _resident(gattn.shape), _resident(gffn.shape), _resident(gfin.shape), _resident(ghead.shape),
        _resident(bif.shape), _resident(fbuck.shape), _smem(), _smem(),
        pl.BlockSpec((nb, None, d), lambda t: (0, 0, 0), pipeline_mode=pl.Buffered(1)),
        _resident(zs.shape), _resident(yatts.shape), _resident(cqs.shape), _resident(scals.shape),
    ]
    out_specs = (
        pl.BlockSpec((None, t_step, d), lambda t: (seq(t), blk(t), 0)),
        pl.BlockSpec((None, WINDOW, 128), lambda t: (seq(t), 0, 0)),
        pl.BlockSpec((None, WINDOW, 128), lambda t: (seq(t), 0, 0)),
        pl.BlockSpec((None, M_HEADS, 64, 128), lambda t: (seq(t), 0, 0, 0)),
        pl.BlockSpec((None, M_HEADS, 64), lambda t: (seq(t), 0, 0)),
        pl.BlockSpec((None, 1, M_HEADS), lambda t: (seq(t), 0, 0)),
        pl.BlockSpec((nb, None, d), lambda t: (0, 0, 0)),
    )
    scratch = [
        pltpu.VMEM((N_SUB, T_BLK, N_CAT), F32),
        pltpu.VMEM((N_SUB, T_BLK, 512), BF16),
        pltpu.VMEM((N_SUB, T_BLK, 512), BF16),
        pltpu.VMEM((T_BLK, d_ff), BF16),
        pltpu.VMEM((N_Q_HEADS, 128, 128), F32),
        pltpu.VMEM((128, 128), F32),
        pltpu.VMEM((128, 128), F32),
        pltpu.VMEM((2, 256, 128), F32),
        pltpu.VMEM((8, 128), F32),
        pltpu.VMEM((d, N_CAT), BF16),
        pltpu.VMEM(wao.shape, BF16), pltpu.VMEM(wmo.shape, BF16), pltpu.VMEM(wout.shape, BF16),
        pltpu.VMEM(wg.shape, BF16), pltpu.VMEM(wu.shape, BF16), pltpu.VMEM(wd.shape, BF16),
        pltpu.SemaphoreType.DMA((STAGE_SLOTS,)),
    ]
    return pl.pallas_call(
        functools.partial(_layer_kernel, nj),
        grid=(n_prompt + 1,),
        in_specs=in_specs,
        out_specs=out_specs,
        out_shape=out_shapes,
        scratch_shapes=scratch,
        compiler_params=pltpu.CompilerParams(
            dimension_semantics=("arbitrary",),
            vmem_limit_bytes=VMEM_LIMIT_BYTES),
        name="layer",
    )(x, wt, wao, wmo, wout, wg, wu, wd, gattn, gffn, gfin, ghead, bif, fbuck, relb, sinks,
      xs3, zs, yatts, cqs, scals)


W_QA, W_KA, W_VA, W_QM, W_KM, W_VM, W_OM, W_IF, W_GA, W_GM, W_END = (
    0, 512, 640, 768, 1024, 1280, 1792, 2304, 2312, 3336, 4360)
ZT_KA, ZT_VA, ZT_QM, ZT_KM, ZT_IF, ZT_ROWS = 0, 128, 256, 512, 768, 776


def _sample_inproj_kernel(x_ref, g_ref, wt_hbm, nt_ref, mt_ref, bif_ref,
                          z_ref, zt_ref, no_ref, mo_ref, scal_ref,
                          wt_ref, sems):
    _sample_projection(x_ref, g_ref, wt_hbm, wt_ref, sems, z_ref, zt_ref)
    r8c = lax.broadcasted_iota(jnp.int32, (8, 1), 0)
    bias_col = jnp.zeros((8, 1), F32)
    for r in range(8):
        bias_col = jnp.where(r8c == r, bif_ref[r // M_HEADS, r % M_HEADS], bias_col)
    ift = zt_ref[ZT_IF:ZT_ROWS, :] + bias_col
    m0 = mt_ref[...]
    r16 = lax.broadcasted_iota(jnp.int32, (16, LANES), 0)
    table = jnp.zeros((16, LANES), F32)
    for hd in range(M_HEADS):
        ig = ift[hd:hd + 1, :]
        lf = _log_sigmoid(ift[M_HEADS + hd:M_HEADS + hd + 1, :])
        m_prev = m0[hd:hd + 1, :]
        a = ig - lf
        m_t = lf + jnp.maximum(m_prev, a)
        dgate = jnp.exp(a + lf - m_t)
        inter = jnp.exp(m_prev + lf - m_t)
        qt = zt_ref[ZT_QM + 64 * hd:ZT_QM + 64 * hd + 64, :]
        kt = zt_ref[ZT_KM + 64 * hd:ZT_KM + 64 * hd + 64, :] * (HEAD_DIM ** -0.5)
        nt = nt_ref[hd]
        no_ref[hd] = inter * nt + dgate * kt
        mo_ref[hd:hd + 1, :] = m_t
        qk = jnp.sum(qt * kt, axis=0, keepdims=True)
        nq = jnp.sum(nt * qt, axis=0, keepdims=True)
        w = dgate * qk
        den = inter * nq + w
        scale = 1.0 / jnp.maximum(jnp.abs(den), jnp.exp(-m_t))
        for base, val in ((SC_INTER, inter), (SC_W, w), (SC_SCALE, scale), (SC_D, dgate)):
            table = jnp.where(r16 == base + hd, val, table)
    full = jnp.concatenate([table, jnp.zeros((LANES - 16, LANES), F32)], axis=0)
    scal_ref[...] = full.T


WT_CHUNKS = ((W_QA, 768), (768, 1536), (1536, W_IF), (W_IF, W_GA), (W_GA, W_GM), (W_GM, W_END))


def _sample_projection(x_ref, g_ref, wt_hbm, wt_ref, sems, z_ref, zt_ref):
    copies = [pltpu.make_async_copy(wt_hbm.at[a:b, :], wt_ref.at[a:b, :], sems.at[k])
              for k, (a, b) in enumerate(WT_CHUNKS)]
    for c in copies:
        c.start()
    h32 = _rms(x_ref[...], g_ref[...])
    h = h32.astype(BF16)
    ht = h32.T.astype(BF16)
    copies[0].wait()
    z_ref[:, QA:768] = _dot_nt(h, wt_ref[W_QA:768, :].astype(BF16))
    zt_ref[ZT_KA:ZT_QM, :] = _dot(wt_ref[W_KA:W_QM, :].astype(BF16), ht)
    copies[1].wait()
    z_ref[:, 768:1536] = _dot_nt(h, wt_ref[768:1536, :].astype(BF16))
    zt_ref[ZT_QM:ZT_IF, :] = _dot(wt_ref[W_QM:W_VM, :].astype(BF16), ht)
    copies[2].wait()
    z_ref[:, 1536:W_IF] = _dot_nt(h, wt_ref[1536:W_IF, :].astype(BF16))
    copies[3].wait()
    w_if = jnp.concatenate([wt_ref[W_IF:W_GA, :], jnp.zeros((LANES - 8, wt_ref.shape[1]), F32)],
                           axis=0).astype(BF16)
    z_ref[:, IF:IF + LANES] = _dot_nt(h, w_if)
    zt_ref[ZT_IF:ZT_ROWS, :] = _dot(w_if, ht)[0:8, :]
    copies[4].wait()
    z_ref[:, GA:GM] = _dot_nt(h, wt_ref[W_GA:W_GM, :].astype(BF16))
    copies[5].wait()
    z_ref[:, GM:N_CAT] = _dot_nt(h, wt_ref[W_GM:W_END, :].astype(BF16))


def _sample_inproj_call(x3, gattn, wt, nt, mt, bif2):
    n, _, d = x3.shape
    assert n == LANES
    full = lambda s: pl.BlockSpec(tuple(s), lambda i: (0,) * len(s))
    return pl.pallas_call(
        _sample_inproj_kernel,
        grid=(1,),
        in_specs=[pl.BlockSpec((n, None, d), lambda i: (0, 0, 0)),
                  pl.BlockSpec((1, d), lambda i: (0, 0)),
                  pl.BlockSpec(memory_space=pl.ANY),
                  full(nt.shape), full(mt.shape), _smem()],
        out_specs=(full((n, N_CAT)), full((ZT_ROWS, n)), full(nt.shape), full(mt.shape),
                   full((n, LANES))),
        out_shape=(jax.ShapeDtypeStruct((n, N_CAT), F32), jax.ShapeDtypeStruct((ZT_ROWS, n), F32),
                   jax.ShapeDtypeStruct(nt.shape, F32), jax.ShapeDtypeStruct(mt.shape, F32),
                   jax.ShapeDtypeStruct((n, LANES), F32)),
        scratch_shapes=[pltpu.VMEM(wt.shape, F32), pltpu.SemaphoreType.DMA((len(WT_CHUNKS),))],
        compiler_params=pltpu.CompilerParams(dimension_semantics=("arbitrary",),
                                             vmem_limit_bytes=VMEM_LIMIT_BYTES),
        name="sample_inproj",
    )(x3, gattn, wt, nt, mt, bif2)


SK2_BB = 16
SC_INTER, SC_W, SC_SCALE, SC_D, SC_COLS = 0, 4, 8, 12, 16
HEAD_ROW_ORDER = (0, 2, 4, 6, 1, 3, 5, 7)


def _sample_mixer_kernel(z_ref, zt_ref, kt_ref, vt_ref, ct_ref, sbuck_ref,
                         scal_ref, relb_ref, sinks_ref,
                         yatt_ref, cq_ref, ko_ref, vo_ref, co_ref,
                         sbias_scr):
    i = pl.program_id(0)
    r8 = lax.broadcasted_iota(jnp.int32, (8, LANES), 0)
    l8 = lax.broadcasted_iota(jnp.int32, (8, LANES), 1)
    r8c = lax.broadcasted_iota(jnp.int32, (8, 1), 0)

    @pl.when(i == 0)
    def _prologue():
        sb = sbuck_ref[...]
        acc = jnp.zeros((8, LANES), F32)
        for rrow, hd in enumerate(HEAD_ROW_ORDER):
            acc = jnp.where(r8 == rrow, _bias_from_buckets(sb, relb_ref, hd), acc)
        sbias_scr[...] = acc

    sink = jnp.zeros((8, 1), F32)
    for rrow, hd in enumerate(HEAD_ROW_ORDER):
        sink = jnp.where(r8c == rrow, sinks_ref[hd], sink)
    sbias = sbias_scr[...]
    lane_w = lax.broadcasted_iota(jnp.int32, (WINDOW, LANES), 1)
    last_lane = lane_w == WINDOW - 1
    r256 = lax.broadcasted_iota(jnp.int32, (8, 2 * LANES), 0)
    l256 = lax.broadcasted_iota(jnp.int32, (8, 2 * LANES), 1)
    own64 = (l256 // HEAD_DIM) == r256
    l512r = lax.broadcasted_iota(jnp.int32, (1, 4 * LANES), 1) // LANES
    lo1 = lax.broadcasted_iota(jnp.int32, (1, LANES), 1) < HEAD_DIM
    row_b = lax.broadcasted_iota(jnp.int32, (LANES, 2 * LANES), 0)
    ktm = [(zt_ref[ZT_KM + 128 * pr:ZT_KM + 128 * pr + 128, :] * (HEAD_DIM ** -0.5)).astype(BF16)
           for pr in range(2)]

    nrow = 8 * SK2_BB
    lst = lax.broadcasted_iota(jnp.int32, (nrow, LANES), 1)
    z1 = jnp.zeros((1, LANES), F32)
    qm_l, so_l, kn_l, vn_l = [], [], [], []
    for bi in range(SK2_BB):
        qp = [z_ref[bi:bi + 1, QA + 128 * p:QA + 128 * p + 128] * (HEAD_DIM ** -0.5) for p in range(4)]
        qpr = [pltpu.roll(x, HEAD_DIM, 1) for x in qp]
        rows = [jnp.where(lo1, qp[0], z1), jnp.where(lo1, qp[1], z1),
                jnp.where(lo1, z1, qpr[2]), jnp.where(lo1, z1, qpr[3]),
                jnp.where(lo1, qpr[0], z1), jnp.where(lo1, qpr[1], z1),
                jnp.where(lo1, z1, qp[2]), jnp.where(lo1, z1, qp[3])]
        qm = jnp.zeros((8, LANES), F32)
        for r in range(8):
            qm = jnp.where(r8 == r, rows[r], qm)
        qm_l.append(qm)
        so_l.append(_dot(qm.astype(BF16), kt_ref[bi].astype(BF16)))
        kn_l.append(jnp.broadcast_to(z_ref[bi:bi + 1, KA:KA + 128], (8, LANES)))
        vn_l.append(jnp.broadcast_to(z_ref[bi:bi + 1, VA:VA + 128], (8, LANES)))
    qm_all = jnp.concatenate(qm_l, axis=0)
    s_old = jnp.concatenate(so_l, axis=0)
    vn_all = jnp.concatenate(vn_l, axis=0)
    s_new = jnp.sum(qm_all * jnp.concatenate(kn_l, axis=0), axis=1, keepdims=True)
    sbias_all = jnp.concatenate([sbias] * SK2_BB, axis=0)
    sink_all = jnp.concatenate([sink] * SK2_BB, axis=0)
    s = jnp.where(lst == WINDOW - 1, s_new, pltpu.roll(s_old, WINDOW - 1, 1)) + sbias_all
    mx = jnp.maximum(jnp.max(s, axis=-1, keepdims=True), sink_all)
    pe = jnp.exp(s - mx)
    den = jnp.sum(pe, axis=-1, keepdims=True) + jnp.exp(sink_all - mx)
    pn = pe * (1.0 / den)
    p_old = jnp.where(lst == 0, 0.0, pltpu.roll(pn, 1, 1))
    oa_l = [_dot_nt(p_old[8 * bi:8 * bi + 8].astype(BF16), vt_ref[bi].astype(BF16))
            for bi in range(SK2_BB)]
    oa = jnp.concatenate(oa_l, axis=0) + pn[:, WINDOW - 1:WINDOW] * vn_all
    oar = pltpu.roll(oa, HEAD_DIM, 1)
    for bi in range(SK2_BB):
        r0 = 8 * bi
        pairs = [jnp.where(lo1, oa[r0:r0 + 1], oar[r0 + 4:r0 + 5]),
                 jnp.where(lo1, oa[r0 + 1:r0 + 2], oar[r0 + 5:r0 + 6]),
                 jnp.where(lo1, oar[r0 + 2:r0 + 3], oa[r0 + 6:r0 + 7]),
                 jnp.where(lo1, oar[r0 + 3:r0 + 4], oa[r0 + 7:r0 + 8])]
        for p in range(4):
            yatt_ref[bi:bi + 1, 128 * p:128 * p + 128] = pairs[p]

    for bi in range(SK2_BB):
        b = i * SK2_BB + bi
        shift = WINDOW - 1 - b
        kcol = pltpu.roll(zt_ref[ZT_KA:ZT_KA + 128, :], shift, 1)
        vcol = pltpu.roll(zt_ref[ZT_VA:ZT_VA + 128, :], shift, 1)
        ko_ref[bi] = jnp.where(last_lane, kcol, pltpu.roll(kt_ref[bi], WINDOW - 1, 1))
        vo_ref[bi] = jnp.where(last_lane, vcol, pltpu.roll(vt_ref[bi], WINDOW - 1, 1))

        qm_row = z_ref[bi:bi + 1, QM:QM + 256]
        vm_row = z_ref[bi:bi + 1, VM:VM + 512]
        qbd = jnp.where(own64, qm_row, 0.0)
        ct_all = jnp.concatenate([ct_ref[bi, hd] for hd in range(M_HEADS)], axis=0)
        cq = _dot(qbd.astype(BF16), ct_all.astype(BF16))
        for hd in range(M_HEADS):
            cq_ref[bi:bi + 1, 128 * hd:128 * hd + 128] = cq[hd:hd + 1, :]
        dsel = jnp.zeros((1, 4 * LANES), F32)
        for hd in range(M_HEADS):
            dsel = jnp.where(l512r == hd, scal_ref[b, SC_D + hd], dsel)
        vs = vm_row * dsel
        for pr in range(2):
            vsel = jnp.where(row_b == b, vs[:, 256 * pr:256 * pr + 256], 0.0).astype(BF16)
            upd = _dot(ktm[pr], vsel)
            for hh in range(2):
                hd = 2 * pr + hh
                co_ref[bi, hd] = (scal_ref[b, SC_INTER + hd] * ct_ref[bi, hd]
                                  + upd[64 * hh:64 * hh + 64, 128 * hh:128 * hh + 128])


def _sample_mixer_call(z, zt, kt, vt, ct, sbuck, scal_small, relb_t, sinks):
    nb = z.shape[0]
    assert nb % SK2_BB == 0 and nb == LANES
    bb = SK2_BB
    out_shapes = (
        jax.ShapeDtypeStruct((nb, 512), F32),
        jax.ShapeDtypeStruct((nb, 512), F32),
        jax.ShapeDtypeStruct(kt.shape, F32),
        jax.ShapeDtypeStruct(vt.shape, F32),
        jax.ShapeDtypeStruct(ct.shape, F32),
    )
    blk = lambda s: pl.BlockSpec((bb,) + tuple(s[1:]), lambda i: (i,) + (0,) * (len(s) - 1))
    full = lambda s: pl.BlockSpec(tuple(s), lambda i: (0,) * len(s))
    in_specs = [blk(z.shape), full(zt.shape), blk(kt.shape), blk(vt.shape), blk(ct.shape),
                full(sbuck.shape), _smem(), _smem(), _smem()]
    out_specs = (blk((nb, 512)), blk((nb, 512)), blk(kt.shape), blk(vt.shape), blk(ct.shape))
    return pl.pallas_call(
        _sample_mixer_kernel,
        grid=(nb // bb,),
        in_specs=in_specs,
        out_specs=out_specs,
        out_shape=out_shapes,
        scratch_shapes=[pltpu.VMEM((8, LANES), F32)],
        compiler_params=pltpu.CompilerParams(dimension_semantics=("arbitrary",)),
        name="sample_mixers",
    )(z, zt, kt, vt, ct, sbuck, scal_small, relb_t, sinks)


def kernel(x_prompt, x_sample, cache_k_win, cache_v_win, state_mlstm_C, state_mlstm_n, state_mlstm_m,
           rel_bias, w_in, b_if, sinks, g_attn_norm, g_head, w_att_out, w_mlstm_out, w_out,
           g_ffn_norm, w_gate, w_up, w_down, g_final):
    depth = w_in.shape[0]
    assert depth == 1
    bsz, s_len, d = x_prompt.shape
    nb = x_sample.shape[0]
    assert x_sample.shape[1] == 1 and cache_k_win.shape[2] == WINDOW

    wt = w_in[0].T
    gattn = g_attn_norm[0].reshape(1, d)
    gffn = g_ffn_norm[0].reshape(1, d)
    gfin = g_final.reshape(1, d)
    ghead = g_head[0].reshape(1, 512)
    bif = jnp.concatenate([b_if[0].reshape(1, 2 * M_HEADS), jnp.zeros((1, LANES - 2 * M_HEADS), F32)], axis=1)
    sink_v = sinks[0]
    relb_t = rel_bias.T

    qi = jnp.arange(WINDOW)[:, None]
    kj = jnp.arange(WINDOW)[None, :]
    fbuck = _t5_bucket(jnp.where(kj <= qi, qi - kj, WINDOW + qi - kj)).astype(jnp.int32).T
    sbuck = jnp.broadcast_to(_t5_bucket(WINDOW - 1 - kj), (8, WINDOW)).astype(jnp.int32)

    feat = N_KV_HEADS * HEAD_DIM
    kt_in = jnp.transpose(cache_k_win[0], (0, 2, 3, 1)).reshape(nb, feat, WINDOW)
    vt_in = jnp.transpose(cache_v_win[0], (0, 2, 3, 1)).reshape(nb, feat, WINDOW)
    ct_in = jnp.transpose(state_mlstm_C[0], (0, 1, 3, 2))
    nt_in = jnp.transpose(state_mlstm_n[0], (1, 2, 0))
    mt_in = state_mlstm_m[0].T
    z_s, zt_s, nt_s, mt_s, scal_s = _sample_inproj_call(x_sample, gattn, wt, nt_in, mt_in, b_if[0])
    yatt_s, cq_s, kt_s, vt_s, ct_s = _sample_mixer_call(
        z_s, zt_s, kt_in, vt_in, ct_in, sbuck, scal_s[:, :SC_COLS], relb_t, sink_v)
    y_p, kt_p, vt_p, ct_p, n_p, m_p, y_s = _layer_call(
        x_prompt, wt, w_att_out[0], w_mlstm_out[0], w_out[0], w_gate[0], w_up[0], w_down[0],
        gattn, gffn, gfin, ghead, bif, fbuck, relb_t, sink_v,
        x_sample, z_s, yatt_s, cq_s, scal_s)

    def window_out(t, n):
        return jnp.transpose(t.reshape(n, N_KV_HEADS, HEAD_DIM, WINDOW), (0, 3, 1, 2))[None]

    return (y_p, y_s,
            window_out(kt_p, bsz), window_out(vt_p, bsz),
            jnp.transpose(ct_p, (0, 1, 3, 2))[None], n_p[None], m_p.reshape(1, bsz, M_HEADS),
            window_out(kt_s, nb), window_out(vt_s, nb),
            jnp.transpose(ct_s, (0, 1, 3, 2))[None], jnp.transpose(nt_s, (2, 0, 1))[None], mt_s.T[None])
```
